```python
import jax
import jax.numpy as jnp
from jax import lax
import numpy as np

D_MODEL = 2048
BATCH = 8
SEQ = 4096
DEPTH = 4

GRID_W = 64
CTX_LEN = 256
N_MIXERS = 2
N_ATTN_LAYERS = (DEPTH + 1) // 2
N_MLSTM_LAYERS = DEPTH // 2
RMS_EPS = 1e-6

ATTN_HEAD_DIM = 128
ATTN_Q_HEADS = D_MODEL // ATTN_HEAD_DIM
ATTN_KV_HEADS = 4
ATTN_GROUP = ATTN_Q_HEADS // ATTN_KV_HEADS
ATTN_QKV_COLS = (ATTN_Q_HEADS + 2 * ATTN_KV_HEADS) * ATTN_HEAD_DIM
WINDOW = 128
ATTN_BLOCK = WINDOW
ROPE_THETA = 10000.0
NEG_INF = -1e30

MLSTM_HEADS = 8
MLSTM_DV = D_MODEL // MLSTM_HEADS
MLSTM_DQK = MLSTM_DV // 2
MLSTM_CHUNK = 64
MLSTM_QK_COLS = MLSTM_HEADS * MLSTM_DQK
MLSTM_V_COLS = MLSTM_HEADS * MLSTM_DV
MLSTM_IN_COLS = 2 * MLSTM_QK_COLS + 2 * MLSTM_V_COLS + 4 * MLSTM_HEADS

D_FF = 5632
CONV_W = 3

kernel_name = 'hybrid_swa_mlstm_convffn_dit'


def rmsnorm(x, g):
    xf = x.astype(jnp.float32)
    y = xf * lax.rsqrt(jnp.mean(xf * xf, axis=-1, keepdims=True) + RMS_EPS)
    return (y * g.astype(jnp.float32)).astype(x.dtype)


def modulate(h, shift, scale):
    return h * (1.0 + scale) + shift


def axial_rope_tables(n_tokens):
    rows = n_tokens // GRID_W
    row = jnp.repeat(jnp.arange(rows, dtype=jnp.float32), GRID_W)
    col = jnp.tile(jnp.arange(GRID_W, dtype=jnp.float32), rows)
    n_freq = ATTN_HEAD_DIM // 4
    inv_freq = ROPE_THETA ** (-jnp.arange(n_freq, dtype=jnp.float32) / n_freq)
    ang = jnp.concatenate([row[:, None] * inv_freq, col[:, None] * inv_freq], axis=-1)
    ang = jnp.concatenate([ang, ang], axis=-1)
    return jnp.cos(ang), jnp.sin(ang)


def apply_rope(t, cos, sin):
    half = t.shape[-1] // 2
    tf = t.astype(jnp.float32)
    rot = jnp.concatenate([-tf[..., half:], tf[..., :half]], axis=-1)
    return (tf * cos[None, :, None, :] + rot * sin[None, :, None, :]).astype(t.dtype)


def windowed_gqa_with_context(hx, hc, w_qkv, sink, w_o, cos, sin, need_ctx_out):
    B, L, _ = hx.shape
    Lc = hc.shape[1]
    H, KVH, G, Dh, BLK = ATTN_Q_HEADS, ATTN_KV_HEADS, ATTN_GROUP, ATTN_HEAD_DIM, ATTN_BLOCK
    scale = Dh ** -0.5

    def project(h):
        T = h.shape[1]
        q, k, v = jnp.split(h @ w_qkv, [H * Dh, (H + KVH) * Dh], axis=-1)
        return q.reshape(B, T, H, Dh), k.reshape(B, T, KVH, Dh), v.reshape(B, T, KVH, Dh)

    qx, kx, vx = project(hx)
    qc, kc, vc = project(hc)
    qx = apply_rope(qx, cos, sin)
    kx = apply_rope(kx, cos, sin)
    sink_g = sink.reshape(KVH, G).astype(jnp.float32)

    nb = L // BLK
    qb = qx.reshape(B, nb, BLK, KVH, G, Dh) * scale

    def band(t):
        tp = jnp.pad(t, ((0, 0), (BLK, BLK), (0, 0), (0, 0)))
        tb = tp.reshape(B, nb + 2, BLK, KVH, Dh)
        return jnp.concatenate([tb[:, :-2], tb[:, 1:-1], tb[:, 2:]], axis=2)

    kb, vb = band(kx), band(vx)
    qi = jnp.arange(BLK)[:, None]
    kj = jnp.arange(3 * BLK)[None, :]
    kpos = (jnp.arange(nb)[:, None, None] - 1) * BLK + kj[None]
    valid = (jnp.abs(kj - BLK - qi) <= WINDOW)[None] & (kpos >= 0) & (kpos < L)

    s_loc = jnp.einsum('bnqhgd,bnkhd->bhgnqk', qb, kb).astype(jnp.float32)
    s_loc = jnp.where(valid, s_loc, NEG_INF)
    s_ctx = jnp.einsum('bnqhgd,bchd->bhgnqc', qb, kc).astype(jnp.float32)
    s_sink = jnp.broadcast_to(sink_g[None, :, :, None, None, None], s_loc.shape[:-1] + (1,))
    p = jax.nn.softmax(jnp.concatenate([s_loc, s_ctx, s_sink], axis=-1), axis=-1).astype(vx.dtype)
    ox = (jnp.einsum('bhgnqk,bnkhd->bnqhgd', p[..., :3 * BLK], vb)
          + jnp.einsum('bhgnqc,bchd->bnqhgd', p[..., 3 * BLK:3 * BLK + Lc], vc))
    out_x = ox.reshape(B, L, H * Dh) @ w_o
    if not need_ctx_out:
        return out_x, None

    qcg = qc.reshape(B, Lc, KVH, G, Dh) * scale
    s_cc = jnp.einsum('bqhgd,bkhd->bhgqk', qcg, kc).astype(jnp.float32)
    s_sink_c = jnp.broadcast_to(sink_g[None, :, :, None, None], s_cc.shape[:-1] + (1,))
    pc = jax.nn.softmax(jnp.concatenate([s_cc, s_sink_c], axis=-1), axis=-1)[..., :Lc].astype(vc.dtype)
    oc = jnp.einsum('bhgqk,bkhd->bqhgd', pc, vc)
    out_c = oc.reshape(B, Lc, H * Dh) @ w_o
    return out_x, out_c


def mlstm_chunk_scan(q, k, v, i_pre, log_f, state):
    B, NH, T, _ = q.shape
    CH = MLSTM_CHUNK
    nc = T // CH

    def chunks(t):
        t = t.reshape(t.shape[:2] + (nc, CH) + t.shape[3:])
        return jnp.moveaxis(t, 2, 0)

    causal = jnp.tril(jnp.ones((CH, CH), dtype=bool))

    def step(carry, inp):
        C, n, m = carry
        qc, kc, vc, ic, fc = inp
        b = jnp.cumsum(fc, axis=-1)
        d = b[..., :, None] - b[..., None, :] + ic[..., None, :]
        d = jnp.where(causal, d, -jnp.inf)
        inter = b + m[..., None]
        m_t = jnp.maximum(inter, jnp.max(d, axis=-1))
        w = jnp.exp(d - m_t[..., None])
        e_inter = jnp.exp(inter - m_t)
        s = jnp.einsum('bhtd,bhsd->bhts', qc, kc) * w
        num = e_inter[..., None] * jnp.einsum('bhtd,bhdv->bhtv', qc, C) + jnp.einsum('bhts,bhsv->bhtv', s, vc)
        den = e_inter * jnp.einsum('bhtd,bhd->bht', qc, n) + jnp.sum(s, axis=-1)
        h = num / jnp.maximum(jnp.abs(den), jnp.exp(-m_t))[..., None]
        b_end = b[..., -1]
        g = b_end[..., None] - b + ic
        m_new = jnp.maximum(b_end + m, jnp.max(g, axis=-1))
        e_prev = jnp.exp(b_end + m - m_new)
        wg = jnp.exp(g - m_new[..., None])
        C_new = e_prev[..., None, None] * C + jnp.einsum('bhs,bhsd,bhsv->bhdv', wg, kc, vc)
        n_new = e_prev[..., None] * n + jnp.einsum('bhs,bhsd->bhd', wg, kc)
        return (C_new, n_new, m_new), h

    state, h = lax.scan(step, state, (chunks(q), chunks(k), chunks(v), chunks(i_pre), chunks(log_f)))
    h = jnp.moveaxis(h, 0, 2).reshape(B, NH, T, MLSTM_DV)
    return h, state


def mlstm_project(h, w_in, b_in):
    B, T, _ = h.shape
    NH = MLSTM_HEADS
    z = (h @ w_in + b_in).astype(jnp.float32)
    q, k, v, o, g = jnp.split(z, [MLSTM_QK_COLS, 2 * MLSTM_QK_COLS, 2 * MLSTM_QK_COLS + MLSTM_V_COLS,
                                  2 * MLSTM_QK_COLS + 2 * MLSTM_V_COLS], axis=-1)
    q = q.reshape(B, T, NH, MLSTM_DQK).transpose(0, 2, 1, 3)
    k = k.reshape(B, T, NH, MLSTM_DQK).transpose(0, 2, 1, 3) * (MLSTM_DQK ** -0.5)
    v = v.reshape(B, T, NH, MLSTM_DV).transpose(0, 2, 1, 3)
    g = g.reshape(B, T, 4, NH).transpose(2, 0, 3, 1)
    fwd = (g[0], jax.nn.log_sigmoid(g[1]))
    bwd = (g[2], jax.nn.log_sigmoid(g[3]))
    return q, k, v, o, fwd, bwd


def mlstm_output(h, o, g_head, w_o, dtype):
    B, NH, T, DV = h.shape
    hn = h * lax.rsqrt(jnp.mean(h * h, axis=-1, keepdims=True) + RMS_EPS)
    hn = hn.transpose(0, 2, 1, 3).reshape(B, T, NH * DV)
    y = jax.nn.sigmoid(o) * hn * g_head.astype(jnp.float32)
    return y.astype(dtype) @ w_o


def bidirectional_mlstm_with_context(hx, hc, w_in, b_in, g_head, w_o, need_ctx_out):
    B = hx.shape[0]
    qx, kx, vx, ox, fx, bx = mlstm_project(hx, w_in, b_in)
    qc, kc, vc, oc, fc, bc = mlstm_project(hc, w_in, b_in)
    zero = (jnp.zeros((B, MLSTM_HEADS, MLSTM_DQK, MLSTM_DV), jnp.float32),
            jnp.zeros((B, MLSTM_HEADS, MLSTM_DQK), jnp.float32),
            jnp.zeros((B, MLSTM_HEADS), jnp.float32))

    def flip(t):
        return jnp.flip(t, axis=2)

    hc_f, st_f = mlstm_chunk_scan(qc, kc, vc, fc[0], fc[1], zero)
    hx_f, _ = mlstm_chunk_scan(qx, kx, vx, fx[0], fx[1], st_f)
    hc_b, st_b = mlstm_chunk_scan(flip(qc), flip(kc), flip(vc), flip(bc[0]), flip(bc[1]), zero)
    hx_b, _ = mlstm_chunk_scan(flip(qx), flip(kx), flip(vx), flip(bx[0]), flip(bx[1]), st_b)

    out_x = mlstm_output(hx_f + flip(hx_b), ox, g_head, w_o, hx.dtype)
    if not need_ctx_out:
        return out_x, None
    out_c = mlstm_output(hc_f + flip(hc_b), oc, g_head, w_o, hc.dtype)
    return out_x, out_c


def conv_ffn(h, w_up, conv_w, conv_b, w_down):
    T = h.shape[1]
    u = h @ w_up
    pad = CONV_W // 2
    up = jnp.pad(u, ((0, 0), (pad, pad), (0, 0)))
    acc = conv_b
    for j in range(CONV_W):
        acc = acc + up[:, j:j + T] * conv_w[j]
    a, b = jnp.split(acc, 2, axis=-1)
    return (jax.nn.silu(a) * b) @ w_down


def setup_inputs(seed: int = 0) -> dict:
    key = jax.random.key(seed)
    ks = jax.random.split(key, 20)
    f32 = jnp.float32
    D = D_MODEL
    NH = MLSTM_HEADS

    def normal(k, shape, scale=1.0):
        return scale * jax.random.normal(k, shape, f32)

    gate_offset = jnp.concatenate([
        jnp.zeros((2 * MLSTM_QK_COLS + 2 * MLSTM_V_COLS,), f32),
        jnp.full((NH,), -2.0, f32), jnp.linspace(3.0, 6.0, NH, dtype=f32),
        jnp.full((NH,), -2.0, f32), jnp.linspace(3.0, 6.0, NH, dtype=f32)])
    return {
        'x': normal(ks[0], (BATCH, SEQ, D)),
        'c': normal(ks[1], (BATCH, D)),
        'ctx': normal(ks[2], (BATCH, CTX_LEN, D)),
        'c_ctx': normal(ks[3], (D,)),
        'w_mod': normal(ks[4], (DEPTH, D, 6 * D), 0.5 * D ** -0.5),
        'b_mod': normal(ks[5], (DEPTH, 6 * D), 0.02),
        'g_mix': 1.0 + normal(ks[6], (DEPTH, D), 0.02),
        'g_ffn': 1.0 + normal(ks[7], (DEPTH, D), 0.02),
        'attn_w_qkv': normal(ks[8], (N_ATTN_LAYERS, D, ATTN_QKV_COLS), D ** -0.5),
        'attn_sink': normal(ks[9], (N_ATTN_LAYERS, ATTN_Q_HEADS)),
        'attn_w_o': normal(ks[10], (N_ATTN_LAYERS, ATTN_Q_HEADS * ATTN_HEAD_DIM, D), (ATTN_Q_HEADS * ATTN_HEAD_DIM) ** -0.5),
        'mlstm_w_in': normal(ks[11], (N_MLSTM_LAYERS, D, MLSTM_IN_COLS), D ** -0.5),
        'mlstm_b_in': gate_offset + normal(ks[12], (N_MLSTM_LAYERS, MLSTM_IN_COLS), 0.05),
        'mlstm_g_head': 1.0 + normal(ks[13], (N_MLSTM_LAYERS, MLSTM_V_COLS), 0.02),
        'mlstm_w_o': normal(ks[14], (N_MLSTM_LAYERS, MLSTM_V_COLS, D), MLSTM_V_COLS ** -0.5),
        'ffn_w_up': normal(ks[15], (DEPTH, D, 2 * D_FF), D ** -0.5),
        'ffn_conv_w': normal(ks[16], (DEPTH, CONV_W, 2 * D_FF), CONV_W ** -0.5),
        'ffn_conv_b': normal(ks[17], (DEPTH, 2 * D_FF), 0.02),
        'ffn_w_down': normal(ks[18], (DEPTH, D_FF, D), D_FF ** -0.5),
        'g_final': 1.0 + normal(ks[19], (D,), 0.02),
    }


def reference(x, c, ctx, c_ctx, w_mod, b_mod, g_mix, g_ffn, attn_w_qkv, attn_sink, attn_w_o,
              mlstm_w_in, mlstm_b_in, mlstm_g_head, mlstm_w_o, ffn_w_up, ffn_conv_w, ffn_conv_b,
              ffn_w_down, g_final):
    L = x.shape[1]
    cos, sin = axial_rope_tables(L)
    sc = jax.nn.silu(c)
    scc = jax.nn.silu(c_ctx)
    for i in range(DEPTH):
        last = i == DEPTH - 1
        j = i // N_MIXERS
        mx = (sc @ w_mod[i] + b_mod[i])[:, None, :]
        mc = scc @ w_mod[i] + b_mod[i]
        sh1x, sc1x, gt1x, sh2x, sc2x, gt2x = jnp.split(mx, 6, axis=-1)
        sh1c, sc1c, gt1c, sh2c, sc2c, gt2c = jnp.split(mc, 6, axis=-1)

        hx = modulate(rmsnorm(x, g_mix[i]), sh1x, sc1x)
        hc = modulate(rmsnorm(ctx, g_mix[i]), sh1c, sc1c)
        if i % N_MIXERS == 0:
            dx, dc = windowed_gqa_with_context(hx, hc, attn_w_qkv[j], attn_sink[j], attn_w_o[j],
                                               cos, sin, not last)
        else:
            dx, dc = bidirectional_mlstm_with_context(hx, hc, mlstm_w_in[j], mlstm_b_in[j],
                                                      mlstm_g_head[j], mlstm_w_o[j], not last)
        x = x + gt1x * dx
        hx = modulate(rmsnorm(x, g_ffn[i]), sh2x, sc2x)
        x = x + gt2x * conv_ffn(hx, ffn_w_up[i], ffn_conv_w[i], ffn_conv_b[i], ffn_w_down[i])
        if not last:
            ctx = ctx + gt1c * dc
            hc = modulate(rmsnorm(ctx, g_ffn[i]), sh2c, sc2c)
            ctx = ctx + gt2c * conv_ffn(hc, ffn_w_up[i], ffn_conv_w[i], ffn_conv_b[i], ffn_w_down[i])
    return rmsnorm(x, g_final)
```

```python
import functools

import numpy as np
import jax
import jax.numpy as jnp
from jax import lax
from jax.experimental import pallas as pl
from jax.experimental.pallas import tpu as pltpu

F32 = jnp.float32
BF16 = jnp.bfloat16

RMS_EPS = 1e-6
HEAD_DIM = 128
WINDOW = 128
GRID_W = 64
ROPE_THETA = 10000.0
NEG_INF = -1e30
CHUNK = 64
CONV_W = 3
HALO = 8
V7X_VMEM_BYTES = 64 * 1024 * 1024
VMEM_LIMIT = V7X_VMEM_BYTES * 3 // 4


def _params(sem):
    return pltpu.CompilerParams(dimension_semantics=sem, vmem_limit_bytes=VMEM_LIMIT)


def _sigmoid(v):
    return 1.0 / (1.0 + jnp.exp(-v))


def _log_sigmoid(v):
    return jnp.minimum(v, 0.0) - jnp.log(1.0 + jnp.exp(-jnp.abs(v)))


def _norm_mod(xv, g, scale, shift):
    ms = jnp.mean(xv * xv, axis=-1, keepdims=True)
    y = xv * lax.rsqrt(ms + RMS_EPS)
    return (y * g) * (1.0 + scale) + shift


def _mod_kernel(c_ref, w_ref, b_ref, o_ref):
    cv = c_ref[...]
    s = (cv * _sigmoid(cv)).astype(BF16)
    o_ref[...] = jnp.dot(s, w_ref[...].astype(BF16), preferred_element_type=F32) + b_ref[...]


def _modulation(cvec, w_mod, b_mod):
    depth, d, n = w_mod.shape
    r = cvec.shape[0]
    tn = min(1024, n)
    return pl.pallas_call(
        _mod_kernel,
        grid=(depth, n // tn),
        in_specs=[pl.BlockSpec((r, d), lambda i, j: (0, 0)),
                  pl.BlockSpec((None, d, tn), lambda i, j: (i, 0, j)),
                  pl.BlockSpec((None, 1, tn), lambda i, j: (i, 0, j))],
        out_specs=pl.BlockSpec((None, r, tn), lambda i, j: (i, 0, j)),
        out_shape=jax.ShapeDtypeStruct((depth, r, n), F32),
        compiler_params=_params(("parallel", "parallel")),
        name="modulation",
    )(cvec, w_mod, b_mod.reshape(depth, 1, n))


def _qkv_kernel(x_ref, g_ref, sc_ref, sh_ref, w_ref, cos_ref, sin_ref, o_ref, h_scr,
                *, sub, n_q, n_rope, q_scale):
    j = pl.program_id(1)
    tm, tn = o_ref.shape

    @pl.when(j == 0)
    def _():
        for s in range(tm // sub):
            rows = pl.ds(s * sub, sub)
            h = _norm_mod(x_ref[rows, :], g_ref[...], sc_ref[s], sh_ref[s])
            h_scr[rows, :] = h.astype(BF16)

    acc = jnp.dot(h_scr[...], w_ref[...], preferred_element_type=F32)
    cos = cos_ref[...]
    sin = sin_ref[...]
    heads_per_tile = tn // HEAD_DIM
    for l in range(heads_per_tile):
        t = acc[:, l * HEAD_DIM:(l + 1) * HEAD_DIM]
        head = j * heads_per_tile + l
        tr = t * cos + pltpu.roll(t, HEAD_DIM // 2, 1) * sin
        t = jnp.where(head < n_rope, tr, t)
        t = t * jnp.where(head < n_q, q_scale, 1.0)
        o_ref[:, l * HEAD_DIM:(l + 1) * HEAD_DIM] = t.astype(BF16)


def _qkv_proj(u, g, sc, sh, w, cos_u, sin_u, *, tm, sub, n_q, n_kv):
    n_rows, d = u.shape
    n = w.shape[1]
    tn = min(1024, n)
    nsb = tm // sub
    kern = functools.partial(_qkv_kernel, sub=sub, n_q=n_q, n_rope=n_q + n_kv,
                             q_scale=HEAD_DIM ** -0.5)
    return pl.pallas_call(
        kern,
        grid=(n_rows // tm, n // tn),
        in_specs=[pl.BlockSpec((tm, d), lambda i, j: (i, 0)),
                  pl.BlockSpec((1, d), lambda i, j: (0, 0)),
                  pl.BlockSpec((nsb, 1, d), lambda i, j: (i, 0, 0)),
                  pl.BlockSpec((nsb, 1, d), lambda i, j: (i, 0, 0)),
                  pl.BlockSpec((d, tn), lambda i, j: (0, j)),
                  pl.BlockSpec((tm, HEAD_DIM), lambda i, j: (i, 0)),
                  pl.BlockSpec((tm, HEAD_DIM), lambda i, j: (i, 0))],
        out_specs=pl.BlockSpec((tm, tn), lambda i, j: (i, j)),
        out_shape=jax.ShapeDtypeStruct((n_rows, n), BF16),
        scratch_shapes=[pltpu.VMEM((tm, d), BF16)],
        compiler_params=_params(("parallel", "arbitrary")),
        name="attn_qkv",
    )(u, g, sc, sh, w, cos_u, sin_u)


def _softmax_pv(s, sink, v):
    m = jnp.maximum(jnp.max(s, axis=-1, keepdims=True), sink)
    p = jnp.exp(s - m)
    l = jnp.sum(p, axis=-1, keepdims=True) + jnp.exp(sink - m)
    o = jnp.dot(p.astype(BF16), v, preferred_element_type=F32)
    return o * (1.0 / l)


def _stack_heads(q, group):
    return jnp.concatenate([q[:, g * HEAD_DIM:(g + 1) * HEAD_DIM] for g in range(group)], axis=0)


def _attn_x_kernel(q_ref, kp_ref, ko_ref, kn_ref, kc_ref, vp_ref, vo_ref, vn_ref, vc_ref,
                   sink_ref, o_ref, *, group, n_blocks):
    n = pl.program_id(2)
    blk = q_ref.shape[0]
    qs = _stack_heads(q_ref[...], group)
    k = jnp.concatenate([kp_ref[...], ko_ref[...], kn_ref[...], kc_ref[...]], axis=0)
    v = jnp.concatenate([vp_ref[...], vo_ref[...], vn_ref[...], vc_ref[...]], axis=0)
    s = lax.dot_general(qs, k, (((1,), (1,)), ((), ())), preferred_element_type=F32)
    rows = lax.broadcasted_iota(jnp.int32, s.shape, 0)
    cols = lax.broadcasted_iota(jnp.int32, s.shape, 1)
    qi = rows & (blk - 1)
    rel = cols - qi
    valid = (rel >= 0) & (rel <= 2 * WINDOW)
    valid &= (cols >= blk) | (n > 0)
    valid &= (cols < 2 * blk) | (n < n_blocks - 1)
    valid |= cols >= 3 * blk
    s = jnp.where(valid, s, NEG_INF)
    o = _softmax_pv(s, sink_ref[...], v)
    for g in range(group):
        o_ref[:, g * HEAD_DIM:(g + 1) * HEAD_DIM] = o[g * blk:(g + 1) * blk, :].astype(BF16)


def _attn_c_kernel(q_ref, kc_ref, vc_ref, sink_ref, o_ref, *, group):
    lc = q_ref.shape[0]
    qs = _stack_heads(q_ref[...], group)
    s = lax.dot_general(qs, kc_ref[...], (((1,), (1,)), ((), ())), preferred_element_type=F32)
    o = _softmax_pv(s, sink_ref[...], vc_ref[...])
    for g in range(group):
        o_ref[:, g * HEAD_DIM:(g + 1) * HEAD_DIM] = o[g * lc:(g + 1) * lc, :].astype(BF16)


def _attention(qkv, sink, *, batch, seq, ctx, n_q, n_kv):
    n_rows = qkv.shape[0]
    group = n_q // n_kv
    blk = WINDOW
    p_rows = ctx + seq
    bpb = p_rows // blk
    cb = ctx // blk
    nb = seq // blk
    last_blk = n_rows // blk - 1
    gw = group * HEAD_DIM
    sink_g = sink.reshape(n_kv, group, 1).astype(F32)
    sink_x = jnp.broadcast_to(sink_g[:, :, None, :], (n_kv, group, blk, 1)).reshape(n_kv, group * blk, 1)
    sink_c = jnp.broadcast_to(sink_g[:, :, None, :], (n_kv, group, ctx, 1)).reshape(n_kv, group * ctx, 1)

    def kv_specs(col0):
        return [pl.BlockSpec((blk, HEAD_DIM), lambda b, h, n: (jnp.maximum(b * bpb + cb + n - 1, 0), col0 + h)),
                pl.BlockSpec((blk, HEAD_DIM), lambda b, h, n: (b * bpb + cb + n, col0 + h)),
                pl.BlockSpec((blk, HEAD_DIM), lambda b, h, n: (jnp.minimum(b * bpb + cb + n + 1, last_blk), col0 + h)),
                pl.BlockSpec((ctx, HEAD_DIM), lambda b, h, n: (b * (p_rows // ctx), col0 + h))]

    out_x = pl.pallas_call(
        functools.partial(_attn_x_kernel, group=group, n_blocks=nb),
        grid=(batch, n_kv, nb),
        in_specs=[pl.BlockSpec((blk, gw), lambda b, h, n: (b * bpb + cb + n, h))]
        + kv_specs(n_q) + kv_specs(n_q + n_kv)
        + [pl.BlockSpec((None, group * blk, 1), lambda b, h, n: (h, 0, 0))],
        out_specs=pl.BlockSpec((blk, gw), lambda b, h, n: (b * bpb + cb + n, h)),
        out_shape=jax.ShapeDtypeStruct((n_rows, n_q * HEAD_DIM), BF16),
        compiler_params=_params(("parallel", "parallel", "parallel")),
        name="attn_latent",
    )(qkv, qkv, qkv, qkv, qkv, qkv, qkv, qkv, qkv, sink_x)

    spc = p_rows // ctx
    out = pl.pallas_call(
        functools.partial(_attn_c_kernel_alias, group=group),
        grid=(batch, n_kv),
        in_specs=[pl.BlockSpec((ctx, gw), lambda b, h: (b * spc, h)),
                  pl.BlockSpec((ctx, HEAD_DIM), lambda b, h: (b * spc, n_q + h)),
                  pl.BlockSpec((ctx, HEAD_DIM), lambda b, h: (b * spc, n_q + n_kv + h)),
                  pl.BlockSpec((None, group * ctx, 1), lambda b, h: (h, 0, 0)),
                  pl.BlockSpec(memory_space=pl.ANY)],
        out_specs=pl.BlockSpec((ctx, gw), lambda b, h: (b * spc, h)),
        out_shape=jax.ShapeDtypeStruct((n_rows, n_q * HEAD_DIM), BF16),
        input_output_aliases={4: 0},
        compiler_params=_params(("parallel", "parallel")),
        name="attn_context",
    )(qkv, qkv, qkv, sink_c, out_x)
    return out


def _attn_c_kernel_alias(q_ref, kc_ref, vc_ref, sink_ref, prev_ref, o_ref, *, group):
    del prev_ref
    _attn_c_kernel(q_ref, kc_ref, vc_ref, sink_ref, o_ref, group=group)


def _proj_res_kernel(y_ref, w_ref, x_ref, gt_ref, o_ref, *, sub):
    acc = jnp.dot(y_ref[...], w_ref[...], preferred_element_type=F32)
    for s in range(o_ref.shape[0] // sub):
        rows = slice(s * sub, (s + 1) * sub)
        o_ref[rows, :] = x_ref[rows, :] + gt_ref[s] * acc[rows, :]


def _proj_residual(y, w, u, gt, *, tm, sub):
    n_rows, d = u.shape
    k = y.shape[1]
    nsb = tm // sub
    return pl.pallas_call(
        functools.partial(_proj_res_kernel, sub=sub),
        grid=(n_rows // tm,),
        in_specs=[pl.BlockSpec((tm, k), lambda i: (i, 0)),
                  pl.BlockSpec((k, d), lambda i: (0, 0)),
                  pl.BlockSpec((tm, d), lambda i: (i, 0)),
                  pl.BlockSpec((nsb, 1, d), lambda i: (i, 0, 0))],
        out_specs=pl.BlockSpec((tm, d), lambda i: (i, 0)),
        out_shape=jax.ShapeDtypeStruct((n_rows, d), F32),
        compiler_params=_params(("parallel",)),
        name="proj_residual",
    )(y, w, u, gt)


def _mlstm_in_kernel(x_ref, g_ref, sc_ref, sh_ref, w_ref, b_ref, wg_ref, bg_ref, z_ref, gate_ref,
                     h_scr, *, sub, k_lo, k_hi, k_scale):
    j = pl.program_id(1)
    tm, tn = z_ref.shape

    @pl.when(j == 0)
    def _():
        for s in range(tm // sub):
            rows = pl.ds(s * sub, sub)
            h = _norm_mod(x_ref[rows, :], g_ref[...], sc_ref[s], sh_ref[s])
            h_scr[rows, :] = h.astype(BF16)
        gate_ref[...] = jnp.dot(h_scr[...], wg_ref[...], preferred_element_type=F32) + bg_ref[...]

    acc = jnp.dot(h_scr[...], w_ref[...], preferred_element_type=F32) + b_ref[...]
    col0 = j * tn
    is_k = (col0 >= k_lo) & (col0 < k_hi)
    z_ref[...] = (acc * jnp.where(is_k, k_scale, 1.0)).astype(BF16)


def _mlstm_in_proj(u, g, sc, sh, w, b, wg, bg, *, tm, sub, dqk):
    n_rows, d = u.shape
    n = w.shape[1]
    qk_cols = d // 2
    tn = min(1024, qk_cols)
    nsb = tm // sub
    kern = functools.partial(_mlstm_in_kernel, sub=sub, k_lo=qk_cols, k_hi=2 * qk_cols,
                             k_scale=dqk ** -0.5)
    return pl.pallas_call(
        kern,
        grid=(n_rows // tm, n // tn),
        in_specs=[pl.BlockSpec((tm, d), lambda i, j: (i, 0)),
                  pl.BlockSpec((1, d), lambda i, j: (0, 0)),
                  pl.BlockSpec((nsb, 1, d), lambda i, j: (i, 0, 0)),
                  pl.BlockSpec((nsb, 1, d), lambda i, j: (i, 0, 0)),
                  pl.BlockSpec((d, tn), lambda i, j: (0, j)),
                  pl.BlockSpec((1, tn), lambda i, j: (0, j)),
                  pl.BlockSpec((d, 128), lambda i, j: (0, 0)),
                  pl.BlockSpec((1, 128), lambda i, j: (0, 0))],
        out_specs=[pl.BlockSpec((tm, tn), lambda i, j: (i, j)),
                   pl.BlockSpec((tm, 128), lambda i, j: (i, 0))],
        out_shape=[jax.ShapeDtypeStruct((n_rows, n), BF16),
                   jax.ShapeDtypeStruct((n_rows, 128), F32)],
        scratch_shapes=[pltpu.VMEM((tm, d), BF16)],
        compiler_params=_params(("parallel", "arbitrary")),
        name="mlstm_in",
    )(u, g, sc, sh, w, b, wg, bg)


def _mlstm_chunk(q, k, v, i_row, f_row, i_col, f_col, c_scr, n, m, reverse):
    ch = q.shape[0]
    t_idx = lax.broadcasted_iota(jnp.int32, (ch, ch), 0)
    s_idx = lax.broadcasted_iota(jnp.int32, (ch, ch), 1)
    lf_row = _log_sigmoid(f_row)
    lf_col = _log_sigmoid(f_col)
    if reverse:
        upto_t = s_idx >= t_idx
        upto_s = t_idx >= s_idx
    else:
        upto_t = s_idx <= t_idx
        upto_s = t_idx <= s_idx
    b_col = jnp.sum(jnp.where(upto_t, lf_row, 0.0), axis=1, keepdims=True)
    b_row = jnp.sum(jnp.where(upto_s, lf_col, 0.0), axis=0, keepdims=True)
    b_end = jnp.sum(lf_row, axis=1, keepdims=True)

    d = jnp.where(upto_t, b_col - b_row + i_row, -jnp.inf)
    inter = b_col + m
    m_t = jnp.maximum(inter, jnp.max(d, axis=1, keepdims=True))
    w = jnp.exp(d - m_t)
    e_inter = jnp.exp(inter - m_t)
    s = lax.dot_general(q, k, (((1,), (1,)), ((), ())), preferred_element_type=F32) * w
    c_old = c_scr[...]
    num = (e_inter * jnp.dot(q, c_old.astype(BF16), preferred_element_type=F32)
           + jnp.dot(s.astype(BF16), v, preferred_element_type=F32))
    qn = jnp.sum(q.astype(F32) * n, axis=1, keepdims=True)
    den = e_inter * qn + jnp.sum(s, axis=1, keepdims=True)
    h = num * (1.0 / jnp.maximum(jnp.abs(den), jnp.exp(-m_t)))

    g_row = b_end - b_row + i_row
    m_new = jnp.maximum(b_end + m, jnp.max(g_row, axis=1, keepdims=True))
    e_prev = jnp.exp(b_end + m - m_new)
    wg_col = jnp.exp(b_end - b_col + i_col - m_new)
    vw = (wg_col * v.astype(F32)).astype(BF16)
    c_scr[...] = e_prev * c_old + lax.dot_general(k, vw, (((0,), (0,)), ((), ())),
                                                  preferred_element_type=F32)
    n_new = e_prev * n + jnp.sum(wg_col * k.astype(F32), axis=0, keepdims=True)
    return h, n_new, m_new


def _mlstm_core_kernel(q_ref, k_ref, v_ref, o_ref, gr_ref, gc_ref, gh_ref, y_ref,
                       hf_scr, hb_scr, cf_scr, cb_scr, *, n_ctx_chunks, out_rows):
    n_chunks = gr_ref.shape[0]
    dqk = q_ref.shape[1]
    cf_scr[...] = jnp.zeros_like(cf_scr)
    cb_scr[...] = jnp.zeros_like(cb_scr)

    def run(c, c_scr, h_scr, n, m, gate0, reverse):
        rows = pl.ds(pl.multiple_of(c * CHUNK, CHUNK), CHUNK)
        gr = gr_ref[c]
        gc = gc_ref[c]
        h, n, m = _mlstm_chunk(q_ref[rows, :], k_ref[rows, :], v_ref[rows, :],
                               gr[gate0:gate0 + 1, :], gr[gate0 + 1:gate0 + 2, :],
                               gc[:, gate0:gate0 + 1], gc[:, gate0 + 1:gate0 + 2],
                               c_scr, n, m, reverse)
        h_scr[rows, :] = h
        return n, m

    def body(j, carry):
        nf, mf, nb, mb = carry
        nf, mf = run(j, cf_scr, hf_scr, nf, mf, 0, False)
        cb = jnp.where(j < n_ctx_chunks, n_ctx_chunks - 1 - j, n_chunks - 1 + n_ctx_chunks - j)
        nb, mb = run(cb, cb_scr, hb_scr, nb, mb, 2, True)
        return nf, mf, nb, mb

    zn = jnp.zeros((1, dqk), F32)
    zm = jnp.zeros((1, 1), F32)
    lax.fori_loop(0, n_chunks, body, (zn, zm, zn, zm))

    gh = gh_ref[...]

    def finish(r, _):
        rows = pl.ds(pl.multiple_of(r * out_rows, out_rows), out_rows)
        h = hf_scr[rows, :] + hb_scr[rows, :]
        hn = h * lax.rsqrt(jnp.mean(h * h, axis=-1, keepdims=True) + RMS_EPS)
        y_ref[rows, :] = (_sigmoid(o_ref[rows, :].astype(F32)) * hn * gh).astype(BF16)
        return 0

    lax.fori_loop(0, y_ref.shape[0] // out_rows, finish, 0)


def _mlstm_core(z, gates_r, gates_c, g_head, *, batch, p_rows, ctx, n_heads, dqk, dv):
    n_rows = z.shape[0]
    n_chunks = p_rows // CHUNK
    qk_blocks = n_heads
    kern = functools.partial(_mlstm_core_kernel, n_ctx_chunks=ctx // CHUNK, out_rows=ctx)
    return pl.pallas_call(
        kern,
        grid=(batch, n_heads),
        in_specs=[pl.BlockSpec((p_rows, dqk), lambda b, h: (b, h)),
                  pl.BlockSpec((p_rows, dqk), lambda b, h: (b, qk_blocks + h)),
                  pl.BlockSpec((p_rows, dv), lambda b, h: (b, n_heads + h)),
                  pl.BlockSpec((p_rows, dv), lambda b, h: (b, 2 * n_heads + h)),
                  pl.BlockSpec((None, None, n_chunks, 4, CHUNK), lambda b, h: (b, h, 0, 0, 0)),
                  pl.BlockSpec((None, None, n_chunks, CHUNK, 4), lambda b, h: (b, h, 0, 0, 0)),
                  pl.BlockSpec((1, dv), lambda b, h: (0, h))],
        out_specs=pl.BlockSpec((p_rows, dv), lambda b, h: (b, h)),
        out_shape=jax.ShapeDtypeStruct((n_rows, n_heads * dv), BF16),
        scratch_shapes=[pltpu.VMEM((p_rows, dv), F32), pltpu.VMEM((p_rows, dv), F32),
                        pltpu.VMEM((dqk, dv), F32), pltpu.VMEM((dqk, dv), F32)],
        compiler_params=_params(("parallel", "parallel")),
        name="mlstm_core",
    )(z, z, z, z, gates_r, gates_c, g_head)


def _ffn_kernel(x_ref, xp_ref, xn_ref, g_ref, sc_ref, sh_ref, gt_ref,
                wa_ref, wb_ref, cwa_ref, cwb_ref, cba_ref, cbb_ref, wd_ref,
                o_ref, h_scr, acc_scr, *, sub, p_rows, ctx):
    i = pl.program_id(0)
    j = pl.program_id(1)
    tm = o_ref.shape[0]
    nsb = tm // sub

    @pl.when(j == 0)
    def _():
        g = g_ref[...]
        hp = _norm_mod(xp_ref[...], g, sc_ref[0], sh_ref[0])
        h_scr[0:HALO, :] = hp.astype(BF16)
        for s in range(nsb):
            h = _norm_mod(x_ref[s * sub:(s + 1) * sub, :], g, sc_ref[s], sh_ref[s])
            h_scr[HALO + s * sub:HALO + (s + 1) * sub, :] = h.astype(BF16)
        hn = _norm_mod(xn_ref[...], g, sc_ref[nsb - 1], sh_ref[nsb - 1])
        h_scr[HALO + tm:HALO + tm + HALO, :] = hn.astype(BF16)
        acc_scr[...] = jnp.zeros_like(acc_scr)

    pos = (i * tm + lax.broadcasted_iota(jnp.int32, (tm, 1), 0)) % p_rows
    row = lax.broadcasted_iota(jnp.int32, (tm, 1), 0)
    is_first = (pos == 0) | (pos == ctx)
    is_last = (pos == ctx - 1) | (pos == p_rows - 1)
    h_ext = h_scr[...]

    def conv_branch(w_ref, cw_ref, cb_ref):
        ue = jnp.dot(h_ext, w_ref[...], preferred_element_type=F32)
        uc = ue[HALO:HALO + tm, :]
        prev = jnp.where(row == 0, ue[HALO - 1:HALO, :], pltpu.roll(uc, 1, 0))
        nxt = jnp.where(row == tm - 1, ue[HALO + tm:HALO + tm + 1, :], pltpu.roll(uc, tm - 1, 0))
        prev = jnp.where(is_first, 0.0, prev)
        nxt = jnp.where(is_last, 0.0, nxt)
        cw = cw_ref[...]
        return cb_ref[...] + prev * cw[0:1, :] + uc * cw[1:2, :] + nxt * cw[2:3, :]

    a = conv_branch(wa_ref, cwa_ref, cba_ref)
    b = conv_branch(wb_ref, cwb_ref, cbb_ref)
    act = (a * _sigmoid(a) * b).astype(BF16)
    acc_scr[...] += jnp.dot(act, wd_ref[...], preferred_element_type=F32)

    @pl.when(j == pl.num_programs(1) - 1)
    def _():
        for s in range(nsb):
            rows = slice(s * sub, (s + 1) * sub)
            o_ref[rows, :] = x_ref[rows, :] + gt_ref[s] * acc_scr[rows, :]


def _conv_ffn(u, g, sc, sh, gt, w_up, conv_w, conv_b, w_down, *, tm, sub, p_rows, ctx):
    n_rows, d = u.shape
    d_ff = w_down.shape[0]
    fc = 512 if d_ff % 512 == 0 else d_ff
    nfc = d_ff // fc
    nsb = tm // sub
    n_halo_blocks = n_rows // HALO
    kern = functools.partial(_ffn_kernel, sub=sub, p_rows=p_rows, ctx=ctx)
    return pl.pallas_call(
        kern,
        grid=(n_rows // tm, nfc),
        in_specs=[pl.BlockSpec((tm, d), lambda i, j: (i, 0)),
                  pl.BlockSpec((HALO, d), lambda i, j: (jnp.maximum(i * (tm // HALO) - 1, 0), 0)),
                  pl.BlockSpec((HALO, d), lambda i, j: (jnp.minimum((i + 1) * (tm // HALO), n_halo_blocks - 1), 0)),
                  pl.BlockSpec((1, d), lambda i, j: (0, 0)),
                  pl.BlockSpec((nsb, 1, d), lambda i, j: (i, 0, 0)),
                  pl.BlockSpec((nsb, 1, d), lambda i, j: (i, 0, 0)),
                  pl.BlockSpec((nsb, 1, d), lambda i, j: (i, 0, 0)),
                  pl.BlockSpec((d, fc), lambda i, j: (0, j)),
                  pl.BlockSpec((d, fc), lambda i, j: (0, nfc + j)),
                  pl.BlockSpec((CONV_W, fc), lambda i, j: (0, j)),
                  pl.BlockSpec((CONV_W, fc), lambda i, j: (0, nfc + j)),
                  pl.BlockSpec((1, fc), lambda i, j: (0, j)),
                  pl.BlockSpec((1, fc), lambda i, j: (0, nfc + j)),
                  pl.BlockSpec((fc, d), lambda i, j: (j, 0))],
        out_specs=pl.BlockSpec((tm, d), lambda i, j: (i, 0)),
        out_shape=jax.ShapeDtypeStruct((n_rows, d), F32),
        scratch_shapes=[pltpu.VMEM((tm + 2 * HALO, d), BF16), pltpu.VMEM((tm, d), F32)],
        compiler_params=_params(("parallel", "arbitrary")),
        name="conv_ffn",
    )(u, u, u, g, sc, sh, gt, w_up, w_up, conv_w, conv_w, conv_b, conv_b, w_down)


def _final_norm_kernel(x_ref, g_ref, o_ref):
    xv = x_ref[...]
    o_ref[...] = xv * lax.rsqrt(jnp.mean(xv * xv, axis=-1, keepdims=True) + RMS_EPS) * g_ref[...]


def _final_norm(u, g, *, batch, seq, ctx, sub):
    d = u.shape[1]
    spb = (ctx + seq) // sub
    cs = ctx // sub
    return pl.pallas_call(
        _final_norm_kernel,
        grid=(batch, seq // sub),
        in_specs=[pl.BlockSpec((sub, d), lambda b, t: (b * spb + cs + t, 0)),
                  pl.BlockSpec((1, d), lambda b, t: (0, 0))],
        out_specs=pl.BlockSpec((None, sub, d), lambda b, t: (b, t, 0)),
        out_shape=jax.ShapeDtypeStruct((batch, seq, d), F32),
        compiler_params=_params(("parallel", "parallel")),
        name="final_norm",
    )(u, g)


def _rope_tables(batch, seq, ctx):
    rows = seq // GRID_W
    row = jnp.repeat(jnp.arange(rows, dtype=F32), GRID_W)
    col = jnp.tile(jnp.arange(GRID_W, dtype=F32), rows)
    n_freq = HEAD_DIM // 4
    inv_freq = ROPE_THETA ** (-jnp.arange(n_freq, dtype=F32) / n_freq)
    ang = jnp.concatenate([row[:, None] * inv_freq, col[:, None] * inv_freq], axis=-1)
    ang = jnp.concatenate([ang, ang], axis=-1)
    sign = jnp.concatenate([-jnp.ones((HEAD_DIM // 2,), F32), jnp.ones((HEAD_DIM // 2,), F32)])
    cos = jnp.concatenate([jnp.ones((ctx, HEAD_DIM), F32), jnp.cos(ang)], axis=0)
    sin = jnp.concatenate([jnp.zeros((ctx, HEAD_DIM), F32), jnp.sin(ang) * sign], axis=0)
    return jnp.tile(cos, (batch, 1)), jnp.tile(sin, (batch, 1))


def kernel(x, c, ctx, c_ctx, w_mod, b_mod, g_mix, g_ffn, attn_w_qkv, attn_sink, attn_w_o,
           mlstm_w_in, mlstm_b_in, mlstm_g_head, mlstm_w_o, ffn_w_up, ffn_conv_w, ffn_conv_b,
           ffn_w_down, g_final):
    batch, seq, d = x.shape
    lc = ctx.shape[1]
    depth = w_mod.shape[0]
    p_rows = lc + seq
    sub = lc
    assert seq % sub == 0 and sub % WINDOW == 0 and sub % CHUNK == 0 and seq % GRID_W == 0
    n_sub = batch * p_rows // sub
    tm = 2 * sub if n_sub % 2 == 0 else sub
    n_q = attn_sink.shape[1]
    n_kv = (attn_w_qkv.shape[2] // HEAD_DIM - n_q) // 2
    n_heads = (mlstm_w_in.shape[2] - 3 * d) // 4
    dv = d // n_heads
    dqk = dv // 2
    main_cols = 3 * d

    u = jnp.concatenate([ctx, x], axis=1).reshape(batch * p_rows, d)

    n_c = batch + 1
    c_rows = -(-n_c // 8) * 8
    cvec = jnp.concatenate([c, c_ctx[None, :], jnp.zeros((c_rows - n_c, d), F32)], axis=0)
    mods = _modulation(cvec, w_mod, b_mod)
    mods_sb = jnp.concatenate(
        [jnp.broadcast_to(mods[:, None, batch:batch + 1], (depth, batch, lc // sub, 6 * d)),
         jnp.broadcast_to(mods[:, :batch, None], (depth, batch, seq // sub, 6 * d))],
        axis=2).reshape(depth, n_sub, 6, 1, d)

    cos_u, sin_u = _rope_tables(batch, seq, lc)

    for i in range(depth):
        jm = i // 2
        sh1, sc1, gt1, sh2, sc2, gt2 = (mods_sb[i, :, k] for k in range(6))
        gm = g_mix[i].reshape(1, d)
        if i % 2 == 0:
            qkv = _qkv_proj(u, gm, sc1, sh1, attn_w_qkv[jm].astype(BF16), cos_u, sin_u,
                            tm=tm, sub=sub, n_q=n_q, n_kv=n_kv)
            y = _attention(qkv, attn_sink[jm], batch=batch, seq=seq, ctx=lc, n_q=n_q, n_kv=n_kv)
            w_o = attn_w_o[jm]
        else:
            w_in = mlstm_w_in[jm]
            b_in = mlstm_b_in[jm]
            n_gate = 4 * n_heads
            wg = jnp.pad(w_in[:, main_cols:], ((0, 0), (0, 128 - n_gate))).astype(BF16)
            bg = jnp.pad(b_in[main_cols:], (0, 128 - n_gate)).reshape(1, 128)
            z, gates = _mlstm_in_proj(u, gm, sc1, sh1, w_in[:, :main_cols].astype(BF16),
                                      b_in[:main_cols].reshape(1, main_cols), wg, bg,
                                      tm=tm, sub=sub, dqk=dqk)
            g5 = gates[:, :n_gate].reshape(batch, p_rows // CHUNK, CHUNK, 4, n_heads)
            gates_r = g5.transpose(0, 4, 1, 3, 2)
            gates_c = g5.transpose(0, 4, 1, 2, 3)
            y = _mlstm_core(z, gates_r, gates_c, mlstm_g_head[jm].reshape(1, n_heads * dv),
                            batch=batch, p_rows=p_rows, ctx=lc, n_heads=n_heads, dqk=dqk, dv=dv)
            w_o = mlstm_w_o[jm]
        u = _proj_residual(y, w_o.astype(BF16), u, gt1, tm=tm, sub=sub)
        u = _conv_ffn(u, g_ffn[i].reshape(1, d), sc2, sh2, gt2, ffn_w_up[i].astype(BF16),
                      ffn_conv_w[i], ffn_conv_b[i].reshape(1, -1), ffn_w_down[i].astype(BF16),
                      tm=tm, sub=sub, p_rows=p_rows, ctx=lc)
    return _final_norm(u, g_final.reshape(1, d), batch=batch, seq=seq, ctx=lc, sub=sub)
```

```python
import functools

import jax
import jax.numpy as jnp
from jax import lax
from jax.experimental import pallas as pl
from jax.experimental.pallas import tpu as pltpu

F32 = jnp.float32
BF16 = jnp.bfloat16

RMS_EPS = 1e-6
HEAD_DIM = 128
WINDOW = 128
GRID_W = 64
ROPE_THETA = 10000.0
NEG_INF = -1e30
CHUNK = 64
CONV_W = 3
LANE = 128
HALO = 8
BF16_ROWS = 16
FFN_CHUNK = 512
V7X_VMEM_BYTES = 64 * 1024 * 1024
VMEM_LIMIT = V7X_VMEM_BYTES * 3 // 4


def _params(sem):
    return pltpu.CompilerParams(dimension_semantics=sem, vmem_limit_bytes=VMEM_LIMIT)


def _sigmoid(v):
    return 1.0 / (1.0 + jnp.exp(-v))


def _log_sigmoid(v):
    return jnp.minimum(v, 0.0) - jnp.log(1.0 + jnp.exp(-jnp.abs(v)))


def _norm_mod(xv, g, scale, shift):
    ms = jnp.mean(xv * xv, axis=-1, keepdims=True)
    y = xv * lax.rsqrt(ms + RMS_EPS)
    return (y * g) * (1.0 + scale) + shift


def _mod_kernel(c_ref, w_ref, b_ref, o_ref):
    cv = c_ref[...]
    s = (cv * _sigmoid(cv)).astype(BF16)
    o_ref[...] = jnp.dot(s, w_ref[...].astype(BF16), preferred_element_type=F32) + b_ref[...]


def _modulation(cvec, w_mod, b_mod):
    depth, d, n = w_mod.shape
    r = cvec.shape[0]
    tn = min(1024, n)
    return pl.pallas_call(
        _mod_kernel,
        grid=(depth, n // tn),
        in_specs=[pl.BlockSpec((r, d), lambda i, j: (0, 0)),
                  pl.BlockSpec((None, d, tn), lambda i, j: (i, 0, j)),
                  pl.BlockSpec((None, 1, tn), lambda i, j: (i, 0, j))],
        out_specs=pl.BlockSpec((None, r, tn), lambda i, j: (i, 0, j)),
        out_shape=jax.ShapeDtypeStruct((depth, r, n), F32),
        compiler_params=_params(("parallel", "parallel")),
        name="modulation",
    )(cvec, w_mod, b_mod.reshape(depth, 1, n))


def _qkv_kernel(x_ref, g_ref, sc_ref, sh_ref, w_ref, cos_ref, sin_ref, o_ref, h_scr,
                *, sub, n_q, n_rope, q_scale):
    j = pl.program_id(1)
    tm, tn = o_ref.shape

    @pl.when(j == 0)
    def _():
        for s in range(tm // sub):
            rows = pl.ds(s * sub, sub)
            h = _norm_mod(x_ref[rows, :], g_ref[...], sc_ref[s], sh_ref[s])
            h_scr[rows, :] = h.astype(BF16)

    acc = jnp.dot(h_scr[...], w_ref[...], preferred_element_type=F32)
    cos = cos_ref[...]
    sin = sin_ref[...]
    heads_per_tile = tn // HEAD_DIM
    for l in range(heads_per_tile):
        t = acc[:, l * HEAD_DIM:(l + 1) * HEAD_DIM]
        head = j * heads_per_tile + l
        tr = t * cos + pltpu.roll(t, HEAD_DIM // 2, 1) * sin
        t = jnp.where(head < n_rope, tr, t)
        t = t * jnp.where(head < n_q, q_scale, 1.0)
        o_ref[:, l * HEAD_DIM:(l + 1) * HEAD_DIM] = t.astype(BF16)


def _qkv_proj(u, g, sc, sh, w, cos_u, sin_u, *, tm, sub, n_q, n_kv):
    n_rows, d = u.shape
    n = w.shape[1]
    tn = min(1024, n)
    nsb = tm // sub
    kern = functools.partial(_qkv_kernel, sub=sub, n_q=n_q, n_rope=n_q + n_kv,
                             q_scale=HEAD_DIM ** -0.5)
    return pl.pallas_call(
        kern,
        grid=(n_rows // tm, n // tn),
        in_specs=[pl.BlockSpec((tm, d), lambda i, j: (i, 0)),
                  pl.BlockSpec((1, d), lambda i, j: (0, 0)),
                  pl.BlockSpec((nsb, 1, d), lambda i, j: (i, 0, 0)),
                  pl.BlockSpec((nsb, 1, d), lambda i, j: (i, 0, 0)),
                  pl.BlockSpec((d, tn), lambda i, j: (0, j)),
                  pl.BlockSpec((tm, HEAD_DIM), lambda i, j: (i, 0)),
                  pl.BlockSpec((tm, HEAD_DIM), lambda i, j: (i, 0))],
        out_specs=pl.BlockSpec((tm, tn), lambda i, j: (i, j)),
        out_shape=jax.ShapeDtypeStruct((n_rows, n), BF16),
        scratch_shapes=[pltpu.VMEM((tm, d), BF16)],
        compiler_params=_params(("parallel", "arbitrary")),
        name="attn_qkv",
    )(u, g, sc, sh, w, cos_u, sin_u)


def _softmax_pv(s, sink, v):
    m = jnp.maximum(jnp.max(s, axis=-1, keepdims=True), sink)
    p = jnp.exp(s - m)
    l = jnp.sum(p, axis=-1, keepdims=True) + jnp.exp(sink - m)
    o = jnp.dot(p.astype(BF16), v, preferred_element_type=F32)
    return o * (1.0 / l)


def _stack_heads(q, group):
    return jnp.concatenate([q[:, g * HEAD_DIM:(g + 1) * HEAD_DIM] for g in range(group)], axis=0)


def _attn_x_kernel(q_ref, kp_ref, ko_ref, kn_ref, kc_ref, vp_ref, vo_ref, vn_ref, vc_ref,
                   sink_ref, o_ref, *, group, n_blocks):
    n = pl.program_id(2)
    blk = q_ref.shape[0]
    qs = _stack_heads(q_ref[...], group)
    k = jnp.concatenate([kp_ref[...], ko_ref[...], kn_ref[...], kc_ref[...]], axis=0)
    v = jnp.concatenate([vp_ref[...], vo_ref[...], vn_ref[...], vc_ref[...]], axis=0)
    s = lax.dot_general(qs, k, (((1,), (1,)), ((), ())), preferred_element_type=F32)
    rows = lax.broadcasted_iota(jnp.int32, s.shape, 0)
    cols = lax.broadcasted_iota(jnp.int32, s.shape, 1)
    qi = rows & (blk - 1)
    rel = cols - qi
    valid = (rel >= 0) & (rel <= 2 * WINDOW)
    valid &= (cols >= blk) | (n > 0)
    valid &= (cols < 2 * blk) | (n < n_blocks - 1)
    valid |= cols >= 3 * blk
    s = jnp.where(valid, s, NEG_INF)
    o = _softmax_pv(s, sink_ref[...], v)
    for g in range(group):
        o_ref[:, g * HEAD_DIM:(g + 1) * HEAD_DIM] = o[g * blk:(g + 1) * blk, :].astype(BF16)


def _attn_c_kernel(q_ref, kc_ref, vc_ref, sink_ref, o_ref, *, group):
    lc = q_ref.shape[0]
    qs = _stack_heads(q_ref[...], group)
    s = lax.dot_general(qs, kc_ref[...], (((1,), (1,)), ((), ())), preferred_element_type=F32)
    o = _softmax_pv(s, sink_ref[...], vc_ref[...])
    for g in range(group):
        o_ref[:, g * HEAD_DIM:(g + 1) * HEAD_DIM] = o[g * lc:(g + 1) * lc, :].astype(BF16)


def _attention(qkv, sink, *, batch, seq, ctx, n_q, n_kv):
    n_rows = qkv.shape[0]
    group = n_q // n_kv
    blk = WINDOW
    p_rows = ctx + seq
    bpb = p_rows // blk
    cb = ctx // blk
    nb = seq // blk
    last_blk = n_rows // blk - 1
    gw = group * HEAD_DIM
    sink_g = sink.reshape(n_kv, group, 1).astype(F32)
    sink_x = jnp.broadcast_to(sink_g[:, :, None, :], (n_kv, group, blk, 1)).reshape(n_kv, group * blk, 1)
    sink_c = jnp.broadcast_to(sink_g[:, :, None, :], (n_kv, group, ctx, 1)).reshape(n_kv, group * ctx, 1)

    def kv_specs(col0):
        return [pl.BlockSpec((blk, HEAD_DIM), lambda b, h, n: (jnp.maximum(b * bpb + cb + n - 1, 0), col0 + h)),
                pl.BlockSpec((blk, HEAD_DIM), lambda b, h, n: (b * bpb + cb + n, col0 + h)),
                pl.BlockSpec((blk, HEAD_DIM), lambda b, h, n: (jnp.minimum(b * bpb + cb + n + 1, last_blk), col0 + h)),
                pl.BlockSpec((ctx, HEAD_DIM), lambda b, h, n: (b * (p_rows // ctx), col0 + h))]

    out_x = pl.pallas_call(
        functools.partial(_attn_x_kernel, group=group, n_blocks=nb),
        grid=(batch, n_kv, nb),
        in_specs=[pl.BlockSpec((blk, gw), lambda b, h, n: (b * bpb + cb + n, h))]
        + kv_specs(n_q) + kv_specs(n_q + n_kv)
        + [pl.BlockSpec((None, group * blk, 1), lambda b, h, n: (h, 0, 0))],
        out_specs=pl.BlockSpec((blk, gw), lambda b, h, n: (b * bpb + cb + n, h)),
        out_shape=jax.ShapeDtypeStruct((n_rows, n_q * HEAD_DIM), BF16),
        compiler_params=_params(("parallel", "parallel", "parallel")),
        name="attn_latent",
    )(qkv, qkv, qkv, qkv, qkv, qkv, qkv, qkv, qkv, sink_x)

    spc = p_rows // ctx
    out = pl.pallas_call(
        functools.partial(_attn_c_kernel_alias, group=group),
        grid=(batch, n_kv),
        in_specs=[pl.BlockSpec((ctx, gw), lambda b, h: (b * spc, h)),
                  pl.BlockSpec((ctx, HEAD_DIM), lambda b, h: (b * spc, n_q + h)),
                  pl.BlockSpec((ctx, HEAD_DIM), lambda b, h: (b * spc, n_q + n_kv + h)),
                  pl.BlockSpec((None, group * ctx, 1), lambda b, h: (h, 0, 0)),
                  pl.BlockSpec(memory_space=pl.ANY)],
        out_specs=pl.BlockSpec((ctx, gw), lambda b, h: (b * spc, h)),
        out_shape=jax.ShapeDtypeStruct((n_rows, n_q * HEAD_DIM), BF16),
        input_output_aliases={4: 0},
        compiler_params=_params(("parallel", "parallel")),
        name="attn_context",
    )(qkv, qkv, qkv, sink_c, out_x)
    return out


def _attn_c_kernel_alias(q_ref, kc_ref, vc_ref, sink_ref, prev_ref, o_ref, *, group):
    del prev_ref
    _attn_c_kernel(q_ref, kc_ref, vc_ref, sink_ref, o_ref, group=group)


def _proj_res_kernel(y_ref, w_ref, x_ref, gt_ref, g_ref, sc_ref, sh_ref, o_ref, h_ref, *, sub):
    acc = jnp.dot(y_ref[...], w_ref[...], preferred_element_type=F32)
    g = g_ref[...]
    for s in range(o_ref.shape[0] // sub):
        rows = slice(s * sub, (s + 1) * sub)
        xn = x_ref[rows, :] + gt_ref[s] * acc[rows, :]
        o_ref[rows, :] = xn
        h_ref[rows, :] = _norm_mod(xn, g, sc_ref[s], sh_ref[s]).astype(BF16)


def _proj_residual(y, w, u, gt, g, sc, sh, *, tm, sub):
    n_rows, d = u.shape
    k = y.shape[1]
    nsb = tm // sub
    sb_spec = pl.BlockSpec((nsb, 1, d), lambda i: (i, 0, 0))
    return pl.pallas_call(
        functools.partial(_proj_res_kernel, sub=sub),
        grid=(n_rows // tm,),
        in_specs=[pl.BlockSpec((tm, k), lambda i: (i, 0)),
                  pl.BlockSpec((k, d), lambda i: (0, 0)),
                  pl.BlockSpec((tm, d), lambda i: (i, 0)),
                  sb_spec,
                  pl.BlockSpec((1, d), lambda i: (0, 0)),
                  sb_spec, sb_spec],
        out_specs=[pl.BlockSpec((tm, d), lambda i: (i, 0)),
                   pl.BlockSpec((tm, d), lambda i: (i, 0))],
        out_shape=[jax.ShapeDtypeStruct((n_rows, d), F32),
                   jax.ShapeDtypeStruct((n_rows, d), BF16)],
        compiler_params=_params(("parallel",)),
        name="proj_residual",
    )(y, w, u, gt, g, sc, sh)


def _mlstm_in_kernel(x_ref, g_ref, sc_ref, sh_ref, w_ref, b_ref, wg_ref, bg_ref, z_ref, gate_ref,
                     h_scr, *, sub, k_lo, k_hi, k_scale):
    j = pl.program_id(1)
    tm, tn = z_ref.shape

    @pl.when(j == 0)
    def _():
        for s in range(tm // sub):
            rows = pl.ds(s * sub, sub)
            h = _norm_mod(x_ref[rows, :], g_ref[...], sc_ref[s], sh_ref[s])
            h_scr[rows, :] = h.astype(BF16)
        gate_ref[...] = jnp.dot(h_scr[...], wg_ref[...], preferred_element_type=F32) + bg_ref[...]

    acc = jnp.dot(h_scr[...], w_ref[...], preferred_element_type=F32) + b_ref[...]
    col0 = j * tn
    is_k = (col0 >= k_lo) & (col0 < k_hi)
    z_ref[...] = (acc * jnp.where(is_k, k_scale, 1.0)).astype(BF16)


def _mlstm_in_proj(u, g, sc, sh, w, b, wg, bg, *, tm, sub, dqk):
    n_rows, d = u.shape
    n = w.shape[1]
    qk_cols = d // 2
    tn = min(1024, qk_cols)
    nsb = tm // sub
    kern = functools.partial(_mlstm_in_kernel, sub=sub, k_lo=qk_cols, k_hi=2 * qk_cols,
                             k_scale=dqk ** -0.5)
    return pl.pallas_call(
        kern,
        grid=(n_rows // tm, n // tn),
        in_specs=[pl.BlockSpec((tm, d), lambda i, j: (i, 0)),
                  pl.BlockSpec((1, d), lambda i, j: (0, 0)),
                  pl.BlockSpec((nsb, 1, d), lambda i, j: (i, 0, 0)),
                  pl.BlockSpec((nsb, 1, d), lambda i, j: (i, 0, 0)),
                  pl.BlockSpec((d, tn), lambda i, j: (0, j)),
                  pl.BlockSpec((1, tn), lambda i, j: (0, j)),
                  pl.BlockSpec((d, 128), lambda i, j: (0, 0)),
                  pl.BlockSpec((1, 128), lambda i, j: (0, 0))],
        out_specs=[pl.BlockSpec((tm, tn), lambda i, j: (i, j)),
                   pl.BlockSpec((tm, 128), lambda i, j: (i, 0))],
        out_shape=[jax.ShapeDtypeStruct((n_rows, n), BF16),
                   jax.ShapeDtypeStruct((n_rows, 128), F32)],
        scratch_shapes=[pltpu.VMEM((tm, d), BF16)],
        compiler_params=_params(("parallel", "arbitrary")),
        name="mlstm_in",
    )(u, g, sc, sh, w, b, wg, bg)


def _mlstm_chunk(q, k, v, i_row, f_row, i_col, f_col, c_scr, n, m, reverse):
    ch = q.shape[0]
    t_idx = lax.broadcasted_iota(jnp.int32, (ch, ch), 0)
    s_idx = lax.broadcasted_iota(jnp.int32, (ch, ch), 1)
    lf_row = _log_sigmoid(f_row)
    lf_col = _log_sigmoid(f_col)
    if reverse:
        upto_t = s_idx >= t_idx
        upto_s = t_idx >= s_idx
    else:
        upto_t = s_idx <= t_idx
        upto_s = t_idx <= s_idx
    b_col = jnp.sum(jnp.where(upto_t, lf_row, 0.0), axis=1, keepdims=True)
    b_row = jnp.sum(jnp.where(upto_s, lf_col, 0.0), axis=0, keepdims=True)
    b_end = jnp.sum(lf_row, axis=1, keepdims=True)

    d = jnp.where(upto_t, b_col - b_row + i_row, -jnp.inf)
    inter = b_col + m
    m_t = jnp.maximum(inter, jnp.max(d, axis=1, keepdims=True))
    w = jnp.exp(d - m_t)
    e_inter = jnp.exp(inter - m_t)
    s = lax.dot_general(q, k, (((1,), (1,)), ((), ())), preferred_element_type=F32) * w
    c_old = c_scr[...]
    num = (e_inter * jnp.dot(q, c_old.astype(BF16), preferred_element_type=F32)
           + jnp.dot(s.astype(BF16), v, preferred_element_type=F32))
    qn = jnp.sum(q.astype(F32) * n, axis=1, keepdims=True)
    den = e_inter * qn + jnp.sum(s, axis=1, keepdims=True)
    h = num * (1.0 / jnp.maximum(jnp.abs(den), jnp.exp(-m_t)))

    g_row = b_end - b_row + i_row
    m_new = jnp.maximum(b_end + m, jnp.max(g_row, axis=1, keepdims=True))
    e_prev = jnp.exp(b_end + m - m_new)
    wg_col = jnp.exp(b_end - b_col + i_col - m_new)
    vw = (wg_col * v.astype(F32)).astype(BF16)
    c_scr[...] = e_prev * c_old + lax.dot_general(k, vw, (((0,), (0,)), ((), ())),
                                                  preferred_element_type=F32)
    n_new = e_prev * n + jnp.sum(wg_col * k.astype(F32), axis=0, keepdims=True)
    return h, n_new, m_new


def _mlstm_core_kernel(q_ref, k_ref, v_ref, o_ref, gr_ref, gc_ref, gh_ref, y_ref,
                       hf_scr, hb_scr, cf_scr, cb_scr, *, n_ctx_chunks, out_rows):
    n_chunks = gr_ref.shape[0]
    dqk = q_ref.shape[1]
    cf_scr[...] = jnp.zeros_like(cf_scr)
    cb_scr[...] = jnp.zeros_like(cb_scr)

    def run(c, c_scr, h_scr, n, m, gate0, reverse):
        rows = pl.ds(pl.multiple_of(c * CHUNK, CHUNK), CHUNK)
        gr = gr_ref[c]
        gc = gc_ref[c]
        h, n, m = _mlstm_chunk(q_ref[rows, :], k_ref[rows, :], v_ref[rows, :],
                               gr[gate0:gate0 + 1, :], gr[gate0 + 1:gate0 + 2, :],
                               gc[:, gate0:gate0 + 1], gc[:, gate0 + 1:gate0 + 2],
                               c_scr, n, m, reverse)
        h_scr[rows, :] = h
        return n, m

    def body(j, carry):
        nf, mf, nb, mb = carry
        nf, mf = run(j, cf_scr, hf_scr, nf, mf, 0, False)
        cb = jnp.where(j < n_ctx_chunks, n_ctx_chunks - 1 - j, n_chunks - 1 + n_ctx_chunks - j)
        nb, mb = run(cb, cb_scr, hb_scr, nb, mb, 2, True)
        return nf, mf, nb, mb

    zn = jnp.zeros((1, dqk), F32)
    zm = jnp.zeros((1, 1), F32)
    lax.fori_loop(0, n_chunks, body, (zn, zm, zn, zm), unroll=2)

    gh = gh_ref[...]

    def finish(r, _):
        rows = pl.ds(pl.multiple_of(r * out_rows, out_rows), out_rows)
        h = hf_scr[rows, :] + hb_scr[rows, :]
        hn = h * lax.rsqrt(jnp.mean(h * h, axis=-1, keepdims=True) + RMS_EPS)
        y_ref[rows, :] = (_sigmoid(o_ref[rows, :].astype(F32)) * hn * gh).astype(BF16)
        return 0

    lax.fori_loop(0, y_ref.shape[0] // out_rows, finish, 0)


def _mlstm_core(z, gates_r, gates_c, g_head, *, batch, p_rows, ctx, n_heads, dqk, dv):
    n_rows = z.shape[0]
    n_chunks = p_rows // CHUNK
    qk_blocks = n_heads
    kern = functools.partial(_mlstm_core_kernel, n_ctx_chunks=ctx // CHUNK, out_rows=ctx)
    return pl.pallas_call(
        kern,
        grid=(batch, n_heads),
        in_specs=[pl.BlockSpec((p_rows, dqk), lambda b, h: (b, h)),
                  pl.BlockSpec((p_rows, dqk), lambda b, h: (b, qk_blocks + h)),
                  pl.BlockSpec((p_rows, dv), lambda b, h: (b, n_heads + h)),
                  pl.BlockSpec((p_rows, dv), lambda b, h: (b, 2 * n_heads + h)),
                  pl.BlockSpec((None, None, n_chunks, 4, CHUNK), lambda b, h: (b, h, 0, 0, 0)),
                  pl.BlockSpec((None, None, n_chunks, CHUNK, 4), lambda b, h: (b, h, 0, 0, 0)),
                  pl.BlockSpec((1, dv), lambda b, h: (0, h))],
        out_specs=pl.BlockSpec((p_rows, dv), lambda b, h: (b, h)),
        out_shape=jax.ShapeDtypeStruct((n_rows, n_heads * dv), BF16),
        scratch_shapes=[pltpu.VMEM((p_rows, dv), F32), pltpu.VMEM((p_rows, dv), F32),
                        pltpu.VMEM((dqk, dv), F32), pltpu.VMEM((dqk, dv), F32)],
        compiler_params=_params(("parallel", "parallel")),
        name="mlstm_core",
    )(z, z, z, z, gates_r, gates_c, g_head)


def _ffn_kernel(h_ref, hp_ref, hn_ref, x_ref, gt_ref, w_ref, cw_ref, cb_ref, wd_ref,
                o_ref, h_scr, acc_scr, u0_scr, u1_scr, *, sub, p_rows, ctx):
    i = pl.program_id(0)
    j = pl.program_id(1)
    tm = o_ref.shape[0]
    nsb = tm // sub
    fh = w_ref.shape[1] // 4
    lt = fh // LANE
    seg = sub + 2 * HALO

    @pl.when(j == 0)
    def _():
        h_scr[0:BF16_ROWS, :] = hp_ref[...]
        h_scr[BF16_ROWS:BF16_ROWS + tm, :] = h_ref[...]
        h_scr[BF16_ROWS + tm:, :] = hn_ref[...]
        acc_scr[...] = jnp.zeros_like(acc_scr)

    def edge_open(k):
        pos = (i * tm + k * sub) % p_rows
        return jnp.where((pos == 0) | (pos == ctx), 0.0, 1.0)

    opens = [edge_open(k) for k in range(nsb + 1)]
    h_ext = h_scr[...]

    def up(c, u_scr):
        ue = jnp.dot(h_ext, w_ref[:, c * 2 * fh:(c + 1) * 2 * fh], preferred_element_type=F32)
        for s in range(nsb):
            r0 = BF16_ROWS + s * sub
            b0 = s * seg
            for t in range(2 * lt):
                lanes = slice(t * LANE, (t + 1) * LANE)
                u_scr[t, b0:b0 + HALO, :] = ue[r0 - HALO:r0, lanes] * opens[s]
                u_scr[t, b0 + HALO:b0 + HALO + sub, :] = ue[r0:r0 + sub, lanes]
                u_scr[t, b0 + HALO + sub:b0 + seg, :] = ue[r0 + sub:r0 + sub + HALO, lanes] * opens[s + 1]

    def gate(c, u_scr):
        cw = cw_ref[:, c * 2 * fh:(c + 1) * 2 * fh]
        cb = cb_ref[:, c * 2 * fh:(c + 1) * 2 * fh]

        def conv_tile(t, b0):
            lanes = slice(t * LANE, (t + 1) * LANE)
            cur = u_scr[t, b0 + HALO:b0 + HALO + sub, :]
            prv = u_scr[t, b0 + HALO - 1:b0 + HALO - 1 + sub, :]
            nxt = u_scr[t, b0 + HALO + 1:b0 + HALO + 1 + sub, :]
            return cb[:, lanes] + prv * cw[0:1, lanes] + cur * cw[1:2, lanes] + nxt * cw[2:3, lanes]

        rows = []
        for s in range(nsb):
            tiles = []
            for t in range(lt):
                a = conv_tile(t, s * seg)
                b = conv_tile(lt + t, s * seg)
                tiles.append((a * _sigmoid(a) * b).astype(BF16))
            rows.append(jnp.concatenate(tiles, axis=1))
        return jnp.concatenate(rows, axis=0)

    def down(c, act):
        acc_scr[...] += jnp.dot(act, wd_ref[c * fh:(c + 1) * fh, :], preferred_element_type=F32)

    up(0, u0_scr)
    up(1, u1_scr)
    down(0, gate(0, u0_scr))
    down(1, gate(1, u1_scr))

    @pl.when(j == pl.num_programs(1) - 1)
    def _():
        for s in range(nsb):
            rows = slice(s * sub, (s + 1) * sub)
            o_ref[rows, :] = x_ref[rows, :] + gt_ref[s] * acc_scr[rows, :]


def _interleave_half_chunks(w, fc):
    lead = w.shape[:-1]
    nfc = w.shape[-1] // (2 * fc)
    w = w.reshape(lead + (2, nfc, 2, fc // 2))
    w = jnp.moveaxis(w, -4, -2)
    return w.reshape(lead + (2 * nfc * fc,))


def _conv_ffn(h, u, gt, w_up, conv_w, conv_b, w_down, *, tm, sub, p_rows, ctx):
    n_rows, d = u.shape
    d_ff = w_down.shape[0]
    fc = FFN_CHUNK
    assert d_ff % fc == 0
    nfc = d_ff // fc
    nsb = tm // sub
    hb = tm // BF16_ROWS
    n_hblocks = n_rows // BF16_ROWS
    kern = functools.partial(_ffn_kernel, sub=sub, p_rows=p_rows, ctx=ctx)
    u_shape = (fc // LANE, nsb * (sub + 2 * HALO), LANE)
    return pl.pallas_call(
        kern,
        grid=(n_rows // tm, nfc),
        in_specs=[pl.BlockSpec((tm, d), lambda i, j: (i, 0)),
                  pl.BlockSpec((BF16_ROWS, d), lambda i, j: (jnp.maximum(i * hb - 1, 0), 0)),
                  pl.BlockSpec((BF16_ROWS, d), lambda i, j: (jnp.minimum((i + 1) * hb, n_hblocks - 1), 0)),
                  pl.BlockSpec((tm, d), lambda i, j: (i, 0)),
                  pl.BlockSpec((nsb, 1, d), lambda i, j: (i, 0, 0)),
                  pl.BlockSpec((d, 2 * fc), lambda i, j: (0, j)),
                  pl.BlockSpec((CONV_W, 2 * fc), lambda i, j: (0, j)),
                  pl.BlockSpec((1, 2 * fc), lambda i, j: (0, j)),
                  pl.BlockSpec((fc, d), lambda i, j: (j, 0))],
        out_specs=pl.BlockSpec((tm, d), lambda i, j: (i, 0)),
        out_shape=jax.ShapeDtypeStruct((n_rows, d), F32),
        scratch_shapes=[pltpu.VMEM((tm + 2 * BF16_ROWS, d), BF16), pltpu.VMEM((tm, d), F32),
                        pltpu.VMEM(u_shape, F32), pltpu.VMEM(u_shape, F32)],
        compiler_params=_params(("parallel", "arbitrary")),
        name="conv_ffn",
    )(h, h, h, u, gt, _interleave_half_chunks(w_up, fc), _interleave_half_chunks(conv_w, fc),
      _interleave_half_chunks(conv_b, fc), w_down)


def _final_norm_kernel(x_ref, g_ref, o_ref):
    xv = x_ref[...]
    o_ref[...] = xv * lax.rsqrt(jnp.mean(xv * xv, axis=-1, keepdims=True) + RMS_EPS) * g_ref[...]


def _final_norm(u, g, *, batch, seq, ctx, sub):
    d = u.shape[1]
    spb = (ctx + seq) // sub
    cs = ctx // sub
    return pl.pallas_call(
        _final_norm_kernel,
        grid=(batch, seq // sub),
        in_specs=[pl.BlockSpec((sub, d), lambda b, t: (b * spb + cs + t, 0)),
                  pl.BlockSpec((1, d), lambda b, t: (0, 0))],
        out_specs=pl.BlockSpec((None, sub, d), lambda b, t: (b, t, 0)),
        out_shape=jax.ShapeDtypeStruct((batch, seq, d), F32),
        compiler_params=_params(("parallel", "parallel")),
        name="final_norm",
    )(u, g)


def _rope_tables(batch, seq, ctx):
    rows = seq // GRID_W
    row = jnp.repeat(jnp.arange(rows, dtype=F32), GRID_W)
    col = jnp.tile(jnp.arange(GRID_W, dtype=F32), rows)
    n_freq = HEAD_DIM // 4
    inv_freq = ROPE_THETA ** (-jnp.arange(n_freq, dtype=F32) / n_freq)
    ang = jnp.concatenate([row[:, None] * inv_freq, col[:, None] * inv_freq], axis=-1)
    ang = jnp.concatenate([ang, ang], axis=-1)
    sign = jnp.concatenate([-jnp.ones((HEAD_DIM // 2,), F32), jnp.ones((HEAD_DIM // 2,), F32)])
    cos = jnp.concatenate([jnp.ones((ctx, HEAD_DIM), F32), jnp.cos(ang)], axis=0)
    sin = jnp.concatenate([jnp.zeros((ctx, HEAD_DIM), F32), jnp.sin(ang) * sign], axis=0)
    return jnp.tile(cos, (batch, 1)), jnp.tile(sin, (batch, 1))


def kernel(x, c, ctx, c_ctx, w_mod, b_mod, g_mix, g_ffn, attn_w_qkv, attn_sink, attn_w_o,
           mlstm_w_in, mlstm_b_in, mlstm_g_head, mlstm_w_o, ffn_w_up, ffn_conv_w, ffn_conv_b,
           ffn_w_down, g_final):
    batch, seq, d = x.shape
    lc = ctx.shape[1]
    depth = w_mod.shape[0]
    p_rows = lc + seq
    sub = lc
    assert seq % sub == 0 and sub % WINDOW == 0 and sub % CHUNK == 0 and seq % GRID_W == 0
    n_sub = batch * p_rows // sub
    tm = 2 * sub if n_sub % 2 == 0 else sub
    n_q = attn_sink.shape[1]
    n_kv = (attn_w_qkv.shape[2] // HEAD_DIM - n_q) // 2
    n_heads = (mlstm_w_in.shape[2] - 3 * d) // 4
    dv = d // n_heads
    dqk = dv // 2
    main_cols = 3 * d

    u = jnp.concatenate([ctx, x], axis=1).reshape(batch * p_rows, d)

    n_c = batch + 1
    c_rows = -(-n_c // 8) * 8
    cvec = jnp.concatenate([c, c_ctx[None, :], jnp.zeros((c_rows - n_c, d), F32)], axis=0)
    mods = _modulation(cvec, w_mod, b_mod)
    mods_sb = jnp.concatenate(
        [jnp.broadcast_to(mods[:, None, batch:batch + 1], (depth, batch, lc // sub, 6 * d)),
         jnp.broadcast_to(mods[:, :batch, None], (depth, batch, seq // sub, 6 * d))],
        axis=2).reshape(depth, n_sub, 6, 1, d)

    cos_u, sin_u = _rope_tables(batch, seq, lc)

    for i in range(depth):
        jm = i // 2
        sh1, sc1, gt1, sh2, sc2, gt2 = (mods_sb[i, :, k] for k in range(6))
        gm = g_mix[i].reshape(1, d)
        if i % 2 == 0:
            qkv = _qkv_proj(u, gm, sc1, sh1, attn_w_qkv[jm].astype(BF16), cos_u, sin_u,
                            tm=tm, sub=sub, n_q=n_q, n_kv=n_kv)
            y = _attention(qkv, attn_sink[jm], batch=batch, seq=seq, ctx=lc, n_q=n_q, n_kv=n_kv)
            w_o = attn_w_o[jm]
        else:
            w_in = mlstm_w_in[jm]
            b_in = mlstm_b_in[jm]
            n_gate = 4 * n_heads
            wg = jnp.pad(w_in[:, main_cols:], ((0, 0), (0, 128 - n_gate))).astype(BF16)
            bg = jnp.pad(b_in[main_cols:], (0, 128 - n_gate)).reshape(1, 128)
            z, gates = _mlstm_in_proj(u, gm, sc1, sh1, w_in[:, :main_cols].astype(BF16),
                                      b_in[:main_cols].reshape(1, main_cols), wg, bg,
                                      tm=tm, sub=sub, dqk=dqk)
            g5 = gates[:, :n_gate].reshape(batch, p_rows // CHUNK, CHUNK, 4, n_heads)
            gates_r = g5.transpose(0, 4, 1, 3, 2)
            gates_c = g5.transpose(0, 4, 1, 2, 3)
            y = _mlstm_core(z, gates_r, gates_c, mlstm_g_head[jm].reshape(1, n_heads * dv),
                            batch=batch, p_rows=p_rows, ctx=lc, n_heads=n_heads, dqk=dqk, dv=dv)
            w_o = mlstm_w_o[jm]
        u, h2 = _proj_residual(y, w_o.astype(BF16), u, gt1, g_ffn[i].reshape(1, d), sc2, sh2,
                               tm=tm, sub=sub)
        u = _conv_ffn(h2, u, gt2, ffn_w_up[i].astype(BF16), ffn_conv_w[i],
                      ffn_conv_b[i].reshape(1, -1), ffn_w_down[i].astype(BF16),
                      tm=tm, sub=sub, p_rows=p_rows, ctx=lc)
    return _final_norm(u, g_final.reshape(1, d), batch=batch, seq=seq, ctx=lc, sub=sub)
```

```python
import functools

import numpy as np
import jax
import jax.numpy as jnp
from jax import lax
from jax.experimental import pallas as pl
from jax.experimental.pallas import tpu as pltpu

F32 = jnp.float32
BF16 = jnp.bfloat16

RMS_EPS = 1e-6
HEAD_DIM = 128
WINDOW = 128
GRID_W = 64
ROPE_THETA = 10000.0
NEG_INF = -1e30
CHUNK = 64
CONV_W = 3
LANE = 128
HALO = 8
BF16_ROWS = 16
FFN_CHUNK = 512
V7X_VMEM_BYTES = 64 * 1024 * 1024
VMEM_LIMIT = V7X_VMEM_BYTES * 3 // 4


def _params(sem):
    return pltpu.CompilerParams(dimension_semantics=sem, vmem_limit_bytes=VMEM_LIMIT)


def _sigmoid(v):
    return 1.0 / (1.0 + jnp.exp(-v))


def _log_sigmoid(v):
    return jnp.minimum(v, 0.0) - jnp.log(1.0 + jnp.exp(-jnp.abs(v)))


def _norm_mod(xv, g, scale, shift):
    ms = jnp.mean(xv * xv, axis=-1, keepdims=True)
    y = xv * lax.rsqrt(ms + RMS_EPS)
    return (y * g) * (1.0 + scale) + shift


def _mod_kernel(c_ref, w_ref, b_ref, o_ref):
    cv = c_ref[...]
    s = (cv * _sigmoid(cv)).astype(BF16)
    o_ref[...] = jnp.dot(s, w_ref[...].astype(BF16), preferred_element_type=F32) + b_ref[...]


def _modulation(cvec, w_mod, b_mod):
    depth, d, n = w_mod.shape
    r = cvec.shape[0]
    tn = min(1024, n)
    return pl.pallas_call(
        _mod_kernel,
        grid=(depth, n // tn),
        in_specs=[pl.BlockSpec((r, d), lambda i, j: (0, 0)),
                  pl.BlockSpec((None, d, tn), lambda i, j: (i, 0, j)),
                  pl.BlockSpec((None, 1, tn), lambda i, j: (i, 0, j))],
        out_specs=pl.BlockSpec((None, r, tn), lambda i, j: (i, 0, j)),
        out_shape=jax.ShapeDtypeStruct((depth, r, n), F32),
        compiler_params=_params(("parallel", "parallel")),
        name="modulation",
    )(cvec, w_mod, b_mod.reshape(depth, 1, n))


def _qkv_kernel(x_ref, g_ref, sc_ref, sh_ref, w_ref, cos_ref, sin_ref, o_ref, h_scr,
                *, sub, n_q, n_rope, q_scale):
    j = pl.program_id(1)
    tm, tn = o_ref.shape

    @pl.when(j == 0)
    def _():
        for s in range(tm // sub):
            rows = pl.ds(s * sub, sub)
            h = _norm_mod(x_ref[rows, :], g_ref[...], sc_ref[s], sh_ref[s])
            h_scr[rows, :] = h.astype(BF16)

    acc = jnp.dot(h_scr[...], w_ref[...], preferred_element_type=F32)
    cos = cos_ref[...]
    sin = sin_ref[...]
    heads_per_tile = tn // HEAD_DIM
    for l in range(heads_per_tile):
        t = acc[:, l * HEAD_DIM:(l + 1) * HEAD_DIM]
        head = j * heads_per_tile + l
        tr = t * cos + pltpu.roll(t, HEAD_DIM // 2, 1) * sin
        t = jnp.where(head < n_rope, tr, t)
        t = t * jnp.where(head < n_q, q_scale, 1.0)
        o_ref[:, l * HEAD_DIM:(l + 1) * HEAD_DIM] = t.astype(BF16)


def _qkv_proj(u, g, sc, sh, w, cos_u, sin_u, *, tm, sub, n_q, n_kv):
    n_rows, d = u.shape
    n = w.shape[1]
    tn = min(1024, n)
    nsb = tm // sub
    kern = functools.partial(_qkv_kernel, sub=sub, n_q=n_q, n_rope=n_q + n_kv,
                             q_scale=HEAD_DIM ** -0.5)
    return pl.pallas_call(
        kern,
        grid=(n_rows // tm, n // tn),
        in_specs=[pl.BlockSpec((tm, d), lambda i, j: (i, 0)),
                  pl.BlockSpec((1, d), lambda i, j: (0, 0)),
                  pl.BlockSpec((nsb, 1, d), lambda i, j: (i, 0, 0)),
                  pl.BlockSpec((nsb, 1, d), lambda i, j: (i, 0, 0)),
                  pl.BlockSpec((d, tn), lambda i, j: (0, j)),
                  pl.BlockSpec((tm, HEAD_DIM), lambda i, j: (i, 0)),
                  pl.BlockSpec((tm, HEAD_DIM), lambda i, j: (i, 0))],
        out_specs=pl.BlockSpec((tm, tn), lambda i, j: (i, j)),
        out_shape=jax.ShapeDtypeStruct((n_rows, n), BF16),
        scratch_shapes=[pltpu.VMEM((tm, d), BF16)],
        compiler_params=_params(("parallel", "arbitrary")),
        name="attn_qkv",
    )(u, g, sc, sh, w, cos_u, sin_u)


def _softmax_pv(s, sink, v):
    m = jnp.maximum(jnp.max(s, axis=-1, keepdims=True), sink)
    p = jnp.exp(s - m)
    l = jnp.sum(p, axis=-1, keepdims=True) + jnp.exp(sink - m)
    o = jnp.dot(p.astype(BF16), v, preferred_element_type=F32)
    return o * (1.0 / l)


def _stack_heads(q, group):
    return jnp.concatenate([q[:, g * HEAD_DIM:(g + 1) * HEAD_DIM] for g in range(group)], axis=0)


def _attn_x_kernel(q_ref, kp_ref, ko_ref, kn_ref, kc_ref, vp_ref, vo_ref, vn_ref, vc_ref,
                   mask_ref, sink_ref, o_ref, *, group, n_kv):
    blk = q_ref.shape[0]
    valid = mask_ref[...] > 0.0

    def scores(h):
        kv = slice(h * HEAD_DIM, (h + 1) * HEAD_DIM)
        qs = _stack_heads(q_ref[:, h * group * HEAD_DIM:(h + 1) * group * HEAD_DIM], group)
        k = jnp.concatenate([kp_ref[:, kv], ko_ref[:, kv], kn_ref[:, kv], kc_ref[:, kv]], axis=0)
        s = lax.dot_general(qs, k, (((1,), (1,)), ((), ())), preferred_element_type=F32)
        return jnp.concatenate([jnp.where(valid, s[:, :3 * blk], NEG_INF), s[:, 3 * blk:]], axis=1)

    def finish(h, s):
        kv = slice(h * HEAD_DIM, (h + 1) * HEAD_DIM)
        v = jnp.concatenate([vp_ref[:, kv], vo_ref[:, kv], vn_ref[:, kv], vc_ref[:, kv]], axis=0)
        o = _softmax_pv(s, sink_ref[h], v)
        for g in range(group):
            c0 = (h * group + g) * HEAD_DIM
            o_ref[:, c0:c0 + HEAD_DIM] = o[g * blk:(g + 1) * blk, :].astype(BF16)

    s_next = scores(0)
    for h in range(n_kv):
        s_cur = s_next
        if h + 1 < n_kv:
            s_next = scores(h + 1)
        finish(h, s_cur)


def _attn_c_kernel(q_ref, kc_ref, vc_ref, sink_ref, o_ref, *, group):
    lc = q_ref.shape[0]
    qs = _stack_heads(q_ref[...], group)
    s = lax.dot_general(qs, kc_ref[...], (((1,), (1,)), ((), ())), preferred_element_type=F32)
    o = _softmax_pv(s, sink_ref[...], vc_ref[...])
    for g in range(group):
        o_ref[:, g * HEAD_DIM:(g + 1) * HEAD_DIM] = o[g * lc:(g + 1) * lc, :].astype(BF16)


def _window_mask(group, blk, n_blocks):
    qi = np.arange(group * blk)[:, None] % blk
    kj = np.arange(3 * blk)[None, :]
    band = (kj - qi >= 0) & (kj - qi <= 2 * WINDOW)
    first = band & (kj >= blk)
    last = band & (kj < 2 * blk)
    kinds = [first & last if n_blocks == 1 else first, band, last]
    return np.stack(kinds).astype(np.float32)


def _attention(qkv, sink, *, batch, seq, ctx, n_q, n_kv):
    n_rows = qkv.shape[0]
    group = n_q // n_kv
    blk = WINDOW
    p_rows = ctx + seq
    bpb = p_rows // blk
    cb = ctx // blk
    nb = seq // blk
    last_blk = n_rows // blk - 1
    qw = n_q * HEAD_DIM
    kw = n_kv * HEAD_DIM
    assert qw % kw == 0
    k_col = qw // kw
    spc = p_rows // ctx
    gw = group * HEAD_DIM
    sink_g = sink.reshape(n_kv, group, 1).astype(F32)
    sink_x = jnp.broadcast_to(sink_g[:, :, None, :], (n_kv, group, blk, 1)).reshape(n_kv, group * blk, 1)
    sink_c = jnp.broadcast_to(sink_g[:, :, None, :], (n_kv, group, ctx, 1)).reshape(n_kv, group * ctx, 1)
    mask = jnp.asarray(_window_mask(group, blk, nb))

    def kv_specs(col):
        return [pl.BlockSpec((blk, kw), lambda b, n: (jnp.maximum(b * bpb + cb + n - 1, 0), col)),
                pl.BlockSpec((blk, kw), lambda b, n: (b * bpb + cb + n, col)),
                pl.BlockSpec((blk, kw), lambda b, n: (jnp.minimum(b * bpb + cb + n + 1, last_blk), col)),
                pl.BlockSpec((ctx, kw), lambda b, n: (b * spc, col))]

    out_x = pl.pallas_call(
        functools.partial(_attn_x_kernel, group=group, n_kv=n_kv),
        grid=(batch, nb),
        in_specs=[pl.BlockSpec((blk, qw), lambda b, n: (b * bpb + cb + n, 0))]
        + kv_specs(k_col) + kv_specs(k_col + 1)
        + [pl.BlockSpec((None, group * blk, 3 * blk),
                        lambda b, n: (jnp.where(n == 0, 0, jnp.where(n == nb - 1, 2, 1)), 0, 0)),
           pl.BlockSpec((n_kv, group * blk, 1), lambda b, n: (0, 0, 0))],
        out_specs=pl.BlockSpec((blk, qw), lambda b, n: (b * bpb + cb + n, 0)),
        out_shape=jax.ShapeDtypeStruct((n_rows, qw), BF16),
        compiler_params=_params(("parallel", "parallel")),
        name="attn_latent",
    )(qkv, qkv, qkv, qkv, qkv, qkv, qkv, qkv, qkv, mask, sink_x)

    out = pl.pallas_call(
        functools.partial(_attn_c_kernel_alias, group=group),
        grid=(batch, n_kv),
        in_specs=[pl.BlockSpec((ctx, gw), lambda b, h: (b * spc, h)),
                  pl.BlockSpec((ctx, HEAD_DIM), lambda b, h: (b * spc, n_q + h)),
                  pl.BlockSpec((ctx, HEAD_DIM), lambda b, h: (b * spc, n_q + n_kv + h)),
                  pl.BlockSpec((None, group * ctx, 1), lambda b, h: (h, 0, 0)),
                  pl.BlockSpec(memory_space=pl.ANY)],
        out_specs=pl.BlockSpec((ctx, gw), lambda b, h: (b * spc, h)),
        out_shape=jax.ShapeDtypeStruct((n_rows, qw), BF16),
        input_output_aliases={4: 0},
        compiler_params=_params(("parallel", "parallel")),
        name="attn_context",
    )(qkv, qkv, qkv, sink_c, out_x)
    return out


def _attn_c_kernel_alias(q_ref, kc_ref, vc_ref, sink_ref, prev_ref, o_ref, *, group):
    del prev_ref
    _attn_c_kernel(q_ref, kc_ref, vc_ref, sink_ref, o_ref, group=group)


def _proj_res_kernel(y_ref, w_ref, x_ref, gt_ref, g_ref, sc_ref, sh_ref, o_ref, h_ref, *, sub):
    acc = jnp.dot(y_ref[...], w_ref[...], preferred_element_type=F32)
    g = g_ref[...]
    for s in range(o_ref.shape[0] // sub):
        rows = slice(s * sub, (s + 1) * sub)
        xn = x_ref[rows, :] + gt_ref[s] * acc[rows, :]
        o_ref[rows, :] = xn
        h_ref[rows, :] = _norm_mod(xn, g, sc_ref[s], sh_ref[s]).astype(BF16)


def _proj_residual(y, w, u, gt, g, sc, sh, *, tm, sub):
    n_rows, d = u.shape
    k = y.shape[1]
    nsb = tm // sub
    sb_spec = pl.BlockSpec((nsb, 1, d), lambda i: (i, 0, 0))
    return pl.pallas_call(
        functools.partial(_proj_res_kernel, sub=sub),
        grid=(n_rows // tm,),
        in_specs=[pl.BlockSpec((tm, k), lambda i: (i, 0)),
                  pl.BlockSpec((k, d), lambda i: (0, 0)),
                  pl.BlockSpec((tm, d), lambda i: (i, 0)),
                  sb_spec,
                  pl.BlockSpec((1, d), lambda i: (0, 0)),
                  sb_spec, sb_spec],
        out_specs=[pl.BlockSpec((tm, d), lambda i: (i, 0)),
                   pl.BlockSpec((tm, d), lambda i: (i, 0))],
        out_shape=[jax.ShapeDtypeStruct((n_rows, d), F32),
                   jax.ShapeDtypeStruct((n_rows, d), BF16)],
        compiler_params=_params(("parallel",)),
        name="proj_residual",
    )(y, w, u, gt, g, sc, sh)


def _mlstm_in_kernel(x_ref, g_ref, sc_ref, sh_ref, w_ref, b_ref, wg_ref, bg_ref, z_ref, gate_ref,
                     h_scr, *, sub, k_lo, k_hi, k_scale):
    j = pl.program_id(1)
    tm, tn = z_ref.shape

    @pl.when(j == 0)
    def _():
        for s in range(tm // sub):
            rows = pl.ds(s * sub, sub)
            h = _norm_mod(x_ref[rows, :], g_ref[...], sc_ref[s], sh_ref[s])
            h_scr[rows, :] = h.astype(BF16)
        gate_ref[...] = jnp.dot(h_scr[...], wg_ref[...], preferred_element_type=F32) + bg_ref[...]

    acc = jnp.dot(h_scr[...], w_ref[...], preferred_element_type=F32) + b_ref[...]
    col0 = j * tn
    is_k = (col0 >= k_lo) & (col0 < k_hi)
    z_ref[...] = (acc * jnp.where(is_k, k_scale, 1.0)).astype(BF16)


def _mlstm_in_proj(u, g, sc, sh, w, b, wg, bg, *, tm, sub, dqk):
    n_rows, d = u.shape
    n = w.shape[1]
    qk_cols = d // 2
    tn = min(1024, qk_cols)
    nsb = tm // sub
    kern = functools.partial(_mlstm_in_kernel, sub=sub, k_lo=qk_cols, k_hi=2 * qk_cols,
                             k_scale=dqk ** -0.5)
    return pl.pallas_call(
        kern,
        grid=(n_rows // tm, n // tn),
        in_specs=[pl.BlockSpec((tm, d), lambda i, j: (i, 0)),
                  pl.BlockSpec((1, d), lambda i, j: (0, 0)),
                  pl.BlockSpec((nsb, 1, d), lambda i, j: (i, 0, 0)),
                  pl.BlockSpec((nsb, 1, d), lambda i, j: (i, 0, 0)),
                  pl.BlockSpec((d, tn), lambda i, j: (0, j)),
                  pl.BlockSpec((1, tn), lambda i, j: (0, j)),
                  pl.BlockSpec((d, 128), lambda i, j: (0, 0)),
                  pl.BlockSpec((1, 128), lambda i, j: (0, 0))],
        out_specs=[pl.BlockSpec((tm, tn), lambda i, j: (i, j)),
                   pl.BlockSpec((tm, 128), lambda i, j: (i, 0))],
        out_shape=[jax.ShapeDtypeStruct((n_rows, n), BF16),
                   jax.ShapeDtypeStruct((n_rows, 128), F32)],
        scratch_shapes=[pltpu.VMEM((tm, d), BF16)],
        compiler_params=_params(("parallel", "arbitrary")),
        name="mlstm_in",
    )(u, g, sc, sh, w, b, wg, bg)


GATE_F_LANE = 64
N_GATE_OUT = 6
N_COL = 5


def _chunk_scan(x, scr, pos, op, ident, reverse):
    r = x.shape[0]
    sh = 1
    while sh < CHUNK:
        scr[CHUNK:CHUNK + r, :] = x
        if reverse:
            other = scr[CHUNK + sh:CHUNK + sh + r, :]
            ok = pos < CHUNK - sh
        else:
            other = scr[CHUNK - sh:CHUNK - sh + r, :]
            ok = pos >= sh
        x = op(x, jnp.where(ok, other, ident))
        sh *= 2
    return x


def _mlstm_gate_kernel(g_ref, o_ref, b_scr, cm_scr, shift_scr, *, n_heads, n_ctx_chunks, tile):
    p_rows = g_ref.shape[0]
    n_chunks = p_rows // CHUNK
    shift_scr[...] = jnp.zeros_like(shift_scr)
    o_ref[0:N_COL] = jnp.zeros((N_COL,) + o_ref.shape[1:], F32)
    fwd_tile = lax.broadcasted_iota(jnp.int32, (tile, LANE), 1) < n_heads
    pos = lax.broadcasted_iota(jnp.int32, (tile, LANE), 0) % CHUNK

    def local(t, _):
        rows = pl.ds(pl.multiple_of(t * tile, tile), tile)
        g = g_ref[rows, :]
        lf = _log_sigmoid(pltpu.roll(g, LANE - GATE_F_LANE, 1))
        b = jnp.where(fwd_tile, _chunk_scan(lf, shift_scr, pos, jnp.add, 0.0, False),
                      _chunk_scan(lf, shift_scr, pos, jnp.add, 0.0, True))
        a = g - b
        cm = jnp.where(fwd_tile, _chunk_scan(a, shift_scr, pos, jnp.maximum, -jnp.inf, False),
                       _chunk_scan(a, shift_scr, pos, jnp.maximum, -jnp.inf, True))
        b_scr[rows, :] = b
        cm_scr[rows, :] = cm
        o_ref[N_COL, rows, :] = a
        return 0

    lax.fori_loop(0, p_rows // tile, local, 0)

    fwd_chunk = lax.broadcasted_iota(jnp.int32, (CHUNK, LANE), 1) < n_heads

    def step(c, m, forward):
        rows = pl.ds(pl.multiple_of(c * CHUNK, CHUNK), CHUNK)
        end_row = CHUNK - 1 if forward else 0
        mine = fwd_chunk if forward else jnp.logical_not(fwd_chunk)
        b = b_scr[rows, :]
        a = o_ref[N_COL, rows, :]
        cm = cm_scr[rows, :]
        b_end = b[end_row:end_row + 1, :]
        m_t = b + jnp.maximum(m, cm)
        m_new = jnp.maximum(b_end + m, b_end + cm[end_row:end_row + 1, :])
        vals = (b - m_t, jnp.exp(b + m - m_t), jnp.exp(-m_t), jnp.exp(b_end + a - m_new),
                jnp.broadcast_to(jnp.exp(b_end + m - m_new), (CHUNK, LANE)))
        for q, val in enumerate(vals):
            o_ref[q, rows, :] = jnp.where(mine, val, o_ref[q, rows, :])
        return m_new

    def body(j, carry):
        mf, mb = carry
        mf = step(j, mf, True)
        cb = jnp.where(j < n_ctx_chunks, n_ctx_chunks - 1 - j, n_chunks - 1 + n_ctx_chunks - j)
        mb = step(cb, mb, False)
        return mf, mb

    zm = jnp.zeros((1, LANE), F32)
    lax.fori_loop(0, n_chunks, body, (zm, zm))


def _mlstm_gates(gates, *, batch, p_rows, ctx, n_heads):
    n_rows = gates.shape[0]
    tile = ctx
    kern = functools.partial(_mlstm_gate_kernel, n_heads=n_heads, n_ctx_chunks=ctx // CHUNK, tile=tile)
    return pl.pallas_call(
        kern,
        grid=(batch,),
        in_specs=[pl.BlockSpec((p_rows, LANE), lambda b: (b, 0))],
        out_specs=pl.BlockSpec((N_GATE_OUT, p_rows, LANE), lambda b: (0, b, 0)),
        out_shape=jax.ShapeDtypeStruct((N_GATE_OUT, n_rows, LANE), F32),
        scratch_shapes=[pltpu.VMEM((p_rows, LANE), F32), pltpu.VMEM((p_rows, LANE), F32),
                        pltpu.VMEM((tile + 2 * CHUNK, LANE), F32)],
        compiler_params=_params(("parallel",)),
        name="mlstm_gates",
    )(gates)


def _mlstm_chunk(q, k, v, col, a_row, c_scr, n, d, reverse):
    ch = q.shape[0]
    t_idx = lax.broadcasted_iota(jnp.int32, (ch, ch), 0)
    s_idx = lax.broadcasted_iota(jnp.int32, (ch, ch), 1)
    seen = (s_idx >= t_idx) if reverse else (s_idx <= t_idx)
    c0 = d * N_COL
    row_term = col[:, c0:c0 + 1]
    e_inter = col[:, c0 + 1:c0 + 2]
    floor = col[:, c0 + 2:c0 + 3]
    wg_col = col[:, c0 + 3:c0 + 4]
    e_prev = col[0:1, c0 + 4:c0 + 5]

    w = jnp.where(seen, jnp.exp(row_term + a_row), 0.0)
    s = lax.dot_general(q, k, (((1,), (1,)), ((), ())), preferred_element_type=F32) * w
    c_old = c_scr[...]
    num = (e_inter * jnp.dot(q, c_old.astype(BF16), preferred_element_type=F32)
           + jnp.dot(s.astype(BF16), v, preferred_element_type=F32))
    qn = jnp.sum(q.astype(F32) * n, axis=1, keepdims=True)
    den = e_inter * qn + jnp.sum(s, axis=1, keepdims=True)
    h = num * (1.0 / jnp.maximum(jnp.abs(den), floor))

    vw = (wg_col * v.astype(F32)).astype(BF16)
    c_scr[...] = e_prev * c_old + lax.dot_general(k, vw, (((0,), (0,)), ((), ())),
                                                  preferred_element_type=F32)
    n_new = e_prev * n + jnp.sum(wg_col * k.astype(F32), axis=0, keepdims=True)
    return h, n_new


def _mlstm_core_kernel(q_ref, k_ref, v_ref, o_ref, gr_ref, gc_ref, gh_ref, y_ref,
                       hf_scr, hb_scr, cf_scr, cb_scr, *, n_ctx_chunks, out_rows):
    n_chunks = gr_ref.shape[0]
    dqk = q_ref.shape[1]
    cf_scr[...] = jnp.zeros_like(cf_scr)
    cb_scr[...] = jnp.zeros_like(cb_scr)

    def run(c, c_scr, h_scr, n, d, reverse):
        rows = pl.ds(pl.multiple_of(c * CHUNK, CHUNK), CHUNK)
        h, n = _mlstm_chunk(q_ref[rows, :], k_ref[rows, :], v_ref[rows, :], gc_ref[rows, :],
                            gr_ref[c][d:d + 1, :], c_scr, n, d, reverse)
        h_scr[rows, :] = h
        return n

    def body(j, carry):
        nf, nb = carry
        nf = run(j, cf_scr, hf_scr, nf, 0, False)
        cb = jnp.where(j < n_ctx_chunks, n_ctx_chunks - 1 - j, n_chunks - 1 + n_ctx_chunks - j)
        nb = run(cb, cb_scr, hb_scr, nb, 1, True)
        return nf, nb

    zn = jnp.zeros((1, dqk), F32)
    lax.fori_loop(0, n_chunks, body, (zn, zn), unroll=2)

    gh = gh_ref[...]

    def finish(r, _):
        rows = pl.ds(pl.multiple_of(r * out_rows, out_rows), out_rows)
        h = hf_scr[rows, :] + hb_scr[rows, :]
        hn = h * lax.rsqrt(jnp.mean(h * h, axis=-1, keepdims=True) + RMS_EPS)
        y_ref[rows, :] = (_sigmoid(o_ref[rows, :].astype(F32)) * hn * gh).astype(BF16)
        return 0

    lax.fori_loop(0, y_ref.shape[0] // out_rows, finish, 0)


def _mlstm_core(z, gates_r, gates_c, g_head, *, batch, p_rows, ctx, n_heads, dqk, dv):
    n_rows = z.shape[0]
    n_chunks = p_rows // CHUNK
    kern = functools.partial(_mlstm_core_kernel, n_ctx_chunks=ctx // CHUNK, out_rows=ctx)
    return pl.pallas_call(
        kern,
        grid=(batch, n_heads),
        in_specs=[pl.BlockSpec((p_rows, dqk), lambda b, h: (b, h)),
                  pl.BlockSpec((p_rows, dqk), lambda b, h: (b, n_heads + h)),
                  pl.BlockSpec((p_rows, dv), lambda b, h: (b, n_heads + h)),
                  pl.BlockSpec((p_rows, dv), lambda b, h: (b, 2 * n_heads + h)),
                  pl.BlockSpec((None, None, n_chunks, 2, CHUNK), lambda b, h: (b, h, 0, 0, 0)),
                  pl.BlockSpec((None, None, p_rows, 2 * N_COL), lambda b, h: (b, h, 0, 0)),
                  pl.BlockSpec((1, dv), lambda b, h: (0, h))],
        out_specs=pl.BlockSpec((p_rows, dv), lambda b, h: (b, h)),
        out_shape=jax.ShapeDtypeStruct((n_rows, n_heads * dv), BF16),
        scratch_shapes=[pltpu.VMEM((p_rows, dv), F32), pltpu.VMEM((p_rows, dv), F32),
                        pltpu.VMEM((dqk, dv), F32), pltpu.VMEM((dqk, dv), F32)],
        compiler_params=_params(("parallel", "parallel")),
        name="mlstm_core",
    )(z, z, z, z, gates_r, gates_c, g_head)


def _gate_layouts(og, *, batch, p_rows, n_heads):
    g = og[:, :, :2 * n_heads].reshape(N_GATE_OUT, batch, p_rows, 2, n_heads)
    cols = g[:N_COL].transpose(1, 4, 2, 3, 0).reshape(batch, n_heads, p_rows, 2 * N_COL)
    rows = g[N_COL].reshape(batch, p_rows // CHUNK, CHUNK, 2, n_heads).transpose(0, 4, 1, 3, 2)
    return rows, cols


def _ffn_kernel(h_ref, hp_ref, hn_ref, x_ref, gt_ref, wa0_ref, wb0_ref, wa1_ref, wb1_ref,
                cwa_ref, cwb_ref, cba_ref, cbb_ref, wd_ref,
                o_ref, h_scr, acc_scr, u0_scr, u1_scr, *, sub, p_rows, ctx):
    i = pl.program_id(0)
    j = pl.program_id(1)
    tm = o_ref.shape[0]
    nsb = tm // sub
    fh = wa0_ref.shape[1]
    lt = fh // LANE
    seg = sub + 2 * HALO

    @pl.when(j == 0)
    def _():
        h_scr[0:BF16_ROWS, :] = hp_ref[...]
        h_scr[BF16_ROWS:BF16_ROWS + tm, :] = h_ref[...]
        h_scr[BF16_ROWS + tm:, :] = hn_ref[...]
        acc_scr[...] = jnp.zeros_like(acc_scr)

    def edge_open(k):
        pos = (i * tm + k * sub) % p_rows
        return jnp.where((pos == 0) | (pos == ctx), 0.0, 1.0)

    opens = [edge_open(k) for k in range(nsb + 1)]
    h_ext = h_scr[...]

    def up(wa_ref, wb_ref, u_scr):
        for br, w_ref in enumerate((wa_ref, wb_ref)):
            ue = jnp.dot(h_ext, w_ref[...], preferred_element_type=F32)
            for s in range(nsb):
                r0 = BF16_ROWS + s * sub
                b0 = s * seg
                for t in range(lt):
                    lanes = slice(t * LANE, (t + 1) * LANE)
                    tt = br * lt + t
                    u_scr[tt, b0:b0 + HALO, :] = ue[r0 - HALO:r0, lanes] * opens[s]
                    u_scr[tt, b0 + HALO:b0 + HALO + sub, :] = ue[r0:r0 + sub, lanes]
                    u_scr[tt, b0 + HALO + sub:b0 + seg, :] = ue[r0 + sub:r0 + sub + HALO, lanes] * opens[s + 1]

    def gate(c, u_scr):
        cols = slice(c * fh, (c + 1) * fh)
        cws = (cwa_ref[:, cols], cwb_ref[:, cols])
        cbs = (cba_ref[:, cols], cbb_ref[:, cols])

        def conv_tile(br, t, b0):
            lanes = slice(t * LANE, (t + 1) * LANE)
            tt = br * lt + t
            cw, cb = cws[br], cbs[br]
            cur = u_scr[tt, b0 + HALO:b0 + HALO + sub, :]
            prv = u_scr[tt, b0 + HALO - 1:b0 + HALO - 1 + sub, :]
            nxt = u_scr[tt, b0 + HALO + 1:b0 + HALO + 1 + sub, :]
            return cb[:, lanes] + prv * cw[0:1, lanes] + cur * cw[1:2, lanes] + nxt * cw[2:3, lanes]

        rows = []
        for s in range(nsb):
            tiles = []
            for t in range(lt):
                a = conv_tile(0, t, s * seg)
                b = conv_tile(1, t, s * seg)
                tiles.append((a * _sigmoid(a) * b).astype(BF16))
            rows.append(jnp.concatenate(tiles, axis=1))
        return jnp.concatenate(rows, axis=0)

    def down(c, act):
        acc_scr[...] += jnp.dot(act, wd_ref[c * fh:(c + 1) * fh, :], preferred_element_type=F32)

    up(wa0_ref, wb0_ref, u0_scr)
    up(wa1_ref, wb1_ref, u1_scr)
    down(0, gate(0, u0_scr))
    down(1, gate(1, u1_scr))

    @pl.when(j == pl.num_programs(1) - 1)
    def _():
        for s in range(nsb):
            rows = slice(s * sub, (s + 1) * sub)
            o_ref[rows, :] = x_ref[rows, :] + gt_ref[s] * acc_scr[rows, :]


def _conv_ffn(h, u, gt, w_up, conv_w, conv_b, w_down, *, tm, sub, p_rows, ctx):
    n_rows, d = u.shape
    d_ff = w_down.shape[0]
    fc = FFN_CHUNK
    assert d_ff % fc == 0
    fh = fc // 2
    nfc = d_ff // fc
    nsb = tm // sub
    hb = tm // BF16_ROWS
    n_hblocks = n_rows // BF16_ROWS
    kern = functools.partial(_ffn_kernel, sub=sub, p_rows=p_rows, ctx=ctx)
    u_shape = (fc // LANE, nsb * (sub + 2 * HALO), LANE)
    return pl.pallas_call(
        kern,
        grid=(n_rows // tm, nfc),
        in_specs=[pl.BlockSpec((tm, d), lambda i, j: (i, 0)),
                  pl.BlockSpec((BF16_ROWS, d), lambda i, j: (jnp.maximum(i * hb - 1, 0), 0)),
                  pl.BlockSpec((BF16_ROWS, d), lambda i, j: (jnp.minimum((i + 1) * hb, n_hblocks - 1), 0)),
                  pl.BlockSpec((tm, d), lambda i, j: (i, 0)),
                  pl.BlockSpec((nsb, 1, d), lambda i, j: (i, 0, 0)),
                  pl.BlockSpec((d, fh), lambda i, j: (0, 2 * j)),
                  pl.BlockSpec((d, fh), lambda i, j: (0, 2 * nfc + 2 * j)),
                  pl.BlockSpec((d, fh), lambda i, j: (0, 2 * j + 1)),
                  pl.BlockSpec((d, fh), lambda i, j: (0, 2 * nfc + 2 * j + 1)),
                  pl.BlockSpec((CONV_W, fc), lambda i, j: (0, j)),
                  pl.BlockSpec((CONV_W, fc), lambda i, j: (0, nfc + j)),
                  pl.BlockSpec((1, fc), lambda i, j: (0, j)),
                  pl.BlockSpec((1, fc), lambda i, j: (0, nfc + j)),
                  pl.BlockSpec((fc, d), lambda i, j: (j, 0))],
        out_specs=pl.BlockSpec((tm, d), lambda i, j: (i, 0)),
        out_shape=jax.ShapeDtypeStruct((n_rows, d), F32),
        scratch_shapes=[pltpu.VMEM((tm + 2 * BF16_ROWS, d), BF16), pltpu.VMEM((tm, d), F32),
                        pltpu.VMEM(u_shape, F32), pltpu.VMEM(u_shape, F32)],
        compiler_params=_params(("parallel", "arbitrary")),
        name="conv_ffn",
    )(h, h, h, u, gt, w_up, w_up, w_up, w_up, conv_w, conv_w, conv_b, conv_b, w_down)


def _final_norm_kernel(x_ref, g_ref, o_ref):
    xv = x_ref[...]
    o_ref[...] = xv * lax.rsqrt(jnp.mean(xv * xv, axis=-1, keepdims=True) + RMS_EPS) * g_ref[...]


def _final_norm(u, g, *, batch, seq, ctx, sub):
    d = u.shape[1]
    spb = (ctx + seq) // sub
    cs = ctx // sub
    return pl.pallas_call(
        _final_norm_kernel,
        grid=(batch, seq // sub),
        in_specs=[pl.BlockSpec((sub, d), lambda b, t: (b * spb + cs + t, 0)),
                  pl.BlockSpec((1, d), lambda b, t: (0, 0))],
        out_specs=pl.BlockSpec((None, sub, d), lambda b, t: (b, t, 0)),
        out_shape=jax.ShapeDtypeStruct((batch, seq, d), F32),
        compiler_params=_params(("parallel", "parallel")),
        name="final_norm",
    )(u, g)


def _rope_tables(batch, seq, ctx):
    rows = seq // GRID_W
    row = jnp.repeat(jnp.arange(rows, dtype=F32), GRID_W)
    col = jnp.tile(jnp.arange(GRID_W, dtype=F32), rows)
    n_freq = HEAD_DIM // 4
    inv_freq = ROPE_THETA ** (-jnp.arange(n_freq, dtype=F32) / n_freq)
    ang = jnp.concatenate([row[:, None] * inv_freq, col[:, None] * inv_freq], axis=-1)
    ang = jnp.concatenate([ang, ang], axis=-1)
    sign = jnp.concatenate([-jnp.ones((HEAD_DIM // 2,), F32), jnp.ones((HEAD_DIM // 2,), F32)])
    cos = jnp.concatenate([jnp.ones((ctx, HEAD_DIM), F32), jnp.cos(ang)], axis=0)
    sin = jnp.concatenate([jnp.zeros((ctx, HEAD_DIM), F32), jnp.sin(ang) * sign], axis=0)
    return jnp.tile(cos, (batch, 1)), jnp.tile(sin, (batch, 1))


def kernel(x, c, ctx, c_ctx, w_mod, b_mod, g_mix, g_ffn, attn_w_qkv, attn_sink, attn_w_o,
           mlstm_w_in, mlstm_b_in, mlstm_g_head, mlstm_w_o, ffn_w_up, ffn_conv_w, ffn_conv_b,
           ffn_w_down, g_final):
    batch, seq, d = x.shape
    lc = ctx.shape[1]
    depth = w_mod.shape[0]
    p_rows = lc + seq
    sub = lc
    assert seq % sub == 0 and sub % WINDOW == 0 and sub % CHUNK == 0 and seq % GRID_W == 0
    n_sub = batch * p_rows // sub
    tm = 2 * sub if n_sub % 2 == 0 else sub
    n_q = attn_sink.shape[1]
    n_kv = (attn_w_qkv.shape[2] // HEAD_DIM - n_q) // 2
    n_heads = (mlstm_w_in.shape[2] - 3 * d) // 4
    dv = d // n_heads
    dqk = dv // 2
    main_cols = 3 * d

    u = jnp.concatenate([ctx, x], axis=1).reshape(batch * p_rows, d)

    n_c = batch + 1
    c_rows = -(-n_c // 8) * 8
    cvec = jnp.concatenate([c, c_ctx[None, :], jnp.zeros((c_rows - n_c, d), F32)], axis=0)
    mods = _modulation(cvec, w_mod, b_mod)
    mods_sb = jnp.concatenate(
        [jnp.broadcast_to(mods[:, None, batch:batch + 1], (depth, batch, lc // sub, 6 * d)),
         jnp.broadcast_to(mods[:, :batch, None], (depth, batch, seq // sub, 6 * d))],
        axis=2).reshape(depth, n_sub, 6, 1, d)

    cos_u, sin_u = _rope_tables(batch, seq, lc)

    for i in range(depth):
        jm = i // 2
        sh1, sc1, gt1, sh2, sc2, gt2 = (mods_sb[i, :, k] for k in range(6))
        gm = g_mix[i].reshape(1, d)
        if i % 2 == 0:
            qkv = _qkv_proj(u, gm, sc1, sh1, attn_w_qkv[jm].astype(BF16), cos_u, sin_u,
                            tm=tm, sub=sub, n_q=n_q, n_kv=n_kv)
            y = _attention(qkv, attn_sink[jm], batch=batch, seq=seq, ctx=lc, n_q=n_q, n_kv=n_kv)
            w_o = attn_w_o[jm]
        else:
            w_in = mlstm_w_in[jm]
            b_in = mlstm_b_in[jm]
            wgt = w_in[:, main_cols:].reshape(d, 4, n_heads)
            bgt = b_in[main_cols:].reshape(4, n_heads)
            lane_pad = GATE_F_LANE - 2 * n_heads
            wg = jnp.concatenate([wgt[:, 0], wgt[:, 2], jnp.zeros((d, lane_pad), F32),
                                  wgt[:, 1], wgt[:, 3], jnp.zeros((d, lane_pad), F32)], axis=1).astype(BF16)
            bg = jnp.concatenate([bgt[0], bgt[2], jnp.zeros((lane_pad,), F32),
                                  bgt[1], bgt[3], jnp.zeros((lane_pad,), F32)]).reshape(1, LANE)
            z, gates = _mlstm_in_proj(u, gm, sc1, sh1, w_in[:, :main_cols].astype(BF16),
                                      b_in[:main_cols].reshape(1, main_cols), wg, bg,
                                      tm=tm, sub=sub, dqk=dqk)
            og = _mlstm_gates(gates, batch=batch, p_rows=p_rows, ctx=lc, n_heads=n_heads)
            gates_r, gates_c = _gate_layouts(og, batch=batch, p_rows=p_rows, n_heads=n_heads)
            y = _mlstm_core(z, gates_r, gates_c, mlstm_g_head[jm].reshape(1, n_heads * dv),
                            batch=batch, p_rows=p_rows, ctx=lc, n_heads=n_heads, dqk=dqk, dv=dv)
            w_o = mlstm_w_o[jm]
        u, h2 = _proj_residual(y, w_o.astype(BF16), u, gt1, g_ffn[i].reshape(1, d), sc2, sh2,
                               tm=tm, sub=sub)
        u = _conv_ffn(h2, u, gt2, ffn_w_up[i].astype(BF16), ffn_conv_w[i],
                      ffn_conv_b[i].reshape(1, -1), ffn_w_down[i].astype(BF16),
                      tm=tm, sub=sub, p_rows=p_rows, ctx=lc)
    return _final_norm(u, g_final.reshape(1, d), batch=batch, seq=seq, ctx=lc, sub=sub)
```

```python
import functools

import numpy as np
import jax
import jax.numpy as jnp
from jax import lax
from jax.experimental import pallas as pl
from jax.experimental.pallas import tpu as pltpu

F32 = jnp.float32
BF16 = jnp.bfloat16

RMS_EPS = 1e-6
HEAD_DIM = 128
WINDOW = 128
GRID_W = 64
ROPE_THETA = 10000.0
NEG_INF = -1e30
CHUNK = 64
CONV_W = 3
LANE = 128
HALO = 8
BF16_ROWS = 16
FFN_CHUNK = 512
GATE_ROWS = 64
NORM_ROWS = 16
V7X_VMEM_BYTES = 64 * 1024 * 1024
VMEM_LIMIT = V7X_VMEM_BYTES * 3 // 4


def _params(sem):
    return pltpu.CompilerParams(dimension_semantics=sem, vmem_limit_bytes=VMEM_LIMIT)


def _sigmoid(v):
    return 1.0 / (1.0 + jnp.exp(-v))


def _log_sigmoid(v):
    return jnp.minimum(v, 0.0) - jnp.log(1.0 + jnp.exp(-jnp.abs(v)))


def _norm_mod(xv, g, scale, shift):
    ms = jnp.mean(xv * xv, axis=-1, keepdims=True)
    y = xv * lax.rsqrt(ms + RMS_EPS)
    return (y * g) * (1.0 + scale) + shift


def _norm_mod_to(h_scr, x_ref, g_ref, sc_ref, sh_ref, sub):
    for s in range(x_ref.shape[0] // sub):
        gain = g_ref[...] * (1.0 + sc_ref[s])
        shift = sh_ref[s]
        for r in range(s * sub, (s + 1) * sub, NORM_ROWS):
            xv = x_ref[r:r + NORM_ROWS, :]
            ms = jnp.mean(xv * xv, axis=-1, keepdims=True)
            h_scr[r:r + NORM_ROWS, :] = (xv * lax.rsqrt(ms + RMS_EPS) * gain + shift).astype(BF16)


def _mod_kernel(c_ref, w_ref, b_ref, o_ref):
    cv = c_ref[...]
    s = (cv * _sigmoid(cv)).astype(BF16)
    o_ref[...] = jnp.dot(s, w_ref[...].astype(BF16), preferred_element_type=F32) + b_ref[...]


def _modulation(cvec, w_mod, b_mod):
    depth, d, n = w_mod.shape
    r = cvec.shape[0]
    tn = min(1024, n)
    return pl.pallas_call(
        _mod_kernel,
        grid=(depth, n // tn),
        in_specs=[pl.BlockSpec((r, d), lambda i, j: (0, 0)),
                  pl.BlockSpec((None, d, tn), lambda i, j: (i, 0, j)),
                  pl.BlockSpec((None, 1, tn), lambda i, j: (i, 0, j))],
        out_specs=pl.BlockSpec((None, r, tn), lambda i, j: (i, 0, j)),
        out_shape=jax.ShapeDtypeStruct((depth, r, n), F32),
        compiler_params=_params(("parallel", "parallel")),
        name="modulation",
    )(cvec, w_mod, b_mod.reshape(depth, 1, n))


def _qkv_kernel(x_ref, g_ref, sc_ref, sh_ref, w_ref, cos_ref, sin_ref, o_ref, h_scr,
                *, sub, n_q, n_rope, q_scale):
    j = pl.program_id(1)
    tm, tn = o_ref.shape

    @pl.when(j == 0)
    def _():
        _norm_mod_to(h_scr, x_ref, g_ref, sc_ref, sh_ref, sub)

    acc = jnp.dot(h_scr[...], w_ref[...], preferred_element_type=F32)
    cos = cos_ref[...]
    sin = sin_ref[...]
    heads_per_tile = tn // HEAD_DIM
    for l in range(heads_per_tile):
        t = acc[:, l * HEAD_DIM:(l + 1) * HEAD_DIM]
        head = j * heads_per_tile + l
        tr = t * cos + pltpu.roll(t, HEAD_DIM // 2, 1) * sin
        t = jnp.where(head < n_rope, tr, t)
        t = t * jnp.where(head < n_q, q_scale, 1.0)
        o_ref[:, l * HEAD_DIM:(l + 1) * HEAD_DIM] = t.astype(BF16)


def _qkv_proj(u, g, sc, sh, w, cos_u, sin_u, *, tm, sub, n_q, n_kv):
    n_rows, d = u.shape
    n = w.shape[1]
    tn = min(1024, n)
    nsb = tm // sub
    kern = functools.partial(_qkv_kernel, sub=sub, n_q=n_q, n_rope=n_q + n_kv,
                             q_scale=HEAD_DIM ** -0.5)
    return pl.pallas_call(
        kern,
        grid=(n_rows // tm, n // tn),
        in_specs=[pl.BlockSpec((tm, d), lambda i, j: (i, 0)),
                  pl.BlockSpec((1, d), lambda i, j: (0, 0)),
                  pl.BlockSpec((nsb, 1, d), lambda i, j: (i, 0, 0)),
                  pl.BlockSpec((nsb, 1, d), lambda i, j: (i, 0, 0)),
                  pl.BlockSpec((d, tn), lambda i, j: (0, j)),
                  pl.BlockSpec((tm, HEAD_DIM), lambda i, j: (i, 0)),
                  pl.BlockSpec((tm, HEAD_DIM), lambda i, j: (i, 0))],
        out_specs=pl.BlockSpec((tm, tn), lambda i, j: (i, j)),
        out_shape=jax.ShapeDtypeStruct((n_rows, n), BF16),
        scratch_shapes=[pltpu.VMEM((tm, d), BF16)],
        compiler_params=_params(("parallel", "arbitrary")),
        name="attn_qkv",
    )(u, g, sc, sh, w, cos_u, sin_u)


def _softmax_pv(s, sink, v):
    m = jnp.maximum(jnp.max(s, axis=-1, keepdims=True), sink)
    p = jnp.exp(s - m)
    l = jnp.sum(p, axis=-1, keepdims=True) + jnp.exp(sink - m)
    o = jnp.dot(p.astype(BF16), v, preferred_element_type=F32)
    return o * (1.0 / l)


def _stack_heads(q, group):
    return jnp.concatenate([q[:, g * HEAD_DIM:(g + 1) * HEAD_DIM] for g in range(group)], axis=0)


def _attn_x_kernel(q_ref, kp_ref, ko_ref, kn_ref, kc_ref, vp_ref, vo_ref, vn_ref, vc_ref,
                   mask_ref, sink_ref, o_ref, *, group, n_kv):
    blk = q_ref.shape[0]
    valid = mask_ref[...] > 0.0

    def scores(h):
        kv = slice(h * HEAD_DIM, (h + 1) * HEAD_DIM)
        qs = _stack_heads(q_ref[:, h * group * HEAD_DIM:(h + 1) * group * HEAD_DIM], group)
        k = jnp.concatenate([kp_ref[:, kv], ko_ref[:, kv], kn_ref[:, kv], kc_ref[:, kv]], axis=0)
        s = lax.dot_general(qs, k, (((1,), (1,)), ((), ())), preferred_element_type=F32)
        return jnp.concatenate([jnp.where(valid, s[:, :3 * blk], NEG_INF), s[:, 3 * blk:]], axis=1)

    def finish(h, s):
        kv = slice(h * HEAD_DIM, (h + 1) * HEAD_DIM)
        v = jnp.concatenate([vp_ref[:, kv], vo_ref[:, kv], vn_ref[:, kv], vc_ref[:, kv]], axis=0)
        o = _softmax_pv(s, sink_ref[h], v)
        for g in range(group):
            c0 = (h * group + g) * HEAD_DIM
            o_ref[:, c0:c0 + HEAD_DIM] = o[g * blk:(g + 1) * blk, :].astype(BF16)

    s_next = scores(0)
    for h in range(n_kv):
        s_cur = s_next
        if h + 1 < n_kv:
            s_next = scores(h + 1)
        finish(h, s_cur)


def _attn_c_kernel(q_ref, kc_ref, vc_ref, sink_ref, o_ref, *, group):
    lc = q_ref.shape[0]
    qs = _stack_heads(q_ref[...], group)
    s = lax.dot_general(qs, kc_ref[...], (((1,), (1,)), ((), ())), preferred_element_type=F32)
    o = _softmax_pv(s, sink_ref[...], vc_ref[...])
    for g in range(group):
        o_ref[:, g * HEAD_DIM:(g + 1) * HEAD_DIM] = o[g * lc:(g + 1) * lc, :].astype(BF16)


def _window_mask(group, blk, n_blocks):
    qi = np.arange(group * blk)[:, None] % blk
    kj = np.arange(3 * blk)[None, :]
    band = (kj - qi >= 0) & (kj - qi <= 2 * WINDOW)
    first = band & (kj >= blk)
    last = band & (kj < 2 * blk)
    kinds = [first & last if n_blocks == 1 else first, band, last]
    return np.stack(kinds).astype(np.float32)


def _attention(qkv, sink, *, batch, seq, ctx, n_q, n_kv):
    n_rows = qkv.shape[0]
    group = n_q // n_kv
    blk = WINDOW
    p_rows = ctx + seq
    bpb = p_rows // blk
    cb = ctx // blk
    nb = seq // blk
    last_blk = n_rows // blk - 1
    qw = n_q * HEAD_DIM
    kw = n_kv * HEAD_DIM
    assert qw % kw == 0
    k_col = qw // kw
    spc = p_rows // ctx
    gw = group * HEAD_DIM
    sink_g = sink.reshape(n_kv, group, 1).astype(F32)
    sink_x = jnp.broadcast_to(sink_g[:, :, None, :], (n_kv, group, blk, 1)).reshape(n_kv, group * blk, 1)
    sink_c = jnp.broadcast_to(sink_g[:, :, None, :], (n_kv, group, ctx, 1)).reshape(n_kv, group * ctx, 1)
    mask = jnp.asarray(_window_mask(group, blk, nb))

    def kv_specs(col):
        return [pl.BlockSpec((blk, kw), lambda b, n: (jnp.maximum(b * bpb + cb + n - 1, 0), col)),
                pl.BlockSpec((blk, kw), lambda b, n: (b * bpb + cb + n, col)),
                pl.BlockSpec((blk, kw), lambda b, n: (jnp.minimum(b * bpb + cb + n + 1, last_blk), col)),
                pl.BlockSpec((ctx, kw), lambda b, n: (b * spc, col))]

    out_x = pl.pallas_call(
        functools.partial(_attn_x_kernel, group=group, n_kv=n_kv),
        grid=(batch, nb),
        in_specs=[pl.BlockSpec((blk, qw), lambda b, n: (b * bpb + cb + n, 0))]
        + kv_specs(k_col) + kv_specs(k_col + 1)
        + [pl.BlockSpec((None, group * blk, 3 * blk),
                        lambda b, n: (jnp.where(n == 0, 0, jnp.where(n == nb - 1, 2, 1)), 0, 0)),
           pl.BlockSpec((n_kv, group * blk, 1), lambda b, n: (0, 0, 0))],
        out_specs=pl.BlockSpec((blk, qw), lambda b, n: (b * bpb + cb + n, 0)),
        out_shape=jax.ShapeDtypeStruct((n_rows, qw), BF16),
        compiler_params=_params(("parallel", "parallel")),
        name="attn_latent",
    )(qkv, qkv, qkv, qkv, qkv, qkv, qkv, qkv, qkv, mask, sink_x)

    out = pl.pallas_call(
        functools.partial(_attn_c_kernel_alias, group=group),
        grid=(batch, n_kv),
        in_specs=[pl.BlockSpec((ctx, gw), lambda b, h: (b * spc, h)),
                  pl.BlockSpec((ctx, HEAD_DIM), lambda b, h: (b * spc, n_q + h)),
                  pl.BlockSpec((ctx, HEAD_DIM), lambda b, h: (b * spc, n_q + n_kv + h)),
                  pl.BlockSpec((None, group * ctx, 1), lambda b, h: (h, 0, 0)),
                  pl.BlockSpec(memory_space=pl.ANY)],
        out_specs=pl.BlockSpec((ctx, gw), lambda b, h: (b * spc, h)),
        out_shape=jax.ShapeDtypeStruct((n_rows, qw), BF16),
        input_output_aliases={4: 0},
        compiler_params=_params(("parallel", "parallel")),
        name="attn_context",
    )(qkv, qkv, qkv, sink_c, out_x)
    return out


def _attn_c_kernel_alias(q_ref, kc_ref, vc_ref, sink_ref, prev_ref, o_ref, *, group):
    del prev_ref
    _attn_c_kernel(q_ref, kc_ref, vc_ref, sink_ref, o_ref, group=group)


def _proj_res_kernel(y_ref, w_ref, x_ref, gt_ref, g_ref, sc_ref, sh_ref, o_ref, h_ref, *, sub):
    acc = jnp.dot(y_ref[...], w_ref[...], preferred_element_type=F32)
    g = g_ref[...]
    for s in range(o_ref.shape[0] // sub):
        rows = slice(s * sub, (s + 1) * sub)
        xn = x_ref[rows, :] + gt_ref[s] * acc[rows, :]
        o_ref[rows, :] = xn
        h_ref[rows, :] = _norm_mod(xn, g, sc_ref[s], sh_ref[s]).astype(BF16)


def _proj_residual(y, w, u, gt, g, sc, sh, *, tm, sub):
    n_rows, d = u.shape
    k = y.shape[1]
    nsb = tm // sub
    sb_spec = pl.BlockSpec((nsb, 1, d), lambda i: (i, 0, 0))
    return pl.pallas_call(
        functools.partial(_proj_res_kernel, sub=sub),
        grid=(n_rows // tm,),
        in_specs=[pl.BlockSpec((tm, k), lambda i: (i, 0)),
                  pl.BlockSpec((k, d), lambda i: (0, 0)),
                  pl.BlockSpec((tm, d), lambda i: (i, 0)),
                  sb_spec,
                  pl.BlockSpec((1, d), lambda i: (0, 0)),
                  sb_spec, sb_spec],
        out_specs=[pl.BlockSpec((tm, d), lambda i: (i, 0)),
                   pl.BlockSpec((tm, d), lambda i: (i, 0))],
        out_shape=[jax.ShapeDtypeStruct((n_rows, d), F32),
                   jax.ShapeDtypeStruct((n_rows, d), BF16)],
        compiler_params=_params(("parallel",)),
        name="proj_residual",
    )(y, w, u, gt, g, sc, sh)


def _mlstm_in_kernel(x_ref, g_ref, sc_ref, sh_ref, w_ref, b_ref, wg_ref, bg_ref, z_ref, gate_ref,
                     h_scr, *, sub, k_lo, k_hi, k_scale):
    j = pl.program_id(1)
    tm, tn = z_ref.shape

    @pl.when(j == 0)
    def _():
        _norm_mod_to(h_scr, x_ref, g_ref, sc_ref, sh_ref, sub)
        gate_ref[...] = jnp.dot(h_scr[...], wg_ref[...], preferred_element_type=F32) + bg_ref[...]

    acc = jnp.dot(h_scr[...], w_ref[...], preferred_element_type=F32) + b_ref[...]
    col0 = j * tn
    is_k = (col0 >= k_lo) & (col0 < k_hi)
    z_ref[...] = (acc * jnp.where(is_k, k_scale, 1.0)).astype(BF16)


def _mlstm_in_proj(u, g, sc, sh, w, b, wg, bg, *, tm, sub, dqk):
    n_rows, d = u.shape
    n = 3 * d
    qk_cols = d // 2
    tn = min(1024, qk_cols)
    nsb = tm // sub
    kern = functools.partial(_mlstm_in_kernel, sub=sub, k_lo=qk_cols, k_hi=2 * qk_cols,
                             k_scale=dqk ** -0.5)
    return pl.pallas_call(
        kern,
        grid=(n_rows // tm, n // tn),
        in_specs=[pl.BlockSpec((tm, d), lambda i, j: (i, 0)),
                  pl.BlockSpec((1, d), lambda i, j: (0, 0)),
                  pl.BlockSpec((nsb, 1, d), lambda i, j: (i, 0, 0)),
                  pl.BlockSpec((nsb, 1, d), lambda i, j: (i, 0, 0)),
                  pl.BlockSpec((d, tn), lambda i, j: (0, j)),
                  pl.BlockSpec((1, tn), lambda i, j: (0, j)),
                  pl.BlockSpec((d, 128), lambda i, j: (0, 0)),
                  pl.BlockSpec((1, 128), lambda i, j: (0, 0))],
        out_specs=[pl.BlockSpec((tm, tn), lambda i, j: (i, j)),
                   pl.BlockSpec((tm, 128), lambda i, j: (i, 0))],
        out_shape=[jax.ShapeDtypeStruct((n_rows, n), BF16),
                   jax.ShapeDtypeStruct((n_rows, 128), F32)],
        scratch_shapes=[pltpu.VMEM((tm, d), BF16)],
        compiler_params=_params(("parallel", "arbitrary")),
        name="mlstm_in",
    )(u, g, sc, sh, w, b, wg, bg)


GATE_F_LANE = 64
N_GATE_OUT = 6
N_COL = 5


def _chunk_scan(x, scr, pos, op, ident, reverse):
    r = x.shape[0]
    sh = 1
    while sh < CHUNK:
        scr[CHUNK:CHUNK + r, :] = x
        if reverse:
            other = scr[CHUNK + sh:CHUNK + sh + r, :]
            ok = pos < CHUNK - sh
        else:
            other = scr[CHUNK - sh:CHUNK - sh + r, :]
            ok = pos >= sh
        x = op(x, jnp.where(ok, other, ident))
        sh *= 2
    return x


def _mlstm_gate_kernel(g_ref, o_ref, b_scr, cm_scr, shift_scr, *, n_heads, n_ctx_chunks, tile):
    p_rows = g_ref.shape[0]
    n_chunks = p_rows // CHUNK
    shift_scr[...] = jnp.zeros_like(shift_scr)
    o_ref[0:N_COL] = jnp.zeros((N_COL,) + o_ref.shape[1:], F32)
    fwd_tile = lax.broadcasted_iota(jnp.int32, (tile, LANE), 1) < n_heads
    pos = lax.broadcasted_iota(jnp.int32, (tile, LANE), 0) % CHUNK

    def local(t, _):
        rows = pl.ds(pl.multiple_of(t * tile, tile), tile)
        g = g_ref[rows, :]
        lf = _log_sigmoid(pltpu.roll(g, LANE - GATE_F_LANE, 1))
        b = jnp.where(fwd_tile, _chunk_scan(lf, shift_scr, pos, jnp.add, 0.0, False),
                      _chunk_scan(lf, shift_scr, pos, jnp.add, 0.0, True))
        a = g - b
        cm = jnp.where(fwd_tile, _chunk_scan(a, shift_scr, pos, jnp.maximum, -jnp.inf, False),
                       _chunk_scan(a, shift_scr, pos, jnp.maximum, -jnp.inf, True))
        b_scr[rows, :] = b
        cm_scr[rows, :] = cm
        o_ref[N_COL, rows, :] = a
        return 0

    lax.fori_loop(0, p_rows // tile, local, 0)

    fwd_chunk = lax.broadcasted_iota(jnp.int32, (CHUNK, LANE), 1) < n_heads

    def step(c, m, forward):
        rows = pl.ds(pl.multiple_of(c * CHUNK, CHUNK), CHUNK)
        end_row = CHUNK - 1 if forward else 0
        mine = fwd_chunk if forward else jnp.logical_not(fwd_chunk)
        b = b_scr[rows, :]
        a = o_ref[N_COL, rows, :]
        cm = cm_scr[rows, :]
        b_end = b[end_row:end_row + 1, :]
        m_t = b + jnp.maximum(m, cm)
        m_new = jnp.maximum(b_end + m, b_end + cm[end_row:end_row + 1, :])
        vals = (b - m_t, jnp.exp(b + m - m_t), jnp.exp(-m_t), jnp.exp(b_end + a - m_new),
                jnp.broadcast_to(jnp.exp(b_end + m - m_new), (CHUNK, LANE)))
        for q, val in enumerate(vals):
            o_ref[q, rows, :] = jnp.where(mine, val, o_ref[q, rows, :])
        return m_new

    def body(j, carry):
        mf, mb = carry
        mf = step(j, mf, True)
        cb = jnp.where(j < n_ctx_chunks, n_ctx_chunks - 1 - j, n_chunks - 1 + n_ctx_chunks - j)
        mb = step(cb, mb, False)
        return mf, mb

    zm = jnp.zeros((1, LANE), F32)
    lax.fori_loop(0, n_chunks, body, (zm, zm))


def _mlstm_gates(gates, *, batch, p_rows, ctx, n_heads):
    n_rows = gates.shape[0]
    tile = ctx
    kern = functools.partial(_mlstm_gate_kernel, n_heads=n_heads, n_ctx_chunks=ctx // CHUNK, tile=tile)
    return pl.pallas_call(
        kern,
        grid=(batch,),
        in_specs=[pl.BlockSpec((p_rows, LANE), lambda b: (b, 0))],
        out_specs=pl.BlockSpec((N_GATE_OUT, p_rows, LANE), lambda b: (0, b, 0)),
        out_shape=jax.ShapeDtypeStruct((N_GATE_OUT, n_rows, LANE), F32),
        scratch_shapes=[pltpu.VMEM((p_rows, LANE), F32), pltpu.VMEM((p_rows, LANE), F32),
                        pltpu.VMEM((tile + 2 * CHUNK, LANE), F32)],
        compiler_params=_params(("parallel",)),
        name="mlstm_gates",
    )(gates)


def _mlstm_chunk(q, k, v, col, a_row, c_scr, n, d, reverse):
    ch = q.shape[0]
    t_idx = lax.broadcasted_iota(jnp.int32, (ch, ch), 0)
    s_idx = lax.broadcasted_iota(jnp.int32, (ch, ch), 1)
    seen = (s_idx >= t_idx) if reverse else (s_idx <= t_idx)
    c0 = d * N_COL
    row_term = col[:, c0:c0 + 1]
    e_inter = col[:, c0 + 1:c0 + 2]
    floor = col[:, c0 + 2:c0 + 3]
    wg_col = col[:, c0 + 3:c0 + 4]
    e_prev = col[0:1, c0 + 4:c0 + 5]

    w = jnp.where(seen, jnp.exp(row_term + a_row), 0.0)
    s = lax.dot_general(q, k, (((1,), (1,)), ((), ())), preferred_element_type=F32) * w
    c_old = c_scr[...]
    num = (e_inter * jnp.dot(q, c_old.astype(BF16), preferred_element_type=F32)
           + jnp.dot(s.astype(BF16), v, preferred_element_type=F32))
    qn = jnp.sum(q.astype(F32) * n, axis=1, keepdims=True)
    den = e_inter * qn + jnp.sum(s, axis=1, keepdims=True)
    h = num * (1.0 / jnp.maximum(jnp.abs(den), floor))

    vw = (wg_col * v.astype(F32)).astype(BF16)
    c_scr[...] = e_prev * c_old + lax.dot_general(k, vw, (((0,), (0,)), ((), ())),
                                                  preferred_element_type=F32)
    n_new = e_prev * n + jnp.sum(wg_col * k.astype(F32), axis=0, keepdims=True)
    return h, n_new


def _mlstm_core_kernel(q_ref, k_ref, v_ref, o_ref, gr_ref, gc_ref, gh_ref, y_ref,
                       hf_scr, hb_scr, cf_scr, cb_scr, *, n_ctx_chunks, out_rows):
    n_chunks = gr_ref.shape[0]
    dqk = q_ref.shape[1]
    cf_scr[...] = jnp.zeros_like(cf_scr)
    cb_scr[...] = jnp.zeros_like(cb_scr)

    def run(c, c_scr, h_scr, n, d, reverse):
        rows = pl.ds(pl.multiple_of(c * CHUNK, CHUNK), CHUNK)
        h, n = _mlstm_chunk(q_ref[rows, :], k_ref[rows, :], v_ref[rows, :], gc_ref[rows, :],
                            gr_ref[c][d:d + 1, :], c_scr, n, d, reverse)
        h_scr[rows, :] = h
        return n

    def body(j, carry):
        nf, nb = carry
        nf = run(j, cf_scr, hf_scr, nf, 0, False)
        cb = jnp.where(j < n_ctx_chunks, n_ctx_chunks - 1 - j, n_chunks - 1 + n_ctx_chunks - j)
        nb = run(cb, cb_scr, hb_scr, nb, 1, True)
        return nf, nb

    zn = jnp.zeros((1, dqk), F32)
    lax.fori_loop(0, n_chunks, body, (zn, zn), unroll=2)

    gh = gh_ref[...]

    def finish(r, _):
        rows = pl.ds(pl.multiple_of(r * out_rows, out_rows), out_rows)
        h = hf_scr[rows, :] + hb_scr[rows, :]
        hn = h * lax.rsqrt(jnp.mean(h * h, axis=-1, keepdims=True) + RMS_EPS)
        y_ref[rows, :] = (_sigmoid(o_ref[rows, :].astype(F32)) * hn * gh).astype(BF16)
        return 0

    lax.fori_loop(0, y_ref.shape[0] // out_rows, finish, 0)


def _mlstm_core(z, gates_r, gates_c, g_head, *, batch, p_rows, ctx, n_heads, dqk, dv):
    n_rows = z.shape[0]
    n_chunks = p_rows // CHUNK
    kern = functools.partial(_mlstm_core_kernel, n_ctx_chunks=ctx // CHUNK, out_rows=ctx)
    return pl.pallas_call(
        kern,
        grid=(batch, n_heads),
        in_specs=[pl.BlockSpec((p_rows, dqk), lambda b, h: (b, h)),
                  pl.BlockSpec((p_rows, dqk), lambda b, h: (b, n_heads + h)),
                  pl.BlockSpec((p_rows, dv), lambda b, h: (b, n_heads + h)),
                  pl.BlockSpec((p_rows, dv), lambda b, h: (b, 2 * n_heads + h)),
                  pl.BlockSpec((None, None, n_chunks, 2, CHUNK), lambda b, h: (b, h, 0, 0, 0)),
                  pl.BlockSpec((None, None, p_rows, 2 * N_COL), lambda b, h: (b, h, 0, 0)),
                  pl.BlockSpec((1, dv), lambda b, h: (0, h))],
        out_specs=pl.BlockSpec((p_rows, dv), lambda b, h: (b, h)),
        out_shape=jax.ShapeDtypeStruct((n_rows, n_heads * dv), BF16),
        scratch_shapes=[pltpu.VMEM((p_rows, dv), F32), pltpu.VMEM((p_rows, dv), F32),
                        pltpu.VMEM((dqk, dv), F32), pltpu.VMEM((dqk, dv), F32)],
        compiler_params=_params(("parallel", "parallel")),
        name="mlstm_core",
    )(z, z, z, z, gates_r, gates_c, g_head)


def _gate_layouts(og, *, batch, p_rows, n_heads):
    g = og[:, :, :2 * n_heads].reshape(N_GATE_OUT, batch, p_rows, 2, n_heads)
    cols = g[:N_COL].transpose(1, 4, 2, 3, 0).reshape(batch, n_heads, p_rows, 2 * N_COL)
    rows = g[N_COL].reshape(batch, p_rows // CHUNK, CHUNK, 2, n_heads).transpose(0, 4, 1, 3, 2)
    return rows, cols


def _halo_up_kernel(h_ref, w_ref, o_ref):
    o_ref[...] = jnp.dot(h_ref[...], w_ref[...], preferred_element_type=F32)


def _halo_up(h, w_up, *, tm):
    n_rows, d = h.shape
    n = w_up.shape[1]
    nt = n_rows // tm
    ht = h.reshape(nt, tm, d)
    first, last = ht[:, 0], ht[:, tm - 1]
    prev = jnp.concatenate([last[:1], last[:nt - 1]], axis=0)
    nxt = jnp.concatenate([first[1:], first[nt - 1:]], axis=0)
    rows = -(-2 * nt // BF16_ROWS) * BF16_ROWS
    hh = jnp.concatenate([prev, nxt, jnp.zeros((rows - 2 * nt, d), BF16)], axis=0)
    tn = 2 * FFN_CHUNK
    uh = pl.pallas_call(
        _halo_up_kernel,
        grid=(n // tn,),
        in_specs=[pl.BlockSpec((rows, d), lambda j: (0, 0)),
                  pl.BlockSpec((d, tn), lambda j: (0, j))],
        out_specs=pl.BlockSpec((rows, tn), lambda j: (0, j)),
        out_shape=jax.ShapeDtypeStruct((rows, n), F32),
        compiler_params=_params(("parallel",)),
        name="ffn_halo",
    )(hh, w_up)
    uh = uh[:2 * nt].reshape(2, nt, n).transpose(1, 0, 2)
    return jnp.pad(uh, ((0, 0), (0, HALO - 2), (0, 0)))


def _ffn_kernel(h_ref, x_ref, gt_ref, wa0_ref, wb0_ref, wa1_ref, wb1_ref,
                ha0_ref, hb0_ref, ha1_ref, hb1_ref, cwa_ref, cwb_ref, cba_ref, cbb_ref, wd_ref,
                o_ref, acc_scr, u0_scr, u1_scr, a0_scr, a1_scr, *, sub, p_rows, ctx):
    i = pl.program_id(0)
    j = pl.program_id(1)
    tm = o_ref.shape[0]
    nsb = tm // sub
    fh = wa0_ref.shape[1]
    lt = fh // LANE
    seg = sub + 2 * HALO

    @pl.when(j == 0)
    def _():
        acc_scr[...] = jnp.zeros_like(acc_scr)

    def edge_open(k):
        pos = (i * tm + k * sub) % p_rows
        return jnp.where((pos == 0) | (pos == ctx), 0.0, 1.0)

    opens = [edge_open(k) for k in range(nsb + 1)]

    def up(w_refs, halo_refs, u_scr):
        for br, (w_ref, halo_ref) in enumerate(zip(w_refs, halo_refs)):
            ue = jnp.dot(h_ref[...], w_ref[...], preferred_element_type=F32)
            halo = halo_ref[...]
            for s in range(nsb):
                r0 = s * sub
                b0 = s * seg
                for t in range(lt):
                    lanes = slice(t * LANE, (t + 1) * LANE)
                    tt = br * lt + t
                    before = (jnp.broadcast_to(halo[0:1, lanes], (HALO, LANE)) if s == 0
                              else ue[r0 - HALO:r0, lanes])
                    after = (jnp.broadcast_to(halo[1:2, lanes], (HALO, LANE)) if s == nsb - 1
                             else ue[r0 + sub:r0 + sub + HALO, lanes])
                    u_scr[tt, b0:b0 + HALO, :] = before * opens[s]
                    u_scr[tt, b0 + HALO:b0 + HALO + sub, :] = ue[r0:r0 + sub, lanes]
                    u_scr[tt, b0 + HALO + sub:b0 + seg, :] = after * opens[s + 1]

    def gate(c, u_scr, act_scr):
        cols = slice(c * fh, (c + 1) * fh)
        cws = (cwa_ref[:, cols], cwb_ref[:, cols])
        cbs = (cba_ref[:, cols], cbb_ref[:, cols])

        def conv_piece(br, t, r0):
            lanes = slice(t * LANE, (t + 1) * LANE)
            tt = br * lt + t + jnp.minimum(j, 0)
            cw, cb = cws[br], cbs[br]
            cur = u_scr[tt, r0:r0 + GATE_ROWS, :]
            prv = u_scr[tt, r0 - 1:r0 - 1 + GATE_ROWS, :]
            nxt = u_scr[tt, r0 + 1:r0 + 1 + GATE_ROWS, :]
            return cb[:, lanes] + prv * cw[0:1, lanes] + cur * cw[1:2, lanes] + nxt * cw[2:3, lanes]

        for s in range(nsb):
            for r in range(sub // GATE_ROWS):
                r0 = s * seg + HALO + r * GATE_ROWS
                o0 = s * sub + r * GATE_ROWS
                for t in range(lt):
                    a = conv_piece(0, t, r0)
                    b = conv_piece(1, t, r0)
                    act_scr[o0:o0 + GATE_ROWS, t * LANE:(t + 1) * LANE] = (a * _sigmoid(a) * b).astype(BF16)

    def down(c, act_scr):
        acc_scr[...] += jnp.dot(act_scr[...], wd_ref[c * fh:(c + 1) * fh, :], preferred_element_type=F32)

    up((wa0_ref, wb0_ref), (ha0_ref, hb0_ref), u0_scr)
    up((wa1_ref, wb1_ref), (ha1_ref, hb1_ref), u1_scr)
    gate(0, u0_scr, a0_scr)
    down(0, a0_scr)
    gate(1, u1_scr, a1_scr)
    down(1, a1_scr)

    @pl.when(j == pl.num_programs(1) - 1)
    def _():
        for s in range(nsb):
            rows = slice(s * sub, (s + 1) * sub)
            o_ref[rows, :] = x_ref[rows, :] + gt_ref[s] * acc_scr[rows, :]


def _conv_ffn(h, u, gt, w_up, conv_w, conv_b, w_down, *, tm, sub, p_rows, ctx):
    n_rows, d = u.shape
    d_ff = w_down.shape[0]
    fc = FFN_CHUNK
    assert d_ff % fc == 0
    fh = fc // 2
    nfc = d_ff // fc
    nsb = tm // sub
    kern = functools.partial(_ffn_kernel, sub=sub, p_rows=p_rows, ctx=ctx)
    u_shape = (fc // LANE, nsb * (sub + 2 * HALO), LANE)
    halo = _halo_up(h, w_up, tm=tm)

    def half_specs(shape, lead):
        return [pl.BlockSpec(shape, lambda i, j, c=c: lead(i) + (c + 2 * j,))
                for c in (0, 2 * nfc, 1, 2 * nfc + 1)]

    return pl.pallas_call(
        kern,
        grid=(n_rows // tm, nfc),
        in_specs=[pl.BlockSpec((tm, d), lambda i, j: (i, 0)),
                  pl.BlockSpec((tm, d), lambda i, j: (i, 0)),
                  pl.BlockSpec((nsb, 1, d), lambda i, j: (i, 0, 0))]
        + half_specs((d, fh), lambda i: (0,))
        + half_specs((None, HALO, fh), lambda i: (i, 0))
        + [pl.BlockSpec((CONV_W, fc), lambda i, j: (0, j)),
           pl.BlockSpec((CONV_W, fc), lambda i, j: (0, nfc + j)),
           pl.BlockSpec((1, fc), lambda i, j: (0, j)),
           pl.BlockSpec((1, fc), lambda i, j: (0, nfc + j)),
           pl.BlockSpec((fc, d), lambda i, j: (j, 0))],
        out_specs=pl.BlockSpec((tm, d), lambda i, j: (i, 0)),
        out_shape=jax.ShapeDtypeStruct((n_rows, d), F32),
        scratch_shapes=[pltpu.VMEM((tm, d), F32), pltpu.VMEM(u_shape, F32), pltpu.VMEM(u_shape, F32),
                        pltpu.VMEM((tm, fh), BF16), pltpu.VMEM((tm, fh), BF16)],
        compiler_params=_params(("parallel", "arbitrary")),
        name="conv_ffn",
    )(h, u, gt, w_up, w_up, w_up, w_up, halo, halo, halo, halo,
      conv_w, conv_w, conv_b, conv_b, w_down)


def _final_norm_kernel(x_ref, g_ref, o_ref):
    xv = x_ref[...]
    o_ref[...] = xv * lax.rsqrt(jnp.mean(xv * xv, axis=-1, keepdims=True) + RMS_EPS) * g_ref[...]


def _final_norm(u, g, *, batch, seq, ctx, sub):
    d = u.shape[1]
    spb = (ctx + seq) // sub
    cs = ctx // sub
    return pl.pallas_call(
        _final_norm_kernel,
        grid=(batch, seq // sub),
        in_specs=[pl.BlockSpec((sub, d), lambda b, t: (b * spb + cs + t, 0)),
                  pl.BlockSpec((1, d), lambda b, t: (0, 0))],
        out_specs=pl.BlockSpec((None, sub, d), lambda b, t: (b, t, 0)),
        out_shape=jax.ShapeDtypeStruct((batch, seq, d), F32),
        compiler_params=_params(("parallel", "parallel")),
        name="final_norm",
    )(u, g)


def _rope_tables(batch, seq, ctx):
    rows = seq // GRID_W
    row = jnp.repeat(jnp.arange(rows, dtype=F32), GRID_W)
    col = jnp.tile(jnp.arange(GRID_W, dtype=F32), rows)
    n_freq = HEAD_DIM // 4
    inv_freq = ROPE_THETA ** (-jnp.arange(n_freq, dtype=F32) / n_freq)
    ang = jnp.concatenate([row[:, None] * inv_freq, col[:, None] * inv_freq], axis=-1)
    ang = jnp.concatenate([ang, ang], axis=-1)
    sign = jnp.concatenate([-jnp.ones((HEAD_DIM // 2,), F32), jnp.ones((HEAD_DIM // 2,), F32)])
    cos = jnp.concatenate([jnp.ones((ctx, HEAD_DIM), F32), jnp.cos(ang)], axis=0)
    sin = jnp.concatenate([jnp.zeros((ctx, HEAD_DIM), F32), jnp.sin(ang) * sign], axis=0)
    return jnp.tile(cos, (batch, 1)), jnp.tile(sin, (batch, 1))


def kernel(x, c, ctx, c_ctx, w_mod, b_mod, g_mix, g_ffn, attn_w_qkv, attn_sink, attn_w_o,
           mlstm_w_in, mlstm_b_in, mlstm_g_head, mlstm_w_o, ffn_w_up, ffn_conv_w, ffn_conv_b,
           ffn_w_down, g_final):
    batch, seq, d = x.shape
    lc = ctx.shape[1]
    depth = w_mod.shape[0]
    p_rows = lc + seq
    sub = lc
    assert seq % sub == 0 and sub % WINDOW == 0 and sub % CHUNK == 0 and seq % GRID_W == 0
    n_sub = batch * p_rows // sub
    tm = 2 * sub if n_sub % 2 == 0 else sub
    n_q = attn_sink.shape[1]
    n_kv = (attn_w_qkv.shape[2] // HEAD_DIM - n_q) // 2
    n_heads = (mlstm_w_in.shape[2] - 3 * d) // 4
    dv = d // n_heads
    dqk = dv // 2
    main_cols = 3 * d

    u = jnp.concatenate([ctx, x], axis=1).reshape(batch * p_rows, d)

    n_c = batch + 1
    c_rows = -(-n_c // 8) * 8
    cvec = jnp.concatenate([c, c_ctx[None, :], jnp.zeros((c_rows - n_c, d), F32)], axis=0)
    mods = _modulation(cvec, w_mod, b_mod)
    mods_sb = jnp.concatenate(
        [jnp.broadcast_to(mods[:, None, batch:batch + 1], (depth, batch, lc // sub, 6 * d)),
         jnp.broadcast_to(mods[:, :batch, None], (depth, batch, seq // sub, 6 * d))],
        axis=2).reshape(depth, n_sub, 6, 1, d)

    cos_u, sin_u = _rope_tables(batch, seq, lc)

    for i in range(depth):
        jm = i // 2
        sh1, sc1, gt1, sh2, sc2, gt2 = (mods_sb[i, :, k] for k in range(6))
        gm = g_mix[i].reshape(1, d)
        if i % 2 == 0:
            qkv = _qkv_proj(u, gm, sc1, sh1, attn_w_qkv[jm].astype(BF16), cos_u, sin_u,
                            tm=tm, sub=sub, n_q=n_q, n_kv=n_kv)
            y = _attention(qkv, attn_sink[jm], batch=batch, seq=seq, ctx=lc, n_q=n_q, n_kv=n_kv)
            w_o = attn_w_o[jm]
        else:
            w_in = mlstm_w_in[jm]
            b_in = mlstm_b_in[jm]
            wgt = w_in[:, main_cols:].reshape(d, 4, n_heads)
            bgt = b_in[main_cols:].reshape(4, n_heads)
            lane_pad = GATE_F_LANE - 2 * n_heads
            wg = jnp.concatenate([wgt[:, 0], wgt[:, 2], jnp.zeros((d, lane_pad), F32),
                                  wgt[:, 1], wgt[:, 3], jnp.zeros((d, lane_pad), F32)], axis=1).astype(BF16)
            bg = jnp.concatenate([bgt[0], bgt[2], jnp.zeros((lane_pad,), F32),
                                  bgt[1], bgt[3], jnp.zeros((lane_pad,), F32)]).reshape(1, LANE)
            z, gates = _mlstm_in_proj(u, gm, sc1, sh1, w_in.astype(BF16), b_in.reshape(1, -1), wg, bg,
                                      tm=tm, sub=sub, dqk=dqk)
            og = _mlstm_gates(gates, batch=batch, p_rows=p_rows, ctx=lc, n_heads=n_heads)
            gates_r, gates_c = _gate_layouts(og, batch=batch, p_rows=p_rows, n_heads=n_heads)
            y = _mlstm_core(z, gates_r, gates_c, mlstm_g_head[jm].reshape(1, n_heads * dv),
                            batch=batch, p_rows=p_rows, ctx=lc, n_heads=n_heads, dqk=dqk, dv=dv)
            w_o = mlstm_w_o[jm]
        u, h2 = _proj_residual(y, w_o.astype(BF16), u, gt1, g_ffn[i].reshape(1, d), sc2, sh2,
                               tm=tm, sub=sub)
        u = _conv_ffn(h2, u, gt2, ffn_w_up[i].astype(BF16), ffn_conv_w[i],
                      ffn_conv_b[i].reshape(1, -1), ffn_w_down[i].astype(BF16),
                      tm=tm, sub=sub, p_rows=p_rows, ctx=lc)
    return _final_norm(u, g_final.reshape(1, d), batch=batch, seq=seq, ctx=lc, sub=sub)
```

```python
import functools

import numpy as np
import jax
import jax.numpy as jnp
from jax import lax
from jax.experimental import pallas as pl
from jax.experimental.pallas import tpu as pltpu

F32 = jnp.float32
BF16 = jnp.bfloat16

RMS_EPS = 1e-6
HEAD_DIM = 128
WINDOW = 128
GRID_W = 64
ROPE_THETA = 10000.0
NEG_INF = -1e30
CHUNK = 128
CONV_W = 3
LANE = 128
HALO = 8
BF16_ROWS = 16
FFN_CHUNK = 512
GATE_ROWS = 64
NORM_ROWS = 16
V7X_VMEM_BYTES = 64 * 1024 * 1024
VMEM_LIMIT = V7X_VMEM_BYTES * 3 // 4


def _params(sem):
    return pltpu.CompilerParams(dimension_semantics=sem, vmem_limit_bytes=VMEM_LIMIT)


def _sigmoid(v):
    return 1.0 / (1.0 + jnp.exp(-v))


def _log_sigmoid(v):
    return jnp.minimum(v, 0.0) - jnp.log(1.0 + jnp.exp(-jnp.abs(v)))


def _norm_mod(xv, g, scale, shift):
    ms = jnp.mean(xv * xv, axis=-1, keepdims=True)
    y = xv * lax.rsqrt(ms + RMS_EPS)
    return (y * g) * (1.0 + scale) + shift


def _norm_mod_to(h_scr, x_ref, g_ref, sc_ref, sh_ref, sub):
    for s in range(x_ref.shape[0] // sub):
        gain = g_ref[...] * (1.0 + sc_ref[s])
        shift = sh_ref[s]
        for r in range(s * sub, (s + 1) * sub, NORM_ROWS):
            xv = x_ref[r:r + NORM_ROWS, :]
            ms = jnp.mean(xv * xv, axis=-1, keepdims=True)
            h_scr[r:r + NORM_ROWS, :] = (xv * lax.rsqrt(ms + RMS_EPS) * gain + shift).astype(BF16)


def _mod_kernel(c_ref, w_ref, b_ref, o_ref):
    cv = c_ref[...]
    s = (cv * _sigmoid(cv)).astype(BF16)
    o_ref[...] = jnp.dot(s, w_ref[...].astype(BF16), preferred_element_type=F32) + b_ref[...]


def _modulation(cvec, w_mod, b_mod):
    depth, d, n = w_mod.shape
    r = cvec.shape[0]
    tn = min(1024, n)
    return pl.pallas_call(
        _mod_kernel,
        grid=(depth, n // tn),
        in_specs=[pl.BlockSpec((r, d), lambda i, j: (0, 0)),
                  pl.BlockSpec((None, d, tn), lambda i, j: (i, 0, j)),
                  pl.BlockSpec((None, 1, tn), lambda i, j: (i, 0, j))],
        out_specs=pl.BlockSpec((None, r, tn), lambda i, j: (i, 0, j)),
        out_shape=jax.ShapeDtypeStruct((depth, r, n), F32),
        compiler_params=_params(("parallel", "parallel")),
        name="modulation",
    )(cvec, w_mod, b_mod.reshape(depth, 1, n))


def _qkv_kernel(x_ref, g_ref, sc_ref, sh_ref, w_ref, cos_ref, sin_ref, o_ref, h_scr,
                *, sub, n_q, n_rope, q_scale):
    j = pl.program_id(1)
    tm, tn = o_ref.shape

    @pl.when(j == 0)
    def _():
        _norm_mod_to(h_scr, x_ref, g_ref, sc_ref, sh_ref, sub)

    acc = jnp.dot(h_scr[...], w_ref[...], preferred_element_type=F32)
    cos = cos_ref[...]
    sin = sin_ref[...]
    heads_per_tile = tn // HEAD_DIM
    for l in range(heads_per_tile):
        t = acc[:, l * HEAD_DIM:(l + 1) * HEAD_DIM]
        head = j * heads_per_tile + l
        tr = t * cos + pltpu.roll(t, HEAD_DIM // 2, 1) * sin
        t = jnp.where(head < n_rope, tr, t)
        t = t * jnp.where(head < n_q, q_scale, 1.0)
        o_ref[:, l * HEAD_DIM:(l + 1) * HEAD_DIM] = t.astype(BF16)


def _qkv_proj(u, g, sc, sh, w, cos_u, sin_u, *, tm, sub, n_q, n_kv):
    n_rows, d = u.shape
    n = w.shape[1]
    tn = min(1024, n)
    nsb = tm // sub
    kern = functools.partial(_qkv_kernel, sub=sub, n_q=n_q, n_rope=n_q + n_kv,
                             q_scale=HEAD_DIM ** -0.5)
    return pl.pallas_call(
        kern,
        grid=(n_rows // tm, n // tn),
        in_specs=[pl.BlockSpec((tm, d), lambda i, j: (i, 0)),
                  pl.BlockSpec((1, d), lambda i, j: (0, 0)),
                  pl.BlockSpec((nsb, 1, d), lambda i, j: (i, 0, 0)),
                  pl.BlockSpec((nsb, 1, d), lambda i, j: (i, 0, 0)),
                  pl.BlockSpec((d, tn), lambda i, j: (0, j)),
                  pl.BlockSpec((tm, HEAD_DIM), lambda i, j: (i, 0)),
                  pl.BlockSpec((tm, HEAD_DIM), lambda i, j: (i, 0))],
        out_specs=pl.BlockSpec((tm, tn), lambda i, j: (i, j)),
        out_shape=jax.ShapeDtypeStruct((n_rows, n), BF16),
        scratch_shapes=[pltpu.VMEM((tm, d), BF16)],
        compiler_params=_params(("parallel", "arbitrary")),
        name="attn_qkv",
    )(u, g, sc, sh, w, cos_u, sin_u)


def _softmax_pv(s, sink, v):
    m = jnp.maximum(jnp.max(s, axis=-1, keepdims=True), sink)
    p = jnp.exp(s - m)
    l = jnp.sum(p, axis=-1, keepdims=True) + jnp.exp(sink - m)
    o = jnp.dot(p.astype(BF16), v, preferred_element_type=F32)
    return o * (1.0 / l)


def _stack_heads(q, group):
    return jnp.concatenate([q[:, g * HEAD_DIM:(g + 1) * HEAD_DIM] for g in range(group)], axis=0)


def _attn_x_kernel(q_ref, kp_ref, ko_ref, kn_ref, kc_ref, vp_ref, vo_ref, vn_ref, vc_ref,
                   mask_ref, sink_ref, o_ref, *, group, n_kv):
    blk = q_ref.shape[0]
    valid = mask_ref[...] > 0.0

    def scores(h):
        kv = slice(h * HEAD_DIM, (h + 1) * HEAD_DIM)
        qs = _stack_heads(q_ref[:, h * group * HEAD_DIM:(h + 1) * group * HEAD_DIM], group)
        k = jnp.concatenate([kp_ref[:, kv], ko_ref[:, kv], kn_ref[:, kv], kc_ref[:, kv]], axis=0)
        s = lax.dot_general(qs, k, (((1,), (1,)), ((), ())), preferred_element_type=F32)
        return jnp.concatenate([jnp.where(valid, s[:, :3 * blk], NEG_INF), s[:, 3 * blk:]], axis=1)

    def finish(h, s):
        kv = slice(h * HEAD_DIM, (h + 1) * HEAD_DIM)
        v = jnp.concatenate([vp_ref[:, kv], vo_ref[:, kv], vn_ref[:, kv], vc_ref[:, kv]], axis=0)
        o = _softmax_pv(s, sink_ref[h], v)
        for g in range(group):
            c0 = (h * group + g) * HEAD_DIM
            o_ref[:, c0:c0 + HEAD_DIM] = o[g * blk:(g + 1) * blk, :].astype(BF16)

    s_next = scores(0)
    for h in range(n_kv):
        s_cur = s_next
        if h + 1 < n_kv:
            s_next = scores(h + 1)
        finish(h, s_cur)


def _attn_c_kernel(q_ref, kc_ref, vc_ref, sink_ref, o_ref, *, group):
    lc = q_ref.shape[0]
    qs = _stack_heads(q_ref[...], group)
    s = lax.dot_general(qs, kc_ref[...], (((1,), (1,)), ((), ())), preferred_element_type=F32)
    o = _softmax_pv(s, sink_ref[...], vc_ref[...])
    for g in range(group):
        o_ref[:, g * HEAD_DIM:(g + 1) * HEAD_DIM] = o[g * lc:(g + 1) * lc, :].astype(BF16)


def _window_mask(group, blk, n_blocks):
    qi = np.arange(group * blk)[:, None] % blk
    kj = np.arange(3 * blk)[None, :]
    band = (kj - qi >= 0) & (kj - qi <= 2 * WINDOW)
    first = band & (kj >= blk)
    last = band & (kj < 2 * blk)
    kinds = [first & last if n_blocks == 1 else first, band, last]
    return np.stack(kinds).astype(np.float32)


def _attention(qkv, sink, *, batch, seq, ctx, n_q, n_kv):
    n_rows = qkv.shape[0]
    group = n_q // n_kv
    blk = WINDOW
    p_rows = ctx + seq
    bpb = p_rows // blk
    cb = ctx // blk
    nb = seq // blk
    last_blk = n_rows // blk - 1
    qw = n_q * HEAD_DIM
    kw = n_kv * HEAD_DIM
    assert qw % kw == 0
    k_col = qw // kw
    spc = p_rows // ctx
    gw = group * HEAD_DIM
    sink_g = sink.reshape(n_kv, group, 1).astype(F32)
    sink_x = jnp.broadcast_to(sink_g[:, :, None, :], (n_kv, group, blk, 1)).reshape(n_kv, group * blk, 1)
    sink_c = jnp.broadcast_to(sink_g[:, :, None, :], (n_kv, group, ctx, 1)).reshape(n_kv, group * ctx, 1)
    mask = jnp.asarray(_window_mask(group, blk, nb))

    def kv_specs(col):
        return [pl.BlockSpec((blk, kw), lambda b, n: (jnp.maximum(b * bpb + cb + n - 1, 0), col)),
                pl.BlockSpec((blk, kw), lambda b, n: (b * bpb + cb + n, col)),
                pl.BlockSpec((blk, kw), lambda b, n: (jnp.minimum(b * bpb + cb + n + 1, last_blk), col)),
                pl.BlockSpec((ctx, kw), lambda b, n: (b * spc, col))]

    out_x = pl.pallas_call(
        functools.partial(_attn_x_kernel, group=group, n_kv=n_kv),
        grid=(batch, nb),
        in_specs=[pl.BlockSpec((blk, qw), lambda b, n: (b * bpb + cb + n, 0))]
        + kv_specs(k_col) + kv_specs(k_col + 1)
        + [pl.BlockSpec((None, group * blk, 3 * blk),
                        lambda b, n: (jnp.where(n == 0, 0, jnp.where(n == nb - 1, 2, 1)), 0, 0)),
           pl.BlockSpec((n_kv, group * blk, 1), lambda b, n: (0, 0, 0))],
        out_specs=pl.BlockSpec((blk, qw), lambda b, n: (b * bpb + cb + n, 0)),
        out_shape=jax.ShapeDtypeStruct((n_rows, qw), BF16),
        compiler_params=_params(("parallel", "parallel")),
        name="attn_latent",
    )(qkv, qkv, qkv, qkv, qkv, qkv, qkv, qkv, qkv, mask, sink_x)

    out = pl.pallas_call(
        functools.partial(_attn_c_kernel_alias, group=group),
        grid=(batch, n_kv),
        in_specs=[pl.BlockSpec((ctx, gw), lambda b, h: (b * spc, h)),
                  pl.BlockSpec((ctx, HEAD_DIM), lambda b, h: (b * spc, n_q + h)),
                  pl.BlockSpec((ctx, HEAD_DIM), lambda b, h: (b * spc, n_q + n_kv + h)),
                  pl.BlockSpec((None, group * ctx, 1), lambda b, h: (h, 0, 0)),
                  pl.BlockSpec(memory_space=pl.ANY)],
        out_specs=pl.BlockSpec((ctx, gw), lambda b, h: (b * spc, h)),
        out_shape=jax.ShapeDtypeStruct((n_rows, qw), BF16),
        input_output_aliases={4: 0},
        compiler_params=_params(("parallel", "parallel")),
        name="attn_context",
    )(qkv, qkv, qkv, sink_c, out_x)
    return out


def _attn_c_kernel_alias(q_ref, kc_ref, vc_ref, sink_ref, prev_ref, o_ref, *, group):
    del prev_ref
    _attn_c_kernel(q_ref, kc_ref, vc_ref, sink_ref, o_ref, group=group)


def _proj_res_kernel(y_ref, w_ref, x_ref, gt_ref, g_ref, sc_ref, sh_ref, o_ref, h_ref, *, sub):
    acc = jnp.dot(y_ref[...], w_ref[...], preferred_element_type=F32)
    g = g_ref[...]
    for s in range(o_ref.shape[0] // sub):
        rows = slice(s * sub, (s + 1) * sub)
        xn = x_ref[rows, :] + gt_ref[s] * acc[rows, :]
        o_ref[rows, :] = xn
        h_ref[rows, :] = _norm_mod(xn, g, sc_ref[s], sh_ref[s]).astype(BF16)


def _proj_residual(y, w, u, gt, g, sc, sh, *, tm, sub):
    n_rows, d = u.shape
    k = y.shape[1]
    nsb = tm // sub
    sb_spec = pl.BlockSpec((nsb, 1, d), lambda i: (i, 0, 0))
    return pl.pallas_call(
        functools.partial(_proj_res_kernel, sub=sub),
        grid=(n_rows // tm,),
        in_specs=[pl.BlockSpec((tm, k), lambda i: (i, 0)),
                  pl.BlockSpec((k, d), lambda i: (0, 0)),
                  pl.BlockSpec((tm, d), lambda i: (i, 0)),
                  sb_spec,
                  pl.BlockSpec((1, d), lambda i: (0, 0)),
                  sb_spec, sb_spec],
        out_specs=[pl.BlockSpec((tm, d), lambda i: (i, 0)),
                   pl.BlockSpec((tm, d), lambda i: (i, 0))],
        out_shape=[jax.ShapeDtypeStruct((n_rows, d), F32),
                   jax.ShapeDtypeStruct((n_rows, d), BF16)],
        compiler_params=_params(("parallel",)),
        name="proj_residual",
    )(y, w, u, gt, g, sc, sh)


def _mlstm_in_kernel(x_ref, g_ref, sc_ref, sh_ref, w_ref, b_ref, wg_ref, bg_ref, z_ref, gate_ref,
                     h_scr, *, sub, k_lo, k_hi, k_scale):
    j = pl.program_id(1)
    tm, tn = z_ref.shape

    @pl.when(j == 0)
    def _():
        _norm_mod_to(h_scr, x_ref, g_ref, sc_ref, sh_ref, sub)
        gate_ref[...] = jnp.dot(h_scr[...], wg_ref[...], preferred_element_type=F32) + bg_ref[...]

    acc = jnp.dot(h_scr[...], w_ref[...], preferred_element_type=F32) + b_ref[...]
    col0 = j * tn
    is_k = (col0 >= k_lo) & (col0 < k_hi)
    z_ref[...] = (acc * jnp.where(is_k, k_scale, 1.0)).astype(BF16)


def _mlstm_in_proj(u, g, sc, sh, w, b, wg, bg, *, tm, sub, dqk):
    n_rows, d = u.shape
    n = 3 * d
    qk_cols = d // 2
    tn = min(1024, qk_cols)
    nsb = tm // sub
    kern = functools.partial(_mlstm_in_kernel, sub=sub, k_lo=qk_cols, k_hi=2 * qk_cols,
                             k_scale=dqk ** -0.5)
    return pl.pallas_call(
        kern,
        grid=(n_rows // tm, n // tn),
        in_specs=[pl.BlockSpec((tm, d), lambda i, j: (i, 0)),
                  pl.BlockSpec((1, d), lambda i, j: (0, 0)),
                  pl.BlockSpec((nsb, 1, d), lambda i, j: (i, 0, 0)),
                  pl.BlockSpec((nsb, 1, d), lambda i, j: (i, 0, 0)),
                  pl.BlockSpec((d, tn), lambda i, j: (0, j)),
                  pl.BlockSpec((1, tn), lambda i, j: (0, j)),
                  pl.BlockSpec((d, 128), lambda i, j: (0, 0)),
                  pl.BlockSpec((1, 128), lambda i, j: (0, 0))],
        out_specs=[pl.BlockSpec((tm, tn), lambda i, j: (i, j)),
                   pl.BlockSpec((tm, 128), lambda i, j: (i, 0))],
        out_shape=[jax.ShapeDtypeStruct((n_rows, n), BF16),
                   jax.ShapeDtypeStruct((n_rows, 128), F32)],
        scratch_shapes=[pltpu.VMEM((tm, d), BF16)],
        compiler_params=_params(("parallel", "arbitrary")),
        name="mlstm_in",
    )(u, g, sc, sh, w, b, wg, bg)


GATE_F_LANE = 64
N_GATE_OUT = 6
N_COL = 5


def _chunk_scan(x, scr, pos, op, ident, reverse):
    r = x.shape[0]
    sh = 1
    while sh < CHUNK:
        scr[CHUNK:CHUNK + r, :] = x
        if reverse:
            other = scr[CHUNK + sh:CHUNK + sh + r, :]
            ok = pos < CHUNK - sh
        else:
            other = scr[CHUNK - sh:CHUNK - sh + r, :]
            ok = pos >= sh
        x = op(x, jnp.where(ok, other, ident))
        sh *= 2
    return x


def _mlstm_gate_kernel(g_ref, o_ref, b_scr, cm_scr, shift_scr, *, n_heads, n_ctx_chunks, tile):
    p_rows = g_ref.shape[0]
    n_chunks = p_rows // CHUNK
    shift_scr[...] = jnp.zeros_like(shift_scr)
    o_ref[0:N_COL] = jnp.zeros((N_COL,) + o_ref.shape[1:], F32)
    fwd_tile = lax.broadcasted_iota(jnp.int32, (tile, LANE), 1) < n_heads
    pos = lax.broadcasted_iota(jnp.int32, (tile, LANE), 0) % CHUNK

    def local(t, _):
        rows = pl.ds(pl.multiple_of(t * tile, tile), tile)
        g = g_ref[rows, :]
        lf = _log_sigmoid(pltpu.roll(g, LANE - GATE_F_LANE, 1))
        b = jnp.where(fwd_tile, _chunk_scan(lf, shift_scr, pos, jnp.add, 0.0, False),
                      _chunk_scan(lf, shift_scr, pos, jnp.add, 0.0, True))
        a = g - b
        cm = jnp.where(fwd_tile, _chunk_scan(a, shift_scr, pos, jnp.maximum, -jnp.inf, False),
                       _chunk_scan(a, shift_scr, pos, jnp.maximum, -jnp.inf, True))
        b_scr[rows, :] = b
        cm_scr[rows, :] = cm
        o_ref[N_COL, rows, :] = a
        return 0

    lax.fori_loop(0, p_rows // tile, local, 0)

    fwd_chunk = lax.broadcasted_iota(jnp.int32, (CHUNK, LANE), 1) < n_heads

    def step(c, m, forward):
        rows = pl.ds(pl.multiple_of(c * CHUNK, CHUNK), CHUNK)
        end_row = CHUNK - 1 if forward else 0
        mine = fwd_chunk if forward else jnp.logical_not(fwd_chunk)
        b = b_scr[rows, :]
        a = o_ref[N_COL, rows, :]
        cm = cm_scr[rows, :]
        b_end = b[end_row:end_row + 1, :]
        m_t = b + jnp.maximum(m, cm)
        m_new = jnp.maximum(b_end + m, b_end + cm[end_row:end_row + 1, :])
        vals = (b - m_t, jnp.exp(b + m - m_t), jnp.exp(-m_t), jnp.exp(b_end + a - m_new),
                jnp.broadcast_to(jnp.exp(b_end + m - m_new), (CHUNK, LANE)))
        for q, val in enumerate(vals):
            o_ref[q, rows, :] = jnp.where(mine, val, o_ref[q, rows, :])
        return m_new

    def body(j, carry):
        mf, mb = carry
        mf = step(j, mf, True)
        cb = jnp.where(j < n_ctx_chunks, n_ctx_chunks - 1 - j, n_chunks - 1 + n_ctx_chunks - j)
        mb = step(cb, mb, False)
        return mf, mb

    zm = jnp.zeros((1, LANE), F32)
    lax.fori_loop(0, n_chunks, body, (zm, zm))


def _mlstm_gates(gates, *, batch, p_rows, ctx, n_heads):
    n_rows = gates.shape[0]
    tile = ctx
    kern = functools.partial(_mlstm_gate_kernel, n_heads=n_heads, n_ctx_chunks=ctx // CHUNK, tile=tile)
    return pl.pallas_call(
        kern,
        grid=(batch,),
        in_specs=[pl.BlockSpec((p_rows, LANE), lambda b: (b, 0))],
        out_specs=pl.BlockSpec((N_GATE_OUT, p_rows, LANE), lambda b: (0, b, 0)),
        out_shape=jax.ShapeDtypeStruct((N_GATE_OUT, n_rows, LANE), F32),
        scratch_shapes=[pltpu.VMEM((p_rows, LANE), F32), pltpu.VMEM((p_rows, LANE), F32),
                        pltpu.VMEM((tile + 2 * CHUNK, LANE), F32)],
        compiler_params=_params(("parallel",)),
        name="mlstm_gates",
    )(gates)


def _mlstm_chunk(q, k, v, col, a_row, c_scr, n, d, reverse):
    ch = q.shape[0]
    t_idx = lax.broadcasted_iota(jnp.int32, (ch, ch), 0)
    s_idx = lax.broadcasted_iota(jnp.int32, (ch, ch), 1)
    seen = (s_idx >= t_idx) if reverse else (s_idx <= t_idx)
    c0 = d * N_COL
    row_term = col[:, c0:c0 + 1]
    e_inter = col[:, c0 + 1:c0 + 2]
    floor = col[:, c0 + 2:c0 + 3]
    wg_col = col[:, c0 + 3:c0 + 4]
    e_prev = col[0:1, c0 + 4:c0 + 5]

    w = jnp.where(seen, jnp.exp(row_term + a_row), 0.0)
    s = lax.dot_general(q, k, (((1,), (1,)), ((), ())), preferred_element_type=F32) * w
    c_old = c_scr[...]
    num = (e_inter * jnp.dot(q, c_old.astype(BF16), preferred_element_type=F32)
           + jnp.dot(s.astype(BF16), v, preferred_element_type=F32))
    qn = jnp.sum(q.astype(F32) * n, axis=1, keepdims=True)
    den = e_inter * qn + jnp.sum(s, axis=1, keepdims=True)
    h = num * (1.0 / jnp.maximum(jnp.abs(den), floor))

    vw = (wg_col * v.astype(F32)).astype(BF16)
    c_scr[...] = e_prev * c_old + lax.dot_general(k, vw, (((0,), (0,)), ((), ())),
                                                  preferred_element_type=F32)
    n_new = e_prev * n + jnp.sum(wg_col * k.astype(F32), axis=0, keepdims=True)
    return h, n_new


def _mlstm_core_kernel(q_ref, k_ref, v_ref, o_ref, gr_ref, gc_ref, gh_ref, y_ref,
                       hf_scr, hb_scr, cf_scr, cb_scr, *, n_ctx_chunks, out_rows):
    n_chunks = gr_ref.shape[0]
    dqk = q_ref.shape[1]
    cf_scr[...] = jnp.zeros_like(cf_scr)
    cb_scr[...] = jnp.zeros_like(cb_scr)

    def run(c, c_scr, h_scr, n, d, reverse):
        rows = pl.ds(pl.multiple_of(c * CHUNK, CHUNK), CHUNK)
        h, n = _mlstm_chunk(q_ref[rows, :], k_ref[rows, :], v_ref[rows, :], gc_ref[rows, :],
                            gr_ref[c][d:d + 1, :], c_scr, n, d, reverse)
        h_scr[rows, :] = h
        return n

    def body(j, carry):
        nf, nb = carry
        nf = run(j, cf_scr, hf_scr, nf, 0, False)
        cb = jnp.where(j < n_ctx_chunks, n_ctx_chunks - 1 - j, n_chunks - 1 + n_ctx_chunks - j)
        nb = run(cb, cb_scr, hb_scr, nb, 1, True)
        return nf, nb

    zn = jnp.zeros((1, dqk), F32)
    lax.fori_loop(0, n_chunks, body, (zn, zn), unroll=2)

    gh = gh_ref[...]

    def finish(r, _):
        rows = pl.ds(pl.multiple_of(r * out_rows, out_rows), out_rows)
        h = hf_scr[rows, :] + hb_scr[rows, :]
        hn = h * lax.rsqrt(jnp.mean(h * h, axis=-1, keepdims=True) + RMS_EPS)
        y_ref[rows, :] = (_sigmoid(o_ref[rows, :].astype(F32)) * hn * gh).astype(BF16)
        return 0

    lax.fori_loop(0, y_ref.shape[0] // out_rows, finish, 0)


def _mlstm_core(z, gates_r, gates_c, g_head, *, batch, p_rows, ctx, n_heads, dqk, dv):
    n_rows = z.shape[0]
    n_chunks = p_rows // CHUNK
    kern = functools.partial(_mlstm_core_kernel, n_ctx_chunks=ctx // CHUNK, out_rows=ctx)
    return pl.pallas_call(
        kern,
        grid=(batch, n_heads),
        in_specs=[pl.BlockSpec((p_rows, dqk), lambda b, h: (b, h)),
                  pl.BlockSpec((p_rows, dqk), lambda b, h: (b, n_heads + h)),
                  pl.BlockSpec((p_rows, dv), lambda b, h: (b, n_heads + h)),
                  pl.BlockSpec((p_rows, dv), lambda b, h: (b, 2 * n_heads + h)),
                  pl.BlockSpec((None, None, n_chunks, 2, CHUNK), lambda b, h: (b, h, 0, 0, 0)),
                  pl.BlockSpec((None, None, p_rows, 2 * N_COL), lambda b, h: (b, h, 0, 0)),
                  pl.BlockSpec((1, dv), lambda b, h: (0, h))],
        out_specs=pl.BlockSpec((p_rows, dv), lambda b, h: (b, h)),
        out_shape=jax.ShapeDtypeStruct((n_rows, n_heads * dv), BF16),
        scratch_shapes=[pltpu.VMEM((p_rows, dv), F32), pltpu.VMEM((p_rows, dv), F32),
                        pltpu.VMEM((dqk, dv), F32), pltpu.VMEM((dqk, dv), F32)],
        compiler_params=_params(("parallel", "parallel")),
        name="mlstm_core",
    )(z, z, z, z, gates_r, gates_c, g_head)


def _gate_layouts(og, *, batch, p_rows, n_heads):
    g = og[:, :, :2 * n_heads].reshape(N_GATE_OUT, batch, p_rows, 2, n_heads)
    cols = g[:N_COL].transpose(1, 4, 2, 3, 0).reshape(batch, n_heads, p_rows, 2 * N_COL)
    rows = g[N_COL].reshape(batch, p_rows // CHUNK, CHUNK, 2, n_heads).transpose(0, 4, 1, 3, 2)
    return rows, cols


def _halo_up_kernel(h_ref, w_ref, o_ref):
    o_ref[...] = jnp.dot(h_ref[...], w_ref[...], preferred_element_type=F32)


def _halo_up(h, w_up, *, tm):
    n_rows, d = h.shape
    n = w_up.shape[1]
    nt = n_rows // tm
    ht = h.reshape(nt, tm, d)
    first, last = ht[:, 0], ht[:, tm - 1]
    prev = jnp.concatenate([last[:1], last[:nt - 1]], axis=0)
    nxt = jnp.concatenate([first[1:], first[nt - 1:]], axis=0)
    hh = jnp.stack([prev, nxt], axis=1)
    hh = jnp.pad(hh, ((0, 0), (0, HALO - 2), (0, 0))).reshape(nt * HALO, d)
    rows = nt * HALO
    tn = 2 * FFN_CHUNK
    uh = pl.pallas_call(
        _halo_up_kernel,
        grid=(n // tn,),
        in_specs=[pl.BlockSpec((rows, d), lambda j: (0, 0)),
                  pl.BlockSpec((d, tn), lambda j: (0, j))],
        out_specs=pl.BlockSpec((rows, tn), lambda j: (0, j)),
        out_shape=jax.ShapeDtypeStruct((rows, n), F32),
        compiler_params=_params(("parallel",)),
        name="ffn_halo",
    )(hh, w_up)
    return uh.reshape(nt, HALO, n)


def _ffn_kernel(h_ref, x_ref, gt_ref, wa0_ref, wb0_ref, wa1_ref, wb1_ref,
                ha0_ref, hb0_ref, ha1_ref, hb1_ref, cwa_ref, cwb_ref, cba_ref, cbb_ref, wd_ref,
                o_ref, acc_scr, u0_scr, u1_scr, a0_scr, a1_scr, *, sub, p_rows, ctx):
    i = pl.program_id(0)
    j = pl.program_id(1)
    tm = o_ref.shape[0]
    nsb = tm // sub
    fh = wa0_ref.shape[1]
    lt = fh // LANE
    seg = sub + 2 * HALO

    @pl.when(j == 0)
    def _():
        acc_scr[...] = jnp.zeros_like(acc_scr)

    def edge_open(k):
        pos = (i * tm + k * sub) % p_rows
        return jnp.where((pos == 0) | (pos == ctx), 0.0, 1.0)

    opens = [edge_open(k) for k in range(nsb + 1)]

    def up(w_refs, halo_refs, u_scr):
        for br, (w_ref, halo_ref) in enumerate(zip(w_refs, halo_refs)):
            ue = jnp.dot(h_ref[...], w_ref[...], preferred_element_type=F32)
            halo = halo_ref[...]
            for s in range(nsb):
                r0 = s * sub
                b0 = s * seg
                for t in range(lt):
                    lanes = slice(t * LANE, (t + 1) * LANE)
                    tt = br * lt + t
                    before = (jnp.broadcast_to(halo[0:1, lanes], (HALO, LANE)) if s == 0
                              else ue[r0 - HALO:r0, lanes])
                    after = (jnp.broadcast_to(halo[1:2, lanes], (HALO, LANE)) if s == nsb - 1
                             else ue[r0 + sub:r0 + sub + HALO, lanes])
                    u_scr[tt, b0:b0 + HALO, :] = before * opens[s]
                    u_scr[tt, b0 + HALO:b0 + HALO + sub, :] = ue[r0:r0 + sub, lanes]
                    u_scr[tt, b0 + HALO + sub:b0 + seg, :] = after * opens[s + 1]

    def gate(c, u_scr, act_scr):
        cols = slice(c * fh, (c + 1) * fh)
        cws = (cwa_ref[:, cols], cwb_ref[:, cols])
        cbs = (cba_ref[:, cols], cbb_ref[:, cols])

        def conv_piece(br, t, r0):
            lanes = slice(t * LANE, (t + 1) * LANE)
            tt = br * lt + t + jnp.minimum(j, 0)
            cw, cb = cws[br], cbs[br]
            cur = u_scr[tt, r0:r0 + GATE_ROWS, :]
            prv = u_scr[tt, r0 - 1:r0 - 1 + GATE_ROWS, :]
            nxt = u_scr[tt, r0 + 1:r0 + 1 + GATE_ROWS, :]
            return cb[:, lanes] + prv * cw[0:1, lanes] + cur * cw[1:2, lanes] + nxt * cw[2:3, lanes]

        for s in range(nsb):
            for r in range(sub // GATE_ROWS):
                r0 = s * seg + HALO + r * GATE_ROWS
                o0 = s * sub + r * GATE_ROWS
                for t in range(lt):
                    a = conv_piece(0, t, r0)
                    b = conv_piece(1, t, r0)
                    act_scr[o0:o0 + GATE_ROWS, t * LANE:(t + 1) * LANE] = (a * _sigmoid(a) * b).astype(BF16)

    def down(c, act_scr):
        acc_scr[...] += jnp.dot(act_scr[...], wd_ref[c * fh:(c + 1) * fh, :], preferred_element_type=F32)

    up((wa0_ref, wb0_ref), (ha0_ref, hb0_ref), u0_scr)
    up((wa1_ref, wb1_ref), (ha1_ref, hb1_ref), u1_scr)
    gate(0, u0_scr, a0_scr)
    down(0, a0_scr)
    gate(1, u1_scr, a1_scr)
    down(1, a1_scr)

    @pl.when(j == pl.num_programs(1) - 1)
    def _():
        for s in range(nsb):
            rows = slice(s * sub, (s + 1) * sub)
            o_ref[rows, :] = x_ref[rows, :] + gt_ref[s] * acc_scr[rows, :]


def _conv_ffn(h, u, gt, w_up, conv_w, conv_b, w_down, *, tm, sub, p_rows, ctx):
    n_rows, d = u.shape
    d_ff = w_down.shape[0]
    fc = FFN_CHUNK
    assert d_ff % fc == 0
    fh = fc // 2
    nfc = d_ff // fc
    nsb = tm // sub
    kern = functools.partial(_ffn_kernel, sub=sub, p_rows=p_rows, ctx=ctx)
    u_shape = (fc // LANE, nsb * (sub + 2 * HALO), LANE)
    halo = _halo_up(h, w_up, tm=tm)

    def half_specs(shape, lead):
        return [pl.BlockSpec(shape, lambda i, j, c=c: lead(i) + (c + 2 * j,))
                for c in (0, 2 * nfc, 1, 2 * nfc + 1)]

    return pl.pallas_call(
        kern,
        grid=(n_rows // tm, nfc),
        in_specs=[pl.BlockSpec((tm, d), lambda i, j: (i, 0)),
                  pl.BlockSpec((tm, d), lambda i, j: (i, 0)),
                  pl.BlockSpec((nsb, 1, d), lambda i, j: (i, 0, 0))]
        + half_specs((d, fh), lambda i: (0,))
        + half_specs((None, HALO, fh), lambda i: (i, 0))
        + [pl.BlockSpec((CONV_W, fc), lambda i, j: (0, j)),
           pl.BlockSpec((CONV_W, fc), lambda i, j: (0, nfc + j)),
           pl.BlockSpec((1, fc), lambda i, j: (0, j)),
           pl.BlockSpec((1, fc), lambda i, j: (0, nfc + j)),
           pl.BlockSpec((fc, d), lambda i, j: (j, 0))],
        out_specs=pl.BlockSpec((tm, d), lambda i, j: (i, 0)),
        out_shape=jax.ShapeDtypeStruct((n_rows, d), F32),
        scratch_shapes=[pltpu.VMEM((tm, d), F32), pltpu.VMEM(u_shape, F32), pltpu.VMEM(u_shape, F32),
                        pltpu.VMEM((tm, fh), BF16), pltpu.VMEM((tm, fh), BF16)],
        compiler_params=_params(("parallel", "arbitrary")),
        name="conv_ffn",
    )(h, u, gt, w_up, w_up, w_up, w_up, halo, halo, halo, halo,
      conv_w, conv_w, conv_b, conv_b, w_down)


def _final_norm_kernel(x_ref, g_ref, o_ref):
    xv = x_ref[...]
    o_ref[...] = xv * lax.rsqrt(jnp.mean(xv * xv, axis=-1, keepdims=True) + RMS_EPS) * g_ref[...]


def _final_norm(u, g, *, batch, seq, ctx, sub):
    d = u.shape[1]
    spb = (ctx + seq) // sub
    cs = ctx // sub
    return pl.pallas_call(
        _final_norm_kernel,
        grid=(batch, seq // sub),
        in_specs=[pl.BlockSpec((sub, d), lambda b, t: (b * spb + cs + t, 0)),
                  pl.BlockSpec((1, d), lambda b, t: (0, 0))],
        out_specs=pl.BlockSpec((None, sub, d), lambda b, t: (b, t, 0)),
        out_shape=jax.ShapeDtypeStruct((batch, seq, d), F32),
        compiler_params=_params(("parallel", "parallel")),
        name="final_norm",
    )(u, g)


def _rope_tables(batch, seq, ctx):
    rows = seq // GRID_W
    row = jnp.repeat(jnp.arange(rows, dtype=F32), GRID_W)
    col = jnp.tile(jnp.arange(GRID_W, dtype=F32), rows)
    n_freq = HEAD_DIM // 4
    inv_freq = ROPE_THETA ** (-jnp.arange(n_freq, dtype=F32) / n_freq)
    ang = jnp.concatenate([row[:, None] * inv_freq, col[:, None] * inv_freq], axis=-1)
    ang = jnp.concatenate([ang, ang], axis=-1)
    sign = jnp.concatenate([-jnp.ones((HEAD_DIM // 2,), F32), jnp.ones((HEAD_DIM // 2,), F32)])
    cos = jnp.concatenate([jnp.ones((ctx, HEAD_DIM), F32), jnp.cos(ang)], axis=0)
    sin = jnp.concatenate([jnp.zeros((ctx, HEAD_DIM), F32), jnp.sin(ang) * sign], axis=0)
    return jnp.tile(cos, (batch, 1)), jnp.tile(sin, (batch, 1))


def kernel(x, c, ctx, c_ctx, w_mod, b_mod, g_mix, g_ffn, attn_w_qkv, attn_sink, attn_w_o,
           mlstm_w_in, mlstm_b_in, mlstm_g_head, mlstm_w_o, ffn_w_up, ffn_conv_w, ffn_conv_b,
           ffn_w_down, g_final):
    batch, seq, d = x.shape
    lc = ctx.shape[1]
    depth = w_mod.shape[0]
    p_rows = lc + seq
    sub = lc
    assert seq % sub == 0 and sub % WINDOW == 0 and sub % CHUNK == 0 and seq % GRID_W == 0
    n_sub = batch * p_rows // sub
    tm = 2 * sub if n_sub % 2 == 0 else sub
    n_q = attn_sink.shape[1]
    n_kv = (attn_w_qkv.shape[2] // HEAD_DIM - n_q) // 2
    n_heads = (mlstm_w_in.shape[2] - 3 * d) // 4
    dv = d // n_heads
    dqk = dv // 2
    main_cols = 3 * d

    u = jnp.concatenate([ctx, x], axis=1).reshape(batch * p_rows, d)

    n_c = batch + 1
    c_rows = -(-n_c // 8) * 8
    cvec = jnp.concatenate([c, c_ctx[None, :], jnp.zeros((c_rows - n_c, d), F32)], axis=0)
    mods = _modulation(cvec, w_mod, b_mod)
    mods_sb = jnp.concatenate(
        [jnp.broadcast_to(mods[:, None, batch:batch + 1], (depth, batch, lc // sub, 6 * d)),
         jnp.broadcast_to(mods[:, :batch, None], (depth, batch, seq // sub, 6 * d))],
        axis=2).reshape(depth, n_sub, 6, 1, d)

    cos_u, sin_u = _rope_tables(batch, seq, lc)

    for i in range(depth):
        jm = i // 2
        sh1, sc1, gt1, sh2, sc2, gt2 = (mods_sb[i, :, k] for k in range(6))
        gm = g_mix[i].reshape(1, d)
        if i % 2 == 0:
            qkv = _qkv_proj(u, gm, sc1, sh1, attn_w_qkv[jm].astype(BF16), cos_u, sin_u,
                            tm=tm, sub=sub, n_q=n_q, n_kv=n_kv)
            y = _attention(qkv, attn_sink[jm], batch=batch, seq=seq, ctx=lc, n_q=n_q, n_kv=n_kv)
            w_o = attn_w_o[jm]
        else:
            w_in = mlstm_w_in[jm]
            b_in = mlstm_b_in[jm]
            wgt = w_in[:, main_cols:].reshape(d, 4, n_heads)
            bgt = b_in[main_cols:].reshape(4, n_heads)
            lane_pad = GATE_F_LANE - 2 * n_heads
            wg = jnp.concatenate([wgt[:, 0], wgt[:, 2], jnp.zeros((d, lane_pad), F32),
                                  wgt[:, 1], wgt[:, 3], jnp.zeros((d, lane_pad), F32)], axis=1).astype(BF16)
            bg = jnp.concatenate([bgt[0], bgt[2], jnp.zeros((lane_pad,), F32),
                                  bgt[1], bgt[3], jnp.zeros((lane_pad,), F32)]).reshape(1, LANE)
            z, gates = _mlstm_in_proj(u, gm, sc1, sh1, w_in.astype(BF16), b_in.reshape(1, -1), wg, bg,
                                      tm=tm, sub=sub, dqk=dqk)
            og = _mlstm_gates(gates, batch=batch, p_rows=p_rows, ctx=lc, n_heads=n_heads)
            gates_r, gates_c = _gate_layouts(og, batch=batch, p_rows=p_rows, n_heads=n_heads)
            y = _mlstm_core(z, gates_r, gates_c, mlstm_g_head[jm].reshape(1, n_heads * dv),
                            batch=batch, p_rows=p_rows, ctx=lc, n_heads=n_heads, dqk=dqk, dv=dv)
            w_o = mlstm_w_o[jm]
        u, h2 = _proj_residual(y, w_o.astype(BF16), u, gt1, g_ffn[i].reshape(1, d), sc2, sh2,
                               tm=tm, sub=sub)
        u = _conv_ffn(h2, u, gt2, ffn_w_up[i].astype(BF16), ffn_conv_w[i],
                      ffn_conv_b[i].reshape(1, -1), ffn_w_down[i].astype(BF16),
                      tm=tm, sub=sub, p_rows=p_rows, ctx=lc)
    return _final_norm(u, g_final.reshape(1, d), batch=batch, seq=seq, ctx=lc, sub=sub)
```

```python
import functools

import numpy as np
import jax
import jax.numpy as jnp
from jax import lax
from jax.experimental import pallas as pl
from jax.experimental.pallas import tpu as pltpu

F32 = jnp.float32
BF16 = jnp.bfloat16

RMS_EPS = 1e-6
HEAD_DIM = 128
WINDOW = 128
GRID_W = 64
ROPE_THETA = 10000.0
NEG_INF = -1e30
CHUNK = 128
CONV_W = 3
LANE = 128
HALO = 8
BF16_ROWS = 16
FFN_CHUNK = 512
GATE_ROWS = 64
NORM_ROWS = 16
PROJ_TILE = 3072
V7X_VMEM_BYTES = 64 * 1024 * 1024
VMEM_LIMIT = V7X_VMEM_BYTES * 7 // 8


def _params(sem):
    return pltpu.CompilerParams(dimension_semantics=sem, vmem_limit_bytes=VMEM_LIMIT)


def _sigmoid(v):
    return 1.0 / (1.0 + jnp.exp(-v))


def _log_sigmoid(v):
    return jnp.minimum(v, 0.0) - jnp.log(1.0 + jnp.exp(-jnp.abs(v)))


def _norm_mod(xv, g, scale, shift):
    ms = jnp.mean(xv * xv, axis=-1, keepdims=True)
    y = xv * lax.rsqrt(ms + RMS_EPS)
    return (y * g) * (1.0 + scale) + shift


def _norm_mod_to(h_scr, x_ref, g_ref, sc_ref, sh_ref, sub):
    for s in range(x_ref.shape[0] // sub):
        gain = g_ref[...] * (1.0 + sc_ref[s])
        shift = sh_ref[s]
        for r in range(s * sub, (s + 1) * sub, NORM_ROWS):
            xv = x_ref[r:r + NORM_ROWS, :]
            ms = jnp.mean(xv * xv, axis=-1, keepdims=True)
            h_scr[r:r + NORM_ROWS, :] = (xv * lax.rsqrt(ms + RMS_EPS) * gain + shift).astype(BF16)


def _mod_kernel(c_ref, w_ref, b_ref, o_ref):
    cv = c_ref[...]
    s = (cv * _sigmoid(cv)).astype(BF16)
    o_ref[...] = jnp.dot(s, w_ref[...].astype(BF16), preferred_element_type=F32) + b_ref[...]


def _modulation(cvec, w_mod, b_mod):
    depth, d, n = w_mod.shape
    r = cvec.shape[0]
    tn = min(1024, n)
    return pl.pallas_call(
        _mod_kernel,
        grid=(depth, n // tn),
        in_specs=[pl.BlockSpec((r, d), lambda i, j: (0, 0)),
                  pl.BlockSpec((None, d, tn), lambda i, j: (i, 0, j)),
                  pl.BlockSpec((None, 1, tn), lambda i, j: (i, 0, j))],
        out_specs=pl.BlockSpec((None, r, tn), lambda i, j: (i, 0, j)),
        out_shape=jax.ShapeDtypeStruct((depth, r, n), F32),
        compiler_params=_params(("parallel", "parallel")),
        name="modulation",
    )(cvec, w_mod, b_mod.reshape(depth, 1, n))


def _qkv_kernel(x_ref, g_ref, sc_ref, sh_ref, w_ref, cos_ref, sin_ref, o_ref, h_scr,
                *, sub, n_q, n_rope, q_scale):
    j = pl.program_id(1)
    tm, tn = o_ref.shape

    @pl.when(j == 0)
    def _():
        _norm_mod_to(h_scr, x_ref, g_ref, sc_ref, sh_ref, sub)

    acc = jnp.dot(h_scr[...], w_ref[...], preferred_element_type=F32)
    cos = cos_ref[...]
    sin = sin_ref[...]
    heads_per_tile = tn // HEAD_DIM
    for l in range(heads_per_tile):
        t = acc[:, l * HEAD_DIM:(l + 1) * HEAD_DIM]
        head = j * heads_per_tile + l
        tr = t * cos + pltpu.roll(t, HEAD_DIM // 2, 1) * sin
        t = jnp.where(head < n_rope, tr, t)
        t = t * jnp.where(head < n_q, q_scale, 1.0)
        o_ref[:, l * HEAD_DIM:(l + 1) * HEAD_DIM] = t.astype(BF16)


def _qkv_proj(u, g, sc, sh, w, cos_u, sin_u, *, tm, sub, n_q, n_kv):
    n_rows, d = u.shape
    n = w.shape[1]
    tn = n if n <= PROJ_TILE else PROJ_TILE
    assert n % tn == 0
    nsb = tm // sub
    kern = functools.partial(_qkv_kernel, sub=sub, n_q=n_q, n_rope=n_q + n_kv,
                             q_scale=HEAD_DIM ** -0.5)
    return pl.pallas_call(
        kern,
        grid=(n_rows // tm, n // tn),
        in_specs=[pl.BlockSpec((tm, d), lambda i, j: (i, 0)),
                  pl.BlockSpec((1, d), lambda i, j: (0, 0)),
                  pl.BlockSpec((nsb, 1, d), lambda i, j: (i, 0, 0)),
                  pl.BlockSpec((nsb, 1, d), lambda i, j: (i, 0, 0)),
                  pl.BlockSpec((d, tn), lambda i, j: (0, j)),
                  pl.BlockSpec((tm, HEAD_DIM), lambda i, j: (i, 0)),
                  pl.BlockSpec((tm, HEAD_DIM), lambda i, j: (i, 0))],
        out_specs=pl.BlockSpec((tm, tn), lambda i, j: (i, j)),
        out_shape=jax.ShapeDtypeStruct((n_rows, n), BF16),
        scratch_shapes=[pltpu.VMEM((tm, d), BF16)],
        compiler_params=_params(("parallel", "arbitrary")),
        name="attn_qkv",
    )(u, g, sc, sh, w, cos_u, sin_u)


def _softmax_pv(s, sink, v):
    m = jnp.maximum(jnp.max(s, axis=-1, keepdims=True), sink)
    p = jnp.exp(s - m)
    l = jnp.sum(p, axis=-1, keepdims=True) + jnp.exp(sink - m)
    o = jnp.dot(p.astype(BF16), v, preferred_element_type=F32)
    return o * (1.0 / l)


def _stack_heads(q, group):
    return jnp.concatenate([q[:, g * HEAD_DIM:(g + 1) * HEAD_DIM] for g in range(group)], axis=0)


def _attn_x_kernel(q_ref, kp_ref, ko_ref, kn_ref, kc_ref, vp_ref, vo_ref, vn_ref, vc_ref,
                   mask_ref, sink_ref, o_ref, *, group, n_kv):
    blk = q_ref.shape[0]
    valid = mask_ref[...] > 0.0

    def scores(h):
        kv = slice(h * HEAD_DIM, (h + 1) * HEAD_DIM)
        qs = _stack_heads(q_ref[:, h * group * HEAD_DIM:(h + 1) * group * HEAD_DIM], group)
        k = jnp.concatenate([kp_ref[:, kv], ko_ref[:, kv], kn_ref[:, kv], kc_ref[:, kv]], axis=0)
        s = lax.dot_general(qs, k, (((1,), (1,)), ((), ())), preferred_element_type=F32)
        return jnp.concatenate([jnp.where(valid, s[:, :3 * blk], NEG_INF), s[:, 3 * blk:]], axis=1)

    def finish(h, s):
        kv = slice(h * HEAD_DIM, (h + 1) * HEAD_DIM)
        v = jnp.concatenate([vp_ref[:, kv], vo_ref[:, kv], vn_ref[:, kv], vc_ref[:, kv]], axis=0)
        o = _softmax_pv(s, sink_ref[h], v)
        for g in range(group):
            c0 = (h * group + g) * HEAD_DIM
            o_ref[:, c0:c0 + HEAD_DIM] = o[g * blk:(g + 1) * blk, :].astype(BF16)

    s_next = scores(0)
    for h in range(n_kv):
        s_cur = s_next
        if h + 1 < n_kv:
            s_next = scores(h + 1)
        finish(h, s_cur)


def _attn_c_kernel(q_ref, kc_ref, vc_ref, sink_ref, o_ref, *, group):
    lc = q_ref.shape[0]
    qs = _stack_heads(q_ref[...], group)
    s = lax.dot_general(qs, kc_ref[...], (((1,), (1,)), ((), ())), preferred_element_type=F32)
    o = _softmax_pv(s, sink_ref[...], vc_ref[...])
    for g in range(group):
        o_ref[:, g * HEAD_DIM:(g + 1) * HEAD_DIM] = o[g * lc:(g + 1) * lc, :].astype(BF16)


def _window_mask(group, blk, n_blocks):
    qi = np.arange(group * blk)[:, None] % blk
    kj = np.arange(3 * blk)[None, :]
    band = (kj - qi >= 0) & (kj - qi <= 2 * WINDOW)
    first = band & (kj >= blk)
    last = band & (kj < 2 * blk)
    kinds = [first & last if n_blocks == 1 else first, band, last]
    return np.stack(kinds).astype(np.float32)


def _attention(qkv, sink, *, batch, seq, ctx, n_q, n_kv):
    n_rows = qkv.shape[0]
    group = n_q // n_kv
    blk = WINDOW
    p_rows = ctx + seq
    bpb = p_rows // blk
    cb = ctx // blk
    nb = seq // blk
    last_blk = n_rows // blk - 1
    qw = n_q * HEAD_DIM
    kw = n_kv * HEAD_DIM
    assert qw % kw == 0
    k_col = qw // kw
    spc = p_rows // ctx
    gw = group * HEAD_DIM
    sink_g = sink.reshape(n_kv, group, 1).astype(F32)
    sink_x = jnp.broadcast_to(sink_g[:, :, None, :], (n_kv, group, blk, 1)).reshape(n_kv, group * blk, 1)
    sink_c = jnp.broadcast_to(sink_g[:, :, None, :], (n_kv, group, ctx, 1)).reshape(n_kv, group * ctx, 1)
    mask = jnp.asarray(_window_mask(group, blk, nb))

    def kv_specs(col):
        return [pl.BlockSpec((blk, kw), lambda b, n: (jnp.maximum(b * bpb + cb + n - 1, 0), col)),
                pl.BlockSpec((blk, kw), lambda b, n: (b * bpb + cb + n, col)),
                pl.BlockSpec((blk, kw), lambda b, n: (jnp.minimum(b * bpb + cb + n + 1, last_blk), col)),
                pl.BlockSpec((ctx, kw), lambda b, n: (b * spc, col))]

    out_x = pl.pallas_call(
        functools.partial(_attn_x_kernel, group=group, n_kv=n_kv),
        grid=(batch, nb),
        in_specs=[pl.BlockSpec((blk, qw), lambda b, n: (b * bpb + cb + n, 0))]
        + kv_specs(k_col) + kv_specs(k_col + 1)
        + [pl.BlockSpec((None, group * blk, 3 * blk),
                        lambda b, n: (jnp.where(n == 0, 0, jnp.where(n == nb - 1, 2, 1)), 0, 0)),
           pl.BlockSpec((n_kv, group * blk, 1), lambda b, n: (0, 0, 0))],
        out_specs=pl.BlockSpec((blk, qw), lambda b, n: (b * bpb + cb + n, 0)),
        out_shape=jax.ShapeDtypeStruct((n_rows, qw), BF16),
        compiler_params=_params(("parallel", "parallel")),
        name="attn_latent",
    )(qkv, qkv, qkv, qkv, qkv, qkv, qkv, qkv, qkv, mask, sink_x)

    out = pl.pallas_call(
        functools.partial(_attn_c_kernel_alias, group=group),
        grid=(batch, n_kv),
        in_specs=[pl.BlockSpec((ctx, gw), lambda b, h: (b * spc, h)),
                  pl.BlockSpec((ctx, HEAD_DIM), lambda b, h: (b * spc, n_q + h)),
                  pl.BlockSpec((ctx, HEAD_DIM), lambda b, h: (b * spc, n_q + n_kv + h)),
                  pl.BlockSpec((None, group * ctx, 1), lambda b, h: (h, 0, 0)),
                  pl.BlockSpec(memory_space=pl.ANY)],
        out_specs=pl.BlockSpec((ctx, gw), lambda b, h: (b * spc, h)),
        out_shape=jax.ShapeDtypeStruct((n_rows, qw), BF16),
        input_output_aliases={4: 0},
        compiler_params=_params(("parallel", "parallel")),
        name="attn_context",
    )(qkv, qkv, qkv, sink_c, out_x)
    return out


def _attn_c_kernel_alias(q_ref, kc_ref, vc_ref, sink_ref, prev_ref, o_ref, *, group):
    del prev_ref
    _attn_c_kernel(q_ref, kc_ref, vc_ref, sink_ref, o_ref, group=group)


def _proj_res_kernel(y_ref, w_ref, x_ref, gt_ref, g_ref, sc_ref, sh_ref, o_ref, h_ref, *, sub):
    acc = jnp.dot(y_ref[...], w_ref[...], preferred_element_type=F32)
    g = g_ref[...]
    for s in range(o_ref.shape[0] // sub):
        rows = slice(s * sub, (s + 1) * sub)
        xn = x_ref[rows, :] + gt_ref[s] * acc[rows, :]
        o_ref[rows, :] = xn
        h_ref[rows, :] = _norm_mod(xn, g, sc_ref[s], sh_ref[s]).astype(BF16)


def _proj_residual(y, w, u, gt, g, sc, sh, *, tm, sub):
    n_rows, d = u.shape
    k = y.shape[1]
    nsb = tm // sub
    sb_spec = pl.BlockSpec((nsb, 1, d), lambda i: (i, 0, 0))
    return pl.pallas_call(
        functools.partial(_proj_res_kernel, sub=sub),
        grid=(n_rows // tm,),
        in_specs=[pl.BlockSpec((tm, k), lambda i: (i, 0)),
                  pl.BlockSpec((k, d), lambda i: (0, 0)),
                  pl.BlockSpec((tm, d), lambda i: (i, 0)),
                  sb_spec,
                  pl.BlockSpec((1, d), lambda i: (0, 0)),
                  sb_spec, sb_spec],
        out_specs=[pl.BlockSpec((tm, d), lambda i: (i, 0)),
                   pl.BlockSpec((tm, d), lambda i: (i, 0))],
        out_shape=[jax.ShapeDtypeStruct((n_rows, d), F32),
                   jax.ShapeDtypeStruct((n_rows, d), BF16)],
        compiler_params=_params(("parallel",)),
        name="proj_residual",
    )(y, w, u, gt, g, sc, sh)


def _mlstm_in_kernel(x_ref, g_ref, sc_ref, sh_ref, w_ref, b_ref, wg_ref, bg_ref, z_ref, gate_ref,
                     h_scr, *, sub, group, k_scale):
    j = pl.program_id(1)
    tm, tn = z_ref.shape

    @pl.when(j == 0)
    def _():
        _norm_mod_to(h_scr, x_ref, g_ref, sc_ref, sh_ref, sub)
        gate_ref[...] = jnp.dot(h_scr[...], wg_ref[...], preferred_element_type=F32) + bg_ref[...]

    for l in range(tn // group):
        cols = slice(l * group, (l + 1) * group)
        acc = jnp.dot(h_scr[...], w_ref[:, cols], preferred_element_type=F32) + b_ref[:, cols]
        is_k = j * (tn // group) + l == 1
        z_ref[:, cols] = (acc * jnp.where(is_k, k_scale, 1.0)).astype(BF16)


def _mlstm_in_proj(u, g, sc, sh, w, b, wg, bg, *, tm, sub, dqk):
    n_rows, d = u.shape
    n = 3 * d
    qk_cols = d // 2
    tn = n // 2 if n // 2 <= PROJ_TILE and (n // 2) % qk_cols == 0 else qk_cols
    nsb = tm // sub
    kern = functools.partial(_mlstm_in_kernel, sub=sub, group=qk_cols, k_scale=dqk ** -0.5)
    return pl.pallas_call(
        kern,
        grid=(n_rows // tm, n // tn),
        in_specs=[pl.BlockSpec((tm, d), lambda i, j: (i, 0)),
                  pl.BlockSpec((1, d), lambda i, j: (0, 0)),
                  pl.BlockSpec((nsb, 1, d), lambda i, j: (i, 0, 0)),
                  pl.BlockSpec((nsb, 1, d), lambda i, j: (i, 0, 0)),
                  pl.BlockSpec((d, tn), lambda i, j: (0, j)),
                  pl.BlockSpec((1, tn), lambda i, j: (0, j)),
                  pl.BlockSpec((d, 128), lambda i, j: (0, 0)),
                  pl.BlockSpec((1, 128), lambda i, j: (0, 0))],
        out_specs=[pl.BlockSpec((tm, tn), lambda i, j: (i, j)),
                   pl.BlockSpec((tm, 128), lambda i, j: (i, 0))],
        out_shape=[jax.ShapeDtypeStruct((n_rows, n), BF16),
                   jax.ShapeDtypeStruct((n_rows, 128), F32)],
        scratch_shapes=[pltpu.VMEM((tm, d), BF16)],
        compiler_params=_params(("parallel", "arbitrary")),
        name="mlstm_in",
    )(u, g, sc, sh, w, b, wg, bg)


GATE_F_LANE = 64
N_GATE_OUT = 6
N_COL = 5


def _chunk_scan(x, scr, pos, op, ident, reverse):
    r = x.shape[0]
    sh = 1
    while sh < CHUNK:
        scr[CHUNK:CHUNK + r, :] = x
        if reverse:
            other = scr[CHUNK + sh:CHUNK + sh + r, :]
            ok = pos < CHUNK - sh
        else:
            other = scr[CHUNK - sh:CHUNK - sh + r, :]
            ok = pos >= sh
        x = op(x, jnp.where(ok, other, ident))
        sh *= 2
    return x


def _mlstm_gate_kernel(g_ref, o_ref, b_scr, cm_scr, shift_scr, *, n_heads, n_ctx_chunks, tile):
    p_rows = g_ref.shape[0]
    n_chunks = p_rows // CHUNK
    shift_scr[...] = jnp.zeros_like(shift_scr)
    o_ref[0:N_COL] = jnp.zeros((N_COL,) + o_ref.shape[1:], F32)
    fwd_tile = lax.broadcasted_iota(jnp.int32, (tile, LANE), 1) < n_heads
    pos = lax.broadcasted_iota(jnp.int32, (tile, LANE), 0) % CHUNK

    def local(t, _):
        rows = pl.ds(pl.multiple_of(t * tile, tile), tile)
        g = g_ref[rows, :]
        lf = _log_sigmoid(pltpu.roll(g, LANE - GATE_F_LANE, 1))
        b = jnp.where(fwd_tile, _chunk_scan(lf, shift_scr, pos, jnp.add, 0.0, False),
                      _chunk_scan(lf, shift_scr, pos, jnp.add, 0.0, True))
        a = g - b
        cm = jnp.where(fwd_tile, _chunk_scan(a, shift_scr, pos, jnp.maximum, -jnp.inf, False),
                       _chunk_scan(a, shift_scr, pos, jnp.maximum, -jnp.inf, True))
        b_scr[rows, :] = b
        cm_scr[rows, :] = cm
        o_ref[N_COL, rows, :] = a
        return 0

    lax.fori_loop(0, p_rows // tile, local, 0)

    fwd_chunk = lax.broadcasted_iota(jnp.int32, (CHUNK, LANE), 1) < n_heads

    def step(c, m, forward):
        rows = pl.ds(pl.multiple_of(c * CHUNK, CHUNK), CHUNK)
        end_row = CHUNK - 1 if forward else 0
        mine = fwd_chunk if forward else jnp.logical_not(fwd_chunk)
        b = b_scr[rows, :]
        a = o_ref[N_COL, rows, :]
        cm = cm_scr[rows, :]
        b_end = b[end_row:end_row + 1, :]
        m_t = b + jnp.maximum(m, cm)
        m_new = jnp.maximum(b_end + m, b_end + cm[end_row:end_row + 1, :])
        vals = (b - m_t, jnp.exp(b + m - m_t), jnp.exp(-m_t), jnp.exp(b_end + a - m_new),
                jnp.broadcast_to(jnp.exp(b_end + m - m_new), (CHUNK, LANE)))
        for q, val in enumerate(vals):
            o_ref[q, rows, :] = jnp.where(mine, val, o_ref[q, rows, :])
        return m_new

    def body(j, carry):
        mf, mb = carry
        mf = step(j, mf, True)
        cb = jnp.where(j < n_ctx_chunks, n_ctx_chunks - 1 - j, n_chunks - 1 + n_ctx_chunks - j)
        mb = step(cb, mb, False)
        return mf, mb

    zm = jnp.zeros((1, LANE), F32)
    lax.fori_loop(0, n_chunks, body, (zm, zm))


def _mlstm_gates(gates, *, batch, p_rows, ctx, n_heads):
    n_rows = gates.shape[0]
    tile = ctx
    kern = functools.partial(_mlstm_gate_kernel, n_heads=n_heads, n_ctx_chunks=ctx // CHUNK, tile=tile)
    return pl.pallas_call(
        kern,
        grid=(batch,),
        in_specs=[pl.BlockSpec((p_rows, LANE), lambda b: (b, 0))],
        out_specs=pl.BlockSpec((N_GATE_OUT, p_rows, LANE), lambda b: (0, b, 0)),
        out_shape=jax.ShapeDtypeStruct((N_GATE_OUT, n_rows, LANE), F32),
        scratch_shapes=[pltpu.VMEM((p_rows, LANE), F32), pltpu.VMEM((p_rows, LANE), F32),
                        pltpu.VMEM((tile + 2 * CHUNK, LANE), F32)],
        compiler_params=_params(("parallel",)),
        name="mlstm_gates",
    )(gates)


def _mlstm_chunk(q, k, v, col, a_row, c_scr, n, d, reverse):
    ch = q.shape[0]
    t_idx = lax.broadcasted_iota(jnp.int32, (ch, ch), 0)
    s_idx = lax.broadcasted_iota(jnp.int32, (ch, ch), 1)
    seen = (s_idx >= t_idx) if reverse else (s_idx <= t_idx)
    c0 = d * N_COL
    row_term = col[:, c0:c0 + 1]
    e_inter = col[:, c0 + 1:c0 + 2]
    floor = col[:, c0 + 2:c0 + 3]
    wg_col = col[:, c0 + 3:c0 + 4]
    e_prev = col[0:1, c0 + 4:c0 + 5]

    w = jnp.where(seen, jnp.exp(row_term + a_row), 0.0)
    s = lax.dot_general(q, k, (((1,), (1,)), ((), ())), preferred_element_type=F32) * w
    c_old = c_scr[...]
    num = (e_inter * jnp.dot(q, c_old.astype(BF16), preferred_element_type=F32)
           + jnp.dot(s.astype(BF16), v, preferred_element_type=F32))
    qn = jnp.sum(q.astype(F32) * n, axis=1, keepdims=True)
    den = e_inter * qn + jnp.sum(s, axis=1, keepdims=True)
    h = num * (1.0 / jnp.maximum(jnp.abs(den), floor))

    vw = (wg_col * v.astype(F32)).astype(BF16)
    c_scr[...] = e_prev * c_old + lax.dot_general(k, vw, (((0,), (0,)), ((), ())),
                                                  preferred_element_type=F32)
    n_new = e_prev * n + jnp.sum(wg_col * k.astype(F32), axis=0, keepdims=True)
    return h, n_new


def _mlstm_core_kernel(q_ref, k_ref, v_ref, o_ref, gr_ref, gc_ref, gh_ref, y_ref,
                       hf_scr, hb_scr, cf_scr, cb_scr, *, n_ctx_chunks, out_rows):
    n_chunks = gr_ref.shape[0]
    dqk = q_ref.shape[1]
    cf_scr[...] = jnp.zeros_like(cf_scr)
    cb_scr[...] = jnp.zeros_like(cb_scr)

    def run(c, c_scr, h_scr, n, d, reverse):
        rows = pl.ds(pl.multiple_of(c * CHUNK, CHUNK), CHUNK)
        h, n = _mlstm_chunk(q_ref[rows, :], k_ref[rows, :], v_ref[rows, :], gc_ref[rows, :],
                            gr_ref[c][d:d + 1, :], c_scr, n, d, reverse)
        h_scr[rows, :] = h
        return n

    def body(j, carry):
        nf, nb = carry
        nf = run(j, cf_scr, hf_scr, nf, 0, False)
        cb = jnp.where(j < n_ctx_chunks, n_ctx_chunks - 1 - j, n_chunks - 1 + n_ctx_chunks - j)
        nb = run(cb, cb_scr, hb_scr, nb, 1, True)
        return nf, nb

    zn = jnp.zeros((1, dqk), F32)
    lax.fori_loop(0, n_chunks, body, (zn, zn), unroll=2)

    gh = gh_ref[...]

    def finish(r, _):
        rows = pl.ds(pl.multiple_of(r * out_rows, out_rows), out_rows)
        h = hf_scr[rows, :] + hb_scr[rows, :]
        hn = h * lax.rsqrt(jnp.mean(h * h, axis=-1, keepdims=True) + RMS_EPS)
        y_ref[rows, :] = (_sigmoid(o_ref[rows, :].astype(F32)) * hn * gh).astype(BF16)
        return 0

    lax.fori_loop(0, y_ref.shape[0] // out_rows, finish, 0)


def _mlstm_core(z, gates_r, gates_c, g_head, *, batch, p_rows, ctx, n_heads, dqk, dv):
    n_rows = z.shape[0]
    n_chunks = p_rows // CHUNK
    kern = functools.partial(_mlstm_core_kernel, n_ctx_chunks=ctx // CHUNK, out_rows=ctx)
    return pl.pallas_call(
        kern,
        grid=(batch, n_heads),
        in_specs=[pl.BlockSpec((p_rows, dqk), lambda b, h: (b, h)),
                  pl.BlockSpec((p_rows, dqk), lambda b, h: (b, n_heads + h)),
                  pl.BlockSpec((p_rows, dv), lambda b, h: (b, n_heads + h)),
                  pl.BlockSpec((p_rows, dv), lambda b, h: (b, 2 * n_heads + h)),
                  pl.BlockSpec((None, None, n_chunks, 2, CHUNK), lambda b, h: (b, h, 0, 0, 0)),
                  pl.BlockSpec((None, None, p_rows, 2 * N_COL), lambda b, h: (b, h, 0, 0)),
                  pl.BlockSpec((1, dv), lambda b, h: (0, h))],
        out_specs=pl.BlockSpec((p_rows, dv), lambda b, h: (b, h)),
        out_shape=jax.ShapeDtypeStruct((n_rows, n_heads * dv), BF16),
        scratch_shapes=[pltpu.VMEM((p_rows, dv), F32), pltpu.VMEM((p_rows, dv), F32),
                        pltpu.VMEM((dqk, dv), F32), pltpu.VMEM((dqk, dv), F32)],
        compiler_params=_params(("parallel", "parallel")),
        name="mlstm_core",
    )(z, z, z, z, gates_r, gates_c, g_head)


def _gate_layouts(og, *, batch, p_rows, n_heads):
    g = og[:, :, :2 * n_heads].reshape(N_GATE_OUT, batch, p_rows, 2, n_heads)
    cols = g[:N_COL].transpose(1, 4, 2, 3, 0).reshape(batch, n_heads, p_rows, 2 * N_COL)
    rows = g[N_COL].reshape(batch, p_rows // CHUNK, CHUNK, 2, n_heads).transpose(0, 4, 1, 3, 2)
    return rows, cols


def _halo_up_kernel(h_ref, w_ref, o_ref):
    o_ref[...] = jnp.dot(h_ref[...], w_ref[...], preferred_element_type=F32)


def _halo_up(h, w_up, *, tm):
    n_rows, d = h.shape
    n = w_up.shape[1]
    nt = n_rows // tm
    ht = h.reshape(nt, tm, d)
    first, last = ht[:, 0], ht[:, tm - 1]
    prev = jnp.concatenate([last[:1], last[:nt - 1]], axis=0)
    nxt = jnp.concatenate([first[1:], first[nt - 1:]], axis=0)
    hh = jnp.stack([prev, nxt], axis=1)
    hh = jnp.pad(hh, ((0, 0), (0, HALO - 2), (0, 0))).reshape(nt * HALO, d)
    rows = nt * HALO
    tn = 2 * FFN_CHUNK
    uh = pl.pallas_call(
        _halo_up_kernel,
        grid=(n // tn,),
        in_specs=[pl.BlockSpec((rows, d), lambda j: (0, 0)),
                  pl.BlockSpec((d, tn), lambda j: (0, j))],
        out_specs=pl.BlockSpec((rows, tn), lambda j: (0, j)),
        out_shape=jax.ShapeDtypeStruct((rows, n), F32),
        compiler_params=_params(("parallel",)),
        name="ffn_halo",
    )(hh, w_up)
    return uh.reshape(nt, HALO, n)


def _ffn_kernel(h_ref, x_ref, gt_ref, wa0_ref, wb0_ref, wa1_ref, wb1_ref,
                ha0_ref, hb0_ref, ha1_ref, hb1_ref, cwa_ref, cwb_ref, cba_ref, cbb_ref, wd_ref,
                o_ref, acc_scr, u0_scr, u1_scr, a0_scr, a1_scr, *, sub, p_rows, ctx):
    i = pl.program_id(0)
    j = pl.program_id(1)
    tm = o_ref.shape[0]
    nsb = tm // sub
    fh = wa0_ref.shape[1]
    lt = fh // LANE
    seg = sub + 2 * HALO

    @pl.when(j == 0)
    def _():
        acc_scr[...] = jnp.zeros_like(acc_scr)

    def edge_open(k):
        pos = (i * tm + k * sub) % p_rows
        return jnp.where((pos == 0) | (pos == ctx), 0.0, 1.0)

    opens = [edge_open(k) for k in range(nsb + 1)]

    def up(w_refs, halo_refs, u_scr):
        for br, (w_ref, halo_ref) in enumerate(zip(w_refs, halo_refs)):
            ue = jnp.dot(h_ref[...], w_ref[...], preferred_element_type=F32)
            halo = halo_ref[...]
            for s in range(nsb):
                r0 = s * sub
                b0 = s * seg
                for t in range(lt):
                    lanes = slice(t * LANE, (t + 1) * LANE)
                    tt = br * lt + t
                    before = (jnp.broadcast_to(halo[0:1, lanes], (HALO, LANE)) if s == 0
                              else ue[r0 - HALO:r0, lanes])
                    after = (jnp.broadcast_to(halo[1:2, lanes], (HALO, LANE)) if s == nsb - 1
                             else ue[r0 + sub:r0 + sub + HALO, lanes])
                    u_scr[tt, b0:b0 + HALO, :] = before * opens[s]
                    u_scr[tt, b0 + HALO:b0 + HALO + sub, :] = ue[r0:r0 + sub, lanes]
                    u_scr[tt, b0 + HALO + sub:b0 + seg, :] = after * opens[s + 1]

    def gate(c, u_scr, act_scr):
        cols = slice(c * fh, (c + 1) * fh)
        cws = (cwa_ref[:, cols], cwb_ref[:, cols])
        cbs = (cba_ref[:, cols], cbb_ref[:, cols])

        def conv_piece(br, t, r0):
            lanes = slice(t * LANE, (t + 1) * LANE)
            tt = br * lt + t + jnp.minimum(j, 0)
            cw, cb = cws[br], cbs[br]
            cur = u_scr[tt, r0:r0 + GATE_ROWS, :]
            prv = u_scr[tt, r0 - 1:r0 - 1 + GATE_ROWS, :]
            nxt = u_scr[tt, r0 + 1:r0 + 1 + GATE_ROWS, :]
            return cb[:, lanes] + prv * cw[0:1, lanes] + cur * cw[1:2, lanes] + nxt * cw[2:3, lanes]

        for s in range(nsb):
            for r in range(sub // GATE_ROWS):
                r0 = s * seg + HALO + r * GATE_ROWS
                o0 = s * sub + r * GATE_ROWS
                for t in range(lt):
                    a = conv_piece(0, t, r0)
                    b = conv_piece(1, t, r0)
                    act_scr[o0:o0 + GATE_ROWS, t * LANE:(t + 1) * LANE] = (a * _sigmoid(a) * b).astype(BF16)

    def down(c, act_scr):
        acc_scr[...] += jnp.dot(act_scr[...], wd_ref[c * fh:(c + 1) * fh, :], preferred_element_type=F32)

    up((wa0_ref, wb0_ref), (ha0_ref, hb0_ref), u0_scr)
    up((wa1_ref, wb1_ref), (ha1_ref, hb1_ref), u1_scr)
    gate(0, u0_scr, a0_scr)
    down(0, a0_scr)
    gate(1, u1_scr, a1_scr)
    down(1, a1_scr)

    @pl.when(j == pl.num_programs(1) - 1)
    def _():
        for s in range(nsb):
            rows = slice(s * sub, (s + 1) * sub)
            o_ref[rows, :] = x_ref[rows, :] + gt_ref[s] * acc_scr[rows, :]


def _conv_ffn(h, u, gt, w_up, conv_w, conv_b, w_down, *, tm, sub, p_rows, ctx):
    n_rows, d = u.shape
    d_ff = w_down.shape[0]
    fc = FFN_CHUNK
    assert d_ff % fc == 0
    fh = fc // 2
    nfc = d_ff // fc
    nsb = tm // sub
    kern = functools.partial(_ffn_kernel, sub=sub, p_rows=p_rows, ctx=ctx)
    u_shape = (fc // LANE, nsb * (sub + 2 * HALO), LANE)
    halo = _halo_up(h, w_up, tm=tm)

    def half_specs(shape, lead):
        return [pl.BlockSpec(shape, lambda i, j, c=c: lead(i) + (c + 2 * j,))
                for c in (0, 2 * nfc, 1, 2 * nfc + 1)]

    return pl.pallas_call(
        kern,
        grid=(n_rows // tm, nfc),
        in_specs=[pl.BlockSpec((tm, d), lambda i, j: (i, 0)),
                  pl.BlockSpec((tm, d), lambda i, j: (i, 0)),
                  pl.BlockSpec((nsb, 1, d), lambda i, j: (i, 0, 0))]
        + half_specs((d, fh), lambda i: (0,))
        + half_specs((None, HALO, fh), lambda i: (i, 0))
        + [pl.BlockSpec((CONV_W, fc), lambda i, j: (0, j)),
           pl.BlockSpec((CONV_W, fc), lambda i, j: (0, nfc + j)),
           pl.BlockSpec((1, fc), lambda i, j: (0, j)),
           pl.BlockSpec((1, fc), lambda i, j: (0, nfc + j)),
           pl.BlockSpec((fc, d), lambda i, j: (j, 0))],
        out_specs=pl.BlockSpec((tm, d), lambda i, j: (i, 0)),
        out_shape=jax.ShapeDtypeStruct((n_rows, d), F32),
        scratch_shapes=[pltpu.VMEM((tm, d), F32), pltpu.VMEM(u_shape, F32), pltpu.VMEM(u_shape, F32),
                        pltpu.VMEM((tm, fh), BF16), pltpu.VMEM((tm, fh), BF16)],
        compiler_params=_params(("parallel", "arbitrary")),
        name="conv_ffn",
    )(h, u, gt, w_up, w_up, w_up, w_up, halo, halo, halo, halo,
      conv_w, conv_w, conv_b, conv_b, w_down)


def _final_norm_kernel(x_ref, g_ref, o_ref):
    xv = x_ref[...]
    o_ref[...] = xv * lax.rsqrt(jnp.mean(xv * xv, axis=-1, keepdims=True) + RMS_EPS) * g_ref[...]


def _final_norm(u, g, *, batch, seq, ctx, sub):
    d = u.shape[1]
    spb = (ctx + seq) // sub
    cs = ctx // sub
    return pl.pallas_call(
        _final_norm_kernel,
        grid=(batch, seq // sub),
        in_specs=[pl.BlockSpec((sub, d), lambda b, t: (b * spb + cs + t, 0)),
                  pl.BlockSpec((1, d), lambda b, t: (0, 0))],
        out_specs=pl.BlockSpec((None, sub, d), lambda b, t: (b, t, 0)),
        out_shape=jax.ShapeDtypeStruct((batch, seq, d), F32),
        compiler_params=_params(("parallel", "parallel")),
        name="final_norm",
    )(u, g)


def _rope_tables(batch, seq, ctx):
    rows = seq // GRID_W
    row = jnp.repeat(jnp.arange(rows, dtype=F32), GRID_W)
    col = jnp.tile(jnp.arange(GRID_W, dtype=F32), rows)
    n_freq = HEAD_DIM // 4
    inv_freq = ROPE_THETA ** (-jnp.arange(n_freq, dtype=F32) / n_freq)
    ang = jnp.concatenate([row[:, None] * inv_freq, col[:, None] * inv_freq], axis=-1)
    ang = jnp.concatenate([ang, ang], axis=-1)
    sign = jnp.concatenate([-jnp.ones((HEAD_DIM // 2,), F32), jnp.ones((HEAD_DIM // 2,), F32)])
    cos = jnp.concatenate([jnp.ones((ctx, HEAD_DIM), F32), jnp.cos(ang)], axis=0)
    sin = jnp.concatenate([jnp.zeros((ctx, HEAD_DIM), F32), jnp.sin(ang) * sign], axis=0)
    return jnp.tile(cos, (batch, 1)), jnp.tile(sin, (batch, 1))


def kernel(x, c, ctx, c_ctx, w_mod, b_mod, g_mix, g_ffn, attn_w_qkv, attn_sink, attn_w_o,
           mlstm_w_in, mlstm_b_in, mlstm_g_head, mlstm_w_o, ffn_w_up, ffn_conv_w, ffn_conv_b,
           ffn_w_down, g_final):
    batch, seq, d = x.shape
    lc = ctx.shape[1]
    depth = w_mod.shape[0]
    p_rows = lc + seq
    sub = lc
    assert seq % sub == 0 and sub % WINDOW == 0 and sub % CHUNK == 0 and seq % GRID_W == 0
    n_sub = batch * p_rows // sub
    tm = 2 * sub if n_sub % 2 == 0 else sub
    n_q = attn_sink.shape[1]
    n_kv = (attn_w_qkv.shape[2] // HEAD_DIM - n_q) // 2
    n_heads = (mlstm_w_in.shape[2] - 3 * d) // 4
    dv = d // n_heads
    dqk = dv // 2
    main_cols = 3 * d

    u = jnp.concatenate([ctx, x], axis=1).reshape(batch * p_rows, d)

    n_c = batch + 1
    c_rows = -(-n_c // 8) * 8
    cvec = jnp.concatenate([c, c_ctx[None, :], jnp.zeros((c_rows - n_c, d), F32)], axis=0)
    mods = _modulation(cvec, w_mod, b_mod)
    mods_sb = jnp.concatenate(
        [jnp.broadcast_to(mods[:, None, batch:batch + 1], (depth, batch, lc // sub, 6 * d)),
         jnp.broadcast_to(mods[:, :batch, None], (depth, batch, seq // sub, 6 * d))],
        axis=2).reshape(depth, n_sub, 6, 1, d)

    cos_u, sin_u = _rope_tables(batch, seq, lc)

    for i in range(depth):
        jm = i // 2
        sh1, sc1, gt1, sh2, sc2, gt2 = (mods_sb[i, :, k] for k in range(6))
        gm = g_mix[i].reshape(1, d)
        if i % 2 == 0:
            qkv = _qkv_proj(u, gm, sc1, sh1, attn_w_qkv[jm].astype(BF16), cos_u, sin_u,
                            tm=tm, sub=sub, n_q=n_q, n_kv=n_kv)
            y = _attention(qkv, attn_sink[jm], batch=batch, seq=seq, ctx=lc, n_q=n_q, n_kv=n_kv)
            w_o = attn_w_o[jm]
        else:
            w_in = mlstm_w_in[jm]
            b_in = mlstm_b_in[jm]
            wgt = w_in[:, main_cols:].reshape(d, 4, n_heads)
            bgt = b_in[main_cols:].reshape(4, n_heads)
            lane_pad = GATE_F_LANE - 2 * n_heads
            wg = jnp.concatenate([wgt[:, 0], wgt[:, 2], jnp.zeros((d, lane_pad), F32),
                                  wgt[:, 1], wgt[:, 3], jnp.zeros((d, lane_pad), F32)], axis=1).astype(BF16)
            bg = jnp.concatenate([bgt[0], bgt[2], jnp.zeros((lane_pad,), F32),
                                  bgt[1], bgt[3], jnp.zeros((lane_pad,), F32)]).reshape(1, LANE)
            z, gates = _mlstm_in_proj(u, gm, sc1, sh1, w_in.astype(BF16), b_in.reshape(1, -1), wg, bg,
                                      tm=tm, sub=sub, dqk=dqk)
            og = _mlstm_gates(gates, batch=batch, p_rows=p_rows, ctx=lc, n_heads=n_heads)
            gates_r, gates_c = _gate_layouts(og, batch=batch, p_rows=p_rows, n_heads=n_heads)
            y = _mlstm_core(z, gates_r, gates_c, mlstm_g_head[jm].reshape(1, n_heads * dv),
                            batch=batch, p_rows=p_rows, ctx=lc, n_heads=n_heads, dqk=dqk, dv=dv)
            w_o = mlstm_w_o[jm]
        u, h2 = _proj_residual(y, w_o.astype(BF16), u, gt1, g_ffn[i].reshape(1, d), sc2, sh2,
                               tm=tm, sub=sub)
        u = _conv_ffn(h2, u, gt2, ffn_w_up[i].astype(BF16), ffn_conv_w[i],
                      ffn_conv_b[i].reshape(1, -1), ffn_w_down[i].astype(BF16),
                      tm=tm, sub=sub, p_rows=p_rows, ctx=lc)
    return _final_norm(u, g_final.reshape(1, d), batch=batch, seq=seq, ctx=lc, sub=sub)
```

```python
import functools

import numpy as np
import jax
import jax.numpy as jnp
from jax import lax
from jax.experimental import pallas as pl
from jax.experimental.pallas import tpu as pltpu

F32 = jnp.float32
BF16 = jnp.bfloat16

RMS_EPS = 1e-6
HEAD_DIM = 128
WINDOW = 128
GRID_W = 64
ROPE_THETA = 10000.0
NEG_INF = -1e30
CHUNK = 128
CONV_W = 3
LANE = 128
HALO = 8
BF16_ROWS = 16
FFN_CHUNK = 512
GATE_ROWS = 64
NORM_ROWS = 16
PROJ_TILE = 3072
MOD_SHIFT1, MOD_SCALE1, MOD_GATE1, MOD_SHIFT2, MOD_SCALE2, MOD_GATE2 = range(6)
V7X_VMEM_BYTES = 64 * 1024 * 1024
VMEM_LIMIT = V7X_VMEM_BYTES * 7 // 8


def _params(sem):
    return pltpu.CompilerParams(dimension_semantics=sem, vmem_limit_bytes=VMEM_LIMIT)


def _sigmoid(v):
    return 1.0 / (1.0 + jnp.exp(-v))


def _log_sigmoid(v):
    return jnp.minimum(v, 0.0) - jnp.log(1.0 + jnp.exp(-jnp.abs(v)))


def _mod_spec(nsb, d, layer, kind):
    return pl.BlockSpec((None, nsb, None, 1, d), lambda i, *_: (layer, i, kind, 0, 0))


def _layer_spec(shape, layer, index):
    return pl.BlockSpec((None,) + shape, lambda *g: (layer,) + index(*g))


def _norm_mod(xv, g, scale, shift):
    ms = jnp.mean(xv * xv, axis=-1, keepdims=True)
    y = xv * lax.rsqrt(ms + RMS_EPS)
    return (y * g) * (1.0 + scale) + shift


def _norm_mod_to(h_scr, x_ref, g_ref, sc_ref, sh_ref, sub):
    for s in range(x_ref.shape[0] // sub):
        gain = g_ref[...] * (1.0 + sc_ref[s])
        shift = sh_ref[s]
        for r in range(s * sub, (s + 1) * sub, NORM_ROWS):
            xv = x_ref[r:r + NORM_ROWS, :]
            ms = jnp.mean(xv * xv, axis=-1, keepdims=True)
            h_scr[r:r + NORM_ROWS, :] = (xv * lax.rsqrt(ms + RMS_EPS) * gain + shift).astype(BF16)


def _mod_kernel(c_ref, w_ref, b_ref, o_ref):
    cv = c_ref[...]
    s = (cv * _sigmoid(cv)).astype(BF16)
    o_ref[...] = jnp.dot(s, w_ref[...].astype(BF16), preferred_element_type=F32) + b_ref[...]


def _modulation(cvec, w_mod, b_mod):
    depth, d, n = w_mod.shape
    r = cvec.shape[0]
    tn = min(1024, n)
    return pl.pallas_call(
        _mod_kernel,
        grid=(depth, n // tn),
        in_specs=[pl.BlockSpec((r, d), lambda i, j: (0, 0)),
                  pl.BlockSpec((None, d, tn), lambda i, j: (i, 0, j)),
                  pl.BlockSpec((None, 1, tn), lambda i, j: (i, 0, j))],
        out_specs=pl.BlockSpec((None, r, tn), lambda i, j: (i, 0, j)),
        out_shape=jax.ShapeDtypeStruct((depth, r, n), F32),
        compiler_params=_params(("parallel", "parallel")),
        name="modulation",
    )(cvec, w_mod, b_mod.reshape(depth, 1, n))


def _qkv_kernel(x_ref, g_ref, sc_ref, sh_ref, w_ref, cos_ref, sin_ref, o_ref, h_scr,
                *, sub, n_q, n_rope, q_scale):
    j = pl.program_id(1)
    tm, tn = o_ref.shape

    @pl.when(j == 0)
    def _():
        _norm_mod_to(h_scr, x_ref, g_ref, sc_ref, sh_ref, sub)

    acc = jnp.dot(h_scr[...], w_ref[...], preferred_element_type=F32)
    cos = cos_ref[...]
    sin = sin_ref[...]
    heads_per_tile = tn // HEAD_DIM
    for l in range(heads_per_tile):
        t = acc[:, l * HEAD_DIM:(l + 1) * HEAD_DIM]
        head = j * heads_per_tile + l
        tr = t * cos + pltpu.roll(t, HEAD_DIM // 2, 1) * sin
        t = jnp.where(head < n_rope, tr, t)
        t = t * jnp.where(head < n_q, q_scale, 1.0)
        o_ref[:, l * HEAD_DIM:(l + 1) * HEAD_DIM] = t.astype(BF16)


def _qkv_proj(u, g, mods, w, cos_u, sin_u, *, layer, w_layer, tm, sub, n_q, n_kv):
    n_rows, d = u.shape
    n = w.shape[2]
    tn = n if n <= PROJ_TILE else PROJ_TILE
    assert n % tn == 0
    nsb = tm // sub
    kern = functools.partial(_qkv_kernel, sub=sub, n_q=n_q, n_rope=n_q + n_kv,
                             q_scale=HEAD_DIM ** -0.5)
    return pl.pallas_call(
        kern,
        grid=(n_rows // tm, n // tn),
        in_specs=[pl.BlockSpec((tm, d), lambda i, j: (i, 0)),
                  _layer_spec((1, d), layer, lambda i, j: (0, 0)),
                  _mod_spec(nsb, d, layer, MOD_SCALE1),
                  _mod_spec(nsb, d, layer, MOD_SHIFT1),
                  _layer_spec((d, tn), w_layer, lambda i, j: (0, j)),
                  pl.BlockSpec((tm, HEAD_DIM), lambda i, j: (i, 0)),
                  pl.BlockSpec((tm, HEAD_DIM), lambda i, j: (i, 0))],
        out_specs=pl.BlockSpec((tm, tn), lambda i, j: (i, j)),
        out_shape=jax.ShapeDtypeStruct((n_rows, n), BF16),
        scratch_shapes=[pltpu.VMEM((tm, d), BF16)],
        compiler_params=_params(("parallel", "arbitrary")),
        name="attn_qkv",
    )(u, g, mods, mods, w, cos_u, sin_u)


def _softmax_pv(s, sink, v):
    m = jnp.maximum(jnp.max(s, axis=-1, keepdims=True), sink)
    p = jnp.exp(s - m)
    l = jnp.sum(p, axis=-1, keepdims=True) + jnp.exp(sink - m)
    o = jnp.dot(p.astype(BF16), v, preferred_element_type=F32)
    return o * (1.0 / l)


def _stack_heads(q, group):
    return jnp.concatenate([q[:, g * HEAD_DIM:(g + 1) * HEAD_DIM] for g in range(group)], axis=0)


def _attn_x_kernel(q_ref, kp_ref, ko_ref, kn_ref, kc_ref, vp_ref, vo_ref, vn_ref, vc_ref,
                   mask_ref, sink_ref, o_ref, *, group, n_kv):
    blk = q_ref.shape[0]
    valid = mask_ref[...] > 0.0

    def scores(h):
        kv = slice(h * HEAD_DIM, (h + 1) * HEAD_DIM)
        qs = _stack_heads(q_ref[:, h * group * HEAD_DIM:(h + 1) * group * HEAD_DIM], group)
        k = jnp.concatenate([kp_ref[:, kv], ko_ref[:, kv], kn_ref[:, kv], kc_ref[:, kv]], axis=0)
        s = lax.dot_general(qs, k, (((1,), (1,)), ((), ())), preferred_element_type=F32)
        return jnp.concatenate([jnp.where(valid, s[:, :3 * blk], NEG_INF), s[:, 3 * blk:]], axis=1)

    def finish(h, s):
        kv = slice(h * HEAD_DIM, (h + 1) * HEAD_DIM)
        v = jnp.concatenate([vp_ref[:, kv], vo_ref[:, kv], vn_ref[:, kv], vc_ref[:, kv]], axis=0)
        o = _softmax_pv(s, sink_ref[h], v)
        for g in range(group):
            c0 = (h * group + g) * HEAD_DIM
            o_ref[:, c0:c0 + HEAD_DIM] = o[g * blk:(g + 1) * blk, :].astype(BF16)

    s_next = scores(0)
    for h in range(n_kv):
        s_cur = s_next
        if h + 1 < n_kv:
            s_next = scores(h + 1)
        finish(h, s_cur)


def _attn_c_kernel(q_ref, kc_ref, vc_ref, sink_ref, o_ref, *, group):
    lc = q_ref.shape[0]
    qs = _stack_heads(q_ref[...], group)
    s = lax.dot_general(qs, kc_ref[...], (((1,), (1,)), ((), ())), preferred_element_type=F32)
    o = _softmax_pv(s, sink_ref[...], vc_ref[...])
    for g in range(group):
        o_ref[:, g * HEAD_DIM:(g + 1) * HEAD_DIM] = o[g * lc:(g + 1) * lc, :].astype(BF16)


def _window_mask(group, blk, n_blocks):
    qi = np.arange(group * blk)[:, None] % blk
    kj = np.arange(3 * blk)[None, :]
    band = (kj - qi >= 0) & (kj - qi <= 2 * WINDOW)
    first = band & (kj >= blk)
    last = band & (kj < 2 * blk)
    kinds = [first & last if n_blocks == 1 else first, band, last]
    return np.stack(kinds).astype(np.float32)


def _attention(qkv, sink, *, batch, seq, ctx, n_q, n_kv):
    n_rows = qkv.shape[0]
    group = n_q // n_kv
    blk = WINDOW
    p_rows = ctx + seq
    bpb = p_rows // blk
    cb = ctx // blk
    nb = seq // blk
    last_blk = n_rows // blk - 1
    qw = n_q * HEAD_DIM
    kw = n_kv * HEAD_DIM
    assert qw % kw == 0
    k_col = qw // kw
    spc = p_rows // ctx
    gw = group * HEAD_DIM
    sink_g = sink.reshape(n_kv, group, 1).astype(F32)
    sink_x = jnp.broadcast_to(sink_g[:, :, None, :], (n_kv, group, blk, 1)).reshape(n_kv, group * blk, 1)
    sink_c = jnp.broadcast_to(sink_g[:, :, None, :], (n_kv, group, ctx, 1)).reshape(n_kv, group * ctx, 1)
    mask = jnp.asarray(_window_mask(group, blk, nb))

    def kv_specs(col):
        return [pl.BlockSpec((blk, kw), lambda b, n: (jnp.maximum(b * bpb + cb + n - 1, 0), col)),
                pl.BlockSpec((blk, kw), lambda b, n: (b * bpb + cb + n, col)),
                pl.BlockSpec((blk, kw), lambda b, n: (jnp.minimum(b * bpb + cb + n + 1, last_blk), col)),
                pl.BlockSpec((ctx, kw), lambda b, n: (b * spc, col))]

    out_x = pl.pallas_call(
        functools.partial(_attn_x_kernel, group=group, n_kv=n_kv),
        grid=(batch, nb),
        in_specs=[pl.BlockSpec((blk, qw), lambda b, n: (b * bpb + cb + n, 0))]
        + kv_specs(k_col) + kv_specs(k_col + 1)
        + [pl.BlockSpec((None, group * blk, 3 * blk),
                        lambda b, n: (jnp.where(n == 0, 0, jnp.where(n == nb - 1, 2, 1)), 0, 0)),
           pl.BlockSpec((n_kv, group * blk, 1), lambda b, n: (0, 0, 0))],
        out_specs=pl.BlockSpec((blk, qw), lambda b, n: (b * bpb + cb + n, 0)),
        out_shape=jax.ShapeDtypeStruct((n_rows, qw), BF16),
        compiler_params=_params(("parallel", "parallel")),
        name="attn_latent",
    )(qkv, qkv, qkv, qkv, qkv, qkv, qkv, qkv, qkv, mask, sink_x)

    out = pl.pallas_call(
        functools.partial(_attn_c_kernel_alias, group=group),
        grid=(batch, n_kv),
        in_specs=[pl.BlockSpec((ctx, gw), lambda b, h: (b * spc, h)),
                  pl.BlockSpec((ctx, HEAD_DIM), lambda b, h: (b * spc, n_q + h)),
                  pl.BlockSpec((ctx, HEAD_DIM), lambda b, h: (b * spc, n_q + n_kv + h)),
                  pl.BlockSpec((None, group * ctx, 1), lambda b, h: (h, 0, 0)),
                  pl.BlockSpec(memory_space=pl.ANY)],
        out_specs=pl.BlockSpec((ctx, gw), lambda b, h: (b * spc, h)),
        out_shape=jax.ShapeDtypeStruct((n_rows, qw), BF16),
        input_output_aliases={4: 0},
        compiler_params=_params(("parallel", "parallel")),
        name="attn_context",
    )(qkv, qkv, qkv, sink_c, out_x)
    return out


def _attn_c_kernel_alias(q_ref, kc_ref, vc_ref, sink_ref, prev_ref, o_ref, *, group):
    del prev_ref
    _attn_c_kernel(q_ref, kc_ref, vc_ref, sink_ref, o_ref, group=group)


def _proj_res_kernel(y_ref, w_ref, x_ref, gt_ref, g_ref, sc_ref, sh_ref, o_ref, h_ref, *, sub):
    acc = jnp.dot(y_ref[...], w_ref[...], preferred_element_type=F32)
    g = g_ref[...]
    for s in range(o_ref.shape[0] // sub):
        rows = slice(s * sub, (s + 1) * sub)
        xn = x_ref[rows, :] + gt_ref[s] * acc[rows, :]
        o_ref[rows, :] = xn
        h_ref[rows, :] = _norm_mod(xn, g, sc_ref[s], sh_ref[s]).astype(BF16)


def _proj_residual(y, w, u, mods, g, *, layer, w_layer, tm, sub):
    n_rows, d = u.shape
    k = y.shape[1]
    nsb = tm // sub
    return pl.pallas_call(
        functools.partial(_proj_res_kernel, sub=sub),
        grid=(n_rows // tm,),
        in_specs=[pl.BlockSpec((tm, k), lambda i: (i, 0)),
                  _layer_spec((k, d), w_layer, lambda i: (0, 0)),
                  pl.BlockSpec((tm, d), lambda i: (i, 0)),
                  _mod_spec(nsb, d, layer, MOD_GATE1),
                  _layer_spec((1, d), layer, lambda i: (0, 0)),
                  _mod_spec(nsb, d, layer, MOD_SCALE2),
                  _mod_spec(nsb, d, layer, MOD_SHIFT2)],
        out_specs=[pl.BlockSpec((tm, d), lambda i: (i, 0)),
                   pl.BlockSpec((tm, d), lambda i: (i, 0))],
        out_shape=[jax.ShapeDtypeStruct((n_rows, d), F32),
                   jax.ShapeDtypeStruct((n_rows, d), BF16)],
        compiler_params=_params(("parallel",)),
        name="proj_residual",
    )(y, w, u, mods, g, mods, mods)


def _mlstm_in_kernel(x_ref, g_ref, sc_ref, sh_ref, w_ref, b_ref, wg_ref, bg_ref, z_ref, gate_ref,
                     h_scr, *, sub, group, k_scale):
    j = pl.program_id(1)
    tm, tn = z_ref.shape

    @pl.when(j == 0)
    def _():
        _norm_mod_to(h_scr, x_ref, g_ref, sc_ref, sh_ref, sub)
        gate_ref[...] = jnp.dot(h_scr[...], wg_ref[...], preferred_element_type=F32) + bg_ref[...]

    for l in range(tn // group):
        cols = slice(l * group, (l + 1) * group)
        acc = jnp.dot(h_scr[...], w_ref[:, cols], preferred_element_type=F32) + b_ref[:, cols]
        is_k = j * (tn // group) + l == 1
        z_ref[:, cols] = (acc * jnp.where(is_k, k_scale, 1.0)).astype(BF16)


def _mlstm_in_proj(u, g, mods, w, b, wg, bg, *, layer, w_layer, tm, sub, dqk):
    n_rows, d = u.shape
    n = 3 * d
    qk_cols = d // 2
    tn = n // 2 if n // 2 <= PROJ_TILE and (n // 2) % qk_cols == 0 else qk_cols
    nsb = tm // sub
    kern = functools.partial(_mlstm_in_kernel, sub=sub, group=qk_cols, k_scale=dqk ** -0.5)
    return pl.pallas_call(
        kern,
        grid=(n_rows // tm, n // tn),
        in_specs=[pl.BlockSpec((tm, d), lambda i, j: (i, 0)),
                  _layer_spec((1, d), layer, lambda i, j: (0, 0)),
                  _mod_spec(nsb, d, layer, MOD_SCALE1),
                  _mod_spec(nsb, d, layer, MOD_SHIFT1),
                  _layer_spec((d, tn), w_layer, lambda i, j: (0, j)),
                  _layer_spec((1, tn), w_layer, lambda i, j: (0, j)),
                  pl.BlockSpec((d, 128), lambda i, j: (0, 0)),
                  pl.BlockSpec((1, 128), lambda i, j: (0, 0))],
        out_specs=[pl.BlockSpec((tm, tn), lambda i, j: (i, j)),
                   pl.BlockSpec((tm, 128), lambda i, j: (i, 0))],
        out_shape=[jax.ShapeDtypeStruct((n_rows, n), BF16),
                   jax.ShapeDtypeStruct((n_rows, 128), F32)],
        scratch_shapes=[pltpu.VMEM((tm, d), BF16)],
        compiler_params=_params(("parallel", "arbitrary")),
        name="mlstm_in",
    )(u, g, mods, mods, w, b, wg, bg)


GATE_F_LANE = 64
N_GATE_OUT = 6
N_COL = 5


def _chunk_scan(x, scr, pos, op, ident, reverse):
    r = x.shape[0]
    sh = 1
    while sh < CHUNK:
        scr[CHUNK:CHUNK + r, :] = x
        if reverse:
            other = scr[CHUNK + sh:CHUNK + sh + r, :]
            ok = pos < CHUNK - sh
        else:
            other = scr[CHUNK - sh:CHUNK - sh + r, :]
            ok = pos >= sh
        x = op(x, jnp.where(ok, other, ident))
        sh *= 2
    return x


def _mlstm_gate_kernel(g_ref, o_ref, b_scr, cm_scr, shift_scr, *, n_heads, n_ctx_chunks, tile):
    p_rows = g_ref.shape[0]
    n_chunks = p_rows // CHUNK
    shift_scr[...] = jnp.zeros_like(shift_scr)
    o_ref[0:N_COL] = jnp.zeros((N_COL,) + o_ref.shape[1:], F32)
    fwd_tile = lax.broadcasted_iota(jnp.int32, (tile, LANE), 1) < n_heads
    pos = lax.broadcasted_iota(jnp.int32, (tile, LANE), 0) % CHUNK

    def local(t, _):
        rows = pl.ds(pl.multiple_of(t * tile, tile), tile)
        g = g_ref[rows, :]
        lf = _log_sigmoid(pltpu.roll(g, LANE - GATE_F_LANE, 1))
        b = jnp.where(fwd_tile, _chunk_scan(lf, shift_scr, pos, jnp.add, 0.0, False),
                      _chunk_scan(lf, shift_scr, pos, jnp.add, 0.0, True))
        a = g - b
        cm = jnp.where(fwd_tile, _chunk_scan(a, shift_scr, pos, jnp.maximum, -jnp.inf, False),
                       _chunk_scan(a, shift_scr, pos, jnp.maximum, -jnp.inf, True))
        b_scr[rows, :] = b
        cm_scr[rows, :] = cm
        o_ref[N_COL, rows, :] = a
        return 0

    lax.fori_loop(0, p_rows // tile, local, 0)

    fwd_chunk = lax.broadcasted_iota(jnp.int32, (CHUNK, LANE), 1) < n_heads

    def step(c, m, forward):
        rows = pl.ds(pl.multiple_of(c * CHUNK, CHUNK), CHUNK)
        end_row = CHUNK - 1 if forward else 0
        mine = fwd_chunk if forward else jnp.logical_not(fwd_chunk)
        b = b_scr[rows, :]
        a = o_ref[N_COL, rows, :]
        cm = cm_scr[rows, :]
        b_end = b[end_row:end_row + 1, :]
        m_t = b + jnp.maximum(m, cm)
        m_new = jnp.maximum(b_end + m, b_end + cm[end_row:end_row + 1, :])
        vals = (b - m_t, jnp.exp(b + m - m_t), jnp.exp(-m_t), jnp.exp(b_end + a - m_new),
                jnp.broadcast_to(jnp.exp(b_end + m - m_new), (CHUNK, LANE)))
        for q, val in enumerate(vals):
            o_ref[q, rows, :] = jnp.where(mine, val, o_ref[q, rows, :])
        return m_new

    def body(j, carry):
        mf, mb = carry
        mf = step(j, mf, True)
        cb = jnp.where(j < n_ctx_chunks, n_ctx_chunks - 1 - j, n_chunks - 1 + n_ctx_chunks - j)
        mb = step(cb, mb, False)
        return mf, mb

    zm = jnp.zeros((1, LANE), F32)
    lax.fori_loop(0, n_chunks, body, (zm, zm))


def _mlstm_gates(gates, *, batch, p_rows, ctx, n_heads):
    n_rows = gates.shape[0]
    tile = ctx
    kern = functools.partial(_mlstm_gate_kernel, n_heads=n_heads, n_ctx_chunks=ctx // CHUNK, tile=tile)
    return pl.pallas_call(
        kern,
        grid=(batch,),
        in_specs=[pl.BlockSpec((p_rows, LANE), lambda b: (b, 0))],
        out_specs=pl.BlockSpec((N_GATE_OUT, p_rows, LANE), lambda b: (0, b, 0)),
        out_shape=jax.ShapeDtypeStruct((N_GATE_OUT, n_rows, LANE), F32),
        scratch_shapes=[pltpu.VMEM((p_rows, LANE), F32), pltpu.VMEM((p_rows, LANE), F32),
                        pltpu.VMEM((tile + 2 * CHUNK, LANE), F32)],
        compiler_params=_params(("parallel",)),
        name="mlstm_gates",
    )(gates)


def _mlstm_chunk(q, k, v, col, a_row, c_scr, n, d, reverse):
    ch = q.shape[0]
    t_idx = lax.broadcasted_iota(jnp.int32, (ch, ch), 0)
    s_idx = lax.broadcasted_iota(jnp.int32, (ch, ch), 1)
    seen = (s_idx >= t_idx) if reverse else (s_idx <= t_idx)
    c0 = d * N_COL
    row_term = col[:, c0:c0 + 1]
    e_inter = col[:, c0 + 1:c0 + 2]
    floor = col[:, c0 + 2:c0 + 3]
    wg_col = col[:, c0 + 3:c0 + 4]
    e_prev = col[0:1, c0 + 4:c0 + 5]

    w = jnp.where(seen, jnp.exp(row_term + a_row), 0.0)
    s = lax.dot_general(q, k, (((1,), (1,)), ((), ())), preferred_element_type=F32) * w
    c_old = c_scr[...]
    num = (e_inter * jnp.dot(q, c_old.astype(BF16), preferred_element_type=F32)
           + jnp.dot(s.astype(BF16), v, preferred_element_type=F32))
    qn = jnp.sum(q.astype(F32) * n, axis=1, keepdims=True)
    den = e_inter * qn + jnp.sum(s, axis=1, keepdims=True)
    h = num * (1.0 / jnp.maximum(jnp.abs(den), floor))

    vw = (wg_col * v.astype(F32)).astype(BF16)
    c_scr[...] = e_prev * c_old + lax.dot_general(k, vw, (((0,), (0,)), ((), ())),
                                                  preferred_element_type=F32)
    n_new = e_prev * n + jnp.sum(wg_col * k.astype(F32), axis=0, keepdims=True)
    return h, n_new


def _mlstm_core_kernel(q_ref, k_ref, v_ref, o_ref, gr_ref, gc_ref, gh_ref, y_ref,
                       hf_scr, hb_scr, cf_scr, cb_scr, *, n_ctx_chunks, out_rows):
    n_chunks = gr_ref.shape[0]
    dqk = q_ref.shape[1]
    cf_scr[...] = jnp.zeros_like(cf_scr)
    cb_scr[...] = jnp.zeros_like(cb_scr)

    def run(c, c_scr, h_scr, n, d, reverse):
        rows = pl.ds(pl.multiple_of(c * CHUNK, CHUNK), CHUNK)
        h, n = _mlstm_chunk(q_ref[rows, :], k_ref[rows, :], v_ref[rows, :], gc_ref[rows, :],
                            gr_ref[c][d:d + 1, :], c_scr, n, d, reverse)
        h_scr[rows, :] = h
        return n

    def body(j, carry):
        nf, nb = carry
        nf = run(j, cf_scr, hf_scr, nf, 0, False)
        cb = jnp.where(j < n_ctx_chunks, n_ctx_chunks - 1 - j, n_chunks - 1 + n_ctx_chunks - j)
        nb = run(cb, cb_scr, hb_scr, nb, 1, True)
        return nf, nb

    zn = jnp.zeros((1, dqk), F32)
    lax.fori_loop(0, n_chunks, body, (zn, zn), unroll=2)

    gh = gh_ref[...]

    def finish(r, _):
        rows = pl.ds(pl.multiple_of(r * out_rows, out_rows), out_rows)
        h = hf_scr[rows, :] + hb_scr[rows, :]
        hn = h * lax.rsqrt(jnp.mean(h * h, axis=-1, keepdims=True) + RMS_EPS)
        y_ref[rows, :] = (_sigmoid(o_ref[rows, :].astype(F32)) * hn * gh).astype(BF16)
        return 0

    lax.fori_loop(0, y_ref.shape[0] // out_rows, finish, 0)


def _mlstm_core(z, gates_r, gates_c, g_head, *, batch, p_rows, ctx, n_heads, dqk, dv):
    n_rows = z.shape[0]
    n_chunks = p_rows // CHUNK
    kern = functools.partial(_mlstm_core_kernel, n_ctx_chunks=ctx // CHUNK, out_rows=ctx)
    return pl.pallas_call(
        kern,
        grid=(batch, n_heads),
        in_specs=[pl.BlockSpec((p_rows, dqk), lambda b, h: (b, h)),
                  pl.BlockSpec((p_rows, dqk), lambda b, h: (b, n_heads + h)),
                  pl.BlockSpec((p_rows, dv), lambda b, h: (b, n_heads + h)),
                  pl.BlockSpec((p_rows, dv), lambda b, h: (b, 2 * n_heads + h)),
                  pl.BlockSpec((None, None, n_chunks, 2, CHUNK), lambda b, h: (b, h, 0, 0, 0)),
                  pl.BlockSpec((None, None, p_rows, 2 * N_COL), lambda b, h: (b, h, 0, 0)),
                  pl.BlockSpec((1, dv), lambda b, h: (0, h))],
        out_specs=pl.BlockSpec((p_rows, dv), lambda b, h: (b, h)),
        out_shape=jax.ShapeDtypeStruct((n_rows, n_heads * dv), BF16),
        scratch_shapes=[pltpu.VMEM((p_rows, dv), F32), pltpu.VMEM((p_rows, dv), F32),
                        pltpu.VMEM((dqk, dv), F32), pltpu.VMEM((dqk, dv), F32)],
        compiler_params=_params(("parallel", "parallel")),
        name="mlstm_core",
    )(z, z, z, z, gates_r, gates_c, g_head)


def _gate_layouts(og, *, batch, p_rows, n_heads):
    g = og[:, :, :2 * n_heads].reshape(N_GATE_OUT, batch, p_rows, 2, n_heads)
    cols = g[:N_COL].transpose(1, 4, 2, 3, 0).reshape(batch, n_heads, p_rows, 2 * N_COL)
    rows = g[N_COL].reshape(batch, p_rows // CHUNK, CHUNK, 2, n_heads).transpose(0, 4, 1, 3, 2)
    return rows, cols


def _halo_up_kernel(h_ref, w_ref, o_ref):
    o_ref[...] = jnp.dot(h_ref[...], w_ref[...], preferred_element_type=F32)


def _halo_up(h, w_up, *, layer, tm):
    n_rows, d = h.shape
    n = w_up.shape[2]
    nt = n_rows // tm
    ht = h.reshape(nt, tm, d)
    first, last = ht[:, 0], ht[:, tm - 1]
    prev = jnp.concatenate([last[:1], last[:nt - 1]], axis=0)
    nxt = jnp.concatenate([first[1:], first[nt - 1:]], axis=0)
    hh = jnp.stack([prev, nxt], axis=1)
    hh = jnp.pad(hh, ((0, 0), (0, HALO - 2), (0, 0))).reshape(nt * HALO, d)
    rows = nt * HALO
    tn = 2 * FFN_CHUNK
    uh = pl.pallas_call(
        _halo_up_kernel,
        grid=(n // tn,),
        in_specs=[pl.BlockSpec((rows, d), lambda j: (0, 0)),
                  _layer_spec((d, tn), layer, lambda j: (0, j))],
        out_specs=pl.BlockSpec((rows, tn), lambda j: (0, j)),
        out_shape=jax.ShapeDtypeStruct((rows, n), F32),
        compiler_params=_params(("parallel",)),
        name="ffn_halo",
    )(hh, w_up)
    return uh.reshape(nt, HALO, n)


def _ffn_kernel(h_ref, x_ref, gt_ref, wa0_ref, wb0_ref, wa1_ref, wb1_ref,
                ha0_ref, hb0_ref, ha1_ref, hb1_ref, cwa_ref, cwb_ref, cba_ref, cbb_ref, wd_ref,
                o_ref, acc_scr, u0_scr, u1_scr, a0_scr, a1_scr, *, sub, p_rows, ctx):
    i = pl.program_id(0)
    j = pl.program_id(1)
    tm = o_ref.shape[0]
    nsb = tm // sub
    fh = wa0_ref.shape[1]
    lt = fh // LANE
    seg = sub + 2 * HALO

    @pl.when(j == 0)
    def _():
        acc_scr[...] = jnp.zeros_like(acc_scr)

    def edge_open(k):
        pos = (i * tm + k * sub) % p_rows
        return jnp.where((pos == 0) | (pos == ctx), 0.0, 1.0)

    opens = [edge_open(k) for k in range(nsb + 1)]

    def up(w_refs, halo_refs, u_scr):
        for br, (w_ref, halo_ref) in enumerate(zip(w_refs, halo_refs)):
            ue = jnp.dot(h_ref[...], w_ref[...], preferred_element_type=F32)
            halo = halo_ref[...]
            for s in range(nsb):
                r0 = s * sub
                b0 = s * seg
                for t in range(lt):
                    lanes = slice(t * LANE, (t + 1) * LANE)
                    tt = br * lt + t
                    before = (jnp.broadcast_to(halo[0:1, lanes], (HALO, LANE)) if s == 0
                              else ue[r0 - HALO:r0, lanes])
                    after = (jnp.broadcast_to(halo[1:2, lanes], (HALO, LANE)) if s == nsb - 1
                             else ue[r0 + sub:r0 + sub + HALO, lanes])
                    u_scr[tt, b0:b0 + HALO, :] = before * opens[s]
                    u_scr[tt, b0 + HALO:b0 + HALO + sub, :] = ue[r0:r0 + sub, lanes]
                    u_scr[tt, b0 + HALO + sub:b0 + seg, :] = after * opens[s + 1]

    def gate(c, u_scr, act_scr):
        cols = slice(c * fh, (c + 1) * fh)
        cws = (cwa_ref[:, cols], cwb_ref[:, cols])
        cbs = (cba_ref[:, cols], cbb_ref[:, cols])

        def conv_piece(br, t, r0):
            lanes = slice(t * LANE, (t + 1) * LANE)
            tt = br * lt + t + jnp.minimum(j, 0)
            cw, cb = cws[br], cbs[br]
            cur = u_scr[tt, r0:r0 + GATE_ROWS, :]
            prv = u_scr[tt, r0 - 1:r0 - 1 + GATE_ROWS, :]
            nxt = u_scr[tt, r0 + 1:r0 + 1 + GATE_ROWS, :]
            return cb[:, lanes] + prv * cw[0:1, lanes] + cur * cw[1:2, lanes] + nxt * cw[2:3, lanes]

        for s in range(nsb):
            for r in range(sub // GATE_ROWS):
                r0 = s * seg + HALO + r * GATE_ROWS
                o0 = s * sub + r * GATE_ROWS
                for t in range(lt):
                    a = conv_piece(0, t, r0)
                    b = conv_piece(1, t, r0)
                    act_scr[o0:o0 + GATE_ROWS, t * LANE:(t + 1) * LANE] = (a * _sigmoid(a) * b).astype(BF16)

    def down(c, act_scr):
        acc_scr[...] += jnp.dot(act_scr[...], wd_ref[c * fh:(c + 1) * fh, :], preferred_element_type=F32)

    up((wa0_ref, wb0_ref), (ha0_ref, hb0_ref), u0_scr)
    up((wa1_ref, wb1_ref), (ha1_ref, hb1_ref), u1_scr)
    gate(0, u0_scr, a0_scr)
    down(0, a0_scr)
    gate(1, u1_scr, a1_scr)
    down(1, a1_scr)

    @pl.when(j == pl.num_programs(1) - 1)
    def _():
        for s in range(nsb):
            rows = slice(s * sub, (s + 1) * sub)
            o_ref[rows, :] = x_ref[rows, :] + gt_ref[s] * acc_scr[rows, :]


def _conv_ffn(h, u, mods, w_up, conv_w, conv_b, w_down, *, layer, tm, sub, p_rows, ctx):
    n_rows, d = u.shape
    d_ff = w_down.shape[1]
    fc = FFN_CHUNK
    assert d_ff % fc == 0
    fh = fc // 2
    nfc = d_ff // fc
    nsb = tm // sub
    kern = functools.partial(_ffn_kernel, sub=sub, p_rows=p_rows, ctx=ctx)
    u_shape = (fc // LANE, nsb * (sub + 2 * HALO), LANE)
    halo = _halo_up(h, w_up, layer=layer, tm=tm)

    def half_specs(shape, lead):
        return [pl.BlockSpec(shape, lambda i, j, c=c: lead(i) + (c + 2 * j,))
                for c in (0, 2 * nfc, 1, 2 * nfc + 1)]

    return pl.pallas_call(
        kern,
        grid=(n_rows // tm, nfc),
        in_specs=[pl.BlockSpec((tm, d), lambda i, j: (i, 0)),
                  pl.BlockSpec((tm, d), lambda i, j: (i, 0)),
                  _mod_spec(nsb, d, layer, MOD_GATE2)]
        + half_specs((None, d, fh), lambda i: (layer, 0))
        + half_specs((None, HALO, fh), lambda i: (i, 0))
        + [_layer_spec((CONV_W, fc), layer, lambda i, j: (0, j)),
           _layer_spec((CONV_W, fc), layer, lambda i, j: (0, nfc + j)),
           _layer_spec((1, fc), layer, lambda i, j: (0, j)),
           _layer_spec((1, fc), layer, lambda i, j: (0, nfc + j)),
           _layer_spec((fc, d), layer, lambda i, j: (j, 0))],
        out_specs=pl.BlockSpec((tm, d), lambda i, j: (i, 0)),
        out_shape=jax.ShapeDtypeStruct((n_rows, d), F32),
        scratch_shapes=[pltpu.VMEM((tm, d), F32), pltpu.VMEM(u_shape, F32), pltpu.VMEM(u_shape, F32),
                        pltpu.VMEM((tm, fh), BF16), pltpu.VMEM((tm, fh), BF16)],
        compiler_params=_params(("parallel", "arbitrary")),
        name="conv_ffn",
    )(h, u, mods, w_up, w_up, w_up, w_up, halo, halo, halo, halo,
      conv_w, conv_w, conv_b, conv_b, w_down)


def _final_norm_kernel(x_ref, g_ref, o_ref):
    xv = x_ref[...]
    o_ref[...] = xv * lax.rsqrt(jnp.mean(xv * xv, axis=-1, keepdims=True) + RMS_EPS) * g_ref[...]


def _final_norm(u, g, *, batch, seq, ctx, sub):
    d = u.shape[1]
    spb = (ctx + seq) // sub
    cs = ctx // sub
    return pl.pallas_call(
        _final_norm_kernel,
        grid=(batch, seq // sub),
        in_specs=[pl.BlockSpec((sub, d), lambda b, t: (b * spb + cs + t, 0)),
                  pl.BlockSpec((1, d), lambda b, t: (0, 0))],
        out_specs=pl.BlockSpec((None, sub, d), lambda b, t: (b, t, 0)),
        out_shape=jax.ShapeDtypeStruct((batch, seq, d), F32),
        compiler_params=_params(("parallel", "parallel")),
        name="final_norm",
    )(u, g)


def _rope_tables(batch, seq, ctx):
    rows = seq // GRID_W
    row = jnp.repeat(jnp.arange(rows, dtype=F32), GRID_W)
    col = jnp.tile(jnp.arange(GRID_W, dtype=F32), rows)
    n_freq = HEAD_DIM // 4
    inv_freq = ROPE_THETA ** (-jnp.arange(n_freq, dtype=F32) / n_freq)
    ang = jnp.concatenate([row[:, None] * inv_freq, col[:, None] * inv_freq], axis=-1)
    ang = jnp.concatenate([ang, ang], axis=-1)
    sign = jnp.concatenate([-jnp.ones((HEAD_DIM // 2,), F32), jnp.ones((HEAD_DIM // 2,), F32)])
    cos = jnp.concatenate([jnp.ones((ctx, HEAD_DIM), F32), jnp.cos(ang)], axis=0)
    sin = jnp.concatenate([jnp.zeros((ctx, HEAD_DIM), F32), jnp.sin(ang) * sign], axis=0)
    return jnp.tile(cos, (batch, 1)), jnp.tile(sin, (batch, 1))


def kernel(x, c, ctx, c_ctx, w_mod, b_mod, g_mix, g_ffn, attn_w_qkv, attn_sink, attn_w_o,
           mlstm_w_in, mlstm_b_in, mlstm_g_head, mlstm_w_o, ffn_w_up, ffn_conv_w, ffn_conv_b,
           ffn_w_down, g_final):
    batch, seq, d = x.shape
    lc = ctx.shape[1]
    depth = w_mod.shape[0]
    p_rows = lc + seq
    sub = lc
    assert seq % sub == 0 and sub % WINDOW == 0 and sub % CHUNK == 0 and seq % GRID_W == 0
    n_sub = batch * p_rows // sub
    tm = 2 * sub if n_sub % 2 == 0 else sub
    n_q = attn_sink.shape[1]
    n_kv = (attn_w_qkv.shape[2] // HEAD_DIM - n_q) // 2
    n_heads = (mlstm_w_in.shape[2] - 3 * d) // 4
    dv = d // n_heads
    dqk = dv // 2
    main_cols = 3 * d

    u = jnp.concatenate([ctx, x], axis=1).reshape(batch * p_rows, d)

    n_c = batch + 1
    c_rows = -(-n_c // 8) * 8
    cvec = jnp.concatenate([c, c_ctx[None, :], jnp.zeros((c_rows - n_c, d), F32)], axis=0)
    mods = _modulation(cvec, w_mod, b_mod)
    mods_sb = jnp.concatenate(
        [jnp.broadcast_to(mods[:, None, batch:batch + 1], (depth, batch, lc // sub, 6 * d)),
         jnp.broadcast_to(mods[:, :batch, None], (depth, batch, seq // sub, 6 * d))],
        axis=2).reshape(depth, n_sub, 6, 1, d)

    cos_u, sin_u = _rope_tables(batch, seq, lc)

    w_qkv = attn_w_qkv.astype(BF16)
    w_attn_o = attn_w_o.astype(BF16)
    w_in = mlstm_w_in.astype(BF16)
    w_mlstm_o = mlstm_w_o.astype(BF16)
    w_up = ffn_w_up.astype(BF16)
    w_down = ffn_w_down.astype(BF16)
    g_mix3 = g_mix.reshape(depth, 1, d)
    g_ffn3 = g_ffn.reshape(depth, 1, d)
    b_in3 = mlstm_b_in.reshape(mlstm_b_in.shape[0], 1, -1)
    conv_b3 = ffn_conv_b.reshape(depth, 1, -1)

    for i in range(depth):
        jm = i // 2
        if i % 2 == 0:
            qkv = _qkv_proj(u, g_mix3, mods_sb, w_qkv, cos_u, sin_u, layer=i, w_layer=jm,
                            tm=tm, sub=sub, n_q=n_q, n_kv=n_kv)
            y = _attention(qkv, attn_sink[jm], batch=batch, seq=seq, ctx=lc, n_q=n_q, n_kv=n_kv)
            w_o = w_attn_o
        else:
            wgt = mlstm_w_in[jm][:, main_cols:].reshape(d, 4, n_heads)
            bgt = mlstm_b_in[jm][main_cols:].reshape(4, n_heads)
            lane_pad = GATE_F_LANE - 2 * n_heads
            wg = jnp.concatenate([wgt[:, 0], wgt[:, 2], jnp.zeros((d, lane_pad), F32),
                                  wgt[:, 1], wgt[:, 3], jnp.zeros((d, lane_pad), F32)], axis=1).astype(BF16)
            bg = jnp.concatenate([bgt[0], bgt[2], jnp.zeros((lane_pad,), F32),
                                  bgt[1], bgt[3], jnp.zeros((lane_pad,), F32)]).reshape(1, LANE)
            z, gates = _mlstm_in_proj(u, g_mix3, mods_sb, w_in, b_in3, wg, bg, layer=i, w_layer=jm,
                                      tm=tm, sub=sub, dqk=dqk)
            og = _mlstm_gates(gates, batch=batch, p_rows=p_rows, ctx=lc, n_heads=n_heads)
            gates_r, gates_c = _gate_layouts(og, batch=batch, p_rows=p_rows, n_heads=n_heads)
            y = _mlstm_core(z, gates_r, gates_c, mlstm_g_head[jm].reshape(1, n_heads * dv),
                            batch=batch, p_rows=p_rows, ctx=lc, n_heads=n_heads, dqk=dqk, dv=dv)
            w_o = w_mlstm_o
        u, h2 = _proj_residual(y, w_o, u, mods_sb, g_ffn3, layer=i, w_layer=jm, tm=tm, sub=sub)
        u = _conv_ffn(h2, u, mods_sb, w_up, ffn_conv_w, conv_b3, w_down, layer=i,
                      tm=tm, sub=sub, p_rows=p_rows, ctx=lc)
    return _final_norm(u, g_final.reshape(1, d), batch=batch, seq=seq, ctx=lc, sub=sub)
```

```python
import functools

import numpy as np
import jax
import jax.numpy as jnp
from jax import lax
from jax.experimental import pallas as pl
from jax.experimental.pallas import tpu as pltpu

F32 = jnp.float32
BF16 = jnp.bfloat16

RMS_EPS = 1e-6
HEAD_DIM = 128
WINDOW = 128
GRID_W = 64
ROPE_THETA = 10000.0
NEG_INF = -1e30
CHUNK = 128
CONV_W = 3
LANE = 128
HALO = 8
BF16_ROWS = 16
FFN_CHUNK = 512
GATE_ROWS = 64
NORM_ROWS = 16
PROJ_TILE = 3072
MOD_SHIFT1, MOD_SCALE1, MOD_GATE1, MOD_SHIFT2, MOD_SCALE2, MOD_GATE2 = range(6)
V7X_VMEM_BYTES = 64 * 1024 * 1024
VMEM_LIMIT = V7X_VMEM_BYTES * 7 // 8


def _params(sem):
    return pltpu.CompilerParams(dimension_semantics=sem, vmem_limit_bytes=VMEM_LIMIT)


def _sigmoid(v):
    return 1.0 / (1.0 + jnp.exp(-v))


def _log_sigmoid(v):
    return jnp.minimum(v, 0.0) - jnp.log(1.0 + jnp.exp(-jnp.abs(v)))


def _mod_spec(nsb, d, layer, kind):
    return pl.BlockSpec((None, nsb, None, 1, d), lambda i, *_: (layer, i, kind, 0, 0))


def _layer_spec(shape, layer, index):
    return pl.BlockSpec((None,) + shape, lambda *g: (layer,) + index(*g))


def _norm_mod(xv, g, scale, shift):
    ms = jnp.mean(xv * xv, axis=-1, keepdims=True)
    y = xv * lax.rsqrt(ms + RMS_EPS)
    return (y * g) * (1.0 + scale) + shift


def _norm_mod_to(h_scr, x_ref, g_ref, sc_ref, sh_ref, sub):
    for s in range(x_ref.shape[0] // sub):
        gain = g_ref[...] * (1.0 + sc_ref[s])
        shift = sh_ref[s]
        for r in range(s * sub, (s + 1) * sub, NORM_ROWS):
            xv = x_ref[r:r + NORM_ROWS, :]
            ms = jnp.mean(xv * xv, axis=-1, keepdims=True)
            h_scr[r:r + NORM_ROWS, :] = (xv * lax.rsqrt(ms + RMS_EPS) * gain + shift).astype(BF16)


def _mod_kernel(c_ref, w_ref, b_ref, o_ref):
    cv = c_ref[...]
    s = (cv * _sigmoid(cv)).astype(BF16)
    o_ref[...] = jnp.dot(s, w_ref[...].astype(BF16), preferred_element_type=F32) + b_ref[...]


def _modulation(cvec, w_mod, b_mod):
    depth, d, n = w_mod.shape
    r = cvec.shape[0]
    tn = min(1024, n)
    return pl.pallas_call(
        _mod_kernel,
        grid=(depth, n // tn),
        in_specs=[pl.BlockSpec((r, d), lambda i, j: (0, 0)),
                  pl.BlockSpec((None, d, tn), lambda i, j: (i, 0, j)),
                  pl.BlockSpec((None, 1, tn), lambda i, j: (i, 0, j))],
        out_specs=pl.BlockSpec((None, r, tn), lambda i, j: (i, 0, j)),
        out_shape=jax.ShapeDtypeStruct((depth, r, n), F32),
        compiler_params=_params(("parallel", "parallel")),
        name="modulation",
    )(cvec, w_mod, b_mod.reshape(depth, 1, n))


def _qkv_kernel(x_ref, g_ref, sc_ref, sh_ref, w_ref, cos_ref, sin_ref, o_ref, h_scr,
                *, sub, n_q, n_rope, q_scale):
    j = pl.program_id(1)
    tm, tn = o_ref.shape

    @pl.when(j == 0)
    def _():
        _norm_mod_to(h_scr, x_ref, g_ref, sc_ref, sh_ref, sub)

    acc = jnp.dot(h_scr[...], w_ref[...], preferred_element_type=F32)
    cos = cos_ref[...]
    sin = sin_ref[...]
    heads_per_tile = tn // HEAD_DIM
    for l in range(heads_per_tile):
        t = acc[:, l * HEAD_DIM:(l + 1) * HEAD_DIM]
        head = j * heads_per_tile + l
        tr = t * cos + pltpu.roll(t, HEAD_DIM // 2, 1) * sin
        t = jnp.where(head < n_rope, tr, t)
        t = t * jnp.where(head < n_q, q_scale, 1.0)
        o_ref[:, l * HEAD_DIM:(l + 1) * HEAD_DIM] = t.astype(BF16)


def _qkv_proj(u, g, mods, w, cos_u, sin_u, *, layer, w_layer, tm, sub, n_q, n_kv):
    n_rows, d = u.shape
    n = w.shape[2]
    tn = n if n <= PROJ_TILE else PROJ_TILE
    assert n % tn == 0
    nsb = tm // sub
    kern = functools.partial(_qkv_kernel, sub=sub, n_q=n_q, n_rope=n_q + n_kv,
                             q_scale=HEAD_DIM ** -0.5)
    return pl.pallas_call(
        kern,
        grid=(n_rows // tm, n // tn),
        in_specs=[pl.BlockSpec((tm, d), lambda i, j: (i, 0)),
                  _layer_spec((1, d), layer, lambda i, j: (0, 0)),
                  _mod_spec(nsb, d, layer, MOD_SCALE1),
                  _mod_spec(nsb, d, layer, MOD_SHIFT1),
                  _layer_spec((d, tn), w_layer, lambda i, j: (0, j)),
                  pl.BlockSpec((tm, HEAD_DIM), lambda i, j: (i, 0)),
                  pl.BlockSpec((tm, HEAD_DIM), lambda i, j: (i, 0))],
        out_specs=pl.BlockSpec((tm, tn), lambda i, j: (i, j)),
        out_shape=jax.ShapeDtypeStruct((n_rows, n), BF16),
        scratch_shapes=[pltpu.VMEM((tm, d), BF16)],
        compiler_params=_params(("parallel", "arbitrary")),
        name="attn_qkv",
    )(u, g, mods, mods, w, cos_u, sin_u)


def _softmax_pv(s, sink, v):
    m = jnp.maximum(jnp.max(s, axis=-1, keepdims=True), sink)
    p = jnp.exp(s - m)
    l = jnp.sum(p, axis=-1, keepdims=True) + jnp.exp(sink - m)
    o = jnp.dot(p.astype(BF16), v, preferred_element_type=F32)
    return o * (1.0 / l)


def _stack_heads(q, group):
    return jnp.concatenate([q[:, g * HEAD_DIM:(g + 1) * HEAD_DIM] for g in range(group)], axis=0)


def _attn_x_kernel(q_ref, kp_ref, ko_ref, kn_ref, kc_ref, vp_ref, vo_ref, vn_ref, vc_ref,
                   mask_ref, sink_ref, o_ref, *, group, n_kv):
    blk = q_ref.shape[0]
    valid = mask_ref[...] > 0.0

    def scores(h):
        kv = slice(h * HEAD_DIM, (h + 1) * HEAD_DIM)
        qs = _stack_heads(q_ref[:, h * group * HEAD_DIM:(h + 1) * group * HEAD_DIM], group)
        k = jnp.concatenate([kp_ref[:, kv], ko_ref[:, kv], kn_ref[:, kv], kc_ref[:, kv]], axis=0)
        s = lax.dot_general(qs, k, (((1,), (1,)), ((), ())), preferred_element_type=F32)
        return jnp.concatenate([jnp.where(valid, s[:, :3 * blk], NEG_INF), s[:, 3 * blk:]], axis=1)

    def finish(h, s):
        kv = slice(h * HEAD_DIM, (h + 1) * HEAD_DIM)
        v = jnp.concatenate([vp_ref[:, kv], vo_ref[:, kv], vn_ref[:, kv], vc_ref[:, kv]], axis=0)
        o = _softmax_pv(s, sink_ref[h], v)
        for g in range(group):
            c0 = (h * group + g) * HEAD_DIM
            o_ref[:, c0:c0 + HEAD_DIM] = o[g * blk:(g + 1) * blk, :].astype(BF16)

    s_next = scores(0)
    for h in range(n_kv):
        s_cur = s_next
        if h + 1 < n_kv:
            s_next = scores(h + 1)
        finish(h, s_cur)


def _attn_c_kernel(q_ref, kc_ref, vc_ref, sink_ref, o_ref, *, group):
    lc = q_ref.shape[0]
    qs = _stack_heads(q_ref[...], group)
    s = lax.dot_general(qs, kc_ref[...], (((1,), (1,)), ((), ())), preferred_element_type=F32)
    o = _softmax_pv(s, sink_ref[...], vc_ref[...])
    for g in range(group):
        o_ref[:, g * HEAD_DIM:(g + 1) * HEAD_DIM] = o[g * lc:(g + 1) * lc, :].astype(BF16)


def _window_mask(group, blk, n_blocks):
    qi = np.arange(group * blk)[:, None] % blk
    kj = np.arange(3 * blk)[None, :]
    band = (kj - qi >= 0) & (kj - qi <= 2 * WINDOW)
    first = band & (kj >= blk)
    last = band & (kj < 2 * blk)
    kinds = [first & last if n_blocks == 1 else first, band, last]
    return np.stack(kinds).astype(np.float32)


def _attention(qkv, sink, *, batch, seq, ctx, n_q, n_kv):
    n_rows = qkv.shape[0]
    group = n_q // n_kv
    blk = WINDOW
    p_rows = ctx + seq
    bpb = p_rows // blk
    cb = ctx // blk
    nb = seq // blk
    last_blk = n_rows // blk - 1
    qw = n_q * HEAD_DIM
    kw = n_kv * HEAD_DIM
    assert qw % kw == 0
    k_col = qw // kw
    spc = p_rows // ctx
    gw = group * HEAD_DIM
    sink_g = sink.reshape(n_kv, group, 1).astype(F32)
    sink_x = jnp.broadcast_to(sink_g[:, :, None, :], (n_kv, group, blk, 1)).reshape(n_kv, group * blk, 1)
    sink_c = jnp.broadcast_to(sink_g[:, :, None, :], (n_kv, group, ctx, 1)).reshape(n_kv, group * ctx, 1)
    mask = jnp.asarray(_window_mask(group, blk, nb))

    def kv_specs(col):
        return [pl.BlockSpec((blk, kw), lambda b, n: (jnp.maximum(b * bpb + cb + n - 1, 0), col)),
                pl.BlockSpec((blk, kw), lambda b, n: (b * bpb + cb + n, col)),
                pl.BlockSpec((blk, kw), lambda b, n: (jnp.minimum(b * bpb + cb + n + 1, last_blk), col)),
                pl.BlockSpec((ctx, kw), lambda b, n: (b * spc, col))]

    out_x = pl.pallas_call(
        functools.partial(_attn_x_kernel, group=group, n_kv=n_kv),
        grid=(batch, nb),
        in_specs=[pl.BlockSpec((blk, qw), lambda b, n: (b * bpb + cb + n, 0))]
        + kv_specs(k_col) + kv_specs(k_col + 1)
        + [pl.BlockSpec((None, group * blk, 3 * blk),
                        lambda b, n: (jnp.where(n == 0, 0, jnp.where(n == nb - 1, 2, 1)), 0, 0)),
           pl.BlockSpec((n_kv, group * blk, 1), lambda b, n: (0, 0, 0))],
        out_specs=pl.BlockSpec((blk, qw), lambda b, n: (b * bpb + cb + n, 0)),
        out_shape=jax.ShapeDtypeStruct((n_rows, qw), BF16),
        compiler_params=_params(("parallel", "parallel")),
        name="attn_latent",
    )(qkv, qkv, qkv, qkv, qkv, qkv, qkv, qkv, qkv, mask, sink_x)

    out = pl.pallas_call(
        functools.partial(_attn_c_kernel_alias, group=group),
        grid=(batch, n_kv),
        in_specs=[pl.BlockSpec((ctx, gw), lambda b, h: (b * spc, h)),
                  pl.BlockSpec((ctx, HEAD_DIM), lambda b, h: (b * spc, n_q + h)),
                  pl.BlockSpec((ctx, HEAD_DIM), lambda b, h: (b * spc, n_q + n_kv + h)),
                  pl.BlockSpec((None, group * ctx, 1), lambda b, h: (h, 0, 0)),
                  pl.BlockSpec(memory_space=pl.ANY)],
        out_specs=pl.BlockSpec((ctx, gw), lambda b, h: (b * spc, h)),
        out_shape=jax.ShapeDtypeStruct((n_rows, qw), BF16),
        input_output_aliases={4: 0},
        compiler_params=_params(("parallel", "parallel")),
        name="attn_context",
    )(qkv, qkv, qkv, sink_c, out_x)
    return out


def _attn_c_kernel_alias(q_ref, kc_ref, vc_ref, sink_ref, prev_ref, o_ref, *, group):
    del prev_ref
    _attn_c_kernel(q_ref, kc_ref, vc_ref, sink_ref, o_ref, group=group)


def _proj_res_kernel(y_ref, w_ref, x_ref, gt_ref, g_ref, sc_ref, sh_ref, o_ref, h_ref, *, sub):
    acc = jnp.dot(y_ref[...], w_ref[...], preferred_element_type=F32)
    g = g_ref[...]
    for s in range(o_ref.shape[0] // sub):
        rows = slice(s * sub, (s + 1) * sub)
        xn = x_ref[rows, :] + gt_ref[s] * acc[rows, :]
        o_ref[rows, :] = xn
        h_ref[rows, :] = _norm_mod(xn, g, sc_ref[s], sh_ref[s]).astype(BF16)


def _proj_residual(y, w, u, mods, g, *, layer, w_layer, tm, sub):
    n_rows, d = u.shape
    k = y.shape[1]
    nsb = tm // sub
    return pl.pallas_call(
        functools.partial(_proj_res_kernel, sub=sub),
        grid=(n_rows // tm,),
        in_specs=[pl.BlockSpec((tm, k), lambda i: (i, 0)),
                  _layer_spec((k, d), w_layer, lambda i: (0, 0)),
                  pl.BlockSpec((tm, d), lambda i: (i, 0)),
                  _mod_spec(nsb, d, layer, MOD_GATE1),
                  _layer_spec((1, d), layer, lambda i: (0, 0)),
                  _mod_spec(nsb, d, layer, MOD_SCALE2),
                  _mod_spec(nsb, d, layer, MOD_SHIFT2)],
        out_specs=[pl.BlockSpec((tm, d), lambda i: (i, 0)),
                   pl.BlockSpec((tm, d), lambda i: (i, 0))],
        out_shape=[jax.ShapeDtypeStruct((n_rows, d), F32),
                   jax.ShapeDtypeStruct((n_rows, d), BF16)],
        compiler_params=_params(("parallel",)),
        name="proj_residual",
    )(y, w, u, mods, g, mods, mods)


def _mlstm_in_kernel(x_ref, g_ref, sc_ref, sh_ref, w_ref, b_ref, wg_ref, bg_ref, z_ref, gate_ref,
                     h_scr, *, sub, group, k_scale):
    j = pl.program_id(1)
    tm, tn = z_ref.shape

    @pl.when(j == 0)
    def _():
        _norm_mod_to(h_scr, x_ref, g_ref, sc_ref, sh_ref, sub)
        gate_ref[...] = jnp.dot(h_scr[...], wg_ref[...], preferred_element_type=F32) + bg_ref[...]

    for l in range(tn // group):
        cols = slice(l * group, (l + 1) * group)
        acc = jnp.dot(h_scr[...], w_ref[:, cols], preferred_element_type=F32) + b_ref[:, cols]
        is_k = j * (tn // group) + l == 1
        z_ref[:, cols] = (acc * jnp.where(is_k, k_scale, 1.0)).astype(BF16)


def _mlstm_in_proj(u, g, mods, w, b, wg, bg, *, layer, w_layer, tm, sub, dqk):
    n_rows, d = u.shape
    n = 3 * d
    qk_cols = d // 2
    tn = n // 2 if n // 2 <= PROJ_TILE and (n // 2) % qk_cols == 0 else qk_cols
    nsb = tm // sub
    kern = functools.partial(_mlstm_in_kernel, sub=sub, group=qk_cols, k_scale=dqk ** -0.5)
    return pl.pallas_call(
        kern,
        grid=(n_rows // tm, n // tn),
        in_specs=[pl.BlockSpec((tm, d), lambda i, j: (i, 0)),
                  _layer_spec((1, d), layer, lambda i, j: (0, 0)),
                  _mod_spec(nsb, d, layer, MOD_SCALE1),
                  _mod_spec(nsb, d, layer, MOD_SHIFT1),
                  _layer_spec((d, tn), w_layer, lambda i, j: (0, j)),
                  _layer_spec((1, tn), w_layer, lambda i, j: (0, j)),
                  pl.BlockSpec((d, 128), lambda i, j: (0, 0)),
                  pl.BlockSpec((1, 128), lambda i, j: (0, 0))],
        out_specs=[pl.BlockSpec((tm, tn), lambda i, j: (i, j)),
                   pl.BlockSpec((tm, 128), lambda i, j: (i, 0))],
        out_shape=[jax.ShapeDtypeStruct((n_rows, n), BF16),
                   jax.ShapeDtypeStruct((n_rows, 128), F32)],
        scratch_shapes=[pltpu.VMEM((tm, d), BF16)],
        compiler_params=_params(("parallel", "arbitrary")),
        name="mlstm_in",
    )(u, g, mods, mods, w, b, wg, bg)


GATE_F_LANE = 64
N_GATE_OUT = 6
N_COL = 5


def _chunk_scan(x, scr, pos, op, ident, reverse):
    r = x.shape[0]
    sh = 1
    while sh < CHUNK:
        scr[CHUNK:CHUNK + r, :] = x
        if reverse:
            other = scr[CHUNK + sh:CHUNK + sh + r, :]
            ok = pos < CHUNK - sh
        else:
            other = scr[CHUNK - sh:CHUNK - sh + r, :]
            ok = pos >= sh
        x = op(x, jnp.where(ok, other, ident))
        sh *= 2
    return x


def _mlstm_gate_kernel(g_ref, o_ref, b_scr, cm_scr, shift_scr, *, n_heads, n_ctx_chunks, tile):
    p_rows = g_ref.shape[0]
    n_chunks = p_rows // CHUNK
    shift_scr[...] = jnp.zeros_like(shift_scr)
    o_ref[0:N_COL] = jnp.zeros((N_COL,) + o_ref.shape[1:], F32)
    fwd_tile = lax.broadcasted_iota(jnp.int32, (tile, LANE), 1) < n_heads
    pos = lax.broadcasted_iota(jnp.int32, (tile, LANE), 0) % CHUNK

    def local(t, _):
        rows = pl.ds(pl.multiple_of(t * tile, tile), tile)
        g = g_ref[rows, :]
        lf = _log_sigmoid(pltpu.roll(g, LANE - GATE_F_LANE, 1))
        b = jnp.where(fwd_tile, _chunk_scan(lf, shift_scr, pos, jnp.add, 0.0, False),
                      _chunk_scan(lf, shift_scr, pos, jnp.add, 0.0, True))
        a = g - b
        cm = jnp.where(fwd_tile, _chunk_scan(a, shift_scr, pos, jnp.maximum, -jnp.inf, False),
                       _chunk_scan(a, shift_scr, pos, jnp.maximum, -jnp.inf, True))
        b_scr[rows, :] = b
        cm_scr[rows, :] = cm
        o_ref[N_COL, rows, :] = a
        return 0

    lax.fori_loop(0, p_rows // tile, local, 0)

    fwd_chunk = lax.broadcasted_iota(jnp.int32, (CHUNK, LANE), 1) < n_heads

    def step(c, m, forward):
        rows = pl.ds(pl.multiple_of(c * CHUNK, CHUNK), CHUNK)
        end_row = CHUNK - 1 if forward else 0
        mine = fwd_chunk if forward else jnp.logical_not(fwd_chunk)
        b = b_scr[rows, :]
        a = o_ref[N_COL, rows, :]
        cm = cm_scr[rows, :]
        b_end = b[end_row:end_row + 1, :]
        m_t = b + jnp.maximum(m, cm)
        m_new = jnp.maximum(b_end + m, b_end + cm[end_row:end_row + 1, :])
        vals = (b - m_t, jnp.exp(b + m - m_t), jnp.exp(-m_t), jnp.exp(b_end + a - m_new),
                jnp.broadcast_to(jnp.exp(b_end + m - m_new), (CHUNK, LANE)))
        for q, val in enumerate(vals):
            o_ref[q, rows, :] = jnp.where(mine, val, o_ref[q, rows, :])
        return m_new

    def body(j, carry):
        mf, mb = carry
        mf = step(j, mf, True)
        cb = jnp.where(j < n_ctx_chunks, n_ctx_chunks - 1 - j, n_chunks - 1 + n_ctx_chunks - j)
        mb = step(cb, mb, False)
        return mf, mb

    zm = jnp.zeros((1, LANE), F32)
    lax.fori_loop(0, n_chunks, body, (zm, zm))


def _mlstm_gates(gates, *, batch, p_rows, ctx, n_heads):
    n_rows = gates.shape[0]
    tile = ctx
    kern = functools.partial(_mlstm_gate_kernel, n_heads=n_heads, n_ctx_chunks=ctx // CHUNK, tile=tile)
    return pl.pallas_call(
        kern,
        grid=(batch,),
        in_specs=[pl.BlockSpec((p_rows, LANE), lambda b: (b, 0))],
        out_specs=pl.BlockSpec((N_GATE_OUT, p_rows, LANE), lambda b: (0, b, 0)),
        out_shape=jax.ShapeDtypeStruct((N_GATE_OUT, n_rows, LANE), F32),
        scratch_shapes=[pltpu.VMEM((p_rows, LANE), F32), pltpu.VMEM((p_rows, LANE), F32),
                        pltpu.VMEM((tile + 2 * CHUNK, LANE), F32)],
        compiler_params=_params(("parallel",)),
        name="mlstm_gates",
    )(gates)


def _mlstm_chunk(q, k, v, col, a_row, c_scr, n, d, reverse):
    ch = q.shape[0]
    t_idx = lax.broadcasted_iota(jnp.int32, (ch, ch), 0)
    s_idx = lax.broadcasted_iota(jnp.int32, (ch, ch), 1)
    seen = (s_idx >= t_idx) if reverse else (s_idx <= t_idx)
    c0 = d * N_COL
    row_term = col[:, c0:c0 + 1]
    e_inter = col[:, c0 + 1:c0 + 2]
    floor = col[:, c0 + 2:c0 + 3]
    wg_col = col[:, c0 + 3:c0 + 4]
    e_prev = col[0:1, c0 + 4:c0 + 5]

    w = jnp.where(seen, jnp.exp(row_term + a_row), 0.0)
    s = lax.dot_general(q, k, (((1,), (1,)), ((), ())), preferred_element_type=F32) * w
    c_old = c_scr[...]
    num = (e_inter * jnp.dot(q, c_old.astype(BF16), preferred_element_type=F32)
           + jnp.dot(s.astype(BF16), v, preferred_element_type=F32))
    qn = jnp.sum(q.astype(F32) * n, axis=1, keepdims=True)
    den = e_inter * qn + jnp.sum(s, axis=1, keepdims=True)
    h = num * (1.0 / jnp.maximum(jnp.abs(den), floor))

    vw = (wg_col * v.astype(F32)).astype(BF16)
    c_scr[...] = e_prev * c_old + lax.dot_general(k, vw, (((0,), (0,)), ((), ())),
                                                  preferred_element_type=F32)
    n_new = e_prev * n + jnp.sum(wg_col * k.astype(F32), axis=0, keepdims=True)
    return h, n_new


def _mlstm_core_kernel(q_ref, k_ref, v_ref, o_ref, gr_ref, gc_ref, gh_ref, y_ref,
                       hf_scr, hb_scr, cf_scr, cb_scr, *, n_ctx_chunks, out_rows):
    n_chunks = gr_ref.shape[0]
    dqk = q_ref.shape[1]
    cf_scr[...] = jnp.zeros_like(cf_scr)
    cb_scr[...] = jnp.zeros_like(cb_scr)

    def run(c, c_scr, h_scr, n, d, reverse):
        rows = pl.ds(pl.multiple_of(c * CHUNK, CHUNK), CHUNK)
        h, n = _mlstm_chunk(q_ref[rows, :], k_ref[rows, :], v_ref[rows, :], gc_ref[rows, :],
                            gr_ref[c][d:d + 1, :], c_scr, n, d, reverse)
        h_scr[rows, :] = h
        return n

    def body(j, carry):
        nf, nb = carry
        nf = run(j, cf_scr, hf_scr, nf, 0, False)
        cb = jnp.where(j < n_ctx_chunks, n_ctx_chunks - 1 - j, n_chunks - 1 + n_ctx_chunks - j)
        nb = run(cb, cb_scr, hb_scr, nb, 1, True)
        return nf, nb

    zn = jnp.zeros((1, dqk), F32)
    lax.fori_loop(0, n_chunks, body, (zn, zn), unroll=2)

    gh = gh_ref[...]

    def finish(r, _):
        rows = pl.ds(pl.multiple_of(r * out_rows, out_rows), out_rows)
        h = hf_scr[rows, :] + hb_scr[rows, :]
        hn = h * lax.rsqrt(jnp.mean(h * h, axis=-1, keepdims=True) + RMS_EPS)
        y_ref[rows, :] = (_sigmoid(o_ref[rows, :].astype(F32)) * hn * gh).astype(BF16)
        return 0

    lax.fori_loop(0, y_ref.shape[0] // out_rows, finish, 0)


def _mlstm_core(z, gates_r, gates_c, g_head, *, batch, p_rows, ctx, n_heads, dqk, dv):
    n_rows = z.shape[0]
    n_chunks = p_rows // CHUNK
    kern = functools.partial(_mlstm_core_kernel, n_ctx_chunks=ctx // CHUNK, out_rows=ctx)
    return pl.pallas_call(
        kern,
        grid=(batch, n_heads),
        in_specs=[pl.BlockSpec((p_rows, dqk), lambda b, h: (b, h)),
                  pl.BlockSpec((p_rows, dqk), lambda b, h: (b, n_heads + h)),
                  pl.BlockSpec((p_rows, dv), lambda b, h: (b, n_heads + h)),
                  pl.BlockSpec((p_rows, dv), lambda b, h: (b, 2 * n_heads + h)),
                  pl.BlockSpec((None, None, n_chunks, 2, CHUNK), lambda b, h: (b, h, 0, 0, 0)),
                  pl.BlockSpec((None, None, p_rows, 2 * N_COL), lambda b, h: (b, h, 0, 0)),
                  pl.BlockSpec((1, dv), lambda b, h: (0, h))],
        out_specs=pl.BlockSpec((p_rows, dv), lambda b, h: (b, h)),
        out_shape=jax.ShapeDtypeStruct((n_rows, n_heads * dv), BF16),
        scratch_shapes=[pltpu.VMEM((p_rows, dv), F32), pltpu.VMEM((p_rows, dv), F32),
                        pltpu.VMEM((dqk, dv), F32), pltpu.VMEM((dqk, dv), F32)],
        compiler_params=_params(("parallel", "parallel")),
        name="mlstm_core",
    )(z, z, z, z, gates_r, gates_c, g_head)


def _gate_layouts(og, *, batch, p_rows, n_heads):
    g = og[:, :, :2 * n_heads].reshape(N_GATE_OUT, batch, p_rows, 2, n_heads)
    cols = g[:N_COL].transpose(1, 4, 2, 3, 0).reshape(batch, n_heads, p_rows, 2 * N_COL)
    rows = g[N_COL].reshape(batch, p_rows // CHUNK, CHUNK, 2, n_heads).transpose(0, 4, 1, 3, 2)
    return rows, cols


def _halo_up_kernel(h_ref, w_ref, o_ref):
    o_ref[...] = jnp.dot(h_ref[...], w_ref[...], preferred_element_type=F32)


def _halo_up(h, w_up, *, layer, tm):
    n_rows, d = h.shape
    n = w_up.shape[2]
    nt = n_rows // tm
    ht = h.reshape(nt, tm, d)
    first, last = ht[:, 0], ht[:, tm - 1]
    prev = jnp.concatenate([last[:1], last[:nt - 1]], axis=0)
    nxt = jnp.concatenate([first[1:], first[nt - 1:]], axis=0)
    hh = jnp.stack([prev, nxt], axis=1).reshape(2 * nt, d)
    rows = 2 * nt
    tn = 2 * FFN_CHUNK
    uh = pl.pallas_call(
        _halo_up_kernel,
        grid=(n // tn,),
        in_specs=[pl.BlockSpec((rows, d), lambda j: (0, 0)),
                  _layer_spec((d, tn), layer, lambda j: (0, j))],
        out_specs=pl.BlockSpec((rows, tn), lambda j: (0, j)),
        out_shape=jax.ShapeDtypeStruct((rows, n), F32),
        compiler_params=_params(("parallel",)),
        name="ffn_halo",
    )(hh, w_up)
    return uh.reshape(nt, 2, n)


def _ffn_kernel(h_ref, x_ref, gt_ref, wa0_ref, wb0_ref, wa1_ref, wb1_ref,
                ha0_ref, hb0_ref, ha1_ref, hb1_ref, cwa_ref, cwb_ref, cba_ref, cbb_ref, wd_ref,
                o_ref, acc_scr, u0_scr, u1_scr, a0_scr, a1_scr, *, sub, p_rows, ctx):
    i = pl.program_id(0)
    j = pl.program_id(1)
    tm = o_ref.shape[0]
    nsb = tm // sub
    fh = wa0_ref.shape[1]
    lt = fh // LANE
    seg = sub + 2 * HALO

    @pl.when(j == 0)
    def _():
        acc_scr[...] = jnp.zeros_like(acc_scr)

    def edge_open(k):
        pos = (i * tm + k * sub) % p_rows
        return jnp.where((pos == 0) | (pos == ctx), 0.0, 1.0)

    opens = [edge_open(k) for k in range(nsb + 1)]

    def up(w_refs, halo_refs, u_scr):
        for br, (w_ref, halo_ref) in enumerate(zip(w_refs, halo_refs)):
            ue = jnp.dot(h_ref[...], w_ref[...], preferred_element_type=F32)
            halo = halo_ref[...]
            for s in range(nsb):
                r0 = s * sub
                b0 = s * seg
                for t in range(lt):
                    lanes = slice(t * LANE, (t + 1) * LANE)
                    tt = br * lt + t
                    before = (jnp.broadcast_to(halo[0:1, lanes], (HALO, LANE)) if s == 0
                              else ue[r0 - HALO:r0, lanes])
                    after = (jnp.broadcast_to(halo[1:2, lanes], (HALO, LANE)) if s == nsb - 1
                             else ue[r0 + sub:r0 + sub + HALO, lanes])
                    u_scr[tt, b0:b0 + HALO, :] = before * opens[s]
                    u_scr[tt, b0 + HALO:b0 + HALO + sub, :] = ue[r0:r0 + sub, lanes]
                    u_scr[tt, b0 + HALO + sub:b0 + seg, :] = after * opens[s + 1]

    def gate(c, u_scr, act_scr):
        cols = slice(c * fh, (c + 1) * fh)
        cws = (cwa_ref[:, cols], cwb_ref[:, cols])
        cbs = (cba_ref[:, cols], cbb_ref[:, cols])

        def conv_piece(br, t, r0):
            lanes = slice(t * LANE, (t + 1) * LANE)
            tt = br * lt + t + jnp.minimum(j, 0)
            cw, cb = cws[br], cbs[br]
            cur = u_scr[tt, r0:r0 + GATE_ROWS, :]
            prv = u_scr[tt, r0 - 1:r0 - 1 + GATE_ROWS, :]
            nxt = u_scr[tt, r0 + 1:r0 + 1 + GATE_ROWS, :]
            return cb[:, lanes] + prv * cw[0:1, lanes] + cur * cw[1:2, lanes] + nxt * cw[2:3, lanes]

        for s in range(nsb):
            for r in range(sub // GATE_ROWS):
                r0 = s * seg + HALO + r * GATE_ROWS
                o0 = s * sub + r * GATE_ROWS
                for t in range(lt):
                    a = conv_piece(0, t, r0)
                    b = conv_piece(1, t, r0)
                    act_scr[o0:o0 + GATE_ROWS, t * LANE:(t + 1) * LANE] = (a * _sigmoid(a) * b).astype(BF16)

    def down(c, act_scr):
        acc_scr[...] += jnp.dot(act_scr[...], wd_ref[c * fh:(c + 1) * fh, :], preferred_element_type=F32)

    up((wa0_ref, wb0_ref), (ha0_ref, hb0_ref), u0_scr)
    up((wa1_ref, wb1_ref), (ha1_ref, hb1_ref), u1_scr)
    gate(0, u0_scr, a0_scr)
    down(0, a0_scr)
    gate(1, u1_scr, a1_scr)
    down(1, a1_scr)

    @pl.when(j == pl.num_programs(1) - 1)
    def _():
        for s in range(nsb):
            rows = slice(s * sub, (s + 1) * sub)
            o_ref[rows, :] = x_ref[rows, :] + gt_ref[s] * acc_scr[rows, :]


def _conv_ffn(h, u, mods, w_up, conv_w, conv_b, w_down, *, layer, tm, sub, p_rows, ctx):
    n_rows, d = u.shape
    d_ff = w_down.shape[1]
    fc = FFN_CHUNK
    assert d_ff % fc == 0
    fh = fc // 2
    nfc = d_ff // fc
    nsb = tm // sub
    kern = functools.partial(_ffn_kernel, sub=sub, p_rows=p_rows, ctx=ctx)
    u_shape = (fc // LANE, nsb * (sub + 2 * HALO), LANE)
    halo = _halo_up(h, w_up, layer=layer, tm=tm)

    def half_specs(shape, lead):
        return [pl.BlockSpec(shape, lambda i, j, c=c: lead(i) + (c + 2 * j,))
                for c in (0, 2 * nfc, 1, 2 * nfc + 1)]

    return pl.pallas_call(
        kern,
        grid=(n_rows // tm, nfc),
        in_specs=[pl.BlockSpec((tm, d), lambda i, j: (i, 0)),
                  pl.BlockSpec((tm, d), lambda i, j: (i, 0)),
                  _mod_spec(nsb, d, layer, MOD_GATE2)]
        + half_specs((None, d, fh), lambda i: (layer, 0))
        + half_specs((None, 2, fh), lambda i: (i, 0))
        + [_layer_spec((CONV_W, fc), layer, lambda i, j: (0, j)),
           _layer_spec((CONV_W, fc), layer, lambda i, j: (0, nfc + j)),
           _layer_spec((1, fc), layer, lambda i, j: (0, j)),
           _layer_spec((1, fc), layer, lambda i, j: (0, nfc + j)),
           _layer_spec((fc, d), layer, lambda i, j: (j, 0))],
        out_specs=pl.BlockSpec((tm, d), lambda i, j: (i, 0)),
        out_shape=jax.ShapeDtypeStruct((n_rows, d), F32),
        scratch_shapes=[pltpu.VMEM((tm, d), F32), pltpu.VMEM(u_shape, F32), pltpu.VMEM(u_shape, F32),
                        pltpu.VMEM((tm, fh), BF16), pltpu.VMEM((tm, fh), BF16)],
        compiler_params=_params(("parallel", "arbitrary")),
        name="conv_ffn",
    )(h, u, mods, w_up, w_up, w_up, w_up, halo, halo, halo, halo,
      conv_w, conv_w, conv_b, conv_b, w_down)


def _final_norm_kernel(x_ref, g_ref, o_ref):
    xv = x_ref[...]
    o_ref[...] = xv * lax.rsqrt(jnp.mean(xv * xv, axis=-1, keepdims=True) + RMS_EPS) * g_ref[...]


def _final_norm(u, g, *, batch, seq, ctx, sub):
    d = u.shape[1]
    spb = (ctx + seq) // sub
    cs = ctx // sub
    return pl.pallas_call(
        _final_norm_kernel,
        grid=(batch, seq // sub),
        in_specs=[pl.BlockSpec((sub, d), lambda b, t: (b * spb + cs + t, 0)),
                  pl.BlockSpec((1, d), lambda b, t: (0, 0))],
        out_specs=pl.BlockSpec((None, sub, d), lambda b, t: (b, t, 0)),
        out_shape=jax.ShapeDtypeStruct((batch, seq, d), F32),
        compiler_params=_params(("parallel", "parallel")),
        name="final_norm",
    )(u, g)


def _rope_tables(batch, seq, ctx):
    rows = seq // GRID_W
    row = jnp.repeat(jnp.arange(rows, dtype=F32), GRID_W)
    col = jnp.tile(jnp.arange(GRID_W, dtype=F32), rows)
    n_freq = HEAD_DIM // 4
    inv_freq = ROPE_THETA ** (-jnp.arange(n_freq, dtype=F32) / n_freq)
    ang = jnp.concatenate([row[:, None] * inv_freq, col[:, None] * inv_freq], axis=-1)
    ang = jnp.concatenate([ang, ang], axis=-1)
    sign = jnp.concatenate([-jnp.ones((HEAD_DIM // 2,), F32), jnp.ones((HEAD_DIM // 2,), F32)])
    cos = jnp.concatenate([jnp.ones((ctx, HEAD_DIM), F32), jnp.cos(ang)], axis=0)
    sin = jnp.concatenate([jnp.zeros((ctx, HEAD_DIM), F32), jnp.sin(ang) * sign], axis=0)
    return jnp.tile(cos, (batch, 1)), jnp.tile(sin, (batch, 1))


def kernel(x, c, ctx, c_ctx, w_mod, b_mod, g_mix, g_ffn, attn_w_qkv, attn_sink, attn_w_o,
           mlstm_w_in, mlstm_b_in, mlstm_g_head, mlstm_w_o, ffn_w_up, ffn_conv_w, ffn_conv_b,
           ffn_w_down, g_final):
    batch, seq, d = x.shape
    lc = ctx.shape[1]
    depth = w_mod.shape[0]
    p_rows = lc + seq
    sub = lc
    assert seq % sub == 0 and sub % WINDOW == 0 and sub % CHUNK == 0 and seq % GRID_W == 0
    n_sub = batch * p_rows // sub
    tm = 2 * sub if n_sub % 2 == 0 else sub
    n_q = attn_sink.shape[1]
    n_kv = (attn_w_qkv.shape[2] // HEAD_DIM - n_q) // 2
    n_heads = (mlstm_w_in.shape[2] - 3 * d) // 4
    dv = d // n_heads
    dqk = dv // 2
    main_cols = 3 * d

    u = jnp.concatenate([ctx, x], axis=1).reshape(batch * p_rows, d)

    n_c = batch + 1
    c_rows = -(-n_c // 8) * 8
    cvec = jnp.concatenate([c, c_ctx[None, :], jnp.zeros((c_rows - n_c, d), F32)], axis=0)
    mods = _modulation(cvec, w_mod, b_mod)
    mods_sb = jnp.concatenate(
        [jnp.broadcast_to(mods[:, None, batch:batch + 1], (depth, batch, lc // sub, 6 * d)),
         jnp.broadcast_to(mods[:, :batch, None], (depth, batch, seq // sub, 6 * d))],
        axis=2).reshape(depth, n_sub, 6, 1, d)

    cos_u, sin_u = _rope_tables(batch, seq, lc)

    w_qkv = attn_w_qkv.astype(BF16)
    w_attn_o = attn_w_o.astype(BF16)
    w_in = mlstm_w_in.astype(BF16)
    w_mlstm_o = mlstm_w_o.astype(BF16)
    w_up = ffn_w_up.astype(BF16)
    w_down = ffn_w_down.astype(BF16)
    g_mix3 = g_mix.reshape(depth, 1, d)
    g_ffn3 = g_ffn.reshape(depth, 1, d)
    b_in3 = mlstm_b_in.reshape(mlstm_b_in.shape[0], 1, -1)
    conv_b3 = ffn_conv_b.reshape(depth, 1, -1)

    for i in range(depth):
        jm = i // 2
        if i % 2 == 0:
            qkv = _qkv_proj(u, g_mix3, mods_sb, w_qkv, cos_u, sin_u, layer=i, w_layer=jm,
                            tm=tm, sub=sub, n_q=n_q, n_kv=n_kv)
            y = _attention(qkv, attn_sink[jm], batch=batch, seq=seq, ctx=lc, n_q=n_q, n_kv=n_kv)
            w_o = w_attn_o
        else:
            wgt = mlstm_w_in[jm][:, main_cols:].reshape(d, 4, n_heads)
            bgt = mlstm_b_in[jm][main_cols:].reshape(4, n_heads)
            lane_pad = GATE_F_LANE - 2 * n_heads
            wg = jnp.concatenate([wgt[:, 0], wgt[:, 2], jnp.zeros((d, lane_pad), F32),
                                  wgt[:, 1], wgt[:, 3], jnp.zeros((d, lane_pad), F32)], axis=1).astype(BF16)
            bg = jnp.concatenate([bgt[0], bgt[2], jnp.zeros((lane_pad,), F32),
                                  bgt[1], bgt[3], jnp.zeros((lane_pad,), F32)]).reshape(1, LANE)
            z, gates = _mlstm_in_proj(u, g_mix3, mods_sb, w_in, b_in3, wg, bg, layer=i, w_layer=jm,
                                      tm=tm, sub=sub, dqk=dqk)
            og = _mlstm_gates(gates, batch=batch, p_rows=p_rows, ctx=lc, n_heads=n_heads)
            gates_r, gates_c = _gate_layouts(og, batch=batch, p_rows=p_rows, n_heads=n_heads)
            y = _mlstm_core(z, gates_r, gates_c, mlstm_g_head[jm].reshape(1, n_heads * dv),
                            batch=batch, p_rows=p_rows, ctx=lc, n_heads=n_heads, dqk=dqk, dv=dv)
            w_o = w_mlstm_o
        u, h2 = _proj_residual(y, w_o, u, mods_sb, g_ffn3, layer=i, w_layer=jm, tm=tm, sub=sub)
        u = _conv_ffn(h2, u, mods_sb, w_up, ffn_conv_w, conv_b3, w_down, layer=i,
                      tm=tm, sub=sub, p_rows=p_rows, ctx=lc)
    return _final_norm(u, g_final.reshape(1, d), batch=batch, seq=seq, ctx=lc, sub=sub)
```

```python
import functools

import numpy as np
import jax
import jax.numpy as jnp
from jax import lax
from jax.experimental import pallas as pl
from jax.experimental.pallas import tpu as pltpu

F32 = jnp.float32
BF16 = jnp.bfloat16

RMS_EPS = 1e-6
HEAD_DIM = 128
WINDOW = 128
GRID_W = 64
ROPE_THETA = 10000.0
NEG_INF = -1e30
CHUNK = 128
CONV_W = 3
LANE = 128
HALO = 8
BF16_ROWS = 16
FFN_CHUNK = 512
GATE_ROWS = 64
NORM_ROWS = 16
PROJ_TILE = 3072
MOD_SHIFT1, MOD_SCALE1, MOD_GATE1, MOD_SHIFT2, MOD_SCALE2, MOD_GATE2 = range(6)
V7X_VMEM_BYTES = 64 * 1024 * 1024
VMEM_LIMIT = V7X_VMEM_BYTES * 7 // 8


def _params(sem):
    return pltpu.CompilerParams(dimension_semantics=sem, vmem_limit_bytes=VMEM_LIMIT)


def _sigmoid(v):
    return 1.0 / (1.0 + jnp.exp(-v))


def _log_sigmoid(v):
    return jnp.minimum(v, 0.0) - jnp.log(1.0 + jnp.exp(-jnp.abs(v)))


def _mod_spec(nsb, d, layer, kind):
    return pl.BlockSpec((None, nsb, None, 1, d), lambda i, *_: (layer, i, kind, 0, 0))


def _layer_spec(shape, layer, index):
    return pl.BlockSpec((None,) + shape, lambda *g: (layer,) + index(*g))


def _norm_mod(xv, g, scale, shift):
    ms = jnp.mean(xv * xv, axis=-1, keepdims=True)
    y = xv * lax.rsqrt(ms + RMS_EPS)
    return (y * g) * (1.0 + scale) + shift


def _norm_mod_to(h_scr, x_ref, g_ref, sc_ref, sh_ref, sub):
    for s in range(x_ref.shape[0] // sub):
        gain = g_ref[...] * (1.0 + sc_ref[s])
        shift = sh_ref[s]
        for r in range(s * sub, (s + 1) * sub, NORM_ROWS):
            xv = x_ref[r:r + NORM_ROWS, :]
            ms = jnp.mean(xv * xv, axis=-1, keepdims=True)
            h_scr[r:r + NORM_ROWS, :] = (xv * lax.rsqrt(ms + RMS_EPS) * gain + shift).astype(BF16)


def _mod_kernel(c_ref, w_ref, b_ref, o_ref):
    cv = c_ref[...]
    s = (cv * _sigmoid(cv)).astype(BF16)
    o_ref[...] = jnp.dot(s, w_ref[...].astype(BF16), preferred_element_type=F32) + b_ref[...]


def _modulation(cvec, w_mod, b_mod):
    depth, d, n = w_mod.shape
    r = cvec.shape[0]
    tn = min(1024, n)
    return pl.pallas_call(
        _mod_kernel,
        grid=(depth, n // tn),
        in_specs=[pl.BlockSpec((r, d), lambda i, j: (0, 0)),
                  pl.BlockSpec((None, d, tn), lambda i, j: (i, 0, j)),
                  pl.BlockSpec((None, 1, tn), lambda i, j: (i, 0, j))],
        out_specs=pl.BlockSpec((None, r, tn), lambda i, j: (i, 0, j)),
        out_shape=jax.ShapeDtypeStruct((depth, r, n), F32),
        compiler_params=_params(("parallel", "parallel")),
        name="modulation",
    )(cvec, w_mod, b_mod.reshape(depth, 1, n))


def _qkv_kernel(x_ref, g_ref, sc_ref, sh_ref, w_ref, cos_ref, sin_ref, o_ref, h_scr,
                *, sub, n_q, n_rope, q_scale):
    j = pl.program_id(1)
    tm, tn = o_ref.shape

    @pl.when(j == 0)
    def _():
        _norm_mod_to(h_scr, x_ref, g_ref, sc_ref, sh_ref, sub)

    acc = jnp.dot(h_scr[...], w_ref[...], preferred_element_type=F32)
    cos = cos_ref[...]
    sin = sin_ref[...]
    heads_per_tile = tn // HEAD_DIM
    for l in range(heads_per_tile):
        t = acc[:, l * HEAD_DIM:(l + 1) * HEAD_DIM]
        head = j * heads_per_tile + l
        tr = t * cos + pltpu.roll(t, HEAD_DIM // 2, 1) * sin
        t = jnp.where(head < n_rope, tr, t)
        t = t * jnp.where(head < n_q, q_scale, 1.0)
        o_ref[:, l * HEAD_DIM:(l + 1) * HEAD_DIM] = t.astype(BF16)


def _qkv_proj(u, g, mods, w, cos_u, sin_u, *, layer, w_layer, tm, sub, n_q, n_kv):
    n_rows, d = u.shape
    n = w.shape[2]
    tn = n if n <= PROJ_TILE else PROJ_TILE
    assert n % tn == 0
    nsb = tm // sub
    kern = functools.partial(_qkv_kernel, sub=sub, n_q=n_q, n_rope=n_q + n_kv,
                             q_scale=HEAD_DIM ** -0.5)
    return pl.pallas_call(
        kern,
        grid=(n_rows // tm, n // tn),
        in_specs=[pl.BlockSpec((tm, d), lambda i, j: (i, 0)),
                  _layer_spec((1, d), layer, lambda i, j: (0, 0)),
                  _mod_spec(nsb, d, layer, MOD_SCALE1),
                  _mod_spec(nsb, d, layer, MOD_SHIFT1),
                  _layer_spec((d, tn), w_layer, lambda i, j: (0, j)),
                  pl.BlockSpec((tm, HEAD_DIM), lambda i, j: (i, 0)),
                  pl.BlockSpec((tm, HEAD_DIM), lambda i, j: (i, 0))],
        out_specs=pl.BlockSpec((tm, tn), lambda i, j: (i, j)),
        out_shape=jax.ShapeDtypeStruct((n_rows, n), BF16),
        scratch_shapes=[pltpu.VMEM((tm, d), BF16)],
        compiler_params=_params(("parallel", "arbitrary")),
        name="attn_qkv",
    )(u, g, mods, mods, w, cos_u, sin_u)


def _softmax_pv(s, sink, v):
    m = jnp.maximum(jnp.max(s, axis=-1, keepdims=True), sink)
    p = jnp.exp(s - m)
    l = jnp.sum(p, axis=-1, keepdims=True) + jnp.exp(sink - m)
    o = jnp.dot(p.astype(BF16), v, preferred_element_type=F32)
    return o * (1.0 / l)


def _stack_heads(q, group):
    return jnp.concatenate([q[:, g * HEAD_DIM:(g + 1) * HEAD_DIM] for g in range(group)], axis=0)


def _attn_x_kernel(q_ref, kp_ref, ko_ref, kn_ref, kc_ref, vp_ref, vo_ref, vn_ref, vc_ref,
                   mask_ref, sink_ref, o_ref, *, group, n_kv):
    blk = q_ref.shape[0]
    valid = mask_ref[...] > 0.0

    def scores(h):
        kv = slice(h * HEAD_DIM, (h + 1) * HEAD_DIM)
        qs = _stack_heads(q_ref[:, h * group * HEAD_DIM:(h + 1) * group * HEAD_DIM], group)
        k = jnp.concatenate([kp_ref[:, kv], ko_ref[:, kv], kn_ref[:, kv], kc_ref[:, kv]], axis=0)
        s = lax.dot_general(qs, k, (((1,), (1,)), ((), ())), preferred_element_type=F32)
        return jnp.concatenate([jnp.where(valid, s[:, :3 * blk], NEG_INF), s[:, 3 * blk:]], axis=1)

    def finish(h, s):
        kv = slice(h * HEAD_DIM, (h + 1) * HEAD_DIM)
        v = jnp.concatenate([vp_ref[:, kv], vo_ref[:, kv], vn_ref[:, kv], vc_ref[:, kv]], axis=0)
        o = _softmax_pv(s, sink_ref[h], v)
        for g in range(group):
            c0 = (h * group + g) * HEAD_DIM
            o_ref[:, c0:c0 + HEAD_DIM] = o[g * blk:(g + 1) * blk, :].astype(BF16)

    s_next = scores(0)
    for h in range(n_kv):
        s_cur = s_next
        if h + 1 < n_kv:
            s_next = scores(h + 1)
        finish(h, s_cur)


def _attn_c_kernel(q_ref, kc_ref, vc_ref, sink_ref, o_ref, *, group):
    lc = q_ref.shape[0]
    qs = _stack_heads(q_ref[...], group)
    s = lax.dot_general(qs, kc_ref[...], (((1,), (1,)), ((), ())), preferred_element_type=F32)
    o = _softmax_pv(s, sink_ref[...], vc_ref[...])
    for g in range(group):
        o_ref[:, g * HEAD_DIM:(g + 1) * HEAD_DIM] = o[g * lc:(g + 1) * lc, :].astype(BF16)


def _window_mask(group, blk, n_blocks):
    qi = np.arange(group * blk)[:, None] % blk
    kj = np.arange(3 * blk)[None, :]
    band = (kj - qi >= 0) & (kj - qi <= 2 * WINDOW)
    first = band & (kj >= blk)
    last = band & (kj < 2 * blk)
    kinds = [first & last if n_blocks == 1 else first, band, last]
    return np.stack(kinds).astype(np.float32)


def _attention(qkv, sink, *, batch, seq, ctx, n_q, n_kv):
    n_rows = qkv.shape[0]
    group = n_q // n_kv
    blk = WINDOW
    p_rows = ctx + seq
    bpb = p_rows // blk
    cb = ctx // blk
    nb = seq // blk
    last_blk = n_rows // blk - 1
    qw = n_q * HEAD_DIM
    kw = n_kv * HEAD_DIM
    assert qw % kw == 0
    k_col = qw // kw
    spc = p_rows // ctx
    gw = group * HEAD_DIM
    sink_g = sink.reshape(n_kv, group, 1).astype(F32)
    sink_x = jnp.broadcast_to(sink_g[:, :, None, :], (n_kv, group, blk, 1)).reshape(n_kv, group * blk, 1)
    sink_c = jnp.broadcast_to(sink_g[:, :, None, :], (n_kv, group, ctx, 1)).reshape(n_kv, group * ctx, 1)
    mask = jnp.asarray(_window_mask(group, blk, nb))

    def kv_specs(col):
        return [pl.BlockSpec((blk, kw), lambda b, n: (jnp.maximum(b * bpb + cb + n - 1, 0), col)),
                pl.BlockSpec((blk, kw), lambda b, n: (b * bpb + cb + n, col)),
                pl.BlockSpec((blk, kw), lambda b, n: (jnp.minimum(b * bpb + cb + n + 1, last_blk), col)),
                pl.BlockSpec((ctx, kw), lambda b, n: (b * spc, col))]

    out_x = pl.pallas_call(
        functools.partial(_attn_x_kernel, group=group, n_kv=n_kv),
        grid=(batch, nb),
        in_specs=[pl.BlockSpec((blk, qw), lambda b, n: (b * bpb + cb + n, 0))]
        + kv_specs(k_col) + kv_specs(k_col + 1)
        + [pl.BlockSpec((None, group * blk, 3 * blk),
                        lambda b, n: (jnp.where(n == 0, 0, jnp.where(n == nb - 1, 2, 1)), 0, 0)),
           pl.BlockSpec((n_kv, group * blk, 1), lambda b, n: (0, 0, 0))],
        out_specs=pl.BlockSpec((blk, qw), lambda b, n: (b * bpb + cb + n, 0)),
        out_shape=jax.ShapeDtypeStruct((n_rows, qw), BF16),
        compiler_params=_params(("parallel", "parallel")),
        name="attn_latent",
    )(qkv, qkv, qkv, qkv, qkv, qkv, qkv, qkv, qkv, mask, sink_x)

    out = pl.pallas_call(
        functools.partial(_attn_c_kernel_alias, group=group),
        grid=(batch, n_kv),
        in_specs=[pl.BlockSpec((ctx, gw), lambda b, h: (b * spc, h)),
                  pl.BlockSpec((ctx, HEAD_DIM), lambda b, h: (b * spc, n_q + h)),
                  pl.BlockSpec((ctx, HEAD_DIM), lambda b, h: (b * spc, n_q + n_kv + h)),
                  pl.BlockSpec((None, group * ctx, 1), lambda b, h: (h, 0, 0)),
                  pl.BlockSpec(memory_space=pl.ANY)],
        out_specs=pl.BlockSpec((ctx, gw), lambda b, h: (b * spc, h)),
        out_shape=jax.ShapeDtypeStruct((n_rows, qw), BF16),
        input_output_aliases={4: 0},
        compiler_params=_params(("parallel", "parallel")),
        name="attn_context",
    )(qkv, qkv, qkv, sink_c, out_x)
    return out


def _attn_c_kernel_alias(q_ref, kc_ref, vc_ref, sink_ref, prev_ref, o_ref, *, group):
    del prev_ref
    _attn_c_kernel(q_ref, kc_ref, vc_ref, sink_ref, o_ref, group=group)


def _proj_res_kernel(y_ref, w_ref, x_ref, gt_ref, g_ref, sc_ref, sh_ref, o_ref, h_ref, *, sub):
    acc = jnp.dot(y_ref[...], w_ref[...], preferred_element_type=F32)
    g = g_ref[...]
    for s in range(o_ref.shape[0] // sub):
        rows = slice(s * sub, (s + 1) * sub)
        xn = x_ref[rows, :] + gt_ref[s] * acc[rows, :]
        o_ref[rows, :] = xn
        h_ref[rows, :] = _norm_mod(xn, g, sc_ref[s], sh_ref[s]).astype(BF16)


def _proj_residual(y, w, u, mods, g, *, layer, w_layer, tm, sub, latent_only=None):
    n_rows, d = u.shape
    k = y.shape[1]
    out_rows, out_map = n_rows, lambda i: (i, 0)
    if latent_only is not None:
        spb, cs = latent_only
        tm = sub
        n_lat = n_rows // sub // spb * (spb - cs)
        out_rows = (n_lat + 1) * sub
        out_map = lambda i: (jnp.where(i % spb >= cs, (i // spb) * (spb - cs) + i % spb - cs, n_lat), 0)
    nsb = tm // sub
    return pl.pallas_call(
        functools.partial(_proj_res_kernel, sub=sub),
        grid=(n_rows // tm,),
        in_specs=[pl.BlockSpec((tm, k), lambda i: (i, 0)),
                  _layer_spec((k, d), w_layer, lambda i: (0, 0)),
                  pl.BlockSpec((tm, d), lambda i: (i, 0)),
                  _mod_spec(nsb, d, layer, MOD_GATE1),
                  _layer_spec((1, d), layer, lambda i: (0, 0)),
                  _mod_spec(nsb, d, layer, MOD_SCALE2),
                  _mod_spec(nsb, d, layer, MOD_SHIFT2)],
        out_specs=[pl.BlockSpec((tm, d), out_map),
                   pl.BlockSpec((tm, d), out_map)],
        out_shape=[jax.ShapeDtypeStruct((out_rows, d), F32),
                   jax.ShapeDtypeStruct((out_rows, d), BF16)],
        compiler_params=_params(("arbitrary",)),
        name="proj_residual",
    )(y, w, u, mods, g, mods, mods)


def _mlstm_in_kernel(x_ref, g_ref, sc_ref, sh_ref, w_ref, b_ref, wg_ref, bg_ref, z_ref, gate_ref,
                     h_scr, *, sub, group, k_scale):
    j = pl.program_id(1)
    tm, tn = z_ref.shape

    @pl.when(j == 0)
    def _():
        _norm_mod_to(h_scr, x_ref, g_ref, sc_ref, sh_ref, sub)
        gate_ref[...] = jnp.dot(h_scr[...], wg_ref[...], preferred_element_type=F32) + bg_ref[...]

    for l in range(tn // group):
        cols = slice(l * group, (l + 1) * group)
        acc = jnp.dot(h_scr[...], w_ref[:, cols], preferred_element_type=F32) + b_ref[:, cols]
        is_k = j * (tn // group) + l == 1
        z_ref[:, cols] = (acc * jnp.where(is_k, k_scale, 1.0)).astype(BF16)


def _mlstm_in_proj(u, g, mods, w, b, wg, bg, *, layer, w_layer, tm, sub, dqk):
    n_rows, d = u.shape
    n = 3 * d
    qk_cols = d // 2
    tn = n // 2 if n // 2 <= PROJ_TILE and (n // 2) % qk_cols == 0 else qk_cols
    nsb = tm // sub
    kern = functools.partial(_mlstm_in_kernel, sub=sub, group=qk_cols, k_scale=dqk ** -0.5)
    return pl.pallas_call(
        kern,
        grid=(n_rows // tm, n // tn),
        in_specs=[pl.BlockSpec((tm, d), lambda i, j: (i, 0)),
                  _layer_spec((1, d), layer, lambda i, j: (0, 0)),
                  _mod_spec(nsb, d, layer, MOD_SCALE1),
                  _mod_spec(nsb, d, layer, MOD_SHIFT1),
                  _layer_spec((d, tn), w_layer, lambda i, j: (0, j)),
                  _layer_spec((1, tn), w_layer, lambda i, j: (0, j)),
                  pl.BlockSpec((d, 128), lambda i, j: (0, 0)),
                  pl.BlockSpec((1, 128), lambda i, j: (0, 0))],
        out_specs=[pl.BlockSpec((tm, tn), lambda i, j: (i, j)),
                   pl.BlockSpec((tm, 128), lambda i, j: (i, 0))],
        out_shape=[jax.ShapeDtypeStruct((n_rows, n), BF16),
                   jax.ShapeDtypeStruct((n_rows, 128), F32)],
        scratch_shapes=[pltpu.VMEM((tm, d), BF16)],
        compiler_params=_params(("parallel", "arbitrary")),
        name="mlstm_in",
    )(u, g, mods, mods, w, b, wg, bg)


GATE_F_LANE = 64
N_GATE_OUT = 6
N_COL = 5


def _chunk_scan(x, scr, pos, op, ident, reverse):
    r = x.shape[0]
    sh = 1
    while sh < CHUNK:
        scr[CHUNK:CHUNK + r, :] = x
        if reverse:
            other = scr[CHUNK + sh:CHUNK + sh + r, :]
            ok = pos < CHUNK - sh
        else:
            other = scr[CHUNK - sh:CHUNK - sh + r, :]
            ok = pos >= sh
        x = op(x, jnp.where(ok, other, ident))
        sh *= 2
    return x


def _mlstm_gate_kernel(g_ref, o_ref, b_scr, cm_scr, shift_scr, *, n_heads, n_ctx_chunks, tile):
    p_rows = g_ref.shape[0]
    n_chunks = p_rows // CHUNK
    shift_scr[...] = jnp.zeros_like(shift_scr)
    o_ref[0:N_COL] = jnp.zeros((N_COL,) + o_ref.shape[1:], F32)
    fwd_tile = lax.broadcasted_iota(jnp.int32, (tile, LANE), 1) < n_heads
    pos = lax.broadcasted_iota(jnp.int32, (tile, LANE), 0) % CHUNK

    def local(t, _):
        rows = pl.ds(pl.multiple_of(t * tile, tile), tile)
        g = g_ref[rows, :]
        lf = _log_sigmoid(pltpu.roll(g, LANE - GATE_F_LANE, 1))
        b = jnp.where(fwd_tile, _chunk_scan(lf, shift_scr, pos, jnp.add, 0.0, False),
                      _chunk_scan(lf, shift_scr, pos, jnp.add, 0.0, True))
        a = g - b
        cm = jnp.where(fwd_tile, _chunk_scan(a, shift_scr, pos, jnp.maximum, -jnp.inf, False),
                       _chunk_scan(a, shift_scr, pos, jnp.maximum, -jnp.inf, True))
        b_scr[rows, :] = b
        cm_scr[rows, :] = cm
        o_ref[N_COL, rows, :] = a
        return 0

    lax.fori_loop(0, p_rows // tile, local, 0)

    fwd_chunk = lax.broadcasted_iota(jnp.int32, (CHUNK, LANE), 1) < n_heads

    def step(c, m, forward):
        rows = pl.ds(pl.multiple_of(c * CHUNK, CHUNK), CHUNK)
        end_row = CHUNK - 1 if forward else 0
        mine = fwd_chunk if forward else jnp.logical_not(fwd_chunk)
        b = b_scr[rows, :]
        a = o_ref[N_COL, rows, :]
        cm = cm_scr[rows, :]
        b_end = b[end_row:end_row + 1, :]
        m_t = b + jnp.maximum(m, cm)
        m_new = jnp.maximum(b_end + m, b_end + cm[end_row:end_row + 1, :])
        vals = (b - m_t, jnp.exp(b + m - m_t), jnp.exp(-m_t), jnp.exp(b_end + a - m_new),
                jnp.broadcast_to(jnp.exp(b_end + m - m_new), (CHUNK, LANE)))
        for q, val in enumerate(vals):
            o_ref[q, rows, :] = jnp.where(mine, val, o_ref[q, rows, :])
        return m_new

    def body(j, carry):
        mf, mb = carry
        mf = step(j, mf, True)
        cb = jnp.where(j < n_ctx_chunks, n_ctx_chunks - 1 - j, n_chunks - 1 + n_ctx_chunks - j)
        mb = step(cb, mb, False)
        return mf, mb

    zm = jnp.zeros((1, LANE), F32)
    lax.fori_loop(0, n_chunks, body, (zm, zm))


def _mlstm_gates(gates, *, batch, p_rows, ctx, n_heads):
    n_rows = gates.shape[0]
    tile = ctx
    kern = functools.partial(_mlstm_gate_kernel, n_heads=n_heads, n_ctx_chunks=ctx // CHUNK, tile=tile)
    return pl.pallas_call(
        kern,
        grid=(batch,),
        in_specs=[pl.BlockSpec((p_rows, LANE), lambda b: (b, 0))],
        out_specs=pl.BlockSpec((N_GATE_OUT, p_rows, LANE), lambda b: (0, b, 0)),
        out_shape=jax.ShapeDtypeStruct((N_GATE_OUT, n_rows, LANE), F32),
        scratch_shapes=[pltpu.VMEM((p_rows, LANE), F32), pltpu.VMEM((p_rows, LANE), F32),
                        pltpu.VMEM((tile + 2 * CHUNK, LANE), F32)],
        compiler_params=_params(("parallel",)),
        name="mlstm_gates",
    )(gates)


def _mlstm_chunk(q, k, v, col, a_row, c_scr, n, d, reverse):
    ch = q.shape[0]
    t_idx = lax.broadcasted_iota(jnp.int32, (ch, ch), 0)
    s_idx = lax.broadcasted_iota(jnp.int32, (ch, ch), 1)
    seen = (s_idx >= t_idx) if reverse else (s_idx <= t_idx)
    c0 = d * N_COL
    row_term = col[:, c0:c0 + 1]
    e_inter = col[:, c0 + 1:c0 + 2]
    floor = col[:, c0 + 2:c0 + 3]
    wg_col = col[:, c0 + 3:c0 + 4]
    e_prev = col[0:1, c0 + 4:c0 + 5]

    w = jnp.where(seen, jnp.exp(row_term + a_row), 0.0)
    s = lax.dot_general(q, k, (((1,), (1,)), ((), ())), preferred_element_type=F32) * w
    c_old = c_scr[...]
    num = (e_inter * jnp.dot(q, c_old.astype(BF16), preferred_element_type=F32)
           + jnp.dot(s.astype(BF16), v, preferred_element_type=F32))
    qn = jnp.sum(q.astype(F32) * n, axis=1, keepdims=True)
    den = e_inter * qn + jnp.sum(s, axis=1, keepdims=True)
    h = num * (1.0 / jnp.maximum(jnp.abs(den), floor))

    vw = (wg_col * v.astype(F32)).astype(BF16)
    c_scr[...] = e_prev * c_old + lax.dot_general(k, vw, (((0,), (0,)), ((), ())),
                                                  preferred_element_type=F32)
    n_new = e_prev * n + jnp.sum(wg_col * k.astype(F32), axis=0, keepdims=True)
    return h, n_new


def _mlstm_core_kernel(q_ref, k_ref, v_ref, o_ref, gr_ref, gc_ref, gh_ref, y_ref,
                       hf_scr, hb_scr, cf_scr, cb_scr, *, n_ctx_chunks, out_rows):
    n_chunks = gr_ref.shape[0]
    dqk = q_ref.shape[1]
    cf_scr[...] = jnp.zeros_like(cf_scr)
    cb_scr[...] = jnp.zeros_like(cb_scr)

    def run(c, c_scr, h_scr, n, d, reverse):
        rows = pl.ds(pl.multiple_of(c * CHUNK, CHUNK), CHUNK)
        h, n = _mlstm_chunk(q_ref[rows, :], k_ref[rows, :], v_ref[rows, :], gc_ref[rows, :],
                            gr_ref[c][d:d + 1, :], c_scr, n, d, reverse)
        h_scr[rows, :] = h
        return n

    def body(j, carry):
        nf, nb = carry
        nf = run(j, cf_scr, hf_scr, nf, 0, False)
        cb = jnp.where(j < n_ctx_chunks, n_ctx_chunks - 1 - j, n_chunks - 1 + n_ctx_chunks - j)
        nb = run(cb, cb_scr, hb_scr, nb, 1, True)
        return nf, nb

    zn = jnp.zeros((1, dqk), F32)
    lax.fori_loop(0, n_chunks, body, (zn, zn), unroll=2)

    gh = gh_ref[...]

    def finish(r, _):
        rows = pl.ds(pl.multiple_of(r * out_rows, out_rows), out_rows)
        h = hf_scr[rows, :] + hb_scr[rows, :]
        hn = h * lax.rsqrt(jnp.mean(h * h, axis=-1, keepdims=True) + RMS_EPS)
        y_ref[rows, :] = (_sigmoid(o_ref[rows, :].astype(F32)) * hn * gh).astype(BF16)
        return 0

    lax.fori_loop(0, y_ref.shape[0] // out_rows, finish, 0)


def _mlstm_core(z, gates_r, gates_c, g_head, *, batch, p_rows, ctx, n_heads, dqk, dv):
    n_rows = z.shape[0]
    n_chunks = p_rows // CHUNK
    kern = functools.partial(_mlstm_core_kernel, n_ctx_chunks=ctx // CHUNK, out_rows=ctx)
    return pl.pallas_call(
        kern,
        grid=(batch, n_heads),
        in_specs=[pl.BlockSpec((p_rows, dqk), lambda b, h: (b, h)),
                  pl.BlockSpec((p_rows, dqk), lambda b, h: (b, n_heads + h)),
                  pl.BlockSpec((p_rows, dv), lambda b, h: (b, n_heads + h)),
                  pl.BlockSpec((p_rows, dv), lambda b, h: (b, 2 * n_heads + h)),
                  pl.BlockSpec((None, None, n_chunks, 2, CHUNK), lambda b, h: (b, h, 0, 0, 0)),
                  pl.BlockSpec((None, None, p_rows, 2 * N_COL), lambda b, h: (b, h, 0, 0)),
                  pl.BlockSpec((1, dv), lambda b, h: (0, h))],
        out_specs=pl.BlockSpec((p_rows, dv), lambda b, h: (b, h)),
        out_shape=jax.ShapeDtypeStruct((n_rows, n_heads * dv), BF16),
        scratch_shapes=[pltpu.VMEM((p_rows, dv), F32), pltpu.VMEM((p_rows, dv), F32),
                        pltpu.VMEM((dqk, dv), F32), pltpu.VMEM((dqk, dv), F32)],
        compiler_params=_params(("parallel", "parallel")),
        name="mlstm_core",
    )(z, z, z, z, gates_r, gates_c, g_head)


def _gate_layouts(og, *, batch, p_rows, n_heads):
    g = og[:, :, :2 * n_heads].reshape(N_GATE_OUT, batch, p_rows, 2, n_heads)
    cols = g[:N_COL].transpose(1, 4, 2, 3, 0).reshape(batch, n_heads, p_rows, 2 * N_COL)
    rows = g[N_COL].reshape(batch, p_rows // CHUNK, CHUNK, 2, n_heads).transpose(0, 4, 1, 3, 2)
    return rows, cols


def _halo_up_kernel(h_ref, w_ref, o_ref):
    o_ref[...] = jnp.dot(h_ref[...], w_ref[...], preferred_element_type=F32)


def _halo_up(h, w_up, *, layer, n_rows, tm):
    d = h.shape[1]
    n = w_up.shape[2]
    nt = n_rows // tm
    ht = h[:n_rows].reshape(nt, tm, d)
    first, last = ht[:, 0], ht[:, tm - 1]
    prev = jnp.concatenate([last[:1], last[:nt - 1]], axis=0)
    nxt = jnp.concatenate([first[1:], first[nt - 1:]], axis=0)
    hh = jnp.stack([prev, nxt], axis=1).reshape(2 * nt, d)
    rows = 2 * nt
    tn = 2 * FFN_CHUNK
    uh = pl.pallas_call(
        _halo_up_kernel,
        grid=(n // tn,),
        in_specs=[pl.BlockSpec((rows, d), lambda j: (0, 0)),
                  _layer_spec((d, tn), layer, lambda j: (0, j))],
        out_specs=pl.BlockSpec((rows, tn), lambda j: (0, j)),
        out_shape=jax.ShapeDtypeStruct((rows, n), F32),
        compiler_params=_params(("parallel",)),
        name="ffn_halo",
    )(hh, w_up)
    return uh.reshape(nt, 2, n)


def _ffn_kernel(h_ref, x_ref, gt_ref, wa0_ref, wb0_ref, wa1_ref, wb1_ref,
                ha0_ref, hb0_ref, ha1_ref, hb1_ref, cwa_ref, cwb_ref, cba_ref, cbb_ref, wd_ref,
                gf_ref, o_ref, acc_scr, u0_scr, u1_scr, a0_scr, a1_scr,
                *, sub, p_rows, ctx, final_norm):
    i = pl.program_id(0)
    j = pl.program_id(1)
    tm = o_ref.shape[0]
    nsb = tm // sub
    fh = wa0_ref.shape[1]
    lt = fh // LANE
    seg = sub + 2 * HALO

    @pl.when(j == 0)
    def _():
        acc_scr[...] = jnp.zeros_like(acc_scr)

    def edge_open(k):
        pos = (i * tm + k * sub) % p_rows
        return jnp.where((pos == 0) | (pos == ctx), 0.0, 1.0)

    opens = [edge_open(k) for k in range(nsb + 1)]

    def up(w_refs, halo_refs, u_scr):
        for br, (w_ref, halo_ref) in enumerate(zip(w_refs, halo_refs)):
            ue = jnp.dot(h_ref[...], w_ref[...], preferred_element_type=F32)
            halo = halo_ref[...]
            for s in range(nsb):
                r0 = s * sub
                b0 = s * seg
                for t in range(lt):
                    lanes = slice(t * LANE, (t + 1) * LANE)
                    tt = br * lt + t
                    before = (jnp.broadcast_to(halo[0:1, lanes], (HALO, LANE)) if s == 0
                              else ue[r0 - HALO:r0, lanes])
                    after = (jnp.broadcast_to(halo[1:2, lanes], (HALO, LANE)) if s == nsb - 1
                             else ue[r0 + sub:r0 + sub + HALO, lanes])
                    u_scr[tt, b0:b0 + HALO, :] = before * opens[s]
                    u_scr[tt, b0 + HALO:b0 + HALO + sub, :] = ue[r0:r0 + sub, lanes]
                    u_scr[tt, b0 + HALO + sub:b0 + seg, :] = after * opens[s + 1]

    def gate(c, u_scr, act_scr):
        cols = slice(c * fh, (c + 1) * fh)
        cws = (cwa_ref[:, cols], cwb_ref[:, cols])
        cbs = (cba_ref[:, cols], cbb_ref[:, cols])

        def conv_piece(br, t, r0):
            lanes = slice(t * LANE, (t + 1) * LANE)
            tt = br * lt + t + jnp.minimum(j, 0)
            cw, cb = cws[br], cbs[br]
            cur = u_scr[tt, r0:r0 + GATE_ROWS, :]
            prv = u_scr[tt, r0 - 1:r0 - 1 + GATE_ROWS, :]
            nxt = u_scr[tt, r0 + 1:r0 + 1 + GATE_ROWS, :]
            return cb[:, lanes] + prv * cw[0:1, lanes] + cur * cw[1:2, lanes] + nxt * cw[2:3, lanes]

        for s in range(nsb):
            for r in range(sub // GATE_ROWS):
                r0 = s * seg + HALO + r * GATE_ROWS
                o0 = s * sub + r * GATE_ROWS
                for t in range(lt):
                    a = conv_piece(0, t, r0)
                    b = conv_piece(1, t, r0)
                    act_scr[o0:o0 + GATE_ROWS, t * LANE:(t + 1) * LANE] = (a * _sigmoid(a) * b).astype(BF16)

    def down(c, act_scr):
        acc_scr[...] += jnp.dot(act_scr[...], wd_ref[c * fh:(c + 1) * fh, :], preferred_element_type=F32)

    up((wa0_ref, wb0_ref), (ha0_ref, hb0_ref), u0_scr)
    up((wa1_ref, wb1_ref), (ha1_ref, hb1_ref), u1_scr)
    gate(0, u0_scr, a0_scr)
    down(0, a0_scr)
    gate(1, u1_scr, a1_scr)
    down(1, a1_scr)

    @pl.when(j == pl.num_programs(1) - 1)
    def _():
        for s in range(nsb):
            if not final_norm:
                rows = slice(s * sub, (s + 1) * sub)
                o_ref[rows, :] = x_ref[rows, :] + gt_ref[s] * acc_scr[rows, :]
                continue
            for r in range(s * sub, (s + 1) * sub, NORM_ROWS):
                rows = slice(r, r + NORM_ROWS)
                xn = x_ref[rows, :] + gt_ref[s] * acc_scr[rows, :]
                ms = jnp.mean(xn * xn, axis=-1, keepdims=True)
                o_ref[rows, :] = xn * lax.rsqrt(ms + RMS_EPS) * gf_ref[...]


def _conv_ffn(h, u, mods, w_up, conv_w, conv_b, w_down, g_final, *, layer, n_rows, tm, sub, p_rows, ctx,
              final_norm):
    d = u.shape[1]
    d_ff = w_down.shape[1]
    fc = FFN_CHUNK
    assert d_ff % fc == 0
    fh = fc // 2
    nfc = d_ff // fc
    nsb = tm // sub
    kern = functools.partial(_ffn_kernel, sub=sub, p_rows=p_rows, ctx=ctx, final_norm=final_norm)
    u_shape = (fc // LANE, nsb * (sub + 2 * HALO), LANE)
    halo = _halo_up(h, w_up, layer=layer, n_rows=n_rows, tm=tm)

    def half_specs(shape, lead):
        return [pl.BlockSpec(shape, lambda i, j, c=c: lead(i) + (c + 2 * j,))
                for c in (0, 2 * nfc, 1, 2 * nfc + 1)]

    return pl.pallas_call(
        kern,
        grid=(n_rows // tm, nfc),
        in_specs=[pl.BlockSpec((tm, d), lambda i, j: (i, 0)),
                  pl.BlockSpec((tm, d), lambda i, j: (i, 0)),
                  _mod_spec(nsb, d, layer, MOD_GATE2)]
        + half_specs((None, d, fh), lambda i: (layer, 0))
        + half_specs((None, 2, fh), lambda i: (i, 0))
        + [_layer_spec((CONV_W, fc), layer, lambda i, j: (0, j)),
           _layer_spec((CONV_W, fc), layer, lambda i, j: (0, nfc + j)),
           _layer_spec((1, fc), layer, lambda i, j: (0, j)),
           _layer_spec((1, fc), layer, lambda i, j: (0, nfc + j)),
           _layer_spec((fc, d), layer, lambda i, j: (j, 0)),
           pl.BlockSpec((1, d), lambda i, j: (0, 0))],
        out_specs=pl.BlockSpec((tm, d), lambda i, j: (i, 0)),
        out_shape=jax.ShapeDtypeStruct((n_rows, d), F32),
        scratch_shapes=[pltpu.VMEM((tm, d), F32), pltpu.VMEM(u_shape, F32), pltpu.VMEM(u_shape, F32),
                        pltpu.VMEM((tm, fh), BF16), pltpu.VMEM((tm, fh), BF16)],
        compiler_params=_params(("parallel", "arbitrary")),
        name="conv_ffn",
    )(h, u, mods, w_up, w_up, w_up, w_up, halo, halo, halo, halo,
      conv_w, conv_w, conv_b, conv_b, w_down, g_final)


def _rope_tables(batch, seq, ctx):
    rows = seq // GRID_W
    row = jnp.repeat(jnp.arange(rows, dtype=F32), GRID_W)
    col = jnp.tile(jnp.arange(GRID_W, dtype=F32), rows)
    n_freq = HEAD_DIM // 4
    inv_freq = ROPE_THETA ** (-jnp.arange(n_freq, dtype=F32) / n_freq)
    ang = jnp.concatenate([row[:, None] * inv_freq, col[:, None] * inv_freq], axis=-1)
    ang = jnp.concatenate([ang, ang], axis=-1)
    sign = jnp.concatenate([-jnp.ones((HEAD_DIM // 2,), F32), jnp.ones((HEAD_DIM // 2,), F32)])
    cos = jnp.concatenate([jnp.ones((ctx, HEAD_DIM), F32), jnp.cos(ang)], axis=0)
    sin = jnp.concatenate([jnp.zeros((ctx, HEAD_DIM), F32), jnp.sin(ang) * sign], axis=0)
    return jnp.tile(cos, (batch, 1)), jnp.tile(sin, (batch, 1))


def kernel(x, c, ctx, c_ctx, w_mod, b_mod, g_mix, g_ffn, attn_w_qkv, attn_sink, attn_w_o,
           mlstm_w_in, mlstm_b_in, mlstm_g_head, mlstm_w_o, ffn_w_up, ffn_conv_w, ffn_conv_b,
           ffn_w_down, g_final):
    batch, seq, d = x.shape
    lc = ctx.shape[1]
    depth = w_mod.shape[0]
    p_rows = lc + seq
    sub = lc
    assert seq % sub == 0 and sub % WINDOW == 0 and sub % CHUNK == 0 and seq % GRID_W == 0
    n_sub = batch * p_rows // sub
    tm = 2 * sub if n_sub % 2 == 0 else sub
    n_q = attn_sink.shape[1]
    n_kv = (attn_w_qkv.shape[2] // HEAD_DIM - n_q) // 2
    n_heads = (mlstm_w_in.shape[2] - 3 * d) // 4
    dv = d // n_heads
    dqk = dv // 2
    main_cols = 3 * d

    u = jnp.concatenate([ctx, x], axis=1).reshape(batch * p_rows, d)

    n_c = batch + 1
    c_rows = -(-n_c // 8) * 8
    cvec = jnp.concatenate([c, c_ctx[None, :], jnp.zeros((c_rows - n_c, d), F32)], axis=0)
    mods = _modulation(cvec, w_mod, b_mod)
    mods_sb = jnp.concatenate(
        [jnp.broadcast_to(mods[:, None, batch:batch + 1], (depth, batch, lc // sub, 6 * d)),
         jnp.broadcast_to(mods[:, :batch, None], (depth, batch, seq // sub, 6 * d))],
        axis=2).reshape(depth, n_sub, 6, 1, d)

    mods_lat = jnp.broadcast_to(mods[:, :batch, None], (depth, batch, seq // sub, 6 * d)
                                ).reshape(depth, batch * seq // sub, 6, 1, d)

    cos_u, sin_u = _rope_tables(batch, seq, lc)

    w_qkv = attn_w_qkv.astype(BF16)
    w_attn_o = attn_w_o.astype(BF16)
    w_in = mlstm_w_in.astype(BF16)
    w_mlstm_o = mlstm_w_o.astype(BF16)
    w_up = ffn_w_up.astype(BF16)
    w_down = ffn_w_down.astype(BF16)
    g_mix3 = g_mix.reshape(depth, 1, d)
    g_ffn3 = g_ffn.reshape(depth, 1, d)
    b_in3 = mlstm_b_in.reshape(mlstm_b_in.shape[0], 1, -1)
    conv_b3 = ffn_conv_b.reshape(depth, 1, -1)
    g_fin = g_final.reshape(1, d)

    for i in range(depth):
        jm = i // 2
        if i % 2 == 0:
            qkv = _qkv_proj(u, g_mix3, mods_sb, w_qkv, cos_u, sin_u, layer=i, w_layer=jm,
                            tm=tm, sub=sub, n_q=n_q, n_kv=n_kv)
            y = _attention(qkv, attn_sink[jm], batch=batch, seq=seq, ctx=lc, n_q=n_q, n_kv=n_kv)
            w_o = w_attn_o
        else:
            wgt = mlstm_w_in[jm][:, main_cols:].reshape(d, 4, n_heads)
            bgt = mlstm_b_in[jm][main_cols:].reshape(4, n_heads)
            lane_pad = GATE_F_LANE - 2 * n_heads
            wg = jnp.concatenate([wgt[:, 0], wgt[:, 2], jnp.zeros((d, lane_pad), F32),
                                  wgt[:, 1], wgt[:, 3], jnp.zeros((d, lane_pad), F32)], axis=1).astype(BF16)
            bg = jnp.concatenate([bgt[0], bgt[2], jnp.zeros((lane_pad,), F32),
                                  bgt[1], bgt[3], jnp.zeros((lane_pad,), F32)]).reshape(1, LANE)
            z, gates = _mlstm_in_proj(u, g_mix3, mods_sb, w_in, b_in3, wg, bg, layer=i, w_layer=jm,
                                      tm=tm, sub=sub, dqk=dqk)
            og = _mlstm_gates(gates, batch=batch, p_rows=p_rows, ctx=lc, n_heads=n_heads)
            gates_r, gates_c = _gate_layouts(og, batch=batch, p_rows=p_rows, n_heads=n_heads)
            y = _mlstm_core(z, gates_r, gates_c, mlstm_g_head[jm].reshape(1, n_heads * dv),
                            batch=batch, p_rows=p_rows, ctx=lc, n_heads=n_heads, dqk=dqk, dv=dv)
            w_o = w_mlstm_o
        if i < depth - 1:
            u, h2 = _proj_residual(y, w_o, u, mods_sb, g_ffn3, layer=i, w_layer=jm, tm=tm, sub=sub)
            u = _conv_ffn(h2, u, mods_sb, w_up, ffn_conv_w, conv_b3, w_down, g_fin, layer=i,
                          n_rows=batch * p_rows, tm=tm, sub=sub, p_rows=p_rows, ctx=lc,
                          final_norm=False)
        else:
            u, h2 = _proj_residual(y, w_o, u, mods_sb, g_ffn3, layer=i, w_layer=jm, tm=tm, sub=sub,
                                   latent_only=(p_rows // sub, lc // sub))
            u = _conv_ffn(h2, u, mods_lat, w_up, ffn_conv_w, conv_b3, w_down, g_fin, layer=i,
                          n_rows=batch * seq, tm=tm, sub=sub, p_rows=seq, ctx=0,
                          final_norm=True)
    return u.reshape(batch, seq, d)
```

```python
import functools

import numpy as np
import jax
import jax.numpy as jnp
from jax import lax
from jax.experimental import pallas as pl
from jax.experimental.pallas import tpu as pltpu

F32 = jnp.float32
BF16 = jnp.bfloat16

RMS_EPS = 1e-6
HEAD_DIM = 128
WINDOW = 128
GRID_W = 64
ROPE_THETA = 10000.0
NEG_INF = -1e30
CHUNK = 128
CONV_W = 3
LANE = 128
HALO = 8
BF16_ROWS = 16
FFN_CHUNK = 512
GATE_ROWS = 64
NORM_ROWS = 16
PROJ_TILE = 3072
MOD_TILE = 1024
MOD_SHIFT1, MOD_SCALE1, MOD_GATE1, MOD_SHIFT2, MOD_SCALE2, MOD_GATE2 = range(6)
V7X_VMEM_BYTES = 64 * 1024 * 1024
VMEM_LIMIT = V7X_VMEM_BYTES * 7 // 8


def _params(sem):
    return pltpu.CompilerParams(dimension_semantics=sem, vmem_limit_bytes=VMEM_LIMIT)


def _sigmoid(v):
    return 1.0 / (1.0 + jnp.exp(-v))


def _log_sigmoid(v):
    return jnp.minimum(v, 0.0) - jnp.log(1.0 + jnp.exp(-jnp.abs(v)))


def _mod_spec(nsb, d, layer, kind):
    return pl.BlockSpec((None, nsb, None, 1, d), lambda i, *_: (layer, i, kind, 0, 0))


def _layer_spec(shape, layer, index):
    return pl.BlockSpec((None,) + shape, lambda *g: (layer,) + index(*g))


def _norm_mod(xv, g, scale, shift):
    ms = jnp.mean(xv * xv, axis=-1, keepdims=True)
    y = xv * lax.rsqrt(ms + RMS_EPS)
    return (y * g) * (1.0 + scale) + shift


def _norm_mod_to(h_scr, x_ref, g_ref, sc_ref, sh_ref, sub):
    for s in range(x_ref.shape[0] // sub):
        gain = g_ref[...] * (1.0 + sc_ref[s])
        shift = sh_ref[s]
        for r in range(s * sub, (s + 1) * sub, NORM_ROWS):
            xv = x_ref[r:r + NORM_ROWS, :]
            ms = jnp.mean(xv * xv, axis=-1, keepdims=True)
            h_scr[r:r + NORM_ROWS, :] = (xv * lax.rsqrt(ms + RMS_EPS) * gain + shift).astype(BF16)


def _mod_kernel(c_ref, w_ref, b_ref, o_ref):
    cv = c_ref[...]
    s = (cv * _sigmoid(cv)).astype(BF16)
    o_ref[...] = jnp.dot(s, w_ref[...].astype(BF16), preferred_element_type=F32) + b_ref[...]


def _modulation(cvec, w_mod, b_mod):
    depth, d, n = w_mod.shape
    r = cvec.shape[0]
    tn = MOD_TILE if n % MOD_TILE == 0 else n
    return pl.pallas_call(
        _mod_kernel,
        grid=(depth, n // tn),
        in_specs=[pl.BlockSpec((r, d), lambda i, j: (0, 0)),
                  pl.BlockSpec((None, d, tn), lambda i, j: (i, 0, j)),
                  pl.BlockSpec((None, 1, tn), lambda i, j: (i, 0, j))],
        out_specs=pl.BlockSpec((None, r, tn), lambda i, j: (i, 0, j)),
        out_shape=jax.ShapeDtypeStruct((depth, r, n), F32),
        compiler_params=_params(("parallel", "parallel")),
        name="modulation",
    )(cvec, w_mod, b_mod.reshape(depth, 1, n))


def _qkv_kernel(x_ref, g_ref, sc_ref, sh_ref, w_ref, cos_ref, sin_ref, o_ref, h_scr,
                *, sub, n_q, n_rope, q_scale):
    j = pl.program_id(1)
    tm, tn = o_ref.shape

    @pl.when(j == 0)
    def _():
        _norm_mod_to(h_scr, x_ref, g_ref, sc_ref, sh_ref, sub)

    acc = jnp.dot(h_scr[...], w_ref[...], preferred_element_type=F32)
    cos = cos_ref[...]
    sin = sin_ref[...]
    heads_per_tile = tn // HEAD_DIM
    for l in range(heads_per_tile):
        t = acc[:, l * HEAD_DIM:(l + 1) * HEAD_DIM]
        head = j * heads_per_tile + l
        tr = t * cos + pltpu.roll(t, HEAD_DIM // 2, 1) * sin
        t = jnp.where(head < n_rope, tr, t)
        t = t * jnp.where(head < n_q, q_scale, 1.0)
        o_ref[:, l * HEAD_DIM:(l + 1) * HEAD_DIM] = t.astype(BF16)


def _qkv_proj(u, g, mods, w, cos_u, sin_u, *, layer, w_layer, tm, sub, n_q, n_kv):
    n_rows, d = u.shape
    n = w.shape[2]
    tn = n if n <= PROJ_TILE else PROJ_TILE
    assert n % tn == 0
    nsb = tm // sub
    kern = functools.partial(_qkv_kernel, sub=sub, n_q=n_q, n_rope=n_q + n_kv,
                             q_scale=HEAD_DIM ** -0.5)
    return pl.pallas_call(
        kern,
        grid=(n_rows // tm, n // tn),
        in_specs=[pl.BlockSpec((tm, d), lambda i, j: (i, 0)),
                  _layer_spec((1, d), layer, lambda i, j: (0, 0)),
                  _mod_spec(nsb, d, layer, MOD_SCALE1),
                  _mod_spec(nsb, d, layer, MOD_SHIFT1),
                  _layer_spec((d, tn), w_layer, lambda i, j: (0, j)),
                  pl.BlockSpec((tm, HEAD_DIM), lambda i, j: (i, 0)),
                  pl.BlockSpec((tm, HEAD_DIM), lambda i, j: (i, 0))],
        out_specs=pl.BlockSpec((tm, tn), lambda i, j: (i, j)),
        out_shape=jax.ShapeDtypeStruct((n_rows, n), BF16),
        scratch_shapes=[pltpu.VMEM((tm, d), BF16)],
        compiler_params=_params(("parallel", "arbitrary")),
        name="attn_qkv",
    )(u, g, mods, mods, w, cos_u, sin_u)


def _softmax_pv(s, sink, v):
    m = jnp.maximum(jnp.max(s, axis=-1, keepdims=True), sink)
    p = jnp.exp(s - m)
    l = jnp.sum(p, axis=-1, keepdims=True) + jnp.exp(sink - m)
    o = jnp.dot(p.astype(BF16), v, preferred_element_type=F32)
    return o * (1.0 / l)


def _stack_heads(q, group):
    return jnp.concatenate([q[:, g * HEAD_DIM:(g + 1) * HEAD_DIM] for g in range(group)], axis=0)


def _attn_x_kernel(q_ref, kp_ref, ko_ref, kn_ref, kc_ref, vp_ref, vo_ref, vn_ref, vc_ref,
                   mask_ref, sink_ref, o_ref, *, group, n_kv):
    blk = q_ref.shape[0]
    valid = mask_ref[...] > 0.0

    def scores(h):
        kv = slice(h * HEAD_DIM, (h + 1) * HEAD_DIM)
        qs = _stack_heads(q_ref[:, h * group * HEAD_DIM:(h + 1) * group * HEAD_DIM], group)
        k = jnp.concatenate([kp_ref[:, kv], ko_ref[:, kv], kn_ref[:, kv], kc_ref[:, kv]], axis=0)
        s = lax.dot_general(qs, k, (((1,), (1,)), ((), ())), preferred_element_type=F32)
        return jnp.concatenate([jnp.where(valid, s[:, :3 * blk], NEG_INF), s[:, 3 * blk:]], axis=1)

    def finish(h, s):
        kv = slice(h * HEAD_DIM, (h + 1) * HEAD_DIM)
        v = jnp.concatenate([vp_ref[:, kv], vo_ref[:, kv], vn_ref[:, kv], vc_ref[:, kv]], axis=0)
        o = _softmax_pv(s, sink_ref[h], v)
        for g in range(group):
            c0 = (h * group + g) * HEAD_DIM
            o_ref[:, c0:c0 + HEAD_DIM] = o[g * blk:(g + 1) * blk, :].astype(BF16)

    s_next = scores(0)
    for h in range(n_kv):
        s_cur = s_next
        if h + 1 < n_kv:
            s_next = scores(h + 1)
        finish(h, s_cur)


def _attn_c_kernel(q_ref, kc_ref, vc_ref, sink_ref, prev_ref, o_ref, *, group, n_kv):
    del prev_ref
    lc = q_ref.shape[0]

    def scores(h):
        qs = _stack_heads(q_ref[:, h * group * HEAD_DIM:(h + 1) * group * HEAD_DIM], group)
        return lax.dot_general(qs, kc_ref[:, h * HEAD_DIM:(h + 1) * HEAD_DIM], (((1,), (1,)), ((), ())),
                               preferred_element_type=F32)

    s_next = scores(0)
    for h in range(n_kv):
        s_cur = s_next
        if h + 1 < n_kv:
            s_next = scores(h + 1)
        o = _softmax_pv(s_cur, sink_ref[h], vc_ref[:, h * HEAD_DIM:(h + 1) * HEAD_DIM])
        for g in range(group):
            c0 = (h * group + g) * HEAD_DIM
            o_ref[:, c0:c0 + HEAD_DIM] = o[g * lc:(g + 1) * lc, :].astype(BF16)


def _window_mask(group, blk, n_blocks):
    qi = np.arange(group * blk)[:, None] % blk
    kj = np.arange(3 * blk)[None, :]
    band = (kj - qi >= 0) & (kj - qi <= 2 * WINDOW)
    first = band & (kj >= blk)
    last = band & (kj < 2 * blk)
    kinds = [first & last if n_blocks == 1 else first, band, last]
    return np.stack(kinds).astype(np.float32)


def _attention(qkv, sink, *, batch, seq, ctx, n_q, n_kv):
    n_rows = qkv.shape[0]
    group = n_q // n_kv
    blk = WINDOW
    p_rows = ctx + seq
    bpb = p_rows // blk
    cb = ctx // blk
    nb = seq // blk
    last_blk = n_rows // blk - 1
    qw = n_q * HEAD_DIM
    kw = n_kv * HEAD_DIM
    assert qw % kw == 0
    k_col = qw // kw
    spc = p_rows // ctx
    sink_g = sink.reshape(n_kv, group, 1).astype(F32)
    sink_x = jnp.broadcast_to(sink_g[:, :, None, :], (n_kv, group, blk, 1)).reshape(n_kv, group * blk, 1)
    sink_c = jnp.broadcast_to(sink_g[:, :, None, :], (n_kv, group, ctx, 1)).reshape(n_kv, group * ctx, 1)
    mask = jnp.asarray(_window_mask(group, blk, nb))

    def kv_specs(col):
        return [pl.BlockSpec((blk, kw), lambda b, n: (jnp.maximum(b * bpb + cb + n - 1, 0), col)),
                pl.BlockSpec((blk, kw), lambda b, n: (b * bpb + cb + n, col)),
                pl.BlockSpec((blk, kw), lambda b, n: (jnp.minimum(b * bpb + cb + n + 1, last_blk), col)),
                pl.BlockSpec((ctx, kw), lambda b, n: (b * spc, col))]

    out_x = pl.pallas_call(
        functools.partial(_attn_x_kernel, group=group, n_kv=n_kv),
        grid=(batch, nb),
        in_specs=[pl.BlockSpec((blk, qw), lambda b, n: (b * bpb + cb + n, 0))]
        + kv_specs(k_col) + kv_specs(k_col + 1)
        + [pl.BlockSpec((None, group * blk, 3 * blk),
                        lambda b, n: (jnp.where(n == 0, 0, jnp.where(n == nb - 1, 2, 1)), 0, 0)),
           pl.BlockSpec((n_kv, group * blk, 1), lambda b, n: (0, 0, 0))],
        out_specs=pl.BlockSpec((blk, qw), lambda b, n: (b * bpb + cb + n, 0)),
        out_shape=jax.ShapeDtypeStruct((n_rows, qw), BF16),
        compiler_params=_params(("parallel", "parallel")),
        name="attn_latent",
    )(qkv, qkv, qkv, qkv, qkv, qkv, qkv, qkv, qkv, mask, sink_x)

    out = pl.pallas_call(
        functools.partial(_attn_c_kernel, group=group, n_kv=n_kv),
        grid=(batch,),
        in_specs=[pl.BlockSpec((ctx, qw), lambda b: (b * spc, 0)),
                  pl.BlockSpec((ctx, kw), lambda b: (b * spc, k_col)),
                  pl.BlockSpec((ctx, kw), lambda b: (b * spc, k_col + 1)),
                  pl.BlockSpec((n_kv, group * ctx, 1), lambda b: (0, 0, 0)),
                  pl.BlockSpec(memory_space=pl.ANY)],
        out_specs=pl.BlockSpec((ctx, qw), lambda b: (b * spc, 0)),
        out_shape=jax.ShapeDtypeStruct((n_rows, qw), BF16),
        input_output_aliases={4: 0},
        compiler_params=_params(("parallel",)),
        name="attn_context",
    )(qkv, qkv, qkv, sink_c, out_x)
    return out


def _proj_res_kernel(y_ref, w_ref, x_ref, gt_ref, g_ref, sc_ref, sh_ref, o_ref, h_ref, *, sub):
    acc = jnp.dot(y_ref[...], w_ref[...], preferred_element_type=F32)
    g = g_ref[...]
    for s in range(o_ref.shape[0] // sub):
        rows = slice(s * sub, (s + 1) * sub)
        xn = x_ref[rows, :] + gt_ref[s] * acc[rows, :]
        o_ref[rows, :] = xn
        h_ref[rows, :] = _norm_mod(xn, g, sc_ref[s], sh_ref[s]).astype(BF16)


def _proj_residual(y, w, u, mods, g, *, layer, w_layer, tm, sub, latent_only=None):
    n_rows, d = u.shape
    k = y.shape[1]
    out_rows, out_map = n_rows, lambda i: (i, 0)
    if latent_only is not None:
        spb, cs = latent_only
        tm = sub
        n_lat = n_rows // sub // spb * (spb - cs)
        out_rows = (n_lat + 1) * sub
        out_map = lambda i: (jnp.where(i % spb >= cs, (i // spb) * (spb - cs) + i % spb - cs, n_lat), 0)
    nsb = tm // sub
    return pl.pallas_call(
        functools.partial(_proj_res_kernel, sub=sub),
        grid=(n_rows // tm,),
        in_specs=[pl.BlockSpec((tm, k), lambda i: (i, 0)),
                  _layer_spec((k, d), w_layer, lambda i: (0, 0)),
                  pl.BlockSpec((tm, d), lambda i: (i, 0)),
                  _mod_spec(nsb, d, layer, MOD_GATE1),
                  _layer_spec((1, d), layer, lambda i: (0, 0)),
                  _mod_spec(nsb, d, layer, MOD_SCALE2),
                  _mod_spec(nsb, d, layer, MOD_SHIFT2)],
        out_specs=[pl.BlockSpec((tm, d), out_map),
                   pl.BlockSpec((tm, d), out_map)],
        out_shape=[jax.ShapeDtypeStruct((out_rows, d), F32),
                   jax.ShapeDtypeStruct((out_rows, d), BF16)],
        compiler_params=_params(("arbitrary",)),
        name="proj_residual",
    )(y, w, u, mods, g, mods, mods)


def _mlstm_in_kernel(x_ref, g_ref, sc_ref, sh_ref, w_ref, b_ref, wg_ref, bg_ref, z_ref, gate_ref,
                     h_scr, *, sub, group, k_scale):
    j = pl.program_id(1)
    tm, tn = z_ref.shape

    @pl.when(j == 0)
    def _():
        _norm_mod_to(h_scr, x_ref, g_ref, sc_ref, sh_ref, sub)
        gate_ref[...] = jnp.dot(h_scr[...], wg_ref[...], preferred_element_type=F32) + bg_ref[...]

    for l in range(tn // group):
        cols = slice(l * group, (l + 1) * group)
        acc = jnp.dot(h_scr[...], w_ref[:, cols], preferred_element_type=F32) + b_ref[:, cols]
        is_k = j * (tn // group) + l == 1
        z_ref[:, cols] = (acc * jnp.where(is_k, k_scale, 1.0)).astype(BF16)


def _mlstm_in_proj(u, g, mods, w, b, wg, bg, *, layer, w_layer, tm, sub, dqk):
    n_rows, d = u.shape
    n = 3 * d
    qk_cols = d // 2
    tn = n // 2 if n // 2 <= PROJ_TILE and (n // 2) % qk_cols == 0 else qk_cols
    nsb = tm // sub
    kern = functools.partial(_mlstm_in_kernel, sub=sub, group=qk_cols, k_scale=dqk ** -0.5)
    return pl.pallas_call(
        kern,
        grid=(n_rows // tm, n // tn),
        in_specs=[pl.BlockSpec((tm, d), lambda i, j: (i, 0)),
                  _layer_spec((1, d), layer, lambda i, j: (0, 0)),
                  _mod_spec(nsb, d, layer, MOD_SCALE1),
                  _mod_spec(nsb, d, layer, MOD_SHIFT1),
                  _layer_spec((d, tn), w_layer, lambda i, j: (0, j)),
                  _layer_spec((1, tn), w_layer, lambda i, j: (0, j)),
                  pl.BlockSpec((d, LANE), lambda i, j: (0, 0)),
                  pl.BlockSpec((1, LANE), lambda i, j: (0, 0))],
        out_specs=[pl.BlockSpec((tm, tn), lambda i, j: (i, j)),
                   pl.BlockSpec((tm, LANE), lambda i, j: (i, 0))],
        out_shape=[jax.ShapeDtypeStruct((n_rows, n), BF16),
                   jax.ShapeDtypeStruct((n_rows, LANE), F32)],
        scratch_shapes=[pltpu.VMEM((tm, d), BF16)],
        compiler_params=_params(("parallel", "arbitrary")),
        name="mlstm_in",
    )(u, g, mods, mods, w, b, wg, bg)


GATE_F_LANE = 64
N_GATE_OUT = 6
N_COL = 5


def _chunk_scan(x, scr, pos, op, ident, reverse):
    r = x.shape[0]
    sh = 1
    while sh < CHUNK:
        scr[CHUNK:CHUNK + r, :] = x
        if reverse:
            other = scr[CHUNK + sh:CHUNK + sh + r, :]
            ok = pos < CHUNK - sh
        else:
            other = scr[CHUNK - sh:CHUNK - sh + r, :]
            ok = pos >= sh
        x = op(x, jnp.where(ok, other, ident))
        sh *= 2
    return x


def _mlstm_gate_kernel(g_ref, o_ref, b_scr, cm_scr, shift_scr, *, n_heads, n_ctx_chunks, tile):
    p_rows = g_ref.shape[0]
    n_chunks = p_rows // CHUNK
    shift_scr[...] = jnp.zeros_like(shift_scr)
    o_ref[0:N_COL] = jnp.zeros((N_COL,) + o_ref.shape[1:], F32)
    fwd_tile = lax.broadcasted_iota(jnp.int32, (tile, LANE), 1) < n_heads
    pos = lax.broadcasted_iota(jnp.int32, (tile, LANE), 0) % CHUNK

    def local(t, _):
        rows = pl.ds(pl.multiple_of(t * tile, tile), tile)
        g = g_ref[rows, :]
        lf = _log_sigmoid(pltpu.roll(g, LANE - GATE_F_LANE, 1))
        b = jnp.where(fwd_tile, _chunk_scan(lf, shift_scr, pos, jnp.add, 0.0, False),
                      _chunk_scan(lf, shift_scr, pos, jnp.add, 0.0, True))
        a = g - b
        cm = jnp.where(fwd_tile, _chunk_scan(a, shift_scr, pos, jnp.maximum, -jnp.inf, False),
                       _chunk_scan(a, shift_scr, pos, jnp.maximum, -jnp.inf, True))
        b_scr[rows, :] = b
        cm_scr[rows, :] = cm
        o_ref[N_COL, rows, :] = a
        return 0

    lax.fori_loop(0, p_rows // tile, local, 0)

    fwd_chunk = lax.broadcasted_iota(jnp.int32, (CHUNK, LANE), 1) < n_heads

    def step(c, m, forward):
        rows = pl.ds(pl.multiple_of(c * CHUNK, CHUNK), CHUNK)
        end_row = CHUNK - 1 if forward else 0
        mine = fwd_chunk if forward else jnp.logical_not(fwd_chunk)
        b = b_scr[rows, :]
        a = o_ref[N_COL, rows, :]
        cm = cm_scr[rows, :]
        b_end = b[end_row:end_row + 1, :]
        m_t = b + jnp.maximum(m, cm)
        m_new = jnp.maximum(b_end + m, b_end + cm[end_row:end_row + 1, :])
        vals = (b - m_t, jnp.exp(b + m - m_t), jnp.exp(-m_t), jnp.exp(b_end + a - m_new),
                jnp.broadcast_to(jnp.exp(b_end + m - m_new), (CHUNK, LANE)))
        for q, val in enumerate(vals):
            o_ref[q, rows, :] = jnp.where(mine, val, o_ref[q, rows, :])
        return m_new

    def body(j, carry):
        mf, mb = carry
        mf = step(j, mf, True)
        cb = jnp.where(j < n_ctx_chunks, n_ctx_chunks - 1 - j, n_chunks - 1 + n_ctx_chunks - j)
        mb = step(cb, mb, False)
        return mf, mb

    zm = jnp.zeros((1, LANE), F32)
    lax.fori_loop(0, n_chunks, body, (zm, zm))


def _mlstm_gates(gates, *, batch, p_rows, ctx, n_heads):
    n_rows = gates.shape[0]
    tile = ctx
    kern = functools.partial(_mlstm_gate_kernel, n_heads=n_heads, n_ctx_chunks=ctx // CHUNK, tile=tile)
    return pl.pallas_call(
        kern,
        grid=(batch,),
        in_specs=[pl.BlockSpec((p_rows, LANE), lambda b: (b, 0))],
        out_specs=pl.BlockSpec((N_GATE_OUT, p_rows, LANE), lambda b: (0, b, 0)),
        out_shape=jax.ShapeDtypeStruct((N_GATE_OUT, n_rows, LANE), F32),
        scratch_shapes=[pltpu.VMEM((p_rows, LANE), F32), pltpu.VMEM((p_rows, LANE), F32),
                        pltpu.VMEM((tile + 2 * CHUNK, LANE), F32)],
        compiler_params=_params(("parallel",)),
        name="mlstm_gates",
    )(gates)


def _mlstm_chunk(q, k, v, col, a_row, c_scr, n, d, reverse):
    ch = q.shape[0]
    t_idx = lax.broadcasted_iota(jnp.int32, (ch, ch), 0)
    s_idx = lax.broadcasted_iota(jnp.int32, (ch, ch), 1)
    seen = (s_idx >= t_idx) if reverse else (s_idx <= t_idx)
    c0 = d * N_COL
    row_term = col[:, c0:c0 + 1]
    e_inter = col[:, c0 + 1:c0 + 2]
    floor = col[:, c0 + 2:c0 + 3]
    wg_col = col[:, c0 + 3:c0 + 4]
    e_prev = col[0:1, c0 + 4:c0 + 5]

    w = jnp.where(seen, jnp.exp(row_term + a_row), 0.0)
    s = lax.dot_general(q, k, (((1,), (1,)), ((), ())), preferred_element_type=F32) * w
    c_old = c_scr[...]
    num = (e_inter * jnp.dot(q, c_old.astype(BF16), preferred_element_type=F32)
           + jnp.dot(s.astype(BF16), v, preferred_element_type=F32))
    qn = jnp.sum(q.astype(F32) * n, axis=1, keepdims=True)
    den = e_inter * qn + jnp.sum(s, axis=1, keepdims=True)
    h = num * (1.0 / jnp.maximum(jnp.abs(den), floor))

    vw = (wg_col * v.astype(F32)).astype(BF16)
    c_scr[...] = e_prev * c_old + lax.dot_general(k, vw, (((0,), (0,)), ((), ())),
                                                  preferred_element_type=F32)
    n_new = e_prev * n + jnp.sum(wg_col * k.astype(F32), axis=0, keepdims=True)
    return h, n_new


def _mlstm_core_kernel(q_ref, k_ref, v_ref, o_ref, gr_ref, gc_ref, gh_ref, y_ref,
                       hf_scr, hb_scr, cf_scr, cb_scr, *, n_ctx_chunks, out_rows):
    n_chunks = gr_ref.shape[0]
    dqk = q_ref.shape[1]
    cf_scr[...] = jnp.zeros_like(cf_scr)
    cb_scr[...] = jnp.zeros_like(cb_scr)

    def run(c, c_scr, h_scr, n, d, reverse):
        rows = pl.ds(pl.multiple_of(c * CHUNK, CHUNK), CHUNK)
        h, n = _mlstm_chunk(q_ref[rows, :], k_ref[rows, :], v_ref[rows, :], gc_ref[rows, :],
                            gr_ref[c][d:d + 1, :], c_scr, n, d, reverse)
        h_scr[rows, :] = h
        return n

    def body(j, carry):
        nf, nb = carry
        nf = run(j, cf_scr, hf_scr, nf, 0, False)
        cb = jnp.where(j < n_ctx_chunks, n_ctx_chunks - 1 - j, n_chunks - 1 + n_ctx_chunks - j)
        nb = run(cb, cb_scr, hb_scr, nb, 1, True)
        return nf, nb

    zn = jnp.zeros((1, dqk), F32)
    lax.fori_loop(0, n_chunks, body, (zn, zn), unroll=2)

    gh = gh_ref[...]

    def finish(r, _):
        rows = pl.ds(pl.multiple_of(r * out_rows, out_rows), out_rows)
        h = hf_scr[rows, :] + hb_scr[rows, :]
        hn = h * lax.rsqrt(jnp.mean(h * h, axis=-1, keepdims=True) + RMS_EPS)
        y_ref[rows, :] = (_sigmoid(o_ref[rows, :].astype(F32)) * hn * gh).astype(BF16)
        return 0

    lax.fori_loop(0, y_ref.shape[0] // out_rows, finish, 0)


def _mlstm_core(z, gates_r, gates_c, g_head, *, batch, p_rows, ctx, n_heads, dqk, dv):
    n_rows = z.shape[0]
    n_chunks = p_rows // CHUNK
    kern = functools.partial(_mlstm_core_kernel, n_ctx_chunks=ctx // CHUNK, out_rows=ctx)
    return pl.pallas_call(
        kern,
        grid=(batch, n_heads),
        in_specs=[pl.BlockSpec((p_rows, dqk), lambda b, h: (b, h)),
                  pl.BlockSpec((p_rows, dqk), lambda b, h: (b, n_heads + h)),
                  pl.BlockSpec((p_rows, dv), lambda b, h: (b, n_heads + h)),
                  pl.BlockSpec((p_rows, dv), lambda b, h: (b, 2 * n_heads + h)),
                  pl.BlockSpec((None, None, n_chunks, 2, CHUNK), lambda b, h: (b, h, 0, 0, 0)),
                  pl.BlockSpec((None, None, p_rows, 2 * N_COL), lambda b, h: (b, h, 0, 0)),
                  pl.BlockSpec((1, dv), lambda b, h: (0, h))],
        out_specs=pl.BlockSpec((p_rows, dv), lambda b, h: (b, h)),
        out_shape=jax.ShapeDtypeStruct((n_rows, n_heads * dv), BF16),
        scratch_shapes=[pltpu.VMEM((p_rows, dv), F32), pltpu.VMEM((p_rows, dv), F32),
                        pltpu.VMEM((dqk, dv), F32), pltpu.VMEM((dqk, dv), F32)],
        compiler_params=_params(("parallel", "parallel")),
        name="mlstm_core",
    )(z, z, z, z, gates_r, gates_c, g_head)


def _gate_layouts(og, *, batch, p_rows, n_heads):
    g = og[:, :, :2 * n_heads].reshape(N_GATE_OUT, batch, p_rows, 2, n_heads)
    cols = g[:N_COL].transpose(1, 4, 2, 3, 0).reshape(batch, n_heads, p_rows, 2 * N_COL)
    rows = g[N_COL].reshape(batch, p_rows // CHUNK, CHUNK, 2, n_heads).transpose(0, 4, 1, 3, 2)
    return rows, cols


def _halo_up_kernel(h_ref, w_ref, o_ref):
    o_ref[...] = jnp.dot(h_ref[...], w_ref[...], preferred_element_type=F32)


def _halo_up(h, w_up, *, layer, n_rows, tm):
    d = h.shape[1]
    n = w_up.shape[2]
    nt = n_rows // tm
    ht = h[:n_rows].reshape(nt, tm, d)
    first, last = ht[:, 0], ht[:, tm - 1]
    prev = jnp.concatenate([last[:1], last[:nt - 1]], axis=0)
    nxt = jnp.concatenate([first[1:], first[nt - 1:]], axis=0)
    hh = jnp.stack([prev, nxt], axis=1).reshape(2 * nt, d)
    rows = 2 * nt
    tn = 2 * FFN_CHUNK
    uh = pl.pallas_call(
        _halo_up_kernel,
        grid=(n // tn,),
        in_specs=[pl.BlockSpec((rows, d), lambda j: (0, 0)),
                  _layer_spec((d, tn), layer, lambda j: (0, j))],
        out_specs=pl.BlockSpec((rows, tn), lambda j: (0, j)),
        out_shape=jax.ShapeDtypeStruct((rows, n), F32),
        compiler_params=_params(("parallel",)),
        name="ffn_halo",
    )(hh, w_up)
    return uh.reshape(nt, 2, n)


def _ffn_kernel(h_ref, x_ref, gt_ref, wa0_ref, wb0_ref, wa1_ref, wb1_ref,
                ha0_ref, hb0_ref, ha1_ref, hb1_ref, cwa_ref, cwb_ref, cba_ref, cbb_ref, wd_ref,
                gf_ref, o_ref, acc_scr, u0_scr, u1_scr, a0_scr, a1_scr,
                *, sub, p_rows, ctx, final_norm):
    i = pl.program_id(0)
    j = pl.program_id(1)
    tm = o_ref.shape[0]
    nsb = tm // sub
    fh = wa0_ref.shape[1]
    lt = fh // LANE
    seg = sub + 2 * HALO

    @pl.when(j == 0)
    def _():
        acc_scr[...] = jnp.zeros_like(acc_scr)

    def edge_open(k):
        pos = (i * tm + k * sub) % p_rows
        return jnp.where((pos == 0) | (pos == ctx), 0.0, 1.0)

    opens = [edge_open(k) for k in range(nsb + 1)]

    def up(w_refs, halo_refs, u_scr):
        for br, (w_ref, halo_ref) in enumerate(zip(w_refs, halo_refs)):
            ue = jnp.dot(h_ref[...], w_ref[...], preferred_element_type=F32)
            halo = halo_ref[...]
            for s in range(nsb):
                r0 = s * sub
                b0 = s * seg
                for t in range(lt):
                    lanes = slice(t * LANE, (t + 1) * LANE)
                    tt = br * lt + t
                    before = (jnp.broadcast_to(halo[0:1, lanes], (HALO, LANE)) if s == 0
                              else ue[r0 - HALO:r0, lanes])
                    after = (jnp.broadcast_to(halo[1:2, lanes], (HALO, LANE)) if s == nsb - 1
                             else ue[r0 + sub:r0 + sub + HALO, lanes])
                    u_scr[tt, b0:b0 + HALO, :] = before * opens[s]
                    u_scr[tt, b0 + HALO:b0 + HALO + sub, :] = ue[r0:r0 + sub, lanes]
                    u_scr[tt, b0 + HALO + sub:b0 + seg, :] = after * opens[s + 1]

    def gate(c, u_scr, act_scr):
        cols = slice(c * fh, (c + 1) * fh)
        cws = (cwa_ref[:, cols], cwb_ref[:, cols])
        cbs = (cba_ref[:, cols], cbb_ref[:, cols])

        def conv_piece(br, t, r0):
            lanes = slice(t * LANE, (t + 1) * LANE)
            tt = br * lt + t + jnp.minimum(j, 0)
            cw, cb = cws[br], cbs[br]
            cur = u_scr[tt, r0:r0 + GATE_ROWS, :]
            prv = u_scr[tt, r0 - 1:r0 - 1 + GATE_ROWS, :]
            nxt = u_scr[tt, r0 + 1:r0 + 1 + GATE_ROWS, :]
            return cb[:, lanes] + prv * cw[0:1, lanes] + cur * cw[1:2, lanes] + nxt * cw[2:3, lanes]

        for s in range(nsb):
            for r in range(sub // GATE_ROWS):
                r0 = s * seg + HALO + r * GATE_ROWS
                o0 = s * sub + r * GATE_ROWS
                for t in range(lt):
                    a = conv_piece(0, t, r0)
                    b = conv_piece(1, t, r0)
                    act_scr[o0:o0 + GATE_ROWS, t * LANE:(t + 1) * LANE] = (a * _sigmoid(a) * b).astype(BF16)

    def down(c, act_scr):
        acc_scr[...] += jnp.dot(act_scr[...], wd_ref[c * fh:(c + 1) * fh, :], preferred_element_type=F32)

    up((wa0_ref, wb0_ref), (ha0_ref, hb0_ref), u0_scr)
    up((wa1_ref, wb1_ref), (ha1_ref, hb1_ref), u1_scr)
    gate(0, u0_scr, a0_scr)
    down(0, a0_scr)
    gate(1, u1_scr, a1_scr)
    down(1, a1_scr)

    @pl.when(j == pl.num_programs(1) - 1)
    def _():
        for s in range(nsb):
            if not final_norm:
                rows = slice(s * sub, (s + 1) * sub)
                o_ref[rows, :] = x_ref[rows, :] + gt_ref[s] * acc_scr[rows, :]
                continue
            for r in range(s * sub, (s + 1) * sub, NORM_ROWS):
                rows = slice(r, r + NORM_ROWS)
                xn = x_ref[rows, :] + gt_ref[s] * acc_scr[rows, :]
                ms = jnp.mean(xn * xn, axis=-1, keepdims=True)
                o_ref[rows, :] = xn * lax.rsqrt(ms + RMS_EPS) * gf_ref[...]


def _conv_ffn(h, u, mods, w_up, conv_w, conv_b, w_down, g_final, *, layer, n_rows, tm, sub, p_rows, ctx,
              final_norm):
    d = u.shape[1]
    d_ff = w_down.shape[1]
    fc = FFN_CHUNK
    assert d_ff % fc == 0
    fh = fc // 2
    nfc = d_ff // fc
    nsb = tm // sub
    kern = functools.partial(_ffn_kernel, sub=sub, p_rows=p_rows, ctx=ctx, final_norm=final_norm)
    u_shape = (fc // LANE, nsb * (sub + 2 * HALO), LANE)
    halo = _halo_up(h, w_up, layer=layer, n_rows=n_rows, tm=tm)

    def half_specs(shape, lead):
        return [pl.BlockSpec(shape, lambda i, j, c=c: lead(i) + (c + 2 * j,))
                for c in (0, 2 * nfc, 1, 2 * nfc + 1)]

    return pl.pallas_call(
        kern,
        grid=(n_rows // tm, nfc),
        in_specs=[pl.BlockSpec((tm, d), lambda i, j: (i, 0)),
                  pl.BlockSpec((tm, d), lambda i, j: (i, 0)),
                  _mod_spec(nsb, d, layer, MOD_GATE2)]
        + half_specs((None, d, fh), lambda i: (layer, 0))
        + half_specs((None, 2, fh), lambda i: (i, 0))
        + [_layer_spec((CONV_W, fc), layer, lambda i, j: (0, j)),
           _layer_spec((CONV_W, fc), layer, lambda i, j: (0, nfc + j)),
           _layer_spec((1, fc), layer, lambda i, j: (0, j)),
           _layer_spec((1, fc), layer, lambda i, j: (0, nfc + j)),
           _layer_spec((fc, d), layer, lambda i, j: (j, 0)),
           pl.BlockSpec((1, d), lambda i, j: (0, 0))],
        out_specs=pl.BlockSpec((tm, d), lambda i, j: (i, 0)),
        out_shape=jax.ShapeDtypeStruct((n_rows, d), F32),
        scratch_shapes=[pltpu.VMEM((tm, d), F32), pltpu.VMEM(u_shape, F32), pltpu.VMEM(u_shape, F32),
                        pltpu.VMEM((tm, fh), BF16), pltpu.VMEM((tm, fh), BF16)],
        compiler_params=_params(("parallel", "arbitrary")),
        name="conv_ffn",
    )(h, u, mods, w_up, w_up, w_up, w_up, halo, halo, halo, halo,
      conv_w, conv_w, conv_b, conv_b, w_down, g_final)


def _rope_tables(batch, seq, ctx):
    rows = seq // GRID_W
    row = jnp.repeat(jnp.arange(rows, dtype=F32), GRID_W)
    col = jnp.tile(jnp.arange(GRID_W, dtype=F32), rows)
    n_freq = HEAD_DIM // 4
    inv_freq = ROPE_THETA ** (-jnp.arange(n_freq, dtype=F32) / n_freq)
    ang = jnp.concatenate([row[:, None] * inv_freq, col[:, None] * inv_freq], axis=-1)
    ang = jnp.concatenate([ang, ang], axis=-1)
    sign = jnp.concatenate([-jnp.ones((HEAD_DIM // 2,), F32), jnp.ones((HEAD_DIM // 2,), F32)])
    cos = jnp.concatenate([jnp.ones((ctx, HEAD_DIM), F32), jnp.cos(ang)], axis=0)
    sin = jnp.concatenate([jnp.zeros((ctx, HEAD_DIM), F32), jnp.sin(ang) * sign], axis=0)
    return jnp.tile(cos, (batch, 1)), jnp.tile(sin, (batch, 1))


def kernel(x, c, ctx, c_ctx, w_mod, b_mod, g_mix, g_ffn, attn_w_qkv, attn_sink, attn_w_o,
           mlstm_w_in, mlstm_b_in, mlstm_g_head, mlstm_w_o, ffn_w_up, ffn_conv_w, ffn_conv_b,
           ffn_w_down, g_final):
    batch, seq, d = x.shape
    lc = ctx.shape[1]
    depth = w_mod.shape[0]
    p_rows = lc + seq
    sub = lc
    assert seq % sub == 0 and sub % WINDOW == 0 and sub % CHUNK == 0 and seq % GRID_W == 0
    n_sub = batch * p_rows // sub
    tm = 2 * sub if n_sub % 2 == 0 else sub
    n_q = attn_sink.shape[1]
    n_kv = (attn_w_qkv.shape[2] // HEAD_DIM - n_q) // 2
    n_heads = (mlstm_w_in.shape[2] - 3 * d) // 4
    dv = d // n_heads
    dqk = dv // 2
    main_cols = 3 * d

    u = jnp.concatenate([ctx, x], axis=1).reshape(batch * p_rows, d)

    n_c = batch + 1
    c_rows = -(-n_c // 8) * 8
    cvec = jnp.concatenate([c, c_ctx[None, :], jnp.zeros((c_rows - n_c, d), F32)], axis=0)
    mods = _modulation(cvec, w_mod, b_mod)
    mods_sb = jnp.concatenate(
        [jnp.broadcast_to(mods[:, None, batch:batch + 1], (depth, batch, lc // sub, 6 * d)),
         jnp.broadcast_to(mods[:, :batch, None], (depth, batch, seq // sub, 6 * d))],
        axis=2).reshape(depth, n_sub, 6, 1, d)

    mods_lat = jnp.broadcast_to(mods[:, :batch, None], (depth, batch, seq // sub, 6 * d)
                                ).reshape(depth, batch * seq // sub, 6, 1, d)

    cos_u, sin_u = _rope_tables(batch, seq, lc)

    w_qkv = attn_w_qkv.astype(BF16)
    w_attn_o = attn_w_o.astype(BF16)
    w_in = mlstm_w_in.astype(BF16)
    w_mlstm_o = mlstm_w_o.astype(BF16)
    w_up = ffn_w_up.astype(BF16)
    w_down = ffn_w_down.astype(BF16)
    g_mix3 = g_mix.reshape(depth, 1, d)
    g_ffn3 = g_ffn.reshape(depth, 1, d)
    b_in3 = mlstm_b_in.reshape(mlstm_b_in.shape[0], 1, -1)
    conv_b3 = ffn_conv_b.reshape(depth, 1, -1)
    g_fin = g_final.reshape(1, d)

    for i in range(depth):
        jm = i // 2
        if i % 2 == 0:
            qkv = _qkv_proj(u, g_mix3, mods_sb, w_qkv, cos_u, sin_u, layer=i, w_layer=jm,
                            tm=tm, sub=sub, n_q=n_q, n_kv=n_kv)
            y = _attention(qkv, attn_sink[jm], batch=batch, seq=seq, ctx=lc, n_q=n_q, n_kv=n_kv)
            w_o = w_attn_o
        else:
            wgt = mlstm_w_in[jm][:, main_cols:].reshape(d, 4, n_heads)
            bgt = mlstm_b_in[jm][main_cols:].reshape(4, n_heads)
            lane_pad = GATE_F_LANE - 2 * n_heads
            wg = jnp.concatenate([wgt[:, 0], wgt[:, 2], jnp.zeros((d, lane_pad), F32),
                                  wgt[:, 1], wgt[:, 3], jnp.zeros((d, lane_pad), F32)], axis=1).astype(BF16)
            bg = jnp.concatenate([bgt[0], bgt[2], jnp.zeros((lane_pad,), F32),
                                  bgt[1], bgt[3], jnp.zeros((lane_pad,), F32)]).reshape(1, LANE)
            z, gates = _mlstm_in_proj(u, g_mix3, mods_sb, w_in, b_in3, wg, bg, layer=i, w_layer=jm,
                                      tm=tm, sub=sub, dqk=dqk)
            og = _mlstm_gates(gates, batch=batch, p_rows=p_rows, ctx=lc, n_heads=n_heads)
            gates_r, gates_c = _gate_layouts(og, batch=batch, p_rows=p_rows, n_heads=n_heads)
            y = _mlstm_core(z, gates_r, gates_c, mlstm_g_head[jm].reshape(1, n_heads * dv),
                            batch=batch, p_rows=p_rows, ctx=lc, n_heads=n_heads, dqk=dqk, dv=dv)
            w_o = w_mlstm_o
        if i < depth - 1:
            u, h2 = _proj_residual(y, w_o, u, mods_sb, g_ffn3, layer=i, w_layer=jm, tm=tm, sub=sub)
            u = _conv_ffn(h2, u, mods_sb, w_up, ffn_conv_w, conv_b3, w_down, g_fin, layer=i,
                          n_rows=batch * p_rows, tm=tm, sub=sub, p_rows=p_rows, ctx=lc,
                          final_norm=False)
        else:
            u, h2 = _proj_residual(y, w_o, u, mods_sb, g_ffn3, layer=i, w_layer=jm, tm=tm, sub=sub,
                                   latent_only=(p_rows // sub, lc // sub))
            u = _conv_ffn(h2, u, mods_lat, w_up, ffn_conv_w, conv_b3, w_down, g_fin, layer=i,
                          n_rows=batch * seq, tm=tm, sub=sub, p_rows=seq, ctx=0,
                          final_norm=True)
    return u.reshape(batch, seq, d)
```

```python
import functools

import numpy as np
import jax
import jax.numpy as jnp
from jax import lax
from jax.experimental import pallas as pl
from jax.experimental.pallas import tpu as pltpu

F32 = jnp.float32
BF16 = jnp.bfloat16

RMS_EPS = 1e-6
HEAD_DIM = 128
WINDOW = 128
GRID_W = 64
ROPE_THETA = 10000.0
NEG_INF = -1e30
CHUNK = 128
CONV_W = 3
LANE = 128
HALO = 8
BF16_ROWS = 16
FFN_CHUNK = 512
GATE_ROWS = 64
NORM_ROWS = 16
PROJ_TILE = 3072
MOD_TILE = 1024
MOD_SHIFT1, MOD_SCALE1, MOD_GATE1, MOD_SHIFT2, MOD_SCALE2, MOD_GATE2 = range(6)
V7X_VMEM_BYTES = 64 * 1024 * 1024
VMEM_LIMIT = V7X_VMEM_BYTES * 7 // 8


def _params(sem):
    return pltpu.CompilerParams(dimension_semantics=sem, vmem_limit_bytes=VMEM_LIMIT)


def _sigmoid(v):
    return 1.0 / (1.0 + jnp.exp(-v))


def _log_sigmoid(v):
    return jnp.minimum(v, 0.0) - jnp.log(1.0 + jnp.exp(-jnp.abs(v)))


def _mod_spec(nsb, d, layer, kind):
    return pl.BlockSpec((None, nsb, None, 1, d), lambda i, *_: (layer, i, kind, 0, 0))


def _layer_spec(shape, layer, index):
    return pl.BlockSpec((None,) + shape, lambda *g: (layer,) + index(*g))


def _norm_mod(xv, g, scale, shift):
    ms = jnp.mean(xv * xv, axis=-1, keepdims=True)
    y = xv * lax.rsqrt(ms + RMS_EPS)
    return (y * g) * (1.0 + scale) + shift


def _norm_mod_to(h_scr, x_ref, g_ref, sc_ref, sh_ref, sub):
    for s in range(x_ref.shape[0] // sub):
        gain = g_ref[...] * (1.0 + sc_ref[s])
        shift = sh_ref[s]
        for r in range(s * sub, (s + 1) * sub, NORM_ROWS):
            xv = x_ref[r:r + NORM_ROWS, :]
            ms = jnp.mean(xv * xv, axis=-1, keepdims=True)
            h_scr[r:r + NORM_ROWS, :] = (xv * lax.rsqrt(ms + RMS_EPS) * gain + shift).astype(BF16)


def _mod_kernel(c_ref, w_ref, b_ref, o_ref):
    cv = c_ref[...]
    s = (cv * _sigmoid(cv)).astype(BF16)
    o_ref[...] = jnp.dot(s, w_ref[...].astype(BF16), preferred_element_type=F32) + b_ref[...]


def _modulation(cvec, w_mod, b_mod):
    depth, d, n = w_mod.shape
    r = cvec.shape[0]
    tn = MOD_TILE if n % MOD_TILE == 0 else n
    return pl.pallas_call(
        _mod_kernel,
        grid=(depth, n // tn),
        in_specs=[pl.BlockSpec((r, d), lambda i, j: (0, 0)),
                  pl.BlockSpec((None, d, tn), lambda i, j: (i, 0, j)),
                  pl.BlockSpec((None, 1, tn), lambda i, j: (i, 0, j))],
        out_specs=pl.BlockSpec((None, r, tn), lambda i, j: (i, 0, j)),
        out_shape=jax.ShapeDtypeStruct((depth, r, n), F32),
        compiler_params=_params(("parallel", "parallel")),
        name="modulation",
    )(cvec, w_mod, b_mod.reshape(depth, 1, n))


def _qkv_kernel(x_ref, g_ref, sc_ref, sh_ref, w_ref, cos_ref, sin_ref, o_ref, h_scr,
                *, sub, n_q, n_rope, q_scale):
    j = pl.program_id(1)
    tm, tn = o_ref.shape

    @pl.when(j == 0)
    def _():
        _norm_mod_to(h_scr, x_ref, g_ref, sc_ref, sh_ref, sub)

    acc = jnp.dot(h_scr[...], w_ref[...], preferred_element_type=F32)
    cos = cos_ref[...]
    sin = sin_ref[...]
    heads_per_tile = tn // HEAD_DIM
    for l in range(heads_per_tile):
        t = acc[:, l * HEAD_DIM:(l + 1) * HEAD_DIM]
        head = j * heads_per_tile + l
        tr = t * cos + pltpu.roll(t, HEAD_DIM // 2, 1) * sin
        t = jnp.where(head < n_rope, tr, t)
        t = t * jnp.where(head < n_q, q_scale, 1.0)
        o_ref[:, l * HEAD_DIM:(l + 1) * HEAD_DIM] = t.astype(BF16)


def _qkv_proj(u, g, mods, w, cos_u, sin_u, *, layer, w_layer, tm, sub, n_q, n_kv):
    n_rows, d = u.shape
    n = w.shape[2]
    tn = n if n <= PROJ_TILE else PROJ_TILE
    assert n % tn == 0
    nsb = tm // sub
    kern = functools.partial(_qkv_kernel, sub=sub, n_q=n_q, n_rope=n_q + n_kv,
                             q_scale=HEAD_DIM ** -0.5)
    return pl.pallas_call(
        kern,
        grid=(n_rows // tm, n // tn),
        in_specs=[pl.BlockSpec((tm, d), lambda i, j: (i, 0)),
                  _layer_spec((1, d), layer, lambda i, j: (0, 0)),
                  _mod_spec(nsb, d, layer, MOD_SCALE1),
                  _mod_spec(nsb, d, layer, MOD_SHIFT1),
                  _layer_spec((d, tn), w_layer, lambda i, j: (0, j)),
                  pl.BlockSpec((tm, HEAD_DIM), lambda i, j: (i, 0)),
                  pl.BlockSpec((tm, HEAD_DIM), lambda i, j: (i, 0))],
        out_specs=pl.BlockSpec((tm, tn), lambda i, j: (i, j)),
        out_shape=jax.ShapeDtypeStruct((n_rows, n), BF16),
        scratch_shapes=[pltpu.VMEM((tm, d), BF16)],
        compiler_params=_params(("parallel", "arbitrary")),
        name="attn_qkv",
    )(u, g, mods, mods, w, cos_u, sin_u)


def _softmax_pv(s, sink, v):
    m = jnp.maximum(jnp.max(s, axis=-1, keepdims=True), sink)
    p = jnp.exp(s - m)
    l = jnp.sum(p, axis=-1, keepdims=True) + jnp.exp(sink - m)
    o = jnp.dot(p.astype(BF16), v, preferred_element_type=F32)
    return o * (1.0 / l)


def _stack_heads(q, group):
    return jnp.concatenate([q[:, g * HEAD_DIM:(g + 1) * HEAD_DIM] for g in range(group)], axis=0)


def _attn_x_kernel(q_ref, kp_ref, ko_ref, kn_ref, kc_ref, vp_ref, vo_ref, vn_ref, vc_ref,
                   mask_ref, sink_ref, o_ref, *, group, n_kv):
    blk = q_ref.shape[0]
    valid = mask_ref[...] > 0.0

    def scores(h):
        kv = slice(h * HEAD_DIM, (h + 1) * HEAD_DIM)
        qs = _stack_heads(q_ref[:, h * group * HEAD_DIM:(h + 1) * group * HEAD_DIM], group)
        k = jnp.concatenate([kp_ref[:, kv], ko_ref[:, kv], kn_ref[:, kv], kc_ref[:, kv]], axis=0)
        s = lax.dot_general(qs, k, (((1,), (1,)), ((), ())), preferred_element_type=F32)
        return jnp.concatenate([jnp.where(valid, s[:, :3 * blk], NEG_INF), s[:, 3 * blk:]], axis=1)

    def finish(h, s):
        kv = slice(h * HEAD_DIM, (h + 1) * HEAD_DIM)
        v = jnp.concatenate([vp_ref[:, kv], vo_ref[:, kv], vn_ref[:, kv], vc_ref[:, kv]], axis=0)
        o = _softmax_pv(s, sink_ref[h], v)
        for g in range(group):
            c0 = (h * group + g) * HEAD_DIM
            o_ref[:, c0:c0 + HEAD_DIM] = o[g * blk:(g + 1) * blk, :].astype(BF16)

    s_next = scores(0)
    for h in range(n_kv):
        s_cur = s_next
        if h + 1 < n_kv:
            s_next = scores(h + 1)
        finish(h, s_cur)


def _attn_c_kernel(q_ref, kc_ref, vc_ref, sink_ref, prev_ref, o_ref, *, group, n_kv):
    del prev_ref
    lc = q_ref.shape[0]

    def scores(h):
        qs = _stack_heads(q_ref[:, h * group * HEAD_DIM:(h + 1) * group * HEAD_DIM], group)
        return lax.dot_general(qs, kc_ref[:, h * HEAD_DIM:(h + 1) * HEAD_DIM], (((1,), (1,)), ((), ())),
                               preferred_element_type=F32)

    s_next = scores(0)
    for h in range(n_kv):
        s_cur = s_next
        if h + 1 < n_kv:
            s_next = scores(h + 1)
        o = _softmax_pv(s_cur, sink_ref[h], vc_ref[:, h * HEAD_DIM:(h + 1) * HEAD_DIM])
        for g in range(group):
            c0 = (h * group + g) * HEAD_DIM
            o_ref[:, c0:c0 + HEAD_DIM] = o[g * lc:(g + 1) * lc, :].astype(BF16)


def _window_mask(group, blk, n_blocks):
    qi = np.arange(group * blk)[:, None] % blk
    kj = np.arange(3 * blk)[None, :]
    band = (kj - qi >= 0) & (kj - qi <= 2 * WINDOW)
    first = band & (kj >= blk)
    last = band & (kj < 2 * blk)
    kinds = [first & last if n_blocks == 1 else first, band, last]
    return np.stack(kinds).astype(np.float32)


def _attention(qkv, sink, *, batch, seq, ctx, n_q, n_kv):
    n_rows = qkv.shape[0]
    group = n_q // n_kv
    blk = WINDOW
    p_rows = ctx + seq
    bpb = p_rows // blk
    cb = ctx // blk
    nb = seq // blk
    last_blk = n_rows // blk - 1
    qw = n_q * HEAD_DIM
    kw = n_kv * HEAD_DIM
    assert qw % kw == 0
    k_col = qw // kw
    spc = p_rows // ctx
    sink_g = sink.reshape(n_kv, group, 1).astype(F32)
    sink_x = jnp.broadcast_to(sink_g[:, :, None, :], (n_kv, group, blk, 1)).reshape(n_kv, group * blk, 1)
    sink_c = jnp.broadcast_to(sink_g[:, :, None, :], (n_kv, group, ctx, 1)).reshape(n_kv, group * ctx, 1)
    mask = jnp.asarray(_window_mask(group, blk, nb))

    def kv_specs(col):
        return [pl.BlockSpec((blk, kw), lambda b, n: (jnp.maximum(b * bpb + cb + n - 1, 0), col)),
                pl.BlockSpec((blk, kw), lambda b, n: (b * bpb + cb + n, col)),
                pl.BlockSpec((blk, kw), lambda b, n: (jnp.minimum(b * bpb + cb + n + 1, last_blk), col)),
                pl.BlockSpec((ctx, kw), lambda b, n: (b * spc, col))]

    out_x = pl.pallas_call(
        functools.partial(_attn_x_kernel, group=group, n_kv=n_kv),
        grid=(batch, nb),
        in_specs=[pl.BlockSpec((blk, qw), lambda b, n: (b * bpb + cb + n, 0))]
        + kv_specs(k_col) + kv_specs(k_col + 1)
        + [pl.BlockSpec((None, group * blk, 3 * blk),
                        lambda b, n: (jnp.where(n == 0, 0, jnp.where(n == nb - 1, 2, 1)), 0, 0)),
           pl.BlockSpec((n_kv, group * blk, 1), lambda b, n: (0, 0, 0))],
        out_specs=pl.BlockSpec((blk, qw), lambda b, n: (b * bpb + cb + n, 0)),
        out_shape=jax.ShapeDtypeStruct((n_rows, qw), BF16),
        compiler_params=_params(("parallel", "parallel")),
        name="attn_latent",
    )(qkv, qkv, qkv, qkv, qkv, qkv, qkv, qkv, qkv, mask, sink_x)

    out = pl.pallas_call(
        functools.partial(_attn_c_kernel, group=group, n_kv=n_kv),
        grid=(batch,),
        in_specs=[pl.BlockSpec((ctx, qw), lambda b: (b * spc, 0)),
                  pl.BlockSpec((ctx, kw), lambda b: (b * spc, k_col)),
                  pl.BlockSpec((ctx, kw), lambda b: (b * spc, k_col + 1)),
                  pl.BlockSpec((n_kv, group * ctx, 1), lambda b: (0, 0, 0)),
                  pl.BlockSpec(memory_space=pl.ANY)],
        out_specs=pl.BlockSpec((ctx, qw), lambda b: (b * spc, 0)),
        out_shape=jax.ShapeDtypeStruct((n_rows, qw), BF16),
        input_output_aliases={4: 0},
        compiler_params=_params(("parallel",)),
        name="attn_context",
    )(qkv, qkv, qkv, sink_c, out_x)
    return out


def _proj_res_kernel(y_ref, w_ref, x_ref, gt_ref, g_ref, sc_ref, sh_ref, o_ref, h_ref, *, sub):
    acc = jnp.dot(y_ref[...], w_ref[...], preferred_element_type=F32)
    g = g_ref[...]
    for s in range(o_ref.shape[0] // sub):
        rows = slice(s * sub, (s + 1) * sub)
        xn = x_ref[rows, :] + gt_ref[s] * acc[rows, :]
        o_ref[rows, :] = xn
        h_ref[rows, :] = _norm_mod(xn, g, sc_ref[s], sh_ref[s]).astype(BF16)


def _proj_residual(y, w, u, mods, g, *, layer, w_layer, tm, sub, latent_only=None):
    n_rows, d = u.shape
    k = y.shape[1]
    out_rows, out_map = n_rows, lambda i: (i, 0)
    if latent_only is not None:
        spb, cs = latent_only
        tm = sub
        n_lat = n_rows // sub // spb * (spb - cs)
        out_rows = (n_lat + 1) * sub
        out_map = lambda i: (jnp.where(i % spb >= cs, (i // spb) * (spb - cs) + i % spb - cs, n_lat), 0)
    nsb = tm // sub
    return pl.pallas_call(
        functools.partial(_proj_res_kernel, sub=sub),
        grid=(n_rows // tm,),
        in_specs=[pl.BlockSpec((tm, k), lambda i: (i, 0)),
                  _layer_spec((k, d), w_layer, lambda i: (0, 0)),
                  pl.BlockSpec((tm, d), lambda i: (i, 0)),
                  _mod_spec(nsb, d, layer, MOD_GATE1),
                  _layer_spec((1, d), layer, lambda i: (0, 0)),
                  _mod_spec(nsb, d, layer, MOD_SCALE2),
                  _mod_spec(nsb, d, layer, MOD_SHIFT2)],
        out_specs=[pl.BlockSpec((tm, d), out_map),
                   pl.BlockSpec((tm, d), out_map)],
        out_shape=[jax.ShapeDtypeStruct((out_rows, d), F32),
                   jax.ShapeDtypeStruct((out_rows, d), BF16)],
        compiler_params=_params(("arbitrary",)),
        name="proj_residual",
    )(y, w, u, mods, g, mods, mods)


def _mlstm_in_kernel(x_ref, g_ref, sc_ref, sh_ref, w_ref, b_ref, wg_ref, bg_ref, z_ref, gate_ref,
                     h_scr, *, sub, group, k_scale):
    j = pl.program_id(1)
    tm, tn = z_ref.shape

    @pl.when(j == 0)
    def _():
        _norm_mod_to(h_scr, x_ref, g_ref, sc_ref, sh_ref, sub)
        gate_ref[...] = jnp.dot(h_scr[...], wg_ref[...], preferred_element_type=F32) + bg_ref[...]

    for l in range(tn // group):
        cols = slice(l * group, (l + 1) * group)
        acc = jnp.dot(h_scr[...], w_ref[:, cols], preferred_element_type=F32) + b_ref[:, cols]
        is_k = j * (tn // group) + l == 1
        z_ref[:, cols] = (acc * jnp.where(is_k, k_scale, 1.0)).astype(BF16)


def _mlstm_in_proj(u, g, mods, w, b, wg, bg, *, layer, w_layer, tm, sub, dqk):
    n_rows, d = u.shape
    n = 3 * d
    qk_cols = d // 2
    tn = n // 2 if n // 2 <= PROJ_TILE and (n // 2) % qk_cols == 0 else qk_cols
    nsb = tm // sub
    kern = functools.partial(_mlstm_in_kernel, sub=sub, group=qk_cols, k_scale=dqk ** -0.5)
    return pl.pallas_call(
        kern,
        grid=(n_rows // tm, n // tn),
        in_specs=[pl.BlockSpec((tm, d), lambda i, j: (i, 0)),
                  _layer_spec((1, d), layer, lambda i, j: (0, 0)),
                  _mod_spec(nsb, d, layer, MOD_SCALE1),
                  _mod_spec(nsb, d, layer, MOD_SHIFT1),
                  _layer_spec((d, tn), w_layer, lambda i, j: (0, j)),
                  _layer_spec((1, tn), w_layer, lambda i, j: (0, j)),
                  pl.BlockSpec((d, LANE), lambda i, j: (0, 0)),
                  pl.BlockSpec((1, LANE), lambda i, j: (0, 0))],
        out_specs=[pl.BlockSpec((tm, tn), lambda i, j: (i, j)),
                   pl.BlockSpec((tm, LANE), lambda i, j: (i, 0))],
        out_shape=[jax.ShapeDtypeStruct((n_rows, n), BF16),
                   jax.ShapeDtypeStruct((n_rows, LANE), F32)],
        scratch_shapes=[pltpu.VMEM((tm, d), BF16)],
        compiler_params=_params(("parallel", "arbitrary")),
        name="mlstm_in",
    )(u, g, mods, mods, w, b, wg, bg)


GATE_F_LANE = 64
N_GATE_OUT = 6
N_COL = 5


def _chunk_scan(x, scr, pos, op, ident, reverse):
    r = x.shape[0]
    sh = 1
    while sh < CHUNK:
        scr[CHUNK:CHUNK + r, :] = x
        if reverse:
            other = scr[CHUNK + sh:CHUNK + sh + r, :]
            ok = pos < CHUNK - sh
        else:
            other = scr[CHUNK - sh:CHUNK - sh + r, :]
            ok = pos >= sh
        x = op(x, jnp.where(ok, other, ident))
        sh *= 2
    return x


def _mlstm_gate_kernel(g_ref, o_ref, b_scr, cm_scr, shift_scr, *, n_heads, n_ctx_chunks, tile):
    p_rows = g_ref.shape[0]
    n_chunks = p_rows // CHUNK
    shift_scr[...] = jnp.zeros_like(shift_scr)
    o_ref[0:N_COL] = jnp.zeros((N_COL,) + o_ref.shape[1:], F32)
    fwd_tile = lax.broadcasted_iota(jnp.int32, (tile, LANE), 1) < n_heads
    pos = lax.broadcasted_iota(jnp.int32, (tile, LANE), 0) % CHUNK

    def local(t, _):
        rows = pl.ds(pl.multiple_of(t * tile, tile), tile)
        g = g_ref[rows, :]
        lf = _log_sigmoid(pltpu.roll(g, LANE - GATE_F_LANE, 1))
        b = jnp.where(fwd_tile, _chunk_scan(lf, shift_scr, pos, jnp.add, 0.0, False),
                      _chunk_scan(lf, shift_scr, pos, jnp.add, 0.0, True))
        a = g - b
        cm = jnp.where(fwd_tile, _chunk_scan(a, shift_scr, pos, jnp.maximum, -jnp.inf, False),
                       _chunk_scan(a, shift_scr, pos, jnp.maximum, -jnp.inf, True))
        b_scr[rows, :] = b
        cm_scr[rows, :] = cm
        o_ref[N_COL, rows, :] = a
        return 0

    lax.fori_loop(0, p_rows // tile, local, 0)

    fwd_chunk = lax.broadcasted_iota(jnp.int32, (CHUNK, LANE), 1) < n_heads

    def step(c, m, forward):
        rows = pl.ds(pl.multiple_of(c * CHUNK, CHUNK), CHUNK)
        end_row = CHUNK - 1 if forward else 0
        mine = fwd_chunk if forward else jnp.logical_not(fwd_chunk)
        b = b_scr[rows, :]
        a = o_ref[N_COL, rows, :]
        cm = cm_scr[rows, :]
        b_end = b[end_row:end_row + 1, :]
        m_t = b + jnp.maximum(m, cm)
        m_new = jnp.maximum(b_end + m, b_end + cm[end_row:end_row + 1, :])
        vals = (b - m_t, jnp.exp(b + m - m_t), jnp.exp(-m_t), jnp.exp(b_end + a - m_new),
                jnp.broadcast_to(jnp.exp(b_end + m - m_new), (CHUNK, LANE)))
        for q, val in enumerate(vals):
            o_ref[q, rows, :] = jnp.where(mine, val, o_ref[q, rows, :])
        return m_new

    def body(j, carry):
        mf, mb = carry
        mf = step(j, mf, True)
        cb = jnp.where(j < n_ctx_chunks, n_ctx_chunks - 1 - j, n_chunks - 1 + n_ctx_chunks - j)
        mb = step(cb, mb, False)
        return mf, mb

    zm = jnp.zeros((1, LANE), F32)
    lax.fori_loop(0, n_chunks, body, (zm, zm))


def _mlstm_gates(gates, *, batch, p_rows, ctx, n_heads):
    n_rows = gates.shape[0]
    tile = ctx
    kern = functools.partial(_mlstm_gate_kernel, n_heads=n_heads, n_ctx_chunks=ctx // CHUNK, tile=tile)
    return pl.pallas_call(
        kern,
        grid=(batch,),
        in_specs=[pl.BlockSpec((p_rows, LANE), lambda b: (b, 0))],
        out_specs=pl.BlockSpec((N_GATE_OUT, p_rows, LANE), lambda b: (0, b, 0)),
        out_shape=jax.ShapeDtypeStruct((N_GATE_OUT, n_rows, LANE), F32),
        scratch_shapes=[pltpu.VMEM((p_rows, LANE), F32), pltpu.VMEM((p_rows, LANE), F32),
                        pltpu.VMEM((tile + 2 * CHUNK, LANE), F32)],
        compiler_params=_params(("parallel",)),
        name="mlstm_gates",
    )(gates)


def _mlstm_chunk(q, k, v, col, a_row, c_scr, n, d, reverse):
    ch = q.shape[0]
    t_idx = lax.broadcasted_iota(jnp.int32, (ch, ch), 0)
    s_idx = lax.broadcasted_iota(jnp.int32, (ch, ch), 1)
    seen = (s_idx >= t_idx) if reverse else (s_idx <= t_idx)
    c0 = d * N_COL
    row_term = col[:, c0:c0 + 1]
    e_inter = col[:, c0 + 1:c0 + 2]
    floor = col[:, c0 + 2:c0 + 3]
    wg_col = col[:, c0 + 3:c0 + 4]
    e_prev = col[0:1, c0 + 4:c0 + 5]

    w = jnp.where(seen, jnp.exp(row_term + a_row), 0.0)
    s = lax.dot_general(q, k, (((1,), (1,)), ((), ())), preferred_element_type=F32) * w
    c_old = c_scr[...]
    num = (e_inter * jnp.dot(q, c_old.astype(BF16), preferred_element_type=F32)
           + jnp.dot(s.astype(BF16), v, preferred_element_type=F32))
    qn = jnp.sum(q.astype(F32) * n, axis=1, keepdims=True)
    den = e_inter * qn + jnp.sum(s, axis=1, keepdims=True)
    h = num * (1.0 / jnp.maximum(jnp.abs(den), floor))

    vw = (wg_col * v.astype(F32)).astype(BF16)
    c_scr[...] = e_prev * c_old + lax.dot_general(k, vw, (((0,), (0,)), ((), ())),
                                                  preferred_element_type=F32)
    n_new = e_prev * n + jnp.sum(wg_col * k.astype(F32), axis=0, keepdims=True)
    return h, n_new


def _mlstm_core_kernel(q_ref, k_ref, v_ref, o_ref, gr_ref, gc_ref, gh_ref, y_ref,
                       hf_scr, hb_scr, cf_scr, cb_scr, *, n_ctx_chunks, out_rows):
    n_chunks = gr_ref.shape[0]
    dqk = q_ref.shape[1]
    cf_scr[...] = jnp.zeros_like(cf_scr)
    cb_scr[...] = jnp.zeros_like(cb_scr)

    def run(c, c_scr, h_scr, n, d, reverse):
        rows = pl.ds(pl.multiple_of(c * CHUNK, CHUNK), CHUNK)
        h, n = _mlstm_chunk(q_ref[rows, :], k_ref[rows, :], v_ref[rows, :], gc_ref[rows, :],
                            gr_ref[c][d:d + 1, :], c_scr, n, d, reverse)
        h_scr[rows, :] = h
        return n

    def body(j, carry):
        nf, nb = carry
        nf = run(j, cf_scr, hf_scr, nf, 0, False)
        cb = jnp.where(j < n_ctx_chunks, n_ctx_chunks - 1 - j, n_chunks - 1 + n_ctx_chunks - j)
        nb = run(cb, cb_scr, hb_scr, nb, 1, True)
        return nf, nb

    zn = jnp.zeros((1, dqk), F32)
    lax.fori_loop(0, n_chunks, body, (zn, zn), unroll=2)

    gh = gh_ref[...]

    def finish(r, _):
        rows = pl.ds(pl.multiple_of(r * out_rows, out_rows), out_rows)
        h = hf_scr[rows, :] + hb_scr[rows, :]
        hn = h * lax.rsqrt(jnp.mean(h * h, axis=-1, keepdims=True) + RMS_EPS)
        y_ref[rows, :] = (_sigmoid(o_ref[rows, :].astype(F32)) * hn * gh).astype(BF16)
        return 0

    lax.fori_loop(0, y_ref.shape[0] // out_rows, finish, 0)


def _mlstm_core(z, gates_r, gates_c, g_head, *, batch, p_rows, ctx, n_heads, dqk, dv):
    n_rows = z.shape[0]
    n_chunks = p_rows // CHUNK
    kern = functools.partial(_mlstm_core_kernel, n_ctx_chunks=ctx // CHUNK, out_rows=ctx)
    return pl.pallas_call(
        kern,
        grid=(batch, n_heads),
        in_specs=[pl.BlockSpec((p_rows, dqk), lambda b, h: (b, h)),
                  pl.BlockSpec((p_rows, dqk), lambda b, h: (b, n_heads + h)),
                  pl.BlockSpec((p_rows, dv), lambda b, h: (b, n_heads + h)),
                  pl.BlockSpec((p_rows, dv), lambda b, h: (b, 2 * n_heads + h)),
                  pl.BlockSpec((None, None, n_chunks, 2, CHUNK), lambda b, h: (b, h, 0, 0, 0)),
                  pl.BlockSpec((None, None, p_rows, 2 * N_COL), lambda b, h: (b, h, 0, 0)),
                  pl.BlockSpec((1, dv), lambda b, h: (0, h))],
        out_specs=pl.BlockSpec((p_rows, dv), lambda b, h: (b, h)),
        out_shape=jax.ShapeDtypeStruct((n_rows, n_heads * dv), BF16),
        scratch_shapes=[pltpu.VMEM((p_rows, dv), F32), pltpu.VMEM((p_rows, dv), F32),
                        pltpu.VMEM((dqk, dv), F32), pltpu.VMEM((dqk, dv), F32)],
        compiler_params=_params(("parallel", "parallel")),
        name="mlstm_core",
    )(z, z, z, z, gates_r, gates_c, g_head)


def _gate_layouts(og, *, batch, p_rows, n_heads):
    g = og[:, :, :2 * n_heads].reshape(N_GATE_OUT, batch, p_rows, 2, n_heads)
    cols = g[:N_COL].transpose(1, 4, 2, 3, 0).reshape(batch, n_heads, p_rows, 2 * N_COL)
    rows = g[N_COL].reshape(batch, p_rows // CHUNK, CHUNK, 2, n_heads).transpose(0, 4, 1, 3, 2)
    return rows, cols


def _halo_up_kernel(h_ref, w_ref, o_ref):
    o_ref[...] = jnp.dot(h_ref[...], w_ref[...], preferred_element_type=F32)


def _halo_up(h, w_up, *, layer, n_rows, tm):
    d = h.shape[1]
    n = w_up.shape[2]
    nt = n_rows // tm
    first, last = h[0:n_rows:tm], h[tm - 1:n_rows:tm]
    prev = jnp.concatenate([last[:1], last[:nt - 1]], axis=0)
    nxt = jnp.concatenate([first[1:], first[nt - 1:]], axis=0)
    hh = jnp.stack([prev, nxt], axis=1).reshape(2 * nt, d)
    rows = 2 * nt
    tn = 2 * FFN_CHUNK
    uh = pl.pallas_call(
        _halo_up_kernel,
        grid=(n // tn,),
        in_specs=[pl.BlockSpec((rows, d), lambda j: (0, 0)),
                  _layer_spec((d, tn), layer, lambda j: (0, j))],
        out_specs=pl.BlockSpec((rows, tn), lambda j: (0, j)),
        out_shape=jax.ShapeDtypeStruct((rows, n), F32),
        compiler_params=_params(("parallel",)),
        name="ffn_halo",
    )(hh, w_up)
    return uh.reshape(nt, 2, n)


def _ffn_kernel(h_ref, x_ref, gt_ref, wa0_ref, wb0_ref, wa1_ref, wb1_ref,
                ha0_ref, hb0_ref, ha1_ref, hb1_ref, cwa_ref, cwb_ref, cba_ref, cbb_ref, wd_ref,
                gf_ref, o_ref, acc_scr, u0_scr, u1_scr, a0_scr, a1_scr,
                *, sub, p_rows, ctx, final_norm):
    i = pl.program_id(0)
    j = pl.program_id(1)
    tm = o_ref.shape[0]
    nsb = tm // sub
    fh = wa0_ref.shape[1]
    lt = fh // LANE
    seg = sub + 2 * HALO

    @pl.when(j == 0)
    def _():
        acc_scr[...] = jnp.zeros_like(acc_scr)

    def edge_open(k):
        pos = (i * tm + k * sub) % p_rows
        return jnp.where((pos == 0) | (pos == ctx), 0.0, 1.0)

    opens = [edge_open(k) for k in range(nsb + 1)]

    def up(w_refs, halo_refs, u_scr):
        for br, (w_ref, halo_ref) in enumerate(zip(w_refs, halo_refs)):
            ue = jnp.dot(h_ref[...], w_ref[...], preferred_element_type=F32)
            halo = halo_ref[...]
            for s in range(nsb):
                r0 = s * sub
                b0 = s * seg
                for t in range(lt):
                    lanes = slice(t * LANE, (t + 1) * LANE)
                    tt = br * lt + t
                    before = (jnp.broadcast_to(halo[0:1, lanes], (HALO, LANE)) if s == 0
                              else ue[r0 - HALO:r0, lanes])
                    after = (jnp.broadcast_to(halo[1:2, lanes], (HALO, LANE)) if s == nsb - 1
                             else ue[r0 + sub:r0 + sub + HALO, lanes])
                    u_scr[tt, b0:b0 + HALO, :] = before * opens[s]
                    u_scr[tt, b0 + HALO:b0 + HALO + sub, :] = ue[r0:r0 + sub, lanes]
                    u_scr[tt, b0 + HALO + sub:b0 + seg, :] = after * opens[s + 1]

    def gate(c, u_scr, act_scr):
        cols = slice(c * fh, (c + 1) * fh)
        cws = (cwa_ref[:, cols], cwb_ref[:, cols])
        cbs = (cba_ref[:, cols], cbb_ref[:, cols])

        def conv_piece(br, t, r0):
            lanes = slice(t * LANE, (t + 1) * LANE)
            tt = br * lt + t + jnp.minimum(j, 0)
            cw, cb = cws[br], cbs[br]
            cur = u_scr[tt, r0:r0 + GATE_ROWS, :]
            prv = u_scr[tt, r0 - 1:r0 - 1 + GATE_ROWS, :]
            nxt = u_scr[tt, r0 + 1:r0 + 1 + GATE_ROWS, :]
            return cb[:, lanes] + prv * cw[0:1, lanes] + cur * cw[1:2, lanes] + nxt * cw[2:3, lanes]

        for s in range(nsb):
            for r in range(sub // GATE_ROWS):
                r0 = s * seg + HALO + r * GATE_ROWS
                o0 = s * sub + r * GATE_ROWS
                for t in range(lt):
                    a = conv_piece(0, t, r0)
                    b = conv_piece(1, t, r0)
                    act_scr[o0:o0 + GATE_ROWS, t * LANE:(t + 1) * LANE] = (a * _sigmoid(a) * b).astype(BF16)

    def down(c, act_scr):
        acc_scr[...] += jnp.dot(act_scr[...], wd_ref[c * fh:(c + 1) * fh, :], preferred_element_type=F32)

    up((wa0_ref, wb0_ref), (ha0_ref, hb0_ref), u0_scr)
    up((wa1_ref, wb1_ref), (ha1_ref, hb1_ref), u1_scr)
    gate(0, u0_scr, a0_scr)
    down(0, a0_scr)
    gate(1, u1_scr, a1_scr)
    down(1, a1_scr)

    @pl.when(j == pl.num_programs(1) - 1)
    def _():
        for s in range(nsb):
            if not final_norm:
                rows = slice(s * sub, (s + 1) * sub)
                o_ref[rows, :] = x_ref[rows, :] + gt_ref[s] * acc_scr[rows, :]
                continue
            for r in range(s * sub, (s + 1) * sub, NORM_ROWS):
                rows = slice(r, r + NORM_ROWS)
                xn = x_ref[rows, :] + gt_ref[s] * acc_scr[rows, :]
                ms = jnp.mean(xn * xn, axis=-1, keepdims=True)
                o_ref[rows, :] = xn * lax.rsqrt(ms + RMS_EPS) * gf_ref[...]


def _conv_ffn(h, u, mods, w_up, conv_w, conv_b, w_down, g_final, *, layer, n_rows, tm, sub, p_rows, ctx,
              final_norm):
    d = u.shape[1]
    d_ff = w_down.shape[1]
    fc = FFN_CHUNK
    assert d_ff % fc == 0
    fh = fc // 2
    nfc = d_ff // fc
    nsb = tm // sub
    kern = functools.partial(_ffn_kernel, sub=sub, p_rows=p_rows, ctx=ctx, final_norm=final_norm)
    u_shape = (fc // LANE, nsb * (sub + 2 * HALO), LANE)
    halo = _halo_up(h, w_up, layer=layer, n_rows=n_rows, tm=tm)

    def half_specs(shape, lead):
        return [pl.BlockSpec(shape, lambda i, j, c=c: lead(i) + (c + 2 * j,))
                for c in (0, 2 * nfc, 1, 2 * nfc + 1)]

    return pl.pallas_call(
        kern,
        grid=(n_rows // tm, nfc),
        in_specs=[pl.BlockSpec((tm, d), lambda i, j: (i, 0)),
                  pl.BlockSpec((tm, d), lambda i, j: (i, 0)),
                  _mod_spec(nsb, d, layer, MOD_GATE2)]
        + half_specs((None, d, fh), lambda i: (layer, 0))
        + half_specs((None, 2, fh), lambda i: (i, 0))
        + [_layer_spec((CONV_W, fc), layer, lambda i, j: (0, j)),
           _layer_spec((CONV_W, fc), layer, lambda i, j: (0, nfc + j)),
           _layer_spec((1, fc), layer, lambda i, j: (0, j)),
           _layer_spec((1, fc), layer, lambda i, j: (0, nfc + j)),
           _layer_spec((fc, d), layer, lambda i, j: (j, 0)),
           pl.BlockSpec((1, d), lambda i, j: (0, 0))],
        out_specs=pl.BlockSpec((tm, d), lambda i, j: (i, 0)),
        out_shape=jax.ShapeDtypeStruct((n_rows, d), F32),
        scratch_shapes=[pltpu.VMEM((tm, d), F32), pltpu.VMEM(u_shape, F32), pltpu.VMEM(u_shape, F32),
                        pltpu.VMEM((tm, fh), BF16), pltpu.VMEM((tm, fh), BF16)],
        compiler_params=_params(("parallel", "arbitrary")),
        name="conv_ffn",
    )(h, u, mods, w_up, w_up, w_up, w_up, halo, halo, halo, halo,
      conv_w, conv_w, conv_b, conv_b, w_down, g_final)


def _rope_tables(batch, seq, ctx):
    rows = seq // GRID_W
    row = jnp.repeat(jnp.arange(rows, dtype=F32), GRID_W)
    col = jnp.tile(jnp.arange(GRID_W, dtype=F32), rows)
    n_freq = HEAD_DIM // 4
    inv_freq = ROPE_THETA ** (-jnp.arange(n_freq, dtype=F32) / n_freq)
    ang = jnp.concatenate([row[:, None] * inv_freq, col[:, None] * inv_freq], axis=-1)
    ang = jnp.concatenate([ang, ang], axis=-1)
    sign = jnp.concatenate([-jnp.ones((HEAD_DIM // 2,), F32), jnp.ones((HEAD_DIM // 2,), F32)])
    cos = jnp.concatenate([jnp.ones((ctx, HEAD_DIM), F32), jnp.cos(ang)], axis=0)
    sin = jnp.concatenate([jnp.zeros((ctx, HEAD_DIM), F32), jnp.sin(ang) * sign], axis=0)
    return jnp.tile(cos, (batch, 1)), jnp.tile(sin, (batch, 1))


def kernel(x, c, ctx, c_ctx, w_mod, b_mod, g_mix, g_ffn, attn_w_qkv, attn_sink, attn_w_o,
           mlstm_w_in, mlstm_b_in, mlstm_g_head, mlstm_w_o, ffn_w_up, ffn_conv_w, ffn_conv_b,
           ffn_w_down, g_final):
    batch, seq, d = x.shape
    lc = ctx.shape[1]
    depth = w_mod.shape[0]
    p_rows = lc + seq
    sub = lc
    assert seq % sub == 0 and sub % WINDOW == 0 and sub % CHUNK == 0 and seq % GRID_W == 0
    n_sub = batch * p_rows // sub
    tm = 2 * sub if n_sub % 2 == 0 else sub
    n_q = attn_sink.shape[1]
    n_kv = (attn_w_qkv.shape[2] // HEAD_DIM - n_q) // 2
    n_heads = (mlstm_w_in.shape[2] - 3 * d) // 4
    dv = d // n_heads
    dqk = dv // 2
    main_cols = 3 * d

    u = jnp.concatenate([ctx, x], axis=1).reshape(batch * p_rows, d)

    n_c = batch + 1
    c_rows = -(-n_c // 8) * 8
    cvec = jnp.concatenate([c, c_ctx[None, :], jnp.zeros((c_rows - n_c, d), F32)], axis=0)
    mods = _modulation(cvec, w_mod, b_mod)
    mods_sb = jnp.concatenate(
        [jnp.broadcast_to(mods[:, None, batch:batch + 1], (depth, batch, lc // sub, 6 * d)),
         jnp.broadcast_to(mods[:, :batch, None], (depth, batch, seq // sub, 6 * d))],
        axis=2).reshape(depth, n_sub, 6, 1, d)

    mods_lat = jnp.broadcast_to(mods[:, :batch, None], (depth, batch, seq // sub, 6 * d)
                                ).reshape(depth, batch * seq // sub, 6, 1, d)

    cos_u, sin_u = _rope_tables(batch, seq, lc)

    w_qkv = attn_w_qkv.astype(BF16)
    w_attn_o = attn_w_o.astype(BF16)
    w_in = mlstm_w_in.astype(BF16)
    w_mlstm_o = mlstm_w_o.astype(BF16)
    w_up = ffn_w_up.astype(BF16)
    w_down = ffn_w_down.astype(BF16)
    g_mix3 = g_mix.reshape(depth, 1, d)
    g_ffn3 = g_ffn.reshape(depth, 1, d)
    b_in3 = mlstm_b_in.reshape(mlstm_b_in.shape[0], 1, -1)
    conv_b3 = ffn_conv_b.reshape(depth, 1, -1)
    g_fin = g_final.reshape(1, d)

    for i in range(depth):
        jm = i // 2
        if i % 2 == 0:
            qkv = _qkv_proj(u, g_mix3, mods_sb, w_qkv, cos_u, sin_u, layer=i, w_layer=jm,
                            tm=tm, sub=sub, n_q=n_q, n_kv=n_kv)
            y = _attention(qkv, attn_sink[jm], batch=batch, seq=seq, ctx=lc, n_q=n_q, n_kv=n_kv)
            w_o = w_attn_o
        else:
            wgt = mlstm_w_in[jm][:, main_cols:].reshape(d, 4, n_heads)
            bgt = mlstm_b_in[jm][main_cols:].reshape(4, n_heads)
            lane_pad = GATE_F_LANE - 2 * n_heads
            wg = jnp.concatenate([wgt[:, 0], wgt[:, 2], jnp.zeros((d, lane_pad), F32),
                                  wgt[:, 1], wgt[:, 3], jnp.zeros((d, lane_pad), F32)], axis=1).astype(BF16)
            bg = jnp.concatenate([bgt[0], bgt[2], jnp.zeros((lane_pad,), F32),
                                  bgt[1], bgt[3], jnp.zeros((lane_pad,), F32)]).reshape(1, LANE)
            z, gates = _mlstm_in_proj(u, g_mix3, mods_sb, w_in, b_in3, wg, bg, layer=i, w_layer=jm,
                                      tm=tm, sub=sub, dqk=dqk)
            og = _mlstm_gates(gates, batch=batch, p_rows=p_rows, ctx=lc, n_heads=n_heads)
            gates_r, gates_c = _gate_layouts(og, batch=batch, p_rows=p_rows, n_heads=n_heads)
            y = _mlstm_core(z, gates_r, gates_c, mlstm_g_head[jm].reshape(1, n_heads * dv),
                            batch=batch, p_rows=p_rows, ctx=lc, n_heads=n_heads, dqk=dqk, dv=dv)
            w_o = w_mlstm_o
        if i < depth - 1:
            u, h2 = _proj_residual(y, w_o, u, mods_sb, g_ffn3, layer=i, w_layer=jm, tm=tm, sub=sub)
            u = _conv_ffn(h2, u, mods_sb, w_up, ffn_conv_w, conv_b3, w_down, g_fin, layer=i,
                          n_rows=batch * p_rows, tm=tm, sub=sub, p_rows=p_rows, ctx=lc,
                          final_norm=False)
        else:
            u, h2 = _proj_residual(y, w_o, u, mods_sb, g_ffn3, layer=i, w_layer=jm, tm=tm, sub=sub,
                                   latent_only=(p_rows // sub, lc // sub))
            u = _conv_ffn(h2, u, mods_lat, w_up, ffn_conv_w, conv_b3, w_down, g_fin, layer=i,
                          n_rows=batch * seq, tm=tm, sub=sub, p_rows=seq, ctx=0,
                          final_norm=True)
    return u.reshape(batch, seq, d)
```

```python
import functools

import numpy as np
import jax
import jax.numpy as jnp
from jax import lax
from jax.experimental import pallas as pl
from jax.experimental.pallas import tpu as pltpu

F32 = jnp.float32
BF16 = jnp.bfloat16

RMS_EPS = 1e-6
HEAD_DIM = 128
WINDOW = 128
GRID_W = 64
ROPE_THETA = 10000.0
NEG_INF = -1e30
CHUNK = 128
CONV_W = 3
LANE = 128
HALO = 8
BF16_ROWS = 16
FFN_CHUNK = 512
GATE_ROWS = 64
NORM_ROWS = 16
PROJ_TILE = 3072
MOD_TILE = 1024
MOD_SHIFT1, MOD_SCALE1, MOD_GATE1, MOD_SHIFT2, MOD_SCALE2, MOD_GATE2 = range(6)
V7X_VMEM_BYTES = 64 * 1024 * 1024
VMEM_LIMIT = V7X_VMEM_BYTES * 7 // 8


def _params(sem):
    return pltpu.CompilerParams(dimension_semantics=sem, vmem_limit_bytes=VMEM_LIMIT)


def _sigmoid(v):
    return 1.0 / (1.0 + jnp.exp(-v))


def _log_sigmoid(v):
    return jnp.minimum(v, 0.0) - jnp.log(1.0 + jnp.exp(-jnp.abs(v)))


def _mod_spec(nsb, d, layer, kind):
    return pl.BlockSpec((None, nsb, None, 1, d), lambda i, *_: (layer, i, kind, 0, 0))


def _layer_spec(shape, layer, index):
    return pl.BlockSpec((None,) + shape, lambda *g: (layer,) + index(*g))


def _norm_mod(xv, g, scale, shift):
    ms = jnp.mean(xv * xv, axis=-1, keepdims=True)
    y = xv * lax.rsqrt(ms + RMS_EPS)
    return (y * g) * (1.0 + scale) + shift


def _norm_mod_to(h_scr, x_ref, g_ref, sc_ref, sh_ref, sub):
    for s in range(x_ref.shape[0] // sub):
        gain = g_ref[...] * (1.0 + sc_ref[s])
        shift = sh_ref[s]
        for r in range(s * sub, (s + 1) * sub, NORM_ROWS):
            xv = x_ref[r:r + NORM_ROWS, :]
            ms = jnp.mean(xv * xv, axis=-1, keepdims=True)
            h_scr[r:r + NORM_ROWS, :] = (xv * lax.rsqrt(ms + RMS_EPS) * gain + shift).astype(BF16)


def _mod_kernel(c_ref, w_ref, b_ref, o_ref):
    cv = c_ref[...]
    s = (cv * _sigmoid(cv)).astype(BF16)
    o_ref[...] = jnp.dot(s, w_ref[...].astype(BF16), preferred_element_type=F32) + b_ref[...]


def _modulation(cvec, w_mod, b_mod):
    depth, d, n = w_mod.shape
    r = cvec.shape[0]
    tn = MOD_TILE if n % MOD_TILE == 0 else n
    return pl.pallas_call(
        _mod_kernel,
        grid=(depth, n // tn),
        in_specs=[pl.BlockSpec((r, d), lambda i, j: (0, 0)),
                  pl.BlockSpec((None, d, tn), lambda i, j: (i, 0, j)),
                  pl.BlockSpec((None, 1, tn), lambda i, j: (i, 0, j))],
        out_specs=pl.BlockSpec((None, r, tn), lambda i, j: (i, 0, j)),
        out_shape=jax.ShapeDtypeStruct((depth, r, n), F32),
        compiler_params=_params(("parallel", "parallel")),
        name="modulation",
    )(cvec, w_mod, b_mod.reshape(depth, 1, n))


def _qkv_kernel(x_ref, g_ref, sc_ref, sh_ref, w_ref, cos_ref, sin_ref, o_ref, h_scr,
                *, sub, n_q, n_rope, q_scale):
    j = pl.program_id(1)
    tm, tn = o_ref.shape

    @pl.when(j == 0)
    def _():
        _norm_mod_to(h_scr, x_ref, g_ref, sc_ref, sh_ref, sub)

    acc = jnp.dot(h_scr[...], w_ref[...], preferred_element_type=F32)
    cos = cos_ref[...]
    sin = sin_ref[...]
    heads_per_tile = tn // HEAD_DIM
    for l in range(heads_per_tile):
        t = acc[:, l * HEAD_DIM:(l + 1) * HEAD_DIM]
        head = j * heads_per_tile + l
        tr = t * cos + pltpu.roll(t, HEAD_DIM // 2, 1) * sin
        t = jnp.where(head < n_rope, tr, t)
        t = t * jnp.where(head < n_q, q_scale, 1.0)
        o_ref[:, l * HEAD_DIM:(l + 1) * HEAD_DIM] = t.astype(BF16)


def _qkv_proj(u, g, mods, w, cos_u, sin_u, *, layer, w_layer, tm, sub, n_q, n_kv):
    n_rows, d = u.shape
    n = w.shape[2]
    tn = n if n <= PROJ_TILE else PROJ_TILE
    assert n % tn == 0
    nsb = tm // sub
    kern = functools.partial(_qkv_kernel, sub=sub, n_q=n_q, n_rope=n_q + n_kv,
                             q_scale=HEAD_DIM ** -0.5)
    return pl.pallas_call(
        kern,
        grid=(n_rows // tm, n // tn),
        in_specs=[pl.BlockSpec((tm, d), lambda i, j: (i, 0)),
                  _layer_spec((1, d), layer, lambda i, j: (0, 0)),
                  _mod_spec(nsb, d, layer, MOD_SCALE1),
                  _mod_spec(nsb, d, layer, MOD_SHIFT1),
                  _layer_spec((d, tn), w_layer, lambda i, j: (0, j)),
                  pl.BlockSpec((tm, HEAD_DIM), lambda i, j: (i, 0)),
                  pl.BlockSpec((tm, HEAD_DIM), lambda i, j: (i, 0))],
        out_specs=pl.BlockSpec((tm, tn), lambda i, j: (i, j)),
        out_shape=jax.ShapeDtypeStruct((n_rows, n), BF16),
        scratch_shapes=[pltpu.VMEM((tm, d), BF16)],
        compiler_params=_params(("parallel", "arbitrary")),
        name="attn_qkv",
    )(u, g, mods, mods, w, cos_u, sin_u)


def _softmax_pv(s, sink, v):
    m = jnp.maximum(jnp.max(s, axis=-1, keepdims=True), sink)
    p = jnp.exp(s - m)
    l = jnp.sum(p, axis=-1, keepdims=True) + jnp.exp(sink - m)
    o = jnp.dot(p.astype(BF16), v, preferred_element_type=F32)
    return o * (1.0 / l)


def _stack_heads(q, group):
    return jnp.concatenate([q[:, g * HEAD_DIM:(g + 1) * HEAD_DIM] for g in range(group)], axis=0)


def _attn_x_kernel(q_ref, kp_ref, ko_ref, kn_ref, kc_ref, vp_ref, vo_ref, vn_ref, vc_ref,
                   mask_ref, sink_ref, o_ref, *, group, n_kv):
    blk = q_ref.shape[0]
    valid = mask_ref[...] > 0.0

    def scores(h):
        kv = slice(h * HEAD_DIM, (h + 1) * HEAD_DIM)
        qs = _stack_heads(q_ref[:, h * group * HEAD_DIM:(h + 1) * group * HEAD_DIM], group)
        k = jnp.concatenate([kp_ref[:, kv], ko_ref[:, kv], kn_ref[:, kv], kc_ref[:, kv]], axis=0)
        s = lax.dot_general(qs, k, (((1,), (1,)), ((), ())), preferred_element_type=F32)
        return jnp.concatenate([jnp.where(valid, s[:, :3 * blk], NEG_INF), s[:, 3 * blk:]], axis=1)

    def finish(h, s):
        kv = slice(h * HEAD_DIM, (h + 1) * HEAD_DIM)
        v = jnp.concatenate([vp_ref[:, kv], vo_ref[:, kv], vn_ref[:, kv], vc_ref[:, kv]], axis=0)
        o = _softmax_pv(s, sink_ref[h], v)
        for g in range(group):
            c0 = (h * group + g) * HEAD_DIM
            o_ref[:, c0:c0 + HEAD_DIM] = o[g * blk:(g + 1) * blk, :].astype(BF16)

    s_next = scores(0)
    for h in range(n_kv):
        s_cur = s_next
        if h + 1 < n_kv:
            s_next = scores(h + 1)
        finish(h, s_cur)


def _attn_c_kernel(q_ref, kc_ref, vc_ref, sink_ref, prev_ref, o_ref, *, group, n_kv):
    del prev_ref
    lc = q_ref.shape[0]

    def scores(h):
        qs = _stack_heads(q_ref[:, h * group * HEAD_DIM:(h + 1) * group * HEAD_DIM], group)
        return lax.dot_general(qs, kc_ref[:, h * HEAD_DIM:(h + 1) * HEAD_DIM], (((1,), (1,)), ((), ())),
                               preferred_element_type=F32)

    s_next = scores(0)
    for h in range(n_kv):
        s_cur = s_next
        if h + 1 < n_kv:
            s_next = scores(h + 1)
        o = _softmax_pv(s_cur, sink_ref[h], vc_ref[:, h * HEAD_DIM:(h + 1) * HEAD_DIM])
        for g in range(group):
            c0 = (h * group + g) * HEAD_DIM
            o_ref[:, c0:c0 + HEAD_DIM] = o[g * lc:(g + 1) * lc, :].astype(BF16)


def _window_mask(group, blk, n_blocks):
    qi = np.arange(group * blk)[:, None] % blk
    kj = np.arange(3 * blk)[None, :]
    band = (kj - qi >= 0) & (kj - qi <= 2 * WINDOW)
    first = band & (kj >= blk)
    last = band & (kj < 2 * blk)
    kinds = [first & last if n_blocks == 1 else first, band, last]
    return np.stack(kinds).astype(np.float32)


def _attention(qkv, sink, *, batch, seq, ctx, n_q, n_kv):
    n_rows = qkv.shape[0]
    group = n_q // n_kv
    blk = WINDOW
    p_rows = ctx + seq
    bpb = p_rows // blk
    cb = ctx // blk
    nb = seq // blk
    last_blk = n_rows // blk - 1
    qw = n_q * HEAD_DIM
    kw = n_kv * HEAD_DIM
    assert qw % kw == 0
    k_col = qw // kw
    spc = p_rows // ctx
    sink_g = sink.reshape(n_kv, group, 1).astype(F32)
    sink_x = jnp.broadcast_to(sink_g[:, :, None, :], (n_kv, group, blk, 1)).reshape(n_kv, group * blk, 1)
    sink_c = jnp.broadcast_to(sink_g[:, :, None, :], (n_kv, group, ctx, 1)).reshape(n_kv, group * ctx, 1)
    mask = jnp.asarray(_window_mask(group, blk, nb))

    def kv_specs(col):
        return [pl.BlockSpec((blk, kw), lambda b, n: (jnp.maximum(b * bpb + cb + n - 1, 0), col)),
                pl.BlockSpec((blk, kw), lambda b, n: (b * bpb + cb + n, col)),
                pl.BlockSpec((blk, kw), lambda b, n: (jnp.minimum(b * bpb + cb + n + 1, last_blk), col)),
                pl.BlockSpec((ctx, kw), lambda b, n: (b * spc, col))]

    out_x = pl.pallas_call(
        functools.partial(_attn_x_kernel, group=group, n_kv=n_kv),
        grid=(batch, nb),
        in_specs=[pl.BlockSpec((blk, qw), lambda b, n: (b * bpb + cb + n, 0))]
        + kv_specs(k_col) + kv_specs(k_col + 1)
        + [pl.BlockSpec((None, group * blk, 3 * blk),
                        lambda b, n: (jnp.where(n == 0, 0, jnp.where(n == nb - 1, 2, 1)), 0, 0)),
           pl.BlockSpec((n_kv, group * blk, 1), lambda b, n: (0, 0, 0))],
        out_specs=pl.BlockSpec((blk, qw), lambda b, n: (b * bpb + cb + n, 0)),
        out_shape=jax.ShapeDtypeStruct((n_rows, qw), BF16),
        compiler_params=_params(("parallel", "parallel")),
        name="attn_latent",
    )(qkv, qkv, qkv, qkv, qkv, qkv, qkv, qkv, qkv, mask, sink_x)

    out = pl.pallas_call(
        functools.partial(_attn_c_kernel, group=group, n_kv=n_kv),
        grid=(batch,),
        in_specs=[pl.BlockSpec((ctx, qw), lambda b: (b * spc, 0)),
                  pl.BlockSpec((ctx, kw), lambda b: (b * spc, k_col)),
                  pl.BlockSpec((ctx, kw), lambda b: (b * spc, k_col + 1)),
                  pl.BlockSpec((n_kv, group * ctx, 1), lambda b: (0, 0, 0)),
                  pl.BlockSpec(memory_space=pl.ANY)],
        out_specs=pl.BlockSpec((ctx, qw), lambda b: (b * spc, 0)),
        out_shape=jax.ShapeDtypeStruct((n_rows, qw), BF16),
        input_output_aliases={4: 0},
        compiler_params=_params(("parallel",)),
        name="attn_context",
    )(qkv, qkv, qkv, sink_c, out_x)
    return out


def _proj_res_kernel(y_ref, w_ref, x_ref, gt_ref, g_ref, sc_ref, sh_ref, o_ref, h_ref, *, sub):
    acc = jnp.dot(y_ref[...], w_ref[...], preferred_element_type=F32)
    g = g_ref[...]
    for s in range(o_ref.shape[0] // sub):
        rows = slice(s * sub, (s + 1) * sub)
        xn = x_ref[rows, :] + gt_ref[s] * acc[rows, :]
        o_ref[rows, :] = xn
        h_ref[rows, :] = _norm_mod(xn, g, sc_ref[s], sh_ref[s]).astype(BF16)


def _proj_residual(y, w, u, mods, g, *, layer, w_layer, tm, sub, latent_only=None):
    n_rows, d = u.shape
    k = y.shape[1]
    out_rows, out_map = n_rows, lambda i: (i, 0)
    if latent_only is not None:
        spb, cs = latent_only
        n_lat = n_rows // sub // spb * (spb - cs)
        out_rows = n_lat * sub + tm
        tm = sub
        out_map = lambda i: (jnp.where(i % spb >= cs, (i // spb) * (spb - cs) + i % spb - cs, n_lat), 0)
    nsb = tm // sub
    return pl.pallas_call(
        functools.partial(_proj_res_kernel, sub=sub),
        grid=(n_rows // tm,),
        in_specs=[pl.BlockSpec((tm, k), lambda i: (i, 0)),
                  _layer_spec((k, d), w_layer, lambda i: (0, 0)),
                  pl.BlockSpec((tm, d), lambda i: (i, 0)),
                  _mod_spec(nsb, d, layer, MOD_GATE1),
                  _layer_spec((1, d), layer, lambda i: (0, 0)),
                  _mod_spec(nsb, d, layer, MOD_SCALE2),
                  _mod_spec(nsb, d, layer, MOD_SHIFT2)],
        out_specs=[pl.BlockSpec((tm, d), out_map),
                   pl.BlockSpec((tm, d), out_map)],
        out_shape=[jax.ShapeDtypeStruct((out_rows, d), F32),
                   jax.ShapeDtypeStruct((out_rows, d), BF16)],
        compiler_params=_params(("arbitrary",)),
        name="proj_residual",
    )(y, w, u, mods, g, mods, mods)


def _mlstm_in_kernel(x_ref, g_ref, sc_ref, sh_ref, w_ref, b_ref, wg_ref, bg_ref, z_ref, gate_ref,
                     h_scr, *, sub, group, k_scale):
    j = pl.program_id(1)
    tm, tn = z_ref.shape

    @pl.when(j == 0)
    def _():
        _norm_mod_to(h_scr, x_ref, g_ref, sc_ref, sh_ref, sub)
        gate_ref[...] = jnp.dot(h_scr[...], wg_ref[...], preferred_element_type=F32) + bg_ref[...]

    for l in range(tn // group):
        cols = slice(l * group, (l + 1) * group)
        acc = jnp.dot(h_scr[...], w_ref[:, cols], preferred_element_type=F32) + b_ref[:, cols]
        is_k = j * (tn // group) + l == 1
        z_ref[:, cols] = (acc * jnp.where(is_k, k_scale, 1.0)).astype(BF16)


def _mlstm_in_proj(u, g, mods, w, b, wg, bg, *, layer, w_layer, tm, sub, dqk):
    n_rows, d = u.shape
    n = 3 * d
    qk_cols = d // 2
    tn = n // 2 if n // 2 <= PROJ_TILE and (n // 2) % qk_cols == 0 else qk_cols
    nsb = tm // sub
    kern = functools.partial(_mlstm_in_kernel, sub=sub, group=qk_cols, k_scale=dqk ** -0.5)
    return pl.pallas_call(
        kern,
        grid=(n_rows // tm, n // tn),
        in_specs=[pl.BlockSpec((tm, d), lambda i, j: (i, 0)),
                  _layer_spec((1, d), layer, lambda i, j: (0, 0)),
                  _mod_spec(nsb, d, layer, MOD_SCALE1),
                  _mod_spec(nsb, d, layer, MOD_SHIFT1),
                  _layer_spec((d, tn), w_layer, lambda i, j: (0, j)),
                  _layer_spec((1, tn), w_layer, lambda i, j: (0, j)),
                  pl.BlockSpec((d, LANE), lambda i, j: (0, 0)),
                  pl.BlockSpec((1, LANE), lambda i, j: (0, 0))],
        out_specs=[pl.BlockSpec((tm, tn), lambda i, j: (i, j)),
                   pl.BlockSpec((tm, LANE), lambda i, j: (i, 0))],
        out_shape=[jax.ShapeDtypeStruct((n_rows, n), BF16),
                   jax.ShapeDtypeStruct((n_rows, LANE), F32)],
        scratch_shapes=[pltpu.VMEM((tm, d), BF16)],
        compiler_params=_params(("parallel", "arbitrary")),
        name="mlstm_in",
    )(u, g, mods, mods, w, b, wg, bg)


GATE_F_LANE = 64
N_GATE_OUT = 6
N_COL = 5


def _chunk_scan(x, scr, pos, op, ident, reverse):
    r = x.shape[0]
    sh = 1
    while sh < CHUNK:
        scr[CHUNK:CHUNK + r, :] = x
        if reverse:
            other = scr[CHUNK + sh:CHUNK + sh + r, :]
            ok = pos < CHUNK - sh
        else:
            other = scr[CHUNK - sh:CHUNK - sh + r, :]
            ok = pos >= sh
        x = op(x, jnp.where(ok, other, ident))
        sh *= 2
    return x


def _mlstm_gate_kernel(g_ref, o_ref, b_scr, cm_scr, shift_scr, *, n_heads, n_ctx_chunks, tile):
    p_rows = g_ref.shape[0]
    n_chunks = p_rows // CHUNK
    shift_scr[...] = jnp.zeros_like(shift_scr)
    o_ref[0:N_COL] = jnp.zeros((N_COL,) + o_ref.shape[1:], F32)
    fwd_tile = lax.broadcasted_iota(jnp.int32, (tile, LANE), 1) < n_heads
    pos = lax.broadcasted_iota(jnp.int32, (tile, LANE), 0) % CHUNK

    def local(t, _):
        rows = pl.ds(pl.multiple_of(t * tile, tile), tile)
        g = g_ref[rows, :]
        lf = _log_sigmoid(pltpu.roll(g, LANE - GATE_F_LANE, 1))
        b = jnp.where(fwd_tile, _chunk_scan(lf, shift_scr, pos, jnp.add, 0.0, False),
                      _chunk_scan(lf, shift_scr, pos, jnp.add, 0.0, True))
        a = g - b
        cm = jnp.where(fwd_tile, _chunk_scan(a, shift_scr, pos, jnp.maximum, -jnp.inf, False),
                       _chunk_scan(a, shift_scr, pos, jnp.maximum, -jnp.inf, True))
        b_scr[rows, :] = b
        cm_scr[rows, :] = cm
        o_ref[N_COL, rows, :] = a
        return 0

    lax.fori_loop(0, p_rows // tile, local, 0)

    fwd_chunk = lax.broadcasted_iota(jnp.int32, (CHUNK, LANE), 1) < n_heads

    def step(c, m, forward):
        rows = pl.ds(pl.multiple_of(c * CHUNK, CHUNK), CHUNK)
        end_row = CHUNK - 1 if forward else 0
        mine = fwd_chunk if forward else jnp.logical_not(fwd_chunk)
        b = b_scr[rows, :]
        a = o_ref[N_COL, rows, :]
        cm = cm_scr[rows, :]
        b_end = b[end_row:end_row + 1, :]
        m_t = b + jnp.maximum(m, cm)
        m_new = jnp.maximum(b_end + m, b_end + cm[end_row:end_row + 1, :])
        vals = (b - m_t, jnp.exp(b + m - m_t), jnp.exp(-m_t), jnp.exp(b_end + a - m_new),
                jnp.broadcast_to(jnp.exp(b_end + m - m_new), (CHUNK, LANE)))
        for q, val in enumerate(vals):
            o_ref[q, rows, :] = jnp.where(mine, val, o_ref[q, rows, :])
        return m_new

    def body(j, carry):
        mf, mb = carry
        mf = step(j, mf, True)
        cb = jnp.where(j < n_ctx_chunks, n_ctx_chunks - 1 - j, n_chunks - 1 + n_ctx_chunks - j)
        mb = step(cb, mb, False)
        return mf, mb

    zm = jnp.zeros((1, LANE), F32)
    lax.fori_loop(0, n_chunks, body, (zm, zm))


def _mlstm_gates(gates, *, batch, p_rows, ctx, n_heads):
    n_rows = gates.shape[0]
    tile = ctx
    kern = functools.partial(_mlstm_gate_kernel, n_heads=n_heads, n_ctx_chunks=ctx // CHUNK, tile=tile)
    return pl.pallas_call(
        kern,
        grid=(batch,),
        in_specs=[pl.BlockSpec((p_rows, LANE), lambda b: (b, 0))],
        out_specs=pl.BlockSpec((N_GATE_OUT, p_rows, LANE), lambda b: (0, b, 0)),
        out_shape=jax.ShapeDtypeStruct((N_GATE_OUT, n_rows, LANE), F32),
        scratch_shapes=[pltpu.VMEM((p_rows, LANE), F32), pltpu.VMEM((p_rows, LANE), F32),
                        pltpu.VMEM((tile + 2 * CHUNK, LANE), F32)],
        compiler_params=_params(("parallel",)),
        name="mlstm_gates",
    )(gates)


def _mlstm_chunk(q, k, v, col, a_row, c_scr, n, d, reverse):
    ch = q.shape[0]
    t_idx = lax.broadcasted_iota(jnp.int32, (ch, ch), 0)
    s_idx = lax.broadcasted_iota(jnp.int32, (ch, ch), 1)
    seen = (s_idx >= t_idx) if reverse else (s_idx <= t_idx)
    c0 = d * N_COL
    row_term = col[:, c0:c0 + 1]
    e_inter = col[:, c0 + 1:c0 + 2]
    floor = col[:, c0 + 2:c0 + 3]
    wg_col = col[:, c0 + 3:c0 + 4]
    e_prev = col[0:1, c0 + 4:c0 + 5]

    w = jnp.where(seen, jnp.exp(row_term + a_row), 0.0)
    s = lax.dot_general(q, k, (((1,), (1,)), ((), ())), preferred_element_type=F32) * w
    c_old = c_scr[...]
    num = (e_inter * jnp.dot(q, c_old.astype(BF16), preferred_element_type=F32)
           + jnp.dot(s.astype(BF16), v, preferred_element_type=F32))
    qn = jnp.sum(q.astype(F32) * n, axis=1, keepdims=True)
    den = e_inter * qn + jnp.sum(s, axis=1, keepdims=True)
    h = num * (1.0 / jnp.maximum(jnp.abs(den), floor))

    vw = (wg_col * v.astype(F32)).astype(BF16)
    c_scr[...] = e_prev * c_old + lax.dot_general(k, vw, (((0,), (0,)), ((), ())),
                                                  preferred_element_type=F32)
    n_new = e_prev * n + jnp.sum(wg_col * k.astype(F32), axis=0, keepdims=True)
    return h, n_new


def _mlstm_core_kernel(q_ref, k_ref, v_ref, o_ref, gr_ref, gc_ref, gh_ref, y_ref,
                       hf_scr, hb_scr, cf_scr, cb_scr, *, n_ctx_chunks, out_rows):
    n_chunks = gr_ref.shape[0]
    dqk = q_ref.shape[1]
    cf_scr[...] = jnp.zeros_like(cf_scr)
    cb_scr[...] = jnp.zeros_like(cb_scr)

    def run(c, c_scr, h_scr, n, d, reverse):
        rows = pl.ds(pl.multiple_of(c * CHUNK, CHUNK), CHUNK)
        h, n = _mlstm_chunk(q_ref[rows, :], k_ref[rows, :], v_ref[rows, :], gc_ref[rows, :],
                            gr_ref[c][d:d + 1, :], c_scr, n, d, reverse)
        h_scr[rows, :] = h
        return n

    def body(j, carry):
        nf, nb = carry
        nf = run(j, cf_scr, hf_scr, nf, 0, False)
        cb = jnp.where(j < n_ctx_chunks, n_ctx_chunks - 1 - j, n_chunks - 1 + n_ctx_chunks - j)
        nb = run(cb, cb_scr, hb_scr, nb, 1, True)
        return nf, nb

    zn = jnp.zeros((1, dqk), F32)
    lax.fori_loop(0, n_chunks, body, (zn, zn), unroll=2)

    gh = gh_ref[...]

    def finish(r, _):
        rows = pl.ds(pl.multiple_of(r * out_rows, out_rows), out_rows)
        h = hf_scr[rows, :] + hb_scr[rows, :]
        hn = h * lax.rsqrt(jnp.mean(h * h, axis=-1, keepdims=True) + RMS_EPS)
        y_ref[rows, :] = (_sigmoid(o_ref[rows, :].astype(F32)) * hn * gh).astype(BF16)
        return 0

    lax.fori_loop(0, y_ref.shape[0] // out_rows, finish, 0)


def _mlstm_core(z, gates_r, gates_c, g_head, *, batch, p_rows, ctx, n_heads, dqk, dv):
    n_rows = z.shape[0]
    n_chunks = p_rows // CHUNK
    kern = functools.partial(_mlstm_core_kernel, n_ctx_chunks=ctx // CHUNK, out_rows=ctx)
    return pl.pallas_call(
        kern,
        grid=(batch, n_heads),
        in_specs=[pl.BlockSpec((p_rows, dqk), lambda b, h: (b, h)),
                  pl.BlockSpec((p_rows, dqk), lambda b, h: (b, n_heads + h)),
                  pl.BlockSpec((p_rows, dv), lambda b, h: (b, n_heads + h)),
                  pl.BlockSpec((p_rows, dv), lambda b, h: (b, 2 * n_heads + h)),
                  pl.BlockSpec((None, None, n_chunks, 2, CHUNK), lambda b, h: (b, h, 0, 0, 0)),
                  pl.BlockSpec((None, None, p_rows, 2 * N_COL), lambda b, h: (b, h, 0, 0)),
                  pl.BlockSpec((1, dv), lambda b, h: (0, h))],
        out_specs=pl.BlockSpec((p_rows, dv), lambda b, h: (b, h)),
        out_shape=jax.ShapeDtypeStruct((n_rows, n_heads * dv), BF16),
        scratch_shapes=[pltpu.VMEM((p_rows, dv), F32), pltpu.VMEM((p_rows, dv), F32),
                        pltpu.VMEM((dqk, dv), F32), pltpu.VMEM((dqk, dv), F32)],
        compiler_params=_params(("parallel", "parallel")),
        name="mlstm_core",
    )(z, z, z, z, gates_r, gates_c, g_head)


def _gate_layouts(og, *, batch, p_rows, n_heads):
    g = og[:, :, :2 * n_heads].reshape(N_GATE_OUT, batch, p_rows, 2, n_heads)
    cols = g[:N_COL].transpose(1, 4, 2, 3, 0).reshape(batch, n_heads, p_rows, 2 * N_COL)
    rows = g[N_COL].reshape(batch, p_rows // CHUNK, CHUNK, 2, n_heads).transpose(0, 4, 1, 3, 2)
    return rows, cols


def _halo_up_kernel(h_ref, w_ref, o_ref):
    o_ref[...] = jnp.dot(h_ref[...], w_ref[...], preferred_element_type=F32)


def _halo_up(h, w_up, *, layer, n_rows, tm):
    d = h.shape[1]
    n = w_up.shape[2]
    nt = n_rows // tm
    ht = h.reshape(h.shape[0] // tm, tm, d)
    first, last = ht[:nt, 0], ht[:nt, tm - 1]
    prev = jnp.concatenate([last[:1], last[:nt - 1]], axis=0)
    nxt = jnp.concatenate([first[1:], first[nt - 1:]], axis=0)
    hh = jnp.stack([prev, nxt], axis=1).reshape(2 * nt, d)
    rows = 2 * nt
    tn = 2 * FFN_CHUNK
    uh = pl.pallas_call(
        _halo_up_kernel,
        grid=(n // tn,),
        in_specs=[pl.BlockSpec((rows, d), lambda j: (0, 0)),
                  _layer_spec((d, tn), layer, lambda j: (0, j))],
        out_specs=pl.BlockSpec((rows, tn), lambda j: (0, j)),
        out_shape=jax.ShapeDtypeStruct((rows, n), F32),
        compiler_params=_params(("parallel",)),
        name="ffn_halo",
    )(hh, w_up)
    return uh.reshape(nt, 2, n)


def _ffn_kernel(h_ref, x_ref, gt_ref, wa0_ref, wb0_ref, wa1_ref, wb1_ref,
                ha0_ref, hb0_ref, ha1_ref, hb1_ref, cwa_ref, cwb_ref, cba_ref, cbb_ref, wd_ref,
                gf_ref, o_ref, acc_scr, u0_scr, u1_scr, a0_scr, a1_scr,
                *, sub, p_rows, ctx, final_norm):
    i = pl.program_id(0)
    j = pl.program_id(1)
    tm = o_ref.shape[0]
    nsb = tm // sub
    fh = wa0_ref.shape[1]
    lt = fh // LANE
    seg = sub + 2 * HALO

    @pl.when(j == 0)
    def _():
        acc_scr[...] = jnp.zeros_like(acc_scr)

    def edge_open(k):
        pos = (i * tm + k * sub) % p_rows
        return jnp.where((pos == 0) | (pos == ctx), 0.0, 1.0)

    opens = [edge_open(k) for k in range(nsb + 1)]

    def up(w_refs, halo_refs, u_scr):
        for br, (w_ref, halo_ref) in enumerate(zip(w_refs, halo_refs)):
            ue = jnp.dot(h_ref[...], w_ref[...], preferred_element_type=F32)
            halo = halo_ref[...]
            for s in range(nsb):
                r0 = s * sub
                b0 = s * seg
                for t in range(lt):
                    lanes = slice(t * LANE, (t + 1) * LANE)
                    tt = br * lt + t
                    before = (jnp.broadcast_to(halo[0:1, lanes], (HALO, LANE)) if s == 0
                              else ue[r0 - HALO:r0, lanes])
                    after = (jnp.broadcast_to(halo[1:2, lanes], (HALO, LANE)) if s == nsb - 1
                             else ue[r0 + sub:r0 + sub + HALO, lanes])
                    u_scr[tt, b0:b0 + HALO, :] = before * opens[s]
                    u_scr[tt, b0 + HALO:b0 + HALO + sub, :] = ue[r0:r0 + sub, lanes]
                    u_scr[tt, b0 + HALO + sub:b0 + seg, :] = after * opens[s + 1]

    def gate(c, u_scr, act_scr):
        cols = slice(c * fh, (c + 1) * fh)
        cws = (cwa_ref[:, cols], cwb_ref[:, cols])
        cbs = (cba_ref[:, cols], cbb_ref[:, cols])

        def conv_piece(br, t, r0):
            lanes = slice(t * LANE, (t + 1) * LANE)
            tt = br * lt + t + jnp.minimum(j, 0)
            cw, cb = cws[br], cbs[br]
            cur = u_scr[tt, r0:r0 + GATE_ROWS, :]
            prv = u_scr[tt, r0 - 1:r0 - 1 + GATE_ROWS, :]
            nxt = u_scr[tt, r0 + 1:r0 + 1 + GATE_ROWS, :]
            return cb[:, lanes] + prv * cw[0:1, lanes] + cur * cw[1:2, lanes] + nxt * cw[2:3, lanes]

        for s in range(nsb):
            for r in range(sub // GATE_ROWS):
                r0 = s * seg + HALO + r * GATE_ROWS
                o0 = s * sub + r * GATE_ROWS
                for t in range(lt):
                    a = conv_piece(0, t, r0)
                    b = conv_piece(1, t, r0)
                    act_scr[o0:o0 + GATE_ROWS, t * LANE:(t + 1) * LANE] = (a * _sigmoid(a) * b).astype(BF16)

    def down(c, act_scr):
        acc_scr[...] += jnp.dot(act_scr[...], wd_ref[c * fh:(c + 1) * fh, :], preferred_element_type=F32)

    up((wa0_ref, wb0_ref), (ha0_ref, hb0_ref), u0_scr)
    up((wa1_ref, wb1_ref), (ha1_ref, hb1_ref), u1_scr)
    gate(0, u0_scr, a0_scr)
    down(0, a0_scr)
    gate(1, u1_scr, a1_scr)
    down(1, a1_scr)

    @pl.when(j == pl.num_programs(1) - 1)
    def _():
        for s in range(nsb):
            if not final_norm:
                rows = slice(s * sub, (s + 1) * sub)
                o_ref[rows, :] = x_ref[rows, :] + gt_ref[s] * acc_scr[rows, :]
                continue
            for r in range(s * sub, (s + 1) * sub, NORM_ROWS):
                rows = slice(r, r + NORM_ROWS)
                xn = x_ref[rows, :] + gt_ref[s] * acc_scr[rows, :]
                ms = jnp.mean(xn * xn, axis=-1, keepdims=True)
                o_ref[rows, :] = xn * lax.rsqrt(ms + RMS_EPS) * gf_ref[...]


def _conv_ffn(h, u, mods, w_up, conv_w, conv_b, w_down, g_final, *, layer, mod_layer, n_rows, tm, sub,
              p_rows, ctx, final_norm):
    d = u.shape[1]
    d_ff = w_down.shape[1]
    fc = FFN_CHUNK
    assert d_ff % fc == 0
    fh = fc // 2
    nfc = d_ff // fc
    nsb = tm // sub
    kern = functools.partial(_ffn_kernel, sub=sub, p_rows=p_rows, ctx=ctx, final_norm=final_norm)
    u_shape = (fc // LANE, nsb * (sub + 2 * HALO), LANE)
    halo = _halo_up(h, w_up, layer=layer, n_rows=n_rows, tm=tm)

    def half_specs(shape, lead):
        return [pl.BlockSpec(shape, lambda i, j, c=c: lead(i) + (c + 2 * j,))
                for c in (0, 2 * nfc, 1, 2 * nfc + 1)]

    return pl.pallas_call(
        kern,
        grid=(n_rows // tm, nfc),
        in_specs=[pl.BlockSpec((tm, d), lambda i, j: (i, 0)),
                  pl.BlockSpec((tm, d), lambda i, j: (i, 0)),
                  _mod_spec(nsb, d, mod_layer, MOD_GATE2)]
        + half_specs((None, d, fh), lambda i: (layer, 0))
        + half_specs((None, 2, fh), lambda i: (i, 0))
        + [_layer_spec((CONV_W, fc), layer, lambda i, j: (0, j)),
           _layer_spec((CONV_W, fc), layer, lambda i, j: (0, nfc + j)),
           _layer_spec((1, fc), layer, lambda i, j: (0, j)),
           _layer_spec((1, fc), layer, lambda i, j: (0, nfc + j)),
           _layer_spec((fc, d), layer, lambda i, j: (j, 0)),
           pl.BlockSpec((1, d), lambda i, j: (0, 0))],
        out_specs=pl.BlockSpec((tm, d), lambda i, j: (i, 0)),
        out_shape=jax.ShapeDtypeStruct((n_rows, d), F32),
        scratch_shapes=[pltpu.VMEM((tm, d), F32), pltpu.VMEM(u_shape, F32), pltpu.VMEM(u_shape, F32),
                        pltpu.VMEM((tm, fh), BF16), pltpu.VMEM((tm, fh), BF16)],
        compiler_params=_params(("parallel", "arbitrary")),
        name="conv_ffn",
    )(h, u, mods, w_up, w_up, w_up, w_up, halo, halo, halo, halo,
      conv_w, conv_w, conv_b, conv_b, w_down, g_final)


def _rope_tables(batch, seq, ctx):
    rows = seq // GRID_W
    row = jnp.repeat(jnp.arange(rows, dtype=F32), GRID_W)
    col = jnp.tile(jnp.arange(GRID_W, dtype=F32), rows)
    n_freq = HEAD_DIM // 4
    inv_freq = ROPE_THETA ** (-jnp.arange(n_freq, dtype=F32) / n_freq)
    ang = jnp.concatenate([row[:, None] * inv_freq, col[:, None] * inv_freq], axis=-1)
    ang = jnp.concatenate([ang, ang], axis=-1)
    sign = jnp.concatenate([-jnp.ones((HEAD_DIM // 2,), F32), jnp.ones((HEAD_DIM // 2,), F32)])
    cos = jnp.concatenate([jnp.ones((ctx, HEAD_DIM), F32), jnp.cos(ang)], axis=0)
    sin = jnp.concatenate([jnp.zeros((ctx, HEAD_DIM), F32), jnp.sin(ang) * sign], axis=0)
    return jnp.tile(cos, (batch, 1)), jnp.tile(sin, (batch, 1))


def kernel(x, c, ctx, c_ctx, w_mod, b_mod, g_mix, g_ffn, attn_w_qkv, attn_sink, attn_w_o,
           mlstm_w_in, mlstm_b_in, mlstm_g_head, mlstm_w_o, ffn_w_up, ffn_conv_w, ffn_conv_b,
           ffn_w_down, g_final):
    batch, seq, d = x.shape
    lc = ctx.shape[1]
    depth = w_mod.shape[0]
    p_rows = lc + seq
    sub = lc
    assert seq % sub == 0 and sub % WINDOW == 0 and sub % CHUNK == 0 and seq % GRID_W == 0
    n_sub = batch * p_rows // sub
    tm = 2 * sub if n_sub % 2 == 0 else sub
    n_q = attn_sink.shape[1]
    n_kv = (attn_w_qkv.shape[2] // HEAD_DIM - n_q) // 2
    n_heads = (mlstm_w_in.shape[2] - 3 * d) // 4
    dv = d // n_heads
    dqk = dv // 2
    main_cols = 3 * d

    u = jnp.concatenate([ctx, x], axis=1).reshape(batch * p_rows, d)

    n_c = batch + 1
    c_rows = -(-n_c // 8) * 8
    cvec = jnp.concatenate([c, c_ctx[None, :], jnp.zeros((c_rows - n_c, d), F32)], axis=0)
    mods = _modulation(cvec, w_mod, b_mod)
    mods_sb = jnp.concatenate(
        [jnp.broadcast_to(mods[:, None, batch:batch + 1], (depth, batch, lc // sub, 6 * d)),
         jnp.broadcast_to(mods[:, :batch, None], (depth, batch, seq // sub, 6 * d))],
        axis=2).reshape(depth, n_sub, 6, 1, d)

    mods_lat = jnp.broadcast_to(mods[depth - 1:, :batch, None], (1, batch, seq // sub, 6 * d)
                                ).reshape(1, batch * seq // sub, 6, 1, d)

    cos_u, sin_u = _rope_tables(batch, seq, lc)

    w_qkv = attn_w_qkv.astype(BF16)
    w_attn_o = attn_w_o.astype(BF16)
    w_in = mlstm_w_in.astype(BF16)
    w_mlstm_o = mlstm_w_o.astype(BF16)
    w_up = ffn_w_up.astype(BF16)
    w_down = ffn_w_down.astype(BF16)
    g_mix3 = g_mix.reshape(depth, 1, d)
    g_ffn3 = g_ffn.reshape(depth, 1, d)
    b_in3 = mlstm_b_in.reshape(mlstm_b_in.shape[0], 1, -1)
    conv_b3 = ffn_conv_b.reshape(depth, 1, -1)
    g_fin = g_final.reshape(1, d)

    for i in range(depth):
        jm = i // 2
        if i % 2 == 0:
            qkv = _qkv_proj(u, g_mix3, mods_sb, w_qkv, cos_u, sin_u, layer=i, w_layer=jm,
                            tm=tm, sub=sub, n_q=n_q, n_kv=n_kv)
            y = _attention(qkv, attn_sink[jm], batch=batch, seq=seq, ctx=lc, n_q=n_q, n_kv=n_kv)
            w_o = w_attn_o
        else:
            wgt = mlstm_w_in[jm][:, main_cols:].reshape(d, 4, n_heads)
            bgt = mlstm_b_in[jm][main_cols:].reshape(4, n_heads)
            lane_pad = GATE_F_LANE - 2 * n_heads
            wg = jnp.concatenate([wgt[:, 0], wgt[:, 2], jnp.zeros((d, lane_pad), F32),
                                  wgt[:, 1], wgt[:, 3], jnp.zeros((d, lane_pad), F32)], axis=1).astype(BF16)
            bg = jnp.concatenate([bgt[0], bgt[2], jnp.zeros((lane_pad,), F32),
                                  bgt[1], bgt[3], jnp.zeros((lane_pad,), F32)]).reshape(1, LANE)
            z, gates = _mlstm_in_proj(u, g_mix3, mods_sb, w_in, b_in3, wg, bg, layer=i, w_layer=jm,
                                      tm=tm, sub=sub, dqk=dqk)
            og = _mlstm_gates(gates, batch=batch, p_rows=p_rows, ctx=lc, n_heads=n_heads)
            gates_r, gates_c = _gate_layouts(og, batch=batch, p_rows=p_rows, n_heads=n_heads)
            y = _mlstm_core(z, gates_r, gates_c, mlstm_g_head[jm].reshape(1, n_heads * dv),
                            batch=batch, p_rows=p_rows, ctx=lc, n_heads=n_heads, dqk=dqk, dv=dv)
            w_o = w_mlstm_o
        if i < depth - 1:
            u, h2 = _proj_residual(y, w_o, u, mods_sb, g_ffn3, layer=i, w_layer=jm, tm=tm, sub=sub)
            u = _conv_ffn(h2, u, mods_sb, w_up, ffn_conv_w, conv_b3, w_down, g_fin, layer=i, mod_layer=i,
                          n_rows=batch * p_rows, tm=tm, sub=sub, p_rows=p_rows, ctx=lc,
                          final_norm=False)
        else:
            u, h2 = _proj_residual(y, w_o, u, mods_sb, g_ffn3, layer=i, w_layer=jm, tm=tm, sub=sub,
                                   latent_only=(p_rows // sub, lc // sub))
            u = _conv_ffn(h2, u, mods_lat, w_up, ffn_conv_w, conv_b3, w_down, g_fin, layer=i, mod_layer=0,
                          n_rows=batch * seq, tm=tm, sub=sub, p_rows=seq, ctx=0,
                          final_norm=True)
    return u.reshape(batch, seq, d)
```

```python
import functools

import numpy as np
import jax
import jax.numpy as jnp
from jax import lax
from jax.experimental import pallas as pl
from jax.experimental.pallas import tpu as pltpu

F32 = jnp.float32
BF16 = jnp.bfloat16

RMS_EPS = 1e-6
HEAD_DIM = 128
WINDOW = 128
GRID_W = 64
ROPE_THETA = 10000.0
NEG_INF = -1e30
CHUNK = 128
CONV_W = 3
LANE = 128
HALO = 8
BF16_ROWS = 16
FFN_CHUNK = 512
GATE_ROWS = 64
NORM_ROWS = 16
PROJ_TILE = 3072
MOD_TILE = 1024
MOD_SHIFT1, MOD_SCALE1, MOD_GATE1, MOD_SHIFT2, MOD_SCALE2, MOD_GATE2 = range(6)
V7X_VMEM_BYTES = 64 * 1024 * 1024
VMEM_LIMIT = V7X_VMEM_BYTES * 7 // 8


def _params(sem):
    return pltpu.CompilerParams(dimension_semantics=sem, vmem_limit_bytes=VMEM_LIMIT)


def _sigmoid(v):
    return 1.0 / (1.0 + jnp.exp(-v))


def _log_sigmoid(v):
    return jnp.minimum(v, 0.0) - jnp.log(1.0 + jnp.exp(-jnp.abs(v)))


def _mod_spec(nsb, d, layer, kind):
    return pl.BlockSpec((None, nsb, None, 1, d), lambda i, *_: (layer, i, kind, 0, 0))


def _layer_spec(shape, layer, index):
    return pl.BlockSpec((None,) + shape, lambda *g: (layer,) + index(*g))


def _norm_mod(xv, g, scale, shift):
    ms = jnp.mean(xv * xv, axis=-1, keepdims=True)
    y = xv * lax.rsqrt(ms + RMS_EPS)
    return (y * g) * (1.0 + scale) + shift


def _norm_mod_to(h_scr, x_ref, g_ref, sc_ref, sh_ref, sub):
    for s in range(x_ref.shape[0] // sub):
        gain = g_ref[...] * (1.0 + sc_ref[s])
        shift = sh_ref[s]
        for r in range(s * sub, (s + 1) * sub, NORM_ROWS):
            xv = x_ref[r:r + NORM_ROWS, :]
            ms = jnp.mean(xv * xv, axis=-1, keepdims=True)
            h_scr[r:r + NORM_ROWS, :] = (xv * lax.rsqrt(ms + RMS_EPS) * gain + shift).astype(BF16)


def _mod_kernel(c_ref, w_ref, b_ref, o_ref):
    cv = c_ref[...]
    s = (cv * _sigmoid(cv)).astype(BF16)
    o_ref[...] = jnp.dot(s, w_ref[...].astype(BF16), preferred_element_type=F32) + b_ref[...]


def _modulation(cvec, w_mod, b_mod):
    depth, d, n = w_mod.shape
    r = cvec.shape[0]
    tn = MOD_TILE if n % MOD_TILE == 0 else n
    return pl.pallas_call(
        _mod_kernel,
        grid=(depth, n // tn),
        in_specs=[pl.BlockSpec((r, d), lambda i, j: (0, 0)),
                  pl.BlockSpec((None, d, tn), lambda i, j: (i, 0, j)),
                  pl.BlockSpec((None, 1, tn), lambda i, j: (i, 0, j))],
        out_specs=pl.BlockSpec((None, r, tn), lambda i, j: (i, 0, j)),
        out_shape=jax.ShapeDtypeStruct((depth, r, n), F32),
        compiler_params=_params(("parallel", "parallel")),
        name="modulation",
    )(cvec, w_mod, b_mod.reshape(depth, 1, n))


def _qkv_kernel(x_ref, g_ref, sc_ref, sh_ref, w_ref, cos_ref, sin_ref, o_ref, h_scr,
                *, sub, n_q, n_rope, q_scale):
    j = pl.program_id(1)
    tm, tn = o_ref.shape

    @pl.when(j == 0)
    def _():
        _norm_mod_to(h_scr, x_ref, g_ref, sc_ref, sh_ref, sub)

    acc = jnp.dot(h_scr[...], w_ref[...], preferred_element_type=F32)
    cos = cos_ref[...]
    sin = sin_ref[...]
    heads_per_tile = tn // HEAD_DIM
    for l in range(heads_per_tile):
        t = acc[:, l * HEAD_DIM:(l + 1) * HEAD_DIM]
        head = j * heads_per_tile + l
        tr = t * cos + pltpu.roll(t, HEAD_DIM // 2, 1) * sin
        t = jnp.where(head < n_rope, tr, t)
        t = t * jnp.where(head < n_q, q_scale, 1.0)
        o_ref[:, l * HEAD_DIM:(l + 1) * HEAD_DIM] = t.astype(BF16)


def _qkv_proj(u, g, mods, w, cos_u, sin_u, *, layer, w_layer, tm, sub, n_q, n_kv):
    n_rows, d = u.shape
    n = w.shape[2]
    tn = n if n <= PROJ_TILE else PROJ_TILE
    assert n % tn == 0
    nsb = tm // sub
    kern = functools.partial(_qkv_kernel, sub=sub, n_q=n_q, n_rope=n_q + n_kv,
                             q_scale=HEAD_DIM ** -0.5)
    return pl.pallas_call(
        kern,
        grid=(n_rows // tm, n // tn),
        in_specs=[pl.BlockSpec((tm, d), lambda i, j: (i, 0)),
                  _layer_spec((1, d), layer, lambda i, j: (0, 0)),
                  _mod_spec(nsb, d, layer, MOD_SCALE1),
                  _mod_spec(nsb, d, layer, MOD_SHIFT1),
                  _layer_spec((d, tn), w_layer, lambda i, j: (0, j)),
                  pl.BlockSpec((tm, HEAD_DIM), lambda i, j: (i, 0)),
                  pl.BlockSpec((tm, HEAD_DIM), lambda i, j: (i, 0))],
        out_specs=pl.BlockSpec((tm, tn), lambda i, j: (i, j)),
        out_shape=jax.ShapeDtypeStruct((n_rows, n), BF16),
        scratch_shapes=[pltpu.VMEM((tm, d), BF16)],
        compiler_params=_params(("parallel", "arbitrary")),
        name="attn_qkv",
    )(u, g, mods, mods, w, cos_u, sin_u)


def _softmax_pv(s, sink, v):
    m = jnp.maximum(jnp.max(s, axis=-1, keepdims=True), sink)
    p = jnp.exp(s - m)
    l = jnp.sum(p, axis=-1, keepdims=True) + jnp.exp(sink - m)
    o = jnp.dot(p.astype(BF16), v, preferred_element_type=F32)
    return o * (1.0 / l)


def _stack_heads(q, group):
    return jnp.concatenate([q[:, g * HEAD_DIM:(g + 1) * HEAD_DIM] for g in range(group)], axis=0)


def _attn_x_kernel(q_ref, kp_ref, ko_ref, kn_ref, kc_ref, vp_ref, vo_ref, vn_ref, vc_ref,
                   mask_ref, sink_ref, o_ref, *, group, n_kv):
    blk = q_ref.shape[0]
    valid = mask_ref[...] > 0.0

    def scores(h):
        kv = slice(h * HEAD_DIM, (h + 1) * HEAD_DIM)
        qs = _stack_heads(q_ref[:, h * group * HEAD_DIM:(h + 1) * group * HEAD_DIM], group)
        k = jnp.concatenate([kp_ref[:, kv], ko_ref[:, kv], kn_ref[:, kv], kc_ref[:, kv]], axis=0)
        s = lax.dot_general(qs, k, (((1,), (1,)), ((), ())), preferred_element_type=F32)
        return jnp.concatenate([jnp.where(valid, s[:, :3 * blk], NEG_INF), s[:, 3 * blk:]], axis=1)

    def finish(h, s):
        kv = slice(h * HEAD_DIM, (h + 1) * HEAD_DIM)
        v = jnp.concatenate([vp_ref[:, kv], vo_ref[:, kv], vn_ref[:, kv], vc_ref[:, kv]], axis=0)
        o = _softmax_pv(s, sink_ref[h], v)
        for g in range(group):
            c0 = (h * group + g) * HEAD_DIM
            o_ref[:, c0:c0 + HEAD_DIM] = o[g * blk:(g + 1) * blk, :].astype(BF16)

    s_next = scores(0)
    for h in range(n_kv):
        s_cur = s_next
        if h + 1 < n_kv:
            s_next = scores(h + 1)
        finish(h, s_cur)


def _attn_c_kernel(q_ref, kc_ref, vc_ref, sink_ref, prev_ref, o_ref, *, group, n_kv):
    del prev_ref
    lc = q_ref.shape[0]

    def scores(h):
        qs = _stack_heads(q_ref[:, h * group * HEAD_DIM:(h + 1) * group * HEAD_DIM], group)
        return lax.dot_general(qs, kc_ref[:, h * HEAD_DIM:(h + 1) * HEAD_DIM], (((1,), (1,)), ((), ())),
                               preferred_element_type=F32)

    s_next = scores(0)
    for h in range(n_kv):
        s_cur = s_next
        if h + 1 < n_kv:
            s_next = scores(h + 1)
        o = _softmax_pv(s_cur, sink_ref[h], vc_ref[:, h * HEAD_DIM:(h + 1) * HEAD_DIM])
        for g in range(group):
            c0 = (h * group + g) * HEAD_DIM
            o_ref[:, c0:c0 + HEAD_DIM] = o[g * lc:(g + 1) * lc, :].astype(BF16)


def _window_mask(group, blk, n_blocks):
    qi = np.arange(group * blk)[:, None] % blk
    kj = np.arange(3 * blk)[None, :]
    band = (kj - qi >= 0) & (kj - qi <= 2 * WINDOW)
    first = band & (kj >= blk)
    last = band & (kj < 2 * blk)
    kinds = [first & last if n_blocks == 1 else first, band, last]
    return np.stack(kinds).astype(np.float32)


def _attention(qkv, sink, *, batch, seq, ctx, n_q, n_kv):
    n_rows = qkv.shape[0]
    group = n_q // n_kv
    blk = WINDOW
    p_rows = ctx + seq
    bpb = p_rows // blk
    cb = ctx // blk
    nb = seq // blk
    last_blk = n_rows // blk - 1
    qw = n_q * HEAD_DIM
    kw = n_kv * HEAD_DIM
    assert qw % kw == 0
    k_col = qw // kw
    spc = p_rows // ctx
    sink_g = sink.reshape(n_kv, group, 1).astype(F32)
    sink_x = jnp.broadcast_to(sink_g[:, :, None, :], (n_kv, group, blk, 1)).reshape(n_kv, group * blk, 1)
    sink_c = jnp.broadcast_to(sink_g[:, :, None, :], (n_kv, group, ctx, 1)).reshape(n_kv, group * ctx, 1)
    mask = jnp.asarray(_window_mask(group, blk, nb))

    def kv_specs(col):
        return [pl.BlockSpec((blk, kw), lambda b, n: (jnp.maximum(b * bpb + cb + n - 1, 0), col)),
                pl.BlockSpec((blk, kw), lambda b, n: (b * bpb + cb + n, col)),
                pl.BlockSpec((blk, kw), lambda b, n: (jnp.minimum(b * bpb + cb + n + 1, last_blk), col)),
                pl.BlockSpec((ctx, kw), lambda b, n: (b * spc, col))]

    out_x = pl.pallas_call(
        functools.partial(_attn_x_kernel, group=group, n_kv=n_kv),
        grid=(batch, nb),
        in_specs=[pl.BlockSpec((blk, qw), lambda b, n: (b * bpb + cb + n, 0))]
        + kv_specs(k_col) + kv_specs(k_col + 1)
        + [pl.BlockSpec((None, group * blk, 3 * blk),
                        lambda b, n: (jnp.where(n == 0, 0, jnp.where(n == nb - 1, 2, 1)), 0, 0)),
           pl.BlockSpec((n_kv, group * blk, 1), lambda b, n: (0, 0, 0))],
        out_specs=pl.BlockSpec((blk, qw), lambda b, n: (b * bpb + cb + n, 0)),
        out_shape=jax.ShapeDtypeStruct((n_rows, qw), BF16),
        compiler_params=_params(("parallel", "parallel")),
        name="attn_latent",
    )(qkv, qkv, qkv, qkv, qkv, qkv, qkv, qkv, qkv, mask, sink_x)

    out = pl.pallas_call(
        functools.partial(_attn_c_kernel, group=group, n_kv=n_kv),
        grid=(batch,),
        in_specs=[pl.BlockSpec((ctx, qw), lambda b: (b * spc, 0)),
                  pl.BlockSpec((ctx, kw), lambda b: (b * spc, k_col)),
                  pl.BlockSpec((ctx, kw), lambda b: (b * spc, k_col + 1)),
                  pl.BlockSpec((n_kv, group * ctx, 1), lambda b: (0, 0, 0)),
                  pl.BlockSpec(memory_space=pl.ANY)],
        out_specs=pl.BlockSpec((ctx, qw), lambda b: (b * spc, 0)),
        out_shape=jax.ShapeDtypeStruct((n_rows, qw), BF16),
        input_output_aliases={4: 0},
        compiler_params=_params(("parallel",)),
        name="attn_context",
    )(qkv, qkv, qkv, sink_c, out_x)
    return out


def _proj_res_kernel(y_ref, w_ref, x_ref, gt_ref, g_ref, sc_ref, sh_ref, o_ref, h_ref, *, sub):
    acc = jnp.dot(y_ref[...], w_ref[...], preferred_element_type=F32)
    g = g_ref[...]
    for s in range(o_ref.shape[0] // sub):
        rows = slice(s * sub, (s + 1) * sub)
        xn = x_ref[rows, :] + gt_ref[s] * acc[rows, :]
        o_ref[rows, :] = xn
        h_ref[rows, :] = _norm_mod(xn, g, sc_ref[s], sh_ref[s]).astype(BF16)


def _proj_residual(y, w, u, mods, g, *, layer, w_layer, tm, sub, latent_only=None):
    n_rows, d = u.shape
    k = y.shape[1]
    out_rows, out_map = n_rows, lambda i: (i, 0)
    if latent_only is not None:
        spb, cs = latent_only
        n_lat = n_rows // sub // spb * (spb - cs)
        out_rows = n_lat * sub + tm
        tm = sub
        out_map = lambda i: (jnp.where(i % spb >= cs, (i // spb) * (spb - cs) + i % spb - cs, n_lat), 0)
    nsb = tm // sub
    return pl.pallas_call(
        functools.partial(_proj_res_kernel, sub=sub),
        grid=(n_rows // tm,),
        in_specs=[pl.BlockSpec((tm, k), lambda i: (i, 0)),
                  _layer_spec((k, d), w_layer, lambda i: (0, 0)),
                  pl.BlockSpec((tm, d), lambda i: (i, 0)),
                  _mod_spec(nsb, d, layer, MOD_GATE1),
                  _layer_spec((1, d), layer, lambda i: (0, 0)),
                  _mod_spec(nsb, d, layer, MOD_SCALE2),
                  _mod_spec(nsb, d, layer, MOD_SHIFT2)],
        out_specs=[pl.BlockSpec((tm, d), out_map),
                   pl.BlockSpec((tm, d), out_map)],
        out_shape=[jax.ShapeDtypeStruct((out_rows, d), F32),
                   jax.ShapeDtypeStruct((out_rows, d), BF16)],
        compiler_params=_params(("arbitrary",)),
        name="proj_residual",
    )(y, w, u, mods, g, mods, mods)


def _mlstm_in_kernel(x_ref, g_ref, sc_ref, sh_ref, w_ref, b_ref, wg_ref, bg_ref, z_ref, gate_ref,
                     h_scr, *, sub, group, k_scale):
    j = pl.program_id(1)
    tm, tn = z_ref.shape

    @pl.when(j == 0)
    def _():
        _norm_mod_to(h_scr, x_ref, g_ref, sc_ref, sh_ref, sub)
        gate_ref[...] = jnp.dot(h_scr[...], wg_ref[...], preferred_element_type=F32) + bg_ref[...]

    for l in range(tn // group):
        cols = slice(l * group, (l + 1) * group)
        acc = jnp.dot(h_scr[...], w_ref[:, cols], preferred_element_type=F32) + b_ref[:, cols]
        is_k = j * (tn // group) + l == 1
        z_ref[:, cols] = (acc * jnp.where(is_k, k_scale, 1.0)).astype(BF16)


def _mlstm_in_proj(u, g, mods, w, b, wg, bg, *, layer, w_layer, tm, sub, dqk):
    n_rows, d = u.shape
    n = 3 * d
    qk_cols = d // 2
    tn = n // 2 if n // 2 <= PROJ_TILE and (n // 2) % qk_cols == 0 else qk_cols
    nsb = tm // sub
    kern = functools.partial(_mlstm_in_kernel, sub=sub, group=qk_cols, k_scale=dqk ** -0.5)
    return pl.pallas_call(
        kern,
        grid=(n_rows // tm, n // tn),
        in_specs=[pl.BlockSpec((tm, d), lambda i, j: (i, 0)),
                  _layer_spec((1, d), layer, lambda i, j: (0, 0)),
                  _mod_spec(nsb, d, layer, MOD_SCALE1),
                  _mod_spec(nsb, d, layer, MOD_SHIFT1),
                  _layer_spec((d, tn), w_layer, lambda i, j: (0, j)),
                  _layer_spec((1, tn), w_layer, lambda i, j: (0, j)),
                  pl.BlockSpec((d, LANE), lambda i, j: (0, 0)),
                  pl.BlockSpec((1, LANE), lambda i, j: (0, 0))],
        out_specs=[pl.BlockSpec((tm, tn), lambda i, j: (i, j)),
                   pl.BlockSpec((tm, LANE), lambda i, j: (i, 0))],
        out_shape=[jax.ShapeDtypeStruct((n_rows, n), BF16),
                   jax.ShapeDtypeStruct((n_rows, LANE), F32)],
        scratch_shapes=[pltpu.VMEM((tm, d), BF16)],
        compiler_params=_params(("parallel", "arbitrary")),
        name="mlstm_in",
    )(u, g, mods, mods, w, b, wg, bg)


GATE_F_LANE = 64
N_GATE_OUT = 6
N_COL = 5


def _chunk_scan(x, scr, pos, op, ident, reverse):
    r = x.shape[0]
    sh = 1
    while sh < CHUNK:
        scr[CHUNK:CHUNK + r, :] = x
        if reverse:
            other = scr[CHUNK + sh:CHUNK + sh + r, :]
            ok = pos < CHUNK - sh
        else:
            other = scr[CHUNK - sh:CHUNK - sh + r, :]
            ok = pos >= sh
        x = op(x, jnp.where(ok, other, ident))
        sh *= 2
    return x


def _mlstm_gate_kernel(gi_ref, gf_ref, o_ref, b_scr, cm_scr, shift_scr, *, n_heads, n_ctx_chunks, tile):
    p_rows = gi_ref.shape[0]
    n_chunks = p_rows // CHUNK
    shift_scr[...] = jnp.zeros_like(shift_scr)
    o_ref[0:N_COL] = jnp.zeros((N_COL,) + o_ref.shape[1:], F32)
    fwd_tile = lax.broadcasted_iota(jnp.int32, (tile, LANE), 1) % (2 * n_heads) < n_heads
    pos = lax.broadcasted_iota(jnp.int32, (tile, LANE), 0) % CHUNK

    def local(t, _):
        rows = pl.ds(pl.multiple_of(t * tile, tile), tile)
        g = gi_ref[rows, :]
        lf = _log_sigmoid(gf_ref[rows, :])
        b = jnp.where(fwd_tile, _chunk_scan(lf, shift_scr, pos, jnp.add, 0.0, False),
                      _chunk_scan(lf, shift_scr, pos, jnp.add, 0.0, True))
        a = g - b
        cm = jnp.where(fwd_tile, _chunk_scan(a, shift_scr, pos, jnp.maximum, -jnp.inf, False),
                       _chunk_scan(a, shift_scr, pos, jnp.maximum, -jnp.inf, True))
        b_scr[rows, :] = b
        cm_scr[rows, :] = cm
        o_ref[N_COL, rows, :] = a
        return 0

    lax.fori_loop(0, p_rows // tile, local, 0)

    fwd_chunk = lax.broadcasted_iota(jnp.int32, (CHUNK, LANE), 1) % (2 * n_heads) < n_heads

    def step(c, m, forward):
        rows = pl.ds(pl.multiple_of(c * CHUNK, CHUNK), CHUNK)
        end_row = CHUNK - 1 if forward else 0
        mine = fwd_chunk if forward else jnp.logical_not(fwd_chunk)
        b = b_scr[rows, :]
        a = o_ref[N_COL, rows, :]
        cm = cm_scr[rows, :]
        b_end = b[end_row:end_row + 1, :]
        m_t = b + jnp.maximum(m, cm)
        m_new = jnp.maximum(b_end + m, b_end + cm[end_row:end_row + 1, :])
        vals = (b - m_t, jnp.exp(b + m - m_t), jnp.exp(-m_t), jnp.exp(b_end + a - m_new),
                jnp.broadcast_to(jnp.exp(b_end + m - m_new), (CHUNK, LANE)))
        for q, val in enumerate(vals):
            o_ref[q, rows, :] = jnp.where(mine, val, o_ref[q, rows, :])
        return m_new

    def body(j, carry):
        mf, mb = carry
        mf = step(j, mf, True)
        cb = jnp.where(j < n_ctx_chunks, n_ctx_chunks - 1 - j, n_chunks - 1 + n_ctx_chunks - j)
        mb = step(cb, mb, False)
        return mf, mb

    zm = jnp.zeros((1, LANE), F32)
    lax.fori_loop(0, n_chunks, body, (zm, zm))


def _mlstm_gates(gates, *, batch, p_rows, ctx, n_heads):
    width = 2 * n_heads
    assert batch * width <= LANE

    def lanes_by_sample(lane0):
        g = gates[:, lane0:lane0 + width].reshape(batch, p_rows, width).transpose(1, 0, 2)
        return jnp.pad(g.reshape(p_rows, batch * width), ((0, 0), (0, LANE - batch * width)))

    tile = ctx
    kern = functools.partial(_mlstm_gate_kernel, n_heads=n_heads, n_ctx_chunks=ctx // CHUNK, tile=tile)
    return pl.pallas_call(
        kern,
        grid=(1,),
        in_specs=[pl.BlockSpec((p_rows, LANE), lambda b: (0, 0)),
                  pl.BlockSpec((p_rows, LANE), lambda b: (0, 0))],
        out_specs=pl.BlockSpec((N_GATE_OUT, p_rows, LANE), lambda b: (0, 0, 0)),
        out_shape=jax.ShapeDtypeStruct((N_GATE_OUT, p_rows, LANE), F32),
        scratch_shapes=[pltpu.VMEM((p_rows, LANE), F32), pltpu.VMEM((p_rows, LANE), F32),
                        pltpu.VMEM((tile + 2 * CHUNK, LANE), F32)],
        compiler_params=_params(("arbitrary",)),
        name="mlstm_gates",
    )(lanes_by_sample(0), lanes_by_sample(GATE_F_LANE))


def _mlstm_chunk(q, k, v, col, a_row, c_scr, n, d, reverse):
    ch = q.shape[0]
    t_idx = lax.broadcasted_iota(jnp.int32, (ch, ch), 0)
    s_idx = lax.broadcasted_iota(jnp.int32, (ch, ch), 1)
    seen = (s_idx >= t_idx) if reverse else (s_idx <= t_idx)
    c0 = d * N_COL
    row_term = col[:, c0:c0 + 1]
    e_inter = col[:, c0 + 1:c0 + 2]
    floor = col[:, c0 + 2:c0 + 3]
    wg_col = col[:, c0 + 3:c0 + 4]
    e_prev = col[0:1, c0 + 4:c0 + 5]

    w = jnp.where(seen, jnp.exp(row_term + a_row), 0.0)
    s = lax.dot_general(q, k, (((1,), (1,)), ((), ())), preferred_element_type=F32) * w
    c_old = c_scr[...]
    num = (e_inter * jnp.dot(q, c_old.astype(BF16), preferred_element_type=F32)
           + jnp.dot(s.astype(BF16), v, preferred_element_type=F32))
    qn = jnp.sum(q.astype(F32) * n, axis=1, keepdims=True)
    den = e_inter * qn + jnp.sum(s, axis=1, keepdims=True)
    h = num * (1.0 / jnp.maximum(jnp.abs(den), floor))

    vw = (wg_col * v.astype(F32)).astype(BF16)
    c_scr[...] = e_prev * c_old + lax.dot_general(k, vw, (((0,), (0,)), ((), ())),
                                                  preferred_element_type=F32)
    n_new = e_prev * n + jnp.sum(wg_col * k.astype(F32), axis=0, keepdims=True)
    return h, n_new


def _mlstm_core_kernel(q_ref, k_ref, v_ref, o_ref, gr_ref, gc_ref, gh_ref, y_ref,
                       hf_scr, hb_scr, cf_scr, cb_scr, *, n_ctx_chunks, out_rows):
    n_chunks = gr_ref.shape[0]
    dqk = q_ref.shape[1]
    cf_scr[...] = jnp.zeros_like(cf_scr)
    cb_scr[...] = jnp.zeros_like(cb_scr)

    def run(c, c_scr, h_scr, n, d, reverse):
        rows = pl.ds(pl.multiple_of(c * CHUNK, CHUNK), CHUNK)
        h, n = _mlstm_chunk(q_ref[rows, :], k_ref[rows, :], v_ref[rows, :], gc_ref[rows, :],
                            gr_ref[c][d:d + 1, :], c_scr, n, d, reverse)
        h_scr[rows, :] = h
        return n

    def body(j, carry):
        nf, nb = carry
        nf = run(j, cf_scr, hf_scr, nf, 0, False)
        cb = jnp.where(j < n_ctx_chunks, n_ctx_chunks - 1 - j, n_chunks - 1 + n_ctx_chunks - j)
        nb = run(cb, cb_scr, hb_scr, nb, 1, True)
        return nf, nb

    zn = jnp.zeros((1, dqk), F32)
    lax.fori_loop(0, n_chunks, body, (zn, zn), unroll=2)

    gh = gh_ref[...]

    def finish(r, _):
        rows = pl.ds(pl.multiple_of(r * out_rows, out_rows), out_rows)
        h = hf_scr[rows, :] + hb_scr[rows, :]
        hn = h * lax.rsqrt(jnp.mean(h * h, axis=-1, keepdims=True) + RMS_EPS)
        y_ref[rows, :] = (_sigmoid(o_ref[rows, :].astype(F32)) * hn * gh).astype(BF16)
        return 0

    lax.fori_loop(0, y_ref.shape[0] // out_rows, finish, 0)


def _mlstm_core(z, gates_r, gates_c, g_head, *, batch, p_rows, ctx, n_heads, dqk, dv):
    n_rows = z.shape[0]
    n_chunks = p_rows // CHUNK
    kern = functools.partial(_mlstm_core_kernel, n_ctx_chunks=ctx // CHUNK, out_rows=ctx)
    return pl.pallas_call(
        kern,
        grid=(batch, n_heads),
        in_specs=[pl.BlockSpec((p_rows, dqk), lambda b, h: (b, h)),
                  pl.BlockSpec((p_rows, dqk), lambda b, h: (b, n_heads + h)),
                  pl.BlockSpec((p_rows, dv), lambda b, h: (b, n_heads + h)),
                  pl.BlockSpec((p_rows, dv), lambda b, h: (b, 2 * n_heads + h)),
                  pl.BlockSpec((None, None, n_chunks, 2, CHUNK), lambda b, h: (b, h, 0, 0, 0)),
                  pl.BlockSpec((None, None, p_rows, 2 * N_COL), lambda b, h: (b, h, 0, 0)),
                  pl.BlockSpec((1, dv), lambda b, h: (0, h))],
        out_specs=pl.BlockSpec((p_rows, dv), lambda b, h: (b, h)),
        out_shape=jax.ShapeDtypeStruct((n_rows, n_heads * dv), BF16),
        scratch_shapes=[pltpu.VMEM((p_rows, dv), F32), pltpu.VMEM((p_rows, dv), F32),
                        pltpu.VMEM((dqk, dv), F32), pltpu.VMEM((dqk, dv), F32)],
        compiler_params=_params(("parallel", "parallel")),
        name="mlstm_core",
    )(z, z, z, z, gates_r, gates_c, g_head)


def _gate_layouts(og, *, batch, p_rows, n_heads):
    g = og[:, :, :batch * 2 * n_heads].reshape(N_GATE_OUT, p_rows, batch, 2, n_heads)
    cols = g[:N_COL].transpose(2, 4, 1, 3, 0).reshape(batch, n_heads, p_rows, 2 * N_COL)
    rows = g[N_COL].reshape(p_rows // CHUNK, CHUNK, batch, 2, n_heads).transpose(2, 4, 0, 3, 1)
    return rows, cols


def _halo_up_kernel(h_ref, w_ref, o_ref):
    o_ref[...] = jnp.dot(h_ref[...], w_ref[...], preferred_element_type=F32)


def _halo_up(h, w_up, *, layer, n_rows, tm):
    d = h.shape[1]
    n = w_up.shape[2]
    nt = n_rows // tm
    ht = h.reshape(h.shape[0] // tm, tm, d)
    first, last = ht[:nt, 0], ht[:nt, tm - 1]
    prev = jnp.concatenate([last[:1], last[:nt - 1]], axis=0)
    nxt = jnp.concatenate([first[1:], first[nt - 1:]], axis=0)
    hh = jnp.stack([prev, nxt], axis=1).reshape(2 * nt, d)
    rows = 2 * nt
    tn = 2 * FFN_CHUNK
    uh = pl.pallas_call(
        _halo_up_kernel,
        grid=(n // tn,),
        in_specs=[pl.BlockSpec((rows, d), lambda j: (0, 0)),
                  _layer_spec((d, tn), layer, lambda j: (0, j))],
        out_specs=pl.BlockSpec((rows, tn), lambda j: (0, j)),
        out_shape=jax.ShapeDtypeStruct((rows, n), F32),
        compiler_params=_params(("parallel",)),
        name="ffn_halo",
    )(hh, w_up)
    return uh.reshape(nt, 2, n)


def _ffn_kernel(h_ref, x_ref, gt_ref, wa0_ref, wb0_ref, wa1_ref, wb1_ref,
                ha0_ref, hb0_ref, ha1_ref, hb1_ref, cwa_ref, cwb_ref, cba_ref, cbb_ref, wd_ref,
                gf_ref, o_ref, acc_scr, u0_scr, u1_scr, a0_scr, a1_scr,
                *, sub, p_rows, ctx, final_norm):
    i = pl.program_id(0)
    j = pl.program_id(1)
    tm = o_ref.shape[0]
    nsb = tm // sub
    fh = wa0_ref.shape[1]
    lt = fh // LANE
    seg = sub + 2 * HALO

    @pl.when(j == 0)
    def _():
        acc_scr[...] = jnp.zeros_like(acc_scr)

    def edge_open(k):
        pos = (i * tm + k * sub) % p_rows
        return jnp.where((pos == 0) | (pos == ctx), 0.0, 1.0)

    opens = [edge_open(k) for k in range(nsb + 1)]

    def up(w_refs, halo_refs, u_scr):
        for br, (w_ref, halo_ref) in enumerate(zip(w_refs, halo_refs)):
            ue = jnp.dot(h_ref[...], w_ref[...], preferred_element_type=F32)
            halo = halo_ref[...]
            for s in range(nsb):
                r0 = s * sub
                b0 = s * seg
                for t in range(lt):
                    lanes = slice(t * LANE, (t + 1) * LANE)
                    tt = br * lt + t
                    before = (jnp.broadcast_to(halo[0:1, lanes], (HALO, LANE)) if s == 0
                              else ue[r0 - HALO:r0, lanes])
                    after = (jnp.broadcast_to(halo[1:2, lanes], (HALO, LANE)) if s == nsb - 1
                             else ue[r0 + sub:r0 + sub + HALO, lanes])
                    u_scr[tt, b0:b0 + HALO, :] = before * opens[s]
                    u_scr[tt, b0 + HALO:b0 + HALO + sub, :] = ue[r0:r0 + sub, lanes]
                    u_scr[tt, b0 + HALO + sub:b0 + seg, :] = after * opens[s + 1]

    def gate(c, u_scr, act_scr):
        cols = slice(c * fh, (c + 1) * fh)
        cws = (cwa_ref[:, cols], cwb_ref[:, cols])
        cbs = (cba_ref[:, cols], cbb_ref[:, cols])

        def conv_piece(br, t, r0):
            lanes = slice(t * LANE, (t + 1) * LANE)
            tt = br * lt + t + jnp.minimum(j, 0)
            cw, cb = cws[br], cbs[br]
            cur = u_scr[tt, r0:r0 + GATE_ROWS, :]
            prv = u_scr[tt, r0 - 1:r0 - 1 + GATE_ROWS, :]
            nxt = u_scr[tt, r0 + 1:r0 + 1 + GATE_ROWS, :]
            return cb[:, lanes] + prv * cw[0:1, lanes] + cur * cw[1:2, lanes] + nxt * cw[2:3, lanes]

        for s in range(nsb):
            for r in range(sub // GATE_ROWS):
                r0 = s * seg + HALO + r * GATE_ROWS
                o0 = s * sub + r * GATE_ROWS
                for t in range(lt):
                    a = conv_piece(0, t, r0)
                    b = conv_piece(1, t, r0)
                    act_scr[o0:o0 + GATE_ROWS, t * LANE:(t + 1) * LANE] = (a * _sigmoid(a) * b).astype(BF16)

    def down(c, act_scr):
        acc_scr[...] += jnp.dot(act_scr[...], wd_ref[c * fh:(c + 1) * fh, :], preferred_element_type=F32)

    up((wa0_ref, wb0_ref), (ha0_ref, hb0_ref), u0_scr)
    up((wa1_ref, wb1_ref), (ha1_ref, hb1_ref), u1_scr)
    gate(0, u0_scr, a0_scr)
    down(0, a0_scr)
    gate(1, u1_scr, a1_scr)
    down(1, a1_scr)

    @pl.when(j == pl.num_programs(1) - 1)
    def _():
        for s in range(nsb):
            if not final_norm:
                rows = slice(s * sub, (s + 1) * sub)
                o_ref[rows, :] = x_ref[rows, :] + gt_ref[s] * acc_scr[rows, :]
                continue
            for r in range(s * sub, (s + 1) * sub, NORM_ROWS):
                rows = slice(r, r + NORM_ROWS)
                xn = x_ref[rows, :] + gt_ref[s] * acc_scr[rows, :]
                ms = jnp.mean(xn * xn, axis=-1, keepdims=True)
                o_ref[rows, :] = xn * lax.rsqrt(ms + RMS_EPS) * gf_ref[...]


def _conv_ffn(h, u, mods, w_up, conv_w, conv_b, w_down, g_final, *, layer, mod_layer, n_rows, tm, sub,
              p_rows, ctx, final_norm):
    d = u.shape[1]
    d_ff = w_down.shape[1]
    fc = FFN_CHUNK
    assert d_ff % fc == 0
    fh = fc // 2
    nfc = d_ff // fc
    nsb = tm // sub
    kern = functools.partial(_ffn_kernel, sub=sub, p_rows=p_rows, ctx=ctx, final_norm=final_norm)
    u_shape = (fc // LANE, nsb * (sub + 2 * HALO), LANE)
    halo = _halo_up(h, w_up, layer=layer, n_rows=n_rows, tm=tm)

    def half_specs(shape, lead):
        return [pl.BlockSpec(shape, lambda i, j, c=c: lead(i) + (c + 2 * j,))
                for c in (0, 2 * nfc, 1, 2 * nfc + 1)]

    return pl.pallas_call(
        kern,
        grid=(n_rows // tm, nfc),
        in_specs=[pl.BlockSpec((tm, d), lambda i, j: (i, 0)),
                  pl.BlockSpec((tm, d), lambda i, j: (i, 0)),
                  _mod_spec(nsb, d, mod_layer, MOD_GATE2)]
        + half_specs((None, d, fh), lambda i: (layer, 0))
        + half_specs((None, 2, fh), lambda i: (i, 0))
        + [_layer_spec((CONV_W, fc), layer, lambda i, j: (0, j)),
           _layer_spec((CONV_W, fc), layer, lambda i, j: (0, nfc + j)),
           _layer_spec((1, fc), layer, lambda i, j: (0, j)),
           _layer_spec((1, fc), layer, lambda i, j: (0, nfc + j)),
           _layer_spec((fc, d), layer, lambda i, j: (j, 0)),
           pl.BlockSpec((1, d), lambda i, j: (0, 0))],
        out_specs=pl.BlockSpec((tm, d), lambda i, j: (i, 0)),
        out_shape=jax.ShapeDtypeStruct((n_rows, d), F32),
        scratch_shapes=[pltpu.VMEM((tm, d), F32), pltpu.VMEM(u_shape, F32), pltpu.VMEM(u_shape, F32),
                        pltpu.VMEM((tm, fh), BF16), pltpu.VMEM((tm, fh), BF16)],
        compiler_params=_params(("parallel", "arbitrary")),
        name="conv_ffn",
    )(h, u, mods, w_up, w_up, w_up, w_up, halo, halo, halo, halo,
      conv_w, conv_w, conv_b, conv_b, w_down, g_final)


def _rope_tables(batch, seq, ctx):
    rows = seq // GRID_W
    row = jnp.repeat(jnp.arange(rows, dtype=F32), GRID_W)
    col = jnp.tile(jnp.arange(GRID_W, dtype=F32), rows)
    n_freq = HEAD_DIM // 4
    inv_freq = ROPE_THETA ** (-jnp.arange(n_freq, dtype=F32) / n_freq)
    ang = jnp.concatenate([row[:, None] * inv_freq, col[:, None] * inv_freq], axis=-1)
    ang = jnp.concatenate([ang, ang], axis=-1)
    sign = jnp.concatenate([-jnp.ones((HEAD_DIM // 2,), F32), jnp.ones((HEAD_DIM // 2,), F32)])
    cos = jnp.concatenate([jnp.ones((ctx, HEAD_DIM), F32), jnp.cos(ang)], axis=0)
    sin = jnp.concatenate([jnp.zeros((ctx, HEAD_DIM), F32), jnp.sin(ang) * sign], axis=0)
    return jnp.tile(cos, (batch, 1)), jnp.tile(sin, (batch, 1))


def kernel(x, c, ctx, c_ctx, w_mod, b_mod, g_mix, g_ffn, attn_w_qkv, attn_sink, attn_w_o,
           mlstm_w_in, mlstm_b_in, mlstm_g_head, mlstm_w_o, ffn_w_up, ffn_conv_w, ffn_conv_b,
           ffn_w_down, g_final):
    batch, seq, d = x.shape
    lc = ctx.shape[1]
    depth = w_mod.shape[0]
    p_rows = lc + seq
    sub = lc
    assert seq % sub == 0 and sub % WINDOW == 0 and sub % CHUNK == 0 and seq % GRID_W == 0
    n_sub = batch * p_rows // sub
    tm = 2 * sub if n_sub % 2 == 0 else sub
    n_q = attn_sink.shape[1]
    n_kv = (attn_w_qkv.shape[2] // HEAD_DIM - n_q) // 2
    n_heads = (mlstm_w_in.shape[2] - 3 * d) // 4
    dv = d // n_heads
    dqk = dv // 2
    main_cols = 3 * d

    u = jnp.concatenate([ctx, x], axis=1).reshape(batch * p_rows, d)

    n_c = batch + 1
    c_rows = -(-n_c // 8) * 8
    cvec = jnp.concatenate([c, c_ctx[None, :], jnp.zeros((c_rows - n_c, d), F32)], axis=0)
    mods = _modulation(cvec, w_mod, b_mod)
    mods_sb = jnp.concatenate(
        [jnp.broadcast_to(mods[:, None, batch:batch + 1], (depth, batch, lc // sub, 6 * d)),
         jnp.broadcast_to(mods[:, :batch, None], (depth, batch, seq // sub, 6 * d))],
        axis=2).reshape(depth, n_sub, 6, 1, d)

    mods_lat = jnp.broadcast_to(mods[depth - 1:, :batch, None], (1, batch, seq // sub, 6 * d)
                                ).reshape(1, batch * seq // sub, 6, 1, d)

    cos_u, sin_u = _rope_tables(batch, seq, lc)

    w_qkv = attn_w_qkv.astype(BF16)
    w_attn_o = attn_w_o.astype(BF16)
    w_in = mlstm_w_in.astype(BF16)
    w_mlstm_o = mlstm_w_o.astype(BF16)
    w_up = ffn_w_up.astype(BF16)
    w_down = ffn_w_down.astype(BF16)
    g_mix3 = g_mix.reshape(depth, 1, d)
    g_ffn3 = g_ffn.reshape(depth, 1, d)
    b_in3 = mlstm_b_in.reshape(mlstm_b_in.shape[0], 1, -1)
    conv_b3 = ffn_conv_b.reshape(depth, 1, -1)
    g_fin = g_final.reshape(1, d)

    for i in range(depth):
        jm = i // 2
        if i % 2 == 0:
            qkv = _qkv_proj(u, g_mix3, mods_sb, w_qkv, cos_u, sin_u, layer=i, w_layer=jm,
                            tm=tm, sub=sub, n_q=n_q, n_kv=n_kv)
            y = _attention(qkv, attn_sink[jm], batch=batch, seq=seq, ctx=lc, n_q=n_q, n_kv=n_kv)
            w_o = w_attn_o
        else:
            wgt = mlstm_w_in[jm][:, main_cols:].reshape(d, 4, n_heads)
            bgt = mlstm_b_in[jm][main_cols:].reshape(4, n_heads)
            lane_pad = GATE_F_LANE - 2 * n_heads
            wg = jnp.concatenate([wgt[:, 0], wgt[:, 2], jnp.zeros((d, lane_pad), F32),
                                  wgt[:, 1], wgt[:, 3], jnp.zeros((d, lane_pad), F32)], axis=1).astype(BF16)
            bg = jnp.concatenate([bgt[0], bgt[2], jnp.zeros((lane_pad,), F32),
                                  bgt[1], bgt[3], jnp.zeros((lane_pad,), F32)]).reshape(1, LANE)
            z, gates = _mlstm_in_proj(u, g_mix3, mods_sb, w_in, b_in3, wg, bg, layer=i, w_layer=jm,
                                      tm=tm, sub=sub, dqk=dqk)
            og = _mlstm_gates(gates, batch=batch, p_rows=p_rows, ctx=lc, n_heads=n_heads)
            gates_r, gates_c = _gate_layouts(og, batch=batch, p_rows=p_rows, n_heads=n_heads)
            y = _mlstm_core(z, gates_r, gates_c, mlstm_g_head[jm].reshape(1, n_heads * dv),
                            batch=batch, p_rows=p_rows, ctx=lc, n_heads=n_heads, dqk=dqk, dv=dv)
            w_o = w_mlstm_o
        if i < depth - 1:
            u, h2 = _proj_residual(y, w_o, u, mods_sb, g_ffn3, layer=i, w_layer=jm, tm=tm, sub=sub)
            u = _conv_ffn(h2, u, mods_sb, w_up, ffn_conv_w, conv_b3, w_down, g_fin, layer=i, mod_layer=i,
                          n_rows=batch * p_rows, tm=tm, sub=sub, p_rows=p_rows, ctx=lc,
                          final_norm=False)
        else:
            u, h2 = _proj_residual(y, w_o, u, mods_sb, g_ffn3, layer=i, w_layer=jm, tm=tm, sub=sub,
                                   latent_only=(p_rows // sub, lc // sub))
            u = _conv_ffn(h2, u, mods_lat, w_up, ffn_conv_w, conv_b3, w_down, g_fin, layer=i, mod_layer=0,
                          n_rows=batch * seq, tm=tm, sub=sub, p_rows=seq, ctx=0,
                          final_norm=True)
    return u.reshape(batch, seq, d)
```

```python
import functools

import numpy as np
import jax
import jax.numpy as jnp
from jax import lax
from jax.experimental import pallas as pl
from jax.experimental.pallas import tpu as pltpu

F32 = jnp.float32
BF16 = jnp.bfloat16

RMS_EPS = 1e-6
HEAD_DIM = 128
WINDOW = 128
GRID_W = 64
ROPE_THETA = 10000.0
NEG_INF = -1e30
CHUNK = 128
CONV_W = 3
LANE = 128
HALO = 8
BF16_ROWS = 16
FFN_CHUNK = 512
GATE_ROWS = 64
NORM_ROWS = 16
PROJ_TILE = 3072
MOD_TILE = 1024
MOD_SHIFT1, MOD_SCALE1, MOD_GATE1, MOD_SHIFT2, MOD_SCALE2, MOD_GATE2 = range(6)
V7X_VMEM_BYTES = 64 * 1024 * 1024
VMEM_LIMIT = V7X_VMEM_BYTES * 7 // 8


def _params(sem):
    return pltpu.CompilerParams(dimension_semantics=sem, vmem_limit_bytes=VMEM_LIMIT)


def _sigmoid(v):
    return 1.0 / (1.0 + jnp.exp(-v))


def _log_sigmoid(v):
    return jnp.minimum(v, 0.0) - jnp.log(1.0 + jnp.exp(-jnp.abs(v)))


def _mod_spec(nsb, d, layer, kind):
    return pl.BlockSpec((None, nsb, None, 1, d), lambda i, *_: (layer, i, kind, 0, 0))


def _layer_spec(shape, layer, index):
    return pl.BlockSpec((None,) + shape, lambda *g: (layer,) + index(*g))


def _norm_mod_to(h_scr, x_ref, g_ref, sc_ref, sh_ref, sub):
    for s in range(x_ref.shape[0] // sub):
        gain = g_ref[...] * (1.0 + sc_ref[s])
        shift = sh_ref[s]
        for r in range(s * sub, (s + 1) * sub, NORM_ROWS):
            xv = x_ref[r:r + NORM_ROWS, :]
            ms = jnp.mean(xv * xv, axis=-1, keepdims=True)
            h_scr[r:r + NORM_ROWS, :] = (xv * lax.rsqrt(ms + RMS_EPS) * gain + shift).astype(BF16)


def _mod_kernel(c_ref, w_ref, b_ref, o_ref):
    cv = c_ref[...]
    s = (cv * _sigmoid(cv)).astype(BF16)
    o_ref[...] = jnp.dot(s, w_ref[...].astype(BF16), preferred_element_type=F32) + b_ref[...]


def _modulation(cvec, w_mod, b_mod):
    depth, d, n = w_mod.shape
    r = cvec.shape[0]
    tn = MOD_TILE if n % MOD_TILE == 0 else n
    return pl.pallas_call(
        _mod_kernel,
        grid=(depth, n // tn),
        in_specs=[pl.BlockSpec((r, d), lambda i, j: (0, 0)),
                  pl.BlockSpec((None, d, tn), lambda i, j: (i, 0, j)),
                  pl.BlockSpec((None, 1, tn), lambda i, j: (i, 0, j))],
        out_specs=pl.BlockSpec((None, r, tn), lambda i, j: (i, 0, j)),
        out_shape=jax.ShapeDtypeStruct((depth, r, n), F32),
        compiler_params=_params(("parallel", "parallel")),
        name="modulation",
    )(cvec, w_mod, b_mod.reshape(depth, 1, n))


def _qkv_kernel(x_ref, g_ref, sc_ref, sh_ref, w_ref, cos_ref, sin_ref, o_ref, h_scr,
                *, sub, n_q, n_rope, q_scale):
    j = pl.program_id(1)
    tm, tn = o_ref.shape

    @pl.when(j == 0)
    def _():
        _norm_mod_to(h_scr, x_ref, g_ref, sc_ref, sh_ref, sub)

    acc = jnp.dot(h_scr[...], w_ref[...], preferred_element_type=F32)
    cos = cos_ref[...]
    sin = sin_ref[...]
    heads_per_tile = tn // HEAD_DIM
    for l in range(heads_per_tile):
        t = acc[:, l * HEAD_DIM:(l + 1) * HEAD_DIM]
        head = j * heads_per_tile + l
        tr = t * cos + pltpu.roll(t, HEAD_DIM // 2, 1) * sin
        t = jnp.where(head < n_rope, tr, t)
        t = t * jnp.where(head < n_q, q_scale, 1.0)
        o_ref[:, l * HEAD_DIM:(l + 1) * HEAD_DIM] = t.astype(BF16)


def _qkv_proj(u, g, mods, w, cos_u, sin_u, *, layer, w_layer, tm, sub, n_q, n_kv):
    n_rows, d = u.shape
    n = w.shape[2]
    tn = n if n <= PROJ_TILE else PROJ_TILE
    assert n % tn == 0
    nsb = tm // sub
    kern = functools.partial(_qkv_kernel, sub=sub, n_q=n_q, n_rope=n_q + n_kv,
                             q_scale=HEAD_DIM ** -0.5)
    return pl.pallas_call(
        kern,
        grid=(n_rows // tm, n // tn),
        in_specs=[pl.BlockSpec((tm, d), lambda i, j: (i, 0)),
                  _layer_spec((1, d), layer, lambda i, j: (0, 0)),
                  _mod_spec(nsb, d, layer, MOD_SCALE1),
                  _mod_spec(nsb, d, layer, MOD_SHIFT1),
                  _layer_spec((d, tn), w_layer, lambda i, j: (0, j)),
                  pl.BlockSpec((tm, HEAD_DIM), lambda i, j: (i, 0)),
                  pl.BlockSpec((tm, HEAD_DIM), lambda i, j: (i, 0))],
        out_specs=pl.BlockSpec((tm, tn), lambda i, j: (i, j)),
        out_shape=jax.ShapeDtypeStruct((n_rows, n), BF16),
        scratch_shapes=[pltpu.VMEM((tm, d), BF16)],
        compiler_params=_params(("parallel", "arbitrary")),
        name="attn_qkv",
    )(u, g, mods, mods, w, cos_u, sin_u)


def _softmax_pv(s, sink, v):
    m = jnp.maximum(jnp.max(s, axis=-1, keepdims=True), sink)
    p = jnp.exp(s - m)
    l = jnp.sum(p, axis=-1, keepdims=True) + jnp.exp(sink - m)
    o = jnp.dot(p.astype(BF16), v, preferred_element_type=F32)
    return o * (1.0 / l)


def _stack_heads(q, group):
    return jnp.concatenate([q[:, g * HEAD_DIM:(g + 1) * HEAD_DIM] for g in range(group)], axis=0)


def _attn_x_kernel(q_ref, kp_ref, ko_ref, kn_ref, kc_ref, vp_ref, vo_ref, vn_ref, vc_ref,
                   mask_ref, sink_ref, o_ref, *, group, n_kv):
    blk = q_ref.shape[0]
    valid = mask_ref[...] > 0.0

    def scores(h):
        kv = slice(h * HEAD_DIM, (h + 1) * HEAD_DIM)
        qs = _stack_heads(q_ref[:, h * group * HEAD_DIM:(h + 1) * group * HEAD_DIM], group)
        k = jnp.concatenate([kp_ref[:, kv], ko_ref[:, kv], kn_ref[:, kv], kc_ref[:, kv]], axis=0)
        s = lax.dot_general(qs, k, (((1,), (1,)), ((), ())), preferred_element_type=F32)
        return jnp.concatenate([jnp.where(valid, s[:, :3 * blk], NEG_INF), s[:, 3 * blk:]], axis=1)

    def finish(h, s):
        kv = slice(h * HEAD_DIM, (h + 1) * HEAD_DIM)
        v = jnp.concatenate([vp_ref[:, kv], vo_ref[:, kv], vn_ref[:, kv], vc_ref[:, kv]], axis=0)
        o = _softmax_pv(s, sink_ref[h], v)
        for g in range(group):
            c0 = (h * group + g) * HEAD_DIM
            o_ref[:, c0:c0 + HEAD_DIM] = o[g * blk:(g + 1) * blk, :].astype(BF16)

    s_next = scores(0)
    for h in range(n_kv):
        s_cur = s_next
        if h + 1 < n_kv:
            s_next = scores(h + 1)
        finish(h, s_cur)


def _attn_c_kernel(q_ref, kc_ref, vc_ref, sink_ref, prev_ref, o_ref, *, group, n_kv):
    del prev_ref
    lc = q_ref.shape[0]

    def scores(h):
        qs = _stack_heads(q_ref[:, h * group * HEAD_DIM:(h + 1) * group * HEAD_DIM], group)
        return lax.dot_general(qs, kc_ref[:, h * HEAD_DIM:(h + 1) * HEAD_DIM], (((1,), (1,)), ((), ())),
                               preferred_element_type=F32)

    s_next = scores(0)
    for h in range(n_kv):
        s_cur = s_next
        if h + 1 < n_kv:
            s_next = scores(h + 1)
        o = _softmax_pv(s_cur, sink_ref[h], vc_ref[:, h * HEAD_DIM:(h + 1) * HEAD_DIM])
        for g in range(group):
            c0 = (h * group + g) * HEAD_DIM
            o_ref[:, c0:c0 + HEAD_DIM] = o[g * lc:(g + 1) * lc, :].astype(BF16)


def _window_mask(group, blk, n_blocks):
    qi = np.arange(group * blk)[:, None] % blk
    kj = np.arange(3 * blk)[None, :]
    band = (kj - qi >= 0) & (kj - qi <= 2 * WINDOW)
    first = band & (kj >= blk)
    last = band & (kj < 2 * blk)
    kinds = [first & last if n_blocks == 1 else first, band, last]
    return np.stack(kinds).astype(np.float32)


def _attention(qkv, sink, *, batch, seq, ctx, n_q, n_kv):
    n_rows = qkv.shape[0]
    group = n_q // n_kv
    blk = WINDOW
    p_rows = ctx + seq
    bpb = p_rows // blk
    cb = ctx // blk
    nb = seq // blk
    last_blk = n_rows // blk - 1
    qw = n_q * HEAD_DIM
    kw = n_kv * HEAD_DIM
    assert qw % kw == 0
    k_col = qw // kw
    spc = p_rows // ctx
    sink_g = sink.reshape(n_kv, group, 1).astype(F32)
    sink_x = jnp.broadcast_to(sink_g[:, :, None, :], (n_kv, group, blk, 1)).reshape(n_kv, group * blk, 1)
    sink_c = jnp.broadcast_to(sink_g[:, :, None, :], (n_kv, group, ctx, 1)).reshape(n_kv, group * ctx, 1)
    mask = jnp.asarray(_window_mask(group, blk, nb))

    def kv_specs(col):
        return [pl.BlockSpec((blk, kw), lambda b, n: (jnp.maximum(b * bpb + cb + n - 1, 0), col)),
                pl.BlockSpec((blk, kw), lambda b, n: (b * bpb + cb + n, col)),
                pl.BlockSpec((blk, kw), lambda b, n: (jnp.minimum(b * bpb + cb + n + 1, last_blk), col)),
                pl.BlockSpec((ctx, kw), lambda b, n: (b * spc, col))]

    out_x = pl.pallas_call(
        functools.partial(_attn_x_kernel, group=group, n_kv=n_kv),
        grid=(batch, nb),
        in_specs=[pl.BlockSpec((blk, qw), lambda b, n: (b * bpb + cb + n, 0))]
        + kv_specs(k_col) + kv_specs(k_col + 1)
        + [pl.BlockSpec((None, group * blk, 3 * blk),
                        lambda b, n: (jnp.where(n == 0, 0, jnp.where(n == nb - 1, 2, 1)), 0, 0)),
           pl.BlockSpec((n_kv, group * blk, 1), lambda b, n: (0, 0, 0))],
        out_specs=pl.BlockSpec((blk, qw), lambda b, n: (b * bpb + cb + n, 0)),
        out_shape=jax.ShapeDtypeStruct((n_rows, qw), BF16),
        compiler_params=_params(("parallel", "parallel")),
        name="attn_latent",
    )(qkv, qkv, qkv, qkv, qkv, qkv, qkv, qkv, qkv, mask, sink_x)

    out = pl.pallas_call(
        functools.partial(_attn_c_kernel, group=group, n_kv=n_kv),
        grid=(batch,),
        in_specs=[pl.BlockSpec((ctx, qw), lambda b: (b * spc, 0)),
                  pl.BlockSpec((ctx, kw), lambda b: (b * spc, k_col)),
                  pl.BlockSpec((ctx, kw), lambda b: (b * spc, k_col + 1)),
                  pl.BlockSpec((n_kv, group * ctx, 1), lambda b: (0, 0, 0)),
                  pl.BlockSpec(memory_space=pl.ANY)],
        out_specs=pl.BlockSpec((ctx, qw), lambda b: (b * spc, 0)),
        out_shape=jax.ShapeDtypeStruct((n_rows, qw), BF16),
        input_output_aliases={4: 0},
        compiler_params=_params(("parallel",)),
        name="attn_context",
    )(qkv, qkv, qkv, sink_c, out_x)
    return out


def _proj_res_kernel(y_ref, w_ref, x_ref, gt_ref, g_ref, sc_ref, sh_ref, o_ref, h_ref, *, sub):
    acc = jnp.dot(y_ref[...], w_ref[...], preferred_element_type=F32)
    for s in range(o_ref.shape[0] // sub):
        gain = g_ref[...] * (1.0 + sc_ref[s])
        shift = sh_ref[s]
        gate = gt_ref[s]
        for r in range(s * sub, (s + 1) * sub, NORM_ROWS):
            rows = slice(r, r + NORM_ROWS)
            xn = x_ref[rows, :] + gate * acc[rows, :]
            o_ref[rows, :] = xn
            ms = jnp.mean(xn * xn, axis=-1, keepdims=True)
            h_ref[rows, :] = (xn * lax.rsqrt(ms + RMS_EPS) * gain + shift).astype(BF16)


def _proj_residual(y, w, u, mods, g, *, layer, w_layer, tm, sub, latent_only=None):
    n_rows, d = u.shape
    k = y.shape[1]
    out_rows, out_map = n_rows, lambda i: (i, 0)
    if latent_only is not None:
        spb, cs = latent_only
        n_lat = n_rows // sub // spb * (spb - cs)
        out_rows = n_lat * sub + tm
        tm = sub
        out_map = lambda i: (jnp.where(i % spb >= cs, (i // spb) * (spb - cs) + i % spb - cs, n_lat), 0)
    nsb = tm // sub
    return pl.pallas_call(
        functools.partial(_proj_res_kernel, sub=sub),
        grid=(n_rows // tm,),
        in_specs=[pl.BlockSpec((tm, k), lambda i: (i, 0)),
                  _layer_spec((k, d), w_layer, lambda i: (0, 0)),
                  pl.BlockSpec((tm, d), lambda i: (i, 0)),
                  _mod_spec(nsb, d, layer, MOD_GATE1),
                  _layer_spec((1, d), layer, lambda i: (0, 0)),
                  _mod_spec(nsb, d, layer, MOD_SCALE2),
                  _mod_spec(nsb, d, layer, MOD_SHIFT2)],
        out_specs=[pl.BlockSpec((tm, d), out_map),
                   pl.BlockSpec((tm, d), out_map)],
        out_shape=[jax.ShapeDtypeStruct((out_rows, d), F32),
                   jax.ShapeDtypeStruct((out_rows, d), BF16)],
        compiler_params=_params(("arbitrary",)),
        name="proj_residual",
    )(y, w, u, mods, g, mods, mods)


def _mlstm_in_kernel(x_ref, g_ref, sc_ref, sh_ref, w_ref, b_ref, wg_ref, bg_ref, gh_ref, z_ref, gate_ref,
                     h_scr, *, sub, group, k_scale, o_first):
    j = pl.program_id(1)
    tm, tn = z_ref.shape

    @pl.when(j == 0)
    def _():
        _norm_mod_to(h_scr, x_ref, g_ref, sc_ref, sh_ref, sub)
        gate_ref[...] = jnp.dot(h_scr[...], wg_ref[...], preferred_element_type=F32) + bg_ref[...]

    for l in range(tn // group):
        cols = slice(l * group, (l + 1) * group)
        acc = jnp.dot(h_scr[...], w_ref[:, cols], preferred_element_type=F32) + b_ref[:, cols]
        gidx = j * (tn // group) + l

        @pl.when(gidx < o_first)
        def _():
            z_ref[:, cols] = (acc * jnp.where(gidx == 1, k_scale, 1.0)).astype(BF16)

        @pl.when(gidx >= o_first)
        def _():
            z_ref[:, cols] = (_sigmoid(acc) * gh_ref[:, cols]).astype(BF16)


def _mlstm_in_proj(u, g, mods, w, b, wg, bg, gh, *, layer, w_layer, tm, sub, dqk):
    n_rows, d = u.shape
    n = 3 * d
    qk_cols = d // 2
    tn = n // 2 if n // 2 <= PROJ_TILE and (n // 2) % qk_cols == 0 else qk_cols
    nsb = tm // sub
    kern = functools.partial(_mlstm_in_kernel, sub=sub, group=qk_cols, k_scale=dqk ** -0.5,
                             o_first=2 * d // qk_cols)
    return pl.pallas_call(
        kern,
        grid=(n_rows // tm, n // tn),
        in_specs=[pl.BlockSpec((tm, d), lambda i, j: (i, 0)),
                  _layer_spec((1, d), layer, lambda i, j: (0, 0)),
                  _mod_spec(nsb, d, layer, MOD_SCALE1),
                  _mod_spec(nsb, d, layer, MOD_SHIFT1),
                  _layer_spec((d, tn), w_layer, lambda i, j: (0, j)),
                  _layer_spec((1, tn), w_layer, lambda i, j: (0, j)),
                  pl.BlockSpec((d, LANE), lambda i, j: (0, 0)),
                  pl.BlockSpec((1, LANE), lambda i, j: (0, 0)),
                  pl.BlockSpec((1, tn), lambda i, j: (0, j))],
        out_specs=[pl.BlockSpec((tm, tn), lambda i, j: (i, j)),
                   pl.BlockSpec((tm, LANE), lambda i, j: (i, 0))],
        out_shape=[jax.ShapeDtypeStruct((n_rows, n), BF16),
                   jax.ShapeDtypeStruct((n_rows, LANE), F32)],
        scratch_shapes=[pltpu.VMEM((tm, d), BF16)],
        compiler_params=_params(("parallel", "arbitrary")),
        name="mlstm_in",
    )(u, g, mods, mods, w, b, wg, bg, gh)


GATE_F_LANE = 64
N_GATE_OUT = 6
N_COL = 5


def _chunk_scan(x, scr, pos, op, ident, reverse):
    r = x.shape[0]
    sh = 1
    while sh < CHUNK:
        scr[CHUNK:CHUNK + r, :] = x
        if reverse:
            other = scr[CHUNK + sh:CHUNK + sh + r, :]
            ok = pos < CHUNK - sh
        else:
            other = scr[CHUNK - sh:CHUNK - sh + r, :]
            ok = pos >= sh
        x = op(x, jnp.where(ok, other, ident))
        sh *= 2
    return x


def _mlstm_gate_kernel(gi_ref, gf_ref, o_ref, b_scr, cm_scr, shift_scr, *, n_heads, n_ctx_chunks, tile):
    p_rows = gi_ref.shape[0]
    n_chunks = p_rows // CHUNK
    shift_scr[...] = jnp.zeros_like(shift_scr)
    o_ref[0:N_COL] = jnp.zeros((N_COL,) + o_ref.shape[1:], F32)
    fwd_tile = lax.broadcasted_iota(jnp.int32, (tile, LANE), 1) % (2 * n_heads) < n_heads
    pos = lax.broadcasted_iota(jnp.int32, (tile, LANE), 0) % CHUNK

    def local(t, _):
        rows = pl.ds(pl.multiple_of(t * tile, tile), tile)
        g = gi_ref[rows, :]
        lf = _log_sigmoid(gf_ref[rows, :])
        b = jnp.where(fwd_tile, _chunk_scan(lf, shift_scr, pos, jnp.add, 0.0, False),
                      _chunk_scan(lf, shift_scr, pos, jnp.add, 0.0, True))
        a = g - b
        cm = jnp.where(fwd_tile, _chunk_scan(a, shift_scr, pos, jnp.maximum, -jnp.inf, False),
                       _chunk_scan(a, shift_scr, pos, jnp.maximum, -jnp.inf, True))
        b_scr[rows, :] = b
        cm_scr[rows, :] = cm
        o_ref[N_COL, rows, :] = a
        return 0

    lax.fori_loop(0, p_rows // tile, local, 0)

    fwd_chunk = lax.broadcasted_iota(jnp.int32, (CHUNK, LANE), 1) % (2 * n_heads) < n_heads

    def step(c, m, forward):
        rows = pl.ds(pl.multiple_of(c * CHUNK, CHUNK), CHUNK)
        end_row = CHUNK - 1 if forward else 0
        mine = fwd_chunk if forward else jnp.logical_not(fwd_chunk)
        b = b_scr[rows, :]
        a = o_ref[N_COL, rows, :]
        cm = cm_scr[rows, :]
        b_end = b[end_row:end_row + 1, :]
        m_t = b + jnp.maximum(m, cm)
        m_new = jnp.maximum(b_end + m, b_end + cm[end_row:end_row + 1, :])
        vals = (b - m_t, jnp.exp(b + m - m_t), jnp.exp(-m_t), jnp.exp(b_end + a - m_new),
                jnp.broadcast_to(jnp.exp(b_end + m - m_new), (CHUNK, LANE)))
        for q, val in enumerate(vals):
            o_ref[q, rows, :] = jnp.where(mine, val, o_ref[q, rows, :])
        return m_new

    def body(j, carry):
        mf, mb = carry
        mf = step(j, mf, True)
        cb = jnp.where(j < n_ctx_chunks, n_ctx_chunks - 1 - j, n_chunks - 1 + n_ctx_chunks - j)
        mb = step(cb, mb, False)
        return mf, mb

    zm = jnp.zeros((1, LANE), F32)
    lax.fori_loop(0, n_chunks, body, (zm, zm))


def _mlstm_gates(gates, *, batch, p_rows, ctx, n_heads):
    width = 2 * n_heads
    assert batch * width <= LANE

    def lanes_by_sample(lane0):
        g = gates[:, lane0:lane0 + width].reshape(batch, p_rows, width).transpose(1, 0, 2)
        return jnp.pad(g.reshape(p_rows, batch * width), ((0, 0), (0, LANE - batch * width)))

    tile = ctx
    kern = functools.partial(_mlstm_gate_kernel, n_heads=n_heads, n_ctx_chunks=ctx // CHUNK, tile=tile)
    return pl.pallas_call(
        kern,
        grid=(1,),
        in_specs=[pl.BlockSpec((p_rows, LANE), lambda b: (0, 0)),
                  pl.BlockSpec((p_rows, LANE), lambda b: (0, 0))],
        out_specs=pl.BlockSpec((N_GATE_OUT, p_rows, LANE), lambda b: (0, 0, 0)),
        out_shape=jax.ShapeDtypeStruct((N_GATE_OUT, p_rows, LANE), F32),
        scratch_shapes=[pltpu.VMEM((p_rows, LANE), F32), pltpu.VMEM((p_rows, LANE), F32),
                        pltpu.VMEM((tile + 2 * CHUNK, LANE), F32)],
        compiler_params=_params(("arbitrary",)),
        name="mlstm_gates",
    )(lanes_by_sample(0), lanes_by_sample(GATE_F_LANE))


def _mlstm_chunk(q, k, v, col, a_row, c_scr, n, d, reverse):
    ch = q.shape[0]
    t_idx = lax.broadcasted_iota(jnp.int32, (ch, ch), 0)
    s_idx = lax.broadcasted_iota(jnp.int32, (ch, ch), 1)
    seen = (s_idx >= t_idx) if reverse else (s_idx <= t_idx)
    c0 = d * N_COL
    row_term = col[:, c0:c0 + 1]
    e_inter = col[:, c0 + 1:c0 + 2]
    floor = col[:, c0 + 2:c0 + 3]
    wg_col = col[:, c0 + 3:c0 + 4]
    e_prev = col[0:1, c0 + 4:c0 + 5]

    w = jnp.where(seen, jnp.exp(row_term + a_row), 0.0)
    s = lax.dot_general(q, k, (((1,), (1,)), ((), ())), preferred_element_type=F32) * w
    c_old = c_scr[...]
    num = (e_inter * jnp.dot(q, c_old.astype(BF16), preferred_element_type=F32)
           + jnp.dot(s.astype(BF16), v, preferred_element_type=F32))
    qn = jnp.sum(q.astype(F32) * n, axis=1, keepdims=True)
    den = e_inter * qn + jnp.sum(s, axis=1, keepdims=True)
    h = num * (1.0 / jnp.maximum(jnp.abs(den), floor))

    vw = (wg_col * v.astype(F32)).astype(BF16)
    c_scr[...] = e_prev * c_old + lax.dot_general(k, vw, (((0,), (0,)), ((), ())),
                                                  preferred_element_type=F32)
    n_new = e_prev * n + jnp.sum(wg_col * k.astype(F32), axis=0, keepdims=True)
    return h, n_new


def _mlstm_core_kernel(q_ref, k_ref, v_ref, o_ref, gr_ref, gc_ref, y_ref,
                       hf_scr, hb_scr, cf_scr, cb_scr, *, n_ctx_chunks, out_rows):
    n_chunks = gr_ref.shape[0]
    dqk = q_ref.shape[1]
    cf_scr[...] = jnp.zeros_like(cf_scr)
    cb_scr[...] = jnp.zeros_like(cb_scr)

    def run(c, c_scr, h_scr, n, d, reverse):
        rows = pl.ds(pl.multiple_of(c * CHUNK, CHUNK), CHUNK)
        h, n = _mlstm_chunk(q_ref[rows, :], k_ref[rows, :], v_ref[rows, :], gc_ref[rows, :],
                            gr_ref[c][d:d + 1, :], c_scr, n, d, reverse)
        h_scr[rows, :] = h
        return n

    def body(j, carry):
        nf, nb = carry
        nf = run(j, cf_scr, hf_scr, nf, 0, False)
        cb = jnp.where(j < n_ctx_chunks, n_ctx_chunks - 1 - j, n_chunks - 1 + n_ctx_chunks - j)
        nb = run(cb, cb_scr, hb_scr, nb, 1, True)
        return nf, nb

    zn = jnp.zeros((1, dqk), F32)
    lax.fori_loop(0, n_chunks, body, (zn, zn), unroll=2)

    def finish(r, _):
        rows = pl.ds(pl.multiple_of(r * out_rows, out_rows), out_rows)
        h = hf_scr[rows, :] + hb_scr[rows, :]
        hn = h * lax.rsqrt(jnp.mean(h * h, axis=-1, keepdims=True) + RMS_EPS)
        y_ref[rows, :] = (o_ref[rows, :].astype(F32) * hn).astype(BF16)
        return 0

    lax.fori_loop(0, y_ref.shape[0] // out_rows, finish, 0)


def _mlstm_core(z, gates_r, gates_c, *, batch, p_rows, ctx, n_heads, dqk, dv):
    n_rows = z.shape[0]
    n_chunks = p_rows // CHUNK
    kern = functools.partial(_mlstm_core_kernel, n_ctx_chunks=ctx // CHUNK, out_rows=ctx)
    return pl.pallas_call(
        kern,
        grid=(batch, n_heads),
        in_specs=[pl.BlockSpec((p_rows, dqk), lambda b, h: (b, h)),
                  pl.BlockSpec((p_rows, dqk), lambda b, h: (b, n_heads + h)),
                  pl.BlockSpec((p_rows, dv), lambda b, h: (b, n_heads + h)),
                  pl.BlockSpec((p_rows, dv), lambda b, h: (b, 2 * n_heads + h)),
                  pl.BlockSpec((None, None, n_chunks, 2, CHUNK), lambda b, h: (b, h, 0, 0, 0)),
                  pl.BlockSpec((None, None, p_rows, 2 * N_COL), lambda b, h: (b, h, 0, 0))],
        out_specs=pl.BlockSpec((p_rows, dv), lambda b, h: (b, h)),
        out_shape=jax.ShapeDtypeStruct((n_rows, n_heads * dv), BF16),
        scratch_shapes=[pltpu.VMEM((p_rows, dv), F32), pltpu.VMEM((p_rows, dv), F32),
                        pltpu.VMEM((dqk, dv), F32), pltpu.VMEM((dqk, dv), F32)],
        compiler_params=_params(("parallel", "parallel")),
        name="mlstm_core",
    )(z, z, z, z, gates_r, gates_c)


def _gate_layouts(og, *, batch, p_rows, n_heads):
    g = og[:, :, :batch * 2 * n_heads].reshape(N_GATE_OUT, p_rows, batch, 2, n_heads)
    cols = g[:N_COL].transpose(2, 4, 1, 3, 0).reshape(batch, n_heads, p_rows, 2 * N_COL)
    rows = g[N_COL].reshape(p_rows // CHUNK, CHUNK, batch, 2, n_heads).transpose(2, 4, 0, 3, 1)
    return rows, cols


def _halo_up_kernel(h_ref, w_ref, o_ref):
    o_ref[...] = jnp.dot(h_ref[...], w_ref[...], preferred_element_type=F32)


def _halo_up(h, w_up, *, layer, n_rows, tm):
    d = h.shape[1]
    n = w_up.shape[2]
    nt = n_rows // tm
    ht = h.reshape(h.shape[0] // tm, tm, d)
    first, last = ht[:nt, 0], ht[:nt, tm - 1]
    prev = jnp.concatenate([last[:1], last[:nt - 1]], axis=0)
    nxt = jnp.concatenate([first[1:], first[nt - 1:]], axis=0)
    hh = jnp.stack([prev, nxt], axis=1).reshape(2 * nt, d)
    rows = 2 * nt
    tn = 2 * FFN_CHUNK
    uh = pl.pallas_call(
        _halo_up_kernel,
        grid=(n // tn,),
        in_specs=[pl.BlockSpec((rows, d), lambda j: (0, 0)),
                  _layer_spec((d, tn), layer, lambda j: (0, j))],
        out_specs=pl.BlockSpec((rows, tn), lambda j: (0, j)),
        out_shape=jax.ShapeDtypeStruct((rows, n), F32),
        compiler_params=_params(("parallel",)),
        name="ffn_halo",
    )(hh, w_up)
    return uh.reshape(nt, 2, n)


def _ffn_kernel(h_ref, x_ref, gt_ref, wa0_ref, wb0_ref, wa1_ref, wb1_ref,
                ha0_ref, hb0_ref, ha1_ref, hb1_ref, cwa_ref, cwb_ref, cba_ref, cbb_ref, wd_ref,
                gf_ref, o_ref, acc_scr, u0_scr, u1_scr, a0_scr, a1_scr,
                *, sub, p_rows, ctx, final_norm):
    i = pl.program_id(0)
    j = pl.program_id(1)
    tm = o_ref.shape[0]
    nsb = tm // sub
    fh = wa0_ref.shape[1]
    lt = fh // LANE
    seg = sub + 2 * HALO

    @pl.when(j == 0)
    def _():
        acc_scr[...] = jnp.zeros_like(acc_scr)

    def edge_open(k):
        pos = (i * tm + k * sub) % p_rows
        return jnp.where((pos == 0) | (pos == ctx), 0.0, 1.0)

    opens = [edge_open(k) for k in range(nsb + 1)]

    def up(w_refs, halo_refs, u_scr):
        for br, (w_ref, halo_ref) in enumerate(zip(w_refs, halo_refs)):
            ue = jnp.dot(h_ref[...], w_ref[...], preferred_element_type=F32)
            halo = halo_ref[...]
            for s in range(nsb):
                r0 = s * sub
                b0 = s * seg
                for t in range(lt):
                    lanes = slice(t * LANE, (t + 1) * LANE)
                    tt = br * lt + t
                    before = (jnp.broadcast_to(halo[0:1, lanes], (HALO, LANE)) if s == 0
                              else ue[r0 - HALO:r0, lanes])
                    after = (jnp.broadcast_to(halo[1:2, lanes], (HALO, LANE)) if s == nsb - 1
                             else ue[r0 + sub:r0 + sub + HALO, lanes])
                    u_scr[tt, b0:b0 + HALO, :] = before * opens[s]
                    u_scr[tt, b0 + HALO:b0 + HALO + sub, :] = ue[r0:r0 + sub, lanes]
                    u_scr[tt, b0 + HALO + sub:b0 + seg, :] = after * opens[s + 1]

    def gate(c, u_scr, act_scr):
        cols = slice(c * fh, (c + 1) * fh)
        cws = (cwa_ref[:, cols], cwb_ref[:, cols])
        cbs = (cba_ref[:, cols], cbb_ref[:, cols])

        def conv_piece(br, t, r0):
            lanes = slice(t * LANE, (t + 1) * LANE)
            tt = br * lt + t + jnp.minimum(j, 0)
            cw, cb = cws[br], cbs[br]
            cur = u_scr[tt, r0:r0 + GATE_ROWS, :]
            prv = u_scr[tt, r0 - 1:r0 - 1 + GATE_ROWS, :]
            nxt = u_scr[tt, r0 + 1:r0 + 1 + GATE_ROWS, :]
            return cb[:, lanes] + prv * cw[0:1, lanes] + cur * cw[1:2, lanes] + nxt * cw[2:3, lanes]

        for s in range(nsb):
            for r in range(sub // GATE_ROWS):
                r0 = s * seg + HALO + r * GATE_ROWS
                o0 = s * sub + r * GATE_ROWS
                for t in range(lt):
                    a = conv_piece(0, t, r0)
                    b = conv_piece(1, t, r0)
                    act_scr[o0:o0 + GATE_ROWS, t * LANE:(t + 1) * LANE] = (a * _sigmoid(a) * b).astype(BF16)

    def down(c, act_scr):
        acc_scr[...] += jnp.dot(act_scr[...], wd_ref[c * fh:(c + 1) * fh, :], preferred_element_type=F32)

    up((wa0_ref, wb0_ref), (ha0_ref, hb0_ref), u0_scr)
    up((wa1_ref, wb1_ref), (ha1_ref, hb1_ref), u1_scr)
    gate(0, u0_scr, a0_scr)
    down(0, a0_scr)
    gate(1, u1_scr, a1_scr)
    down(1, a1_scr)

    @pl.when(j == pl.num_programs(1) - 1)
    def _():
        for s in range(nsb):
            if not final_norm:
                rows = slice(s * sub, (s + 1) * sub)
                o_ref[rows, :] = x_ref[rows, :] + gt_ref[s] * acc_scr[rows, :]
                continue
            for r in range(s * sub, (s + 1) * sub, NORM_ROWS):
                rows = slice(r, r + NORM_ROWS)
                xn = x_ref[rows, :] + gt_ref[s] * acc_scr[rows, :]
                ms = jnp.mean(xn * xn, axis=-1, keepdims=True)
                o_ref[rows, :] = xn * lax.rsqrt(ms + RMS_EPS) * gf_ref[...]


def _conv_ffn(h, u, mods, w_up, conv_w, conv_b, w_down, g_final, *, layer, mod_layer, n_rows, tm, sub,
              p_rows, ctx, final_norm):
    d = u.shape[1]
    d_ff = w_down.shape[1]
    fc = FFN_CHUNK
    assert d_ff % fc == 0
    fh = fc // 2
    nfc = d_ff // fc
    nsb = tm // sub
    kern = functools.partial(_ffn_kernel, sub=sub, p_rows=p_rows, ctx=ctx, final_norm=final_norm)
    u_shape = (fc // LANE, nsb * (sub + 2 * HALO), LANE)
    halo = _halo_up(h, w_up, layer=layer, n_rows=n_rows, tm=tm)

    def half_specs(shape, lead):
        return [pl.BlockSpec(shape, lambda i, j, c=c: lead(i) + (c + 2 * j,))
                for c in (0, 2 * nfc, 1, 2 * nfc + 1)]

    return pl.pallas_call(
        kern,
        grid=(n_rows // tm, nfc),
        in_specs=[pl.BlockSpec((tm, d), lambda i, j: (i, 0)),
                  pl.BlockSpec((tm, d), lambda i, j: (i, 0)),
                  _mod_spec(nsb, d, mod_layer, MOD_GATE2)]
        + half_specs((None, d, fh), lambda i: (layer, 0))
        + half_specs((None, 2, fh), lambda i: (i, 0))
        + [_layer_spec((CONV_W, fc), layer, lambda i, j: (0, j)),
           _layer_spec((CONV_W, fc), layer, lambda i, j: (0, nfc + j)),
           _layer_spec((1, fc), layer, lambda i, j: (0, j)),
           _layer_spec((1, fc), layer, lambda i, j: (0, nfc + j)),
           _layer_spec((fc, d), layer, lambda i, j: (j, 0)),
           pl.BlockSpec((1, d), lambda i, j: (0, 0))],
        out_specs=pl.BlockSpec((tm, d), lambda i, j: (i, 0)),
        out_shape=jax.ShapeDtypeStruct((n_rows, d), F32),
        scratch_shapes=[pltpu.VMEM((tm, d), F32), pltpu.VMEM(u_shape, F32), pltpu.VMEM(u_shape, F32),
                        pltpu.VMEM((tm, fh), BF16), pltpu.VMEM((tm, fh), BF16)],
        compiler_params=_params(("parallel", "arbitrary")),
        name="conv_ffn",
    )(h, u, mods, w_up, w_up, w_up, w_up, halo, halo, halo, halo,
      conv_w, conv_w, conv_b, conv_b, w_down, g_final)


def _rope_tables(batch, seq, ctx):
    rows = seq // GRID_W
    row = jnp.repeat(jnp.arange(rows, dtype=F32), GRID_W)
    col = jnp.tile(jnp.arange(GRID_W, dtype=F32), rows)
    n_freq = HEAD_DIM // 4
    inv_freq = ROPE_THETA ** (-jnp.arange(n_freq, dtype=F32) / n_freq)
    ang = jnp.concatenate([row[:, None] * inv_freq, col[:, None] * inv_freq], axis=-1)
    ang = jnp.concatenate([ang, ang], axis=-1)
    sign = jnp.concatenate([-jnp.ones((HEAD_DIM // 2,), F32), jnp.ones((HEAD_DIM // 2,), F32)])
    cos = jnp.concatenate([jnp.ones((ctx, HEAD_DIM), F32), jnp.cos(ang)], axis=0)
    sin = jnp.concatenate([jnp.zeros((ctx, HEAD_DIM), F32), jnp.sin(ang) * sign], axis=0)
    return jnp.tile(cos, (batch, 1)), jnp.tile(sin, (batch, 1))


def kernel(x, c, ctx, c_ctx, w_mod, b_mod, g_mix, g_ffn, attn_w_qkv, attn_sink, attn_w_o,
           mlstm_w_in, mlstm_b_in, mlstm_g_head, mlstm_w_o, ffn_w_up, ffn_conv_w, ffn_conv_b,
           ffn_w_down, g_final):
    batch, seq, d = x.shape
    lc = ctx.shape[1]
    depth = w_mod.shape[0]
    p_rows = lc + seq
    sub = lc
    assert seq % sub == 0 and sub % WINDOW == 0 and sub % CHUNK == 0 and seq % GRID_W == 0
    n_sub = batch * p_rows // sub
    tm = 2 * sub if n_sub % 2 == 0 else sub
    n_q = attn_sink.shape[1]
    n_kv = (attn_w_qkv.shape[2] // HEAD_DIM - n_q) // 2
    n_heads = (mlstm_w_in.shape[2] - 3 * d) // 4
    dv = d // n_heads
    dqk = dv // 2
    main_cols = 3 * d

    u = jnp.concatenate([ctx, x], axis=1).reshape(batch * p_rows, d)

    n_c = batch + 1
    c_rows = -(-n_c // 8) * 8
    cvec = jnp.concatenate([c, c_ctx[None, :], jnp.zeros((c_rows - n_c, d), F32)], axis=0)
    mods = _modulation(cvec, w_mod, b_mod)
    mods_sb = jnp.concatenate(
        [jnp.broadcast_to(mods[:, None, batch:batch + 1], (depth, batch, lc // sub, 6 * d)),
         jnp.broadcast_to(mods[:, :batch, None], (depth, batch, seq // sub, 6 * d))],
        axis=2).reshape(depth, n_sub, 6, 1, d)

    mods_lat = jnp.broadcast_to(mods[depth - 1:, :batch, None], (1, batch, seq // sub, 6 * d)
                                ).reshape(1, batch * seq // sub, 6, 1, d)

    cos_u, sin_u = _rope_tables(batch, seq, lc)

    w_qkv = attn_w_qkv.astype(BF16)
    w_attn_o = attn_w_o.astype(BF16)
    w_in = mlstm_w_in.astype(BF16)
    w_mlstm_o = mlstm_w_o.astype(BF16)
    w_up = ffn_w_up.astype(BF16)
    w_down = ffn_w_down.astype(BF16)
    g_mix3 = g_mix.reshape(depth, 1, d)
    g_ffn3 = g_ffn.reshape(depth, 1, d)
    b_in3 = mlstm_b_in.reshape(mlstm_b_in.shape[0], 1, -1)
    conv_b3 = ffn_conv_b.reshape(depth, 1, -1)
    g_fin = g_final.reshape(1, d)

    for i in range(depth):
        jm = i // 2
        if i % 2 == 0:
            qkv = _qkv_proj(u, g_mix3, mods_sb, w_qkv, cos_u, sin_u, layer=i, w_layer=jm,
                            tm=tm, sub=sub, n_q=n_q, n_kv=n_kv)
            y = _attention(qkv, attn_sink[jm], batch=batch, seq=seq, ctx=lc, n_q=n_q, n_kv=n_kv)
            w_o = w_attn_o
        else:
            wgt = mlstm_w_in[jm][:, main_cols:].reshape(d, 4, n_heads)
            bgt = mlstm_b_in[jm][main_cols:].reshape(4, n_heads)
            lane_pad = GATE_F_LANE - 2 * n_heads
            wg = jnp.concatenate([wgt[:, 0], wgt[:, 2], jnp.zeros((d, lane_pad), F32),
                                  wgt[:, 1], wgt[:, 3], jnp.zeros((d, lane_pad), F32)], axis=1).astype(BF16)
            bg = jnp.concatenate([bgt[0], bgt[2], jnp.zeros((lane_pad,), F32),
                                  bgt[1], bgt[3], jnp.zeros((lane_pad,), F32)]).reshape(1, LANE)
            gh = jnp.concatenate([jnp.ones((1, 2 * d), F32), mlstm_g_head[jm].reshape(1, d)], axis=1)
            z, gates = _mlstm_in_proj(u, g_mix3, mods_sb, w_in, b_in3, wg, bg, gh, layer=i, w_layer=jm,
                                      tm=tm, sub=sub, dqk=dqk)
            og = _mlstm_gates(gates, batch=batch, p_rows=p_rows, ctx=lc, n_heads=n_heads)
            gates_r, gates_c = _gate_layouts(og, batch=batch, p_rows=p_rows, n_heads=n_heads)
            y = _mlstm_core(z, gates_r, gates_c,
                            batch=batch, p_rows=p_rows, ctx=lc, n_heads=n_heads, dqk=dqk, dv=dv)
            w_o = w_mlstm_o
        if i < depth - 1:
            u, h2 = _proj_residual(y, w_o, u, mods_sb, g_ffn3, layer=i, w_layer=jm, tm=tm, sub=sub)
            u = _conv_ffn(h2, u, mods_sb, w_up, ffn_conv_w, conv_b3, w_down, g_fin, layer=i, mod_layer=i,
                          n_rows=batch * p_rows, tm=tm, sub=sub, p_rows=p_rows, ctx=lc,
                          final_norm=False)
        else:
            u, h2 = _proj_residual(y, w_o, u, mods_sb, g_ffn3, layer=i, w_layer=jm, tm=tm, sub=sub,
                                   latent_only=(p_rows // sub, lc // sub))
            u = _conv_ffn(h2, u, mods_lat, w_up, ffn_conv_w, conv_b3, w_down, g_fin, layer=i, mod_layer=0,
                          n_rows=batch * seq, tm=tm, sub=sub, p_rows=seq, ctx=0,
                          final_norm=True)
    return u.reshape(batch, seq, d)
```

```python
import functools

import numpy as np
import jax
import jax.numpy as jnp
from jax import lax
from jax.experimental import pallas as pl
from jax.experimental.pallas import tpu as pltpu

F32 = jnp.float32
BF16 = jnp.bfloat16

RMS_EPS = 1e-6
HEAD_DIM = 128
WINDOW = 128
GRID_W = 64
ROPE_THETA = 10000.0
NEG_INF = -1e30
CHUNK = 128
CONV_W = 3
LANE = 128
HALO = 8
BF16_ROWS = 16
FFN_CHUNK = 512
GATE_ROWS = 64
NORM_ROWS = 16
PROJ_TILE = 3072
MOD_TILE = 1024
MOD_SHIFT1, MOD_SCALE1, MOD_GATE1, MOD_SHIFT2, MOD_SCALE2, MOD_GATE2 = range(6)
V7X_VMEM_BYTES = 64 * 1024 * 1024
VMEM_LIMIT = V7X_VMEM_BYTES * 7 // 8


def _params(sem):
    return pltpu.CompilerParams(dimension_semantics=sem, vmem_limit_bytes=VMEM_LIMIT)


def _sigmoid(v):
    return 1.0 / (1.0 + jnp.exp(-v))


def _log_sigmoid(v):
    return jnp.minimum(v, 0.0) - jnp.log(1.0 + jnp.exp(-jnp.abs(v)))


def _mod_spec(nsb, d, layer, kind):
    return pl.BlockSpec((None, nsb, None, 1, d), lambda i, *_: (layer, i, kind, 0, 0))


def _layer_spec(shape, layer, index):
    return pl.BlockSpec((None,) + shape, lambda *g: (layer,) + index(*g))


def _norm_mod(xv, g, scale, shift):
    ms = jnp.mean(xv * xv, axis=-1, keepdims=True)
    y = xv * lax.rsqrt(ms + RMS_EPS)
    return (y * g) * (1.0 + scale) + shift


def _norm_mod_to(h_scr, x_ref, g_ref, sc_ref, sh_ref, sub):
    for s in range(x_ref.shape[0] // sub):
        gain = g_ref[...] * (1.0 + sc_ref[s])
        shift = sh_ref[s]
        for r in range(s * sub, (s + 1) * sub, NORM_ROWS):
            xv = x_ref[r:r + NORM_ROWS, :]
            ms = jnp.mean(xv * xv, axis=-1, keepdims=True)
            h_scr[r:r + NORM_ROWS, :] = (xv * lax.rsqrt(ms + RMS_EPS) * gain + shift).astype(BF16)


def _mod_kernel(c_ref, w_ref, b_ref, o_ref):
    cv = c_ref[...]
    s = (cv * _sigmoid(cv)).astype(BF16)
    o_ref[...] = jnp.dot(s, w_ref[...].astype(BF16), preferred_element_type=F32) + b_ref[...]


def _modulation(cvec, w_mod, b_mod):
    depth, d, n = w_mod.shape
    r = cvec.shape[0]
    tn = MOD_TILE if n % MOD_TILE == 0 else n
    return pl.pallas_call(
        _mod_kernel,
        grid=(depth, n // tn),
        in_specs=[pl.BlockSpec((r, d), lambda i, j: (0, 0)),
                  pl.BlockSpec((None, d, tn), lambda i, j: (i, 0, j)),
                  pl.BlockSpec((None, 1, tn), lambda i, j: (i, 0, j))],
        out_specs=pl.BlockSpec((None, r, tn), lambda i, j: (i, 0, j)),
        out_shape=jax.ShapeDtypeStruct((depth, r, n), F32),
        compiler_params=_params(("parallel", "parallel")),
        name="modulation",
    )(cvec, w_mod, b_mod.reshape(depth, 1, n))


def _qkv_kernel(x_ref, xc_ref, g_ref, sc_ref, sh_ref, w_ref, cos_ref, sin_ref, o_ref, h_scr,
                *, sub, n_q, n_rope, q_scale, split):
    j = pl.program_id(1)
    tm, tn = o_ref.shape
    from_ctx = False if split is None else pl.program_id(0) % split[0] < split[1]

    @pl.when((j == 0) & jnp.logical_not(from_ctx))
    def _():
        _norm_mod_to(h_scr, x_ref, g_ref, sc_ref, sh_ref, sub)

    if split is not None:
        @pl.when((j == 0) & from_ctx)
        def _():
            _norm_mod_to(h_scr, xc_ref, g_ref, sc_ref, sh_ref, sub)

    acc = jnp.dot(h_scr[...], w_ref[...], preferred_element_type=F32)
    cos = cos_ref[...]
    sin = sin_ref[...]
    heads_per_tile = tn // HEAD_DIM
    for l in range(heads_per_tile):
        t = acc[:, l * HEAD_DIM:(l + 1) * HEAD_DIM]
        head = j * heads_per_tile + l
        tr = t * cos + pltpu.roll(t, HEAD_DIM // 2, 1) * sin
        t = jnp.where(head < n_rope, tr, t)
        t = t * jnp.where(head < n_q, q_scale, 1.0)
        o_ref[:, l * HEAD_DIM:(l + 1) * HEAD_DIM] = t.astype(BF16)


def _split_row_specs(sub, d, split):
    spb, cs = split
    lat = lambda i, *_: (jnp.maximum((i // spb) * (spb - cs) + i % spb - cs, 0), 0)
    con = lambda i, *_: ((i // spb) * cs + jnp.minimum(i % spb, cs - 1), 0)
    return [pl.BlockSpec((sub, d), lat), pl.BlockSpec((sub, d), con)]


def _qkv_proj(u, g, mods, w, cos_u, sin_u, *, layer, w_layer, tm, sub, n_q, n_kv, split=None, xc=None):
    d = u.shape[1]
    if split is None:
        n_rows = u.shape[0]
        xc = u
        row_specs = [pl.BlockSpec((tm, d), lambda i, j: (i, 0)), pl.BlockSpec((HALO, d), lambda i, j: (0, 0))]
    else:
        n_rows = u.shape[0] + xc.shape[0]
        tm = sub
        row_specs = _split_row_specs(sub, d, split)
    n = w.shape[2]
    tn = n if n <= PROJ_TILE else PROJ_TILE
    assert n % tn == 0
    nsb = tm // sub
    kern = functools.partial(_qkv_kernel, sub=sub, n_q=n_q, n_rope=n_q + n_kv,
                             q_scale=HEAD_DIM ** -0.5, split=split)
    return pl.pallas_call(
        kern,
        grid=(n_rows // tm, n // tn),
        in_specs=row_specs
        + [_layer_spec((1, d), layer, lambda i, j: (0, 0)),
           _mod_spec(nsb, d, layer, MOD_SCALE1),
           _mod_spec(nsb, d, layer, MOD_SHIFT1),
           _layer_spec((d, tn), w_layer, lambda i, j: (0, j)),
           pl.BlockSpec((tm, HEAD_DIM), lambda i, j: (i, 0)),
           pl.BlockSpec((tm, HEAD_DIM), lambda i, j: (i, 0))],
        out_specs=pl.BlockSpec((tm, tn), lambda i, j: (i, j)),
        out_shape=jax.ShapeDtypeStruct((n_rows, n), BF16),
        scratch_shapes=[pltpu.VMEM((tm, d), BF16)],
        compiler_params=_params(("parallel", "arbitrary")),
        name="attn_qkv",
    )(u, xc, g, mods, mods, w, cos_u, sin_u)


def _softmax_pv(s, sink, v):
    m = jnp.maximum(jnp.max(s, axis=-1, keepdims=True), sink)
    p = jnp.exp(s - m)
    l = jnp.sum(p, axis=-1, keepdims=True) + jnp.exp(sink - m)
    o = jnp.dot(p.astype(BF16), v, preferred_element_type=F32)
    return o * (1.0 / l)


def _stack_heads(q, group):
    return jnp.concatenate([q[:, g * HEAD_DIM:(g + 1) * HEAD_DIM] for g in range(group)], axis=0)


def _attn_x_kernel(q_ref, kp_ref, ko_ref, kn_ref, kc_ref, vp_ref, vo_ref, vn_ref, vc_ref,
                   mask_ref, sink_ref, o_ref, *, group, n_kv):
    blk = q_ref.shape[0]
    valid = mask_ref[...] > 0.0

    def scores(h):
        kv = slice(h * HEAD_DIM, (h + 1) * HEAD_DIM)
        qs = _stack_heads(q_ref[:, h * group * HEAD_DIM:(h + 1) * group * HEAD_DIM], group)
        k = jnp.concatenate([kp_ref[:, kv], ko_ref[:, kv], kn_ref[:, kv], kc_ref[:, kv]], axis=0)
        s = lax.dot_general(qs, k, (((1,), (1,)), ((), ())), preferred_element_type=F32)
        return jnp.concatenate([jnp.where(valid, s[:, :3 * blk], NEG_INF), s[:, 3 * blk:]], axis=1)

    def finish(h, s):
        kv = slice(h * HEAD_DIM, (h + 1) * HEAD_DIM)
        v = jnp.concatenate([vp_ref[:, kv], vo_ref[:, kv], vn_ref[:, kv], vc_ref[:, kv]], axis=0)
        o = _softmax_pv(s, sink_ref[h], v)
        for g in range(group):
            c0 = (h * group + g) * HEAD_DIM
            o_ref[:, c0:c0 + HEAD_DIM] = o[g * blk:(g + 1) * blk, :].astype(BF16)

    s_next = scores(0)
    for h in range(n_kv):
        s_cur = s_next
        if h + 1 < n_kv:
            s_next = scores(h + 1)
        finish(h, s_cur)


def _attn_c_kernel(q_ref, kc_ref, vc_ref, sink_ref, prev_ref, o_ref, *, group, n_kv):
    del prev_ref
    lc = q_ref.shape[0]

    def scores(h):
        qs = _stack_heads(q_ref[:, h * group * HEAD_DIM:(h + 1) * group * HEAD_DIM], group)
        return lax.dot_general(qs, kc_ref[:, h * HEAD_DIM:(h + 1) * HEAD_DIM], (((1,), (1,)), ((), ())),
                               preferred_element_type=F32)

    s_next = scores(0)
    for h in range(n_kv):
        s_cur = s_next
        if h + 1 < n_kv:
            s_next = scores(h + 1)
        o = _softmax_pv(s_cur, sink_ref[h], vc_ref[:, h * HEAD_DIM:(h + 1) * HEAD_DIM])
        for g in range(group):
            c0 = (h * group + g) * HEAD_DIM
            o_ref[:, c0:c0 + HEAD_DIM] = o[g * lc:(g + 1) * lc, :].astype(BF16)


def _window_mask(group, blk, n_blocks):
    qi = np.arange(group * blk)[:, None] % blk
    kj = np.arange(3 * blk)[None, :]
    band = (kj - qi >= 0) & (kj - qi <= 2 * WINDOW)
    first = band & (kj >= blk)
    last = band & (kj < 2 * blk)
    kinds = [first & last if n_blocks == 1 else first, band, last]
    return np.stack(kinds).astype(np.float32)


def _attention(qkv, sink, *, batch, seq, ctx, n_q, n_kv):
    n_rows = qkv.shape[0]
    group = n_q // n_kv
    blk = WINDOW
    p_rows = ctx + seq
    bpb = p_rows // blk
    cb = ctx // blk
    nb = seq // blk
    last_blk = n_rows // blk - 1
    qw = n_q * HEAD_DIM
    kw = n_kv * HEAD_DIM
    assert qw % kw == 0
    k_col = qw // kw
    spc = p_rows // ctx
    sink_g = sink.reshape(n_kv, group, 1).astype(F32)
    sink_x = jnp.broadcast_to(sink_g[:, :, None, :], (n_kv, group, blk, 1)).reshape(n_kv, group * blk, 1)
    sink_c = jnp.broadcast_to(sink_g[:, :, None, :], (n_kv, group, ctx, 1)).reshape(n_kv, group * ctx, 1)
    mask = jnp.asarray(_window_mask(group, blk, nb))

    def kv_specs(col):
        return [pl.BlockSpec((blk, kw), lambda b, n: (jnp.maximum(b * bpb + cb + n - 1, 0), col)),
                pl.BlockSpec((blk, kw), lambda b, n: (b * bpb + cb + n, col)),
                pl.BlockSpec((blk, kw), lambda b, n: (jnp.minimum(b * bpb + cb + n + 1, last_blk), col)),
                pl.BlockSpec((ctx, kw), lambda b, n: (b * spc, col))]

    out_x = pl.pallas_call(
        functools.partial(_attn_x_kernel, group=group, n_kv=n_kv),
        grid=(batch, nb),
        in_specs=[pl.BlockSpec((blk, qw), lambda b, n: (b * bpb + cb + n, 0))]
        + kv_specs(k_col) + kv_specs(k_col + 1)
        + [pl.BlockSpec((None, group * blk, 3 * blk),
                        lambda b, n: (jnp.where(n == 0, 0, jnp.where(n == nb - 1, 2, 1)), 0, 0)),
           pl.BlockSpec((n_kv, group * blk, 1), lambda b, n: (0, 0, 0))],
        out_specs=pl.BlockSpec((blk, qw), lambda b, n: (b * bpb + cb + n, 0)),
        out_shape=jax.ShapeDtypeStruct((n_rows, qw), BF16),
        compiler_params=_params(("parallel", "parallel")),
        name="attn_latent",
    )(qkv, qkv, qkv, qkv, qkv, qkv, qkv, qkv, qkv, mask, sink_x)

    out = pl.pallas_call(
        functools.partial(_attn_c_kernel, group=group, n_kv=n_kv),
        grid=(batch,),
        in_specs=[pl.BlockSpec((ctx, qw), lambda b: (b * spc, 0)),
                  pl.BlockSpec((ctx, kw), lambda b: (b * spc, k_col)),
                  pl.BlockSpec((ctx, kw), lambda b: (b * spc, k_col + 1)),
                  pl.BlockSpec((n_kv, group * ctx, 1), lambda b: (0, 0, 0)),
                  pl.BlockSpec(memory_space=pl.ANY)],
        out_specs=pl.BlockSpec((ctx, qw), lambda b: (b * spc, 0)),
        out_shape=jax.ShapeDtypeStruct((n_rows, qw), BF16),
        input_output_aliases={4: 0},
        compiler_params=_params(("parallel",)),
        name="attn_context",
    )(qkv, qkv, qkv, sink_c, out_x)
    return out


def _proj_res_kernel(y_ref, w_ref, x_ref, xc_ref, gt_ref, g_ref, sc_ref, sh_ref, o_ref, h_ref, *, sub, split):
    acc = jnp.dot(y_ref[...], w_ref[...], preferred_element_type=F32)
    g = g_ref[...]
    for s in range(o_ref.shape[0] // sub):
        rows = slice(s * sub, (s + 1) * sub)
        xv = x_ref[rows, :]
        if split is not None:
            xv = jnp.where(pl.program_id(0) % split[0] < split[1], xc_ref[rows, :], xv)
        xn = xv + gt_ref[s] * acc[rows, :]
        o_ref[rows, :] = xn
        h_ref[rows, :] = _norm_mod(xn, g, sc_ref[s], sh_ref[s]).astype(BF16)


def _proj_residual(y, w, u, mods, g, *, layer, w_layer, tm, sub, latent_only=None, split=None, xc=None):
    d = u.shape[1]
    k = y.shape[1]
    if split is None:
        n_rows = u.shape[0]
        xc = u
        res_specs = None
    else:
        n_rows = u.shape[0] + xc.shape[0]
        tm = sub
        res_specs = _split_row_specs(sub, d, split)
    out_rows, out_map = n_rows, lambda i: (i, 0)
    if latent_only is not None:
        spb, cs = latent_only
        n_lat = n_rows // sub // spb * (spb - cs)
        out_rows = n_lat * sub + tm
        tm = sub
        out_map = lambda i: (jnp.where(i % spb >= cs, (i // spb) * (spb - cs) + i % spb - cs, n_lat), 0)
    nsb = tm // sub
    return pl.pallas_call(
        functools.partial(_proj_res_kernel, sub=sub, split=split),
        grid=(n_rows // tm,),
        in_specs=[pl.BlockSpec((tm, k), lambda i: (i, 0)),
                  _layer_spec((k, d), w_layer, lambda i: (0, 0))]
        + (res_specs or [pl.BlockSpec((tm, d), lambda i: (i, 0)), pl.BlockSpec((HALO, d), lambda i: (0, 0))])
        + [_mod_spec(nsb, d, layer, MOD_GATE1),
                  _layer_spec((1, d), layer, lambda i: (0, 0)),
                  _mod_spec(nsb, d, layer, MOD_SCALE2),
                  _mod_spec(nsb, d, layer, MOD_SHIFT2)],
        out_specs=[pl.BlockSpec((tm, d), out_map),
                   pl.BlockSpec((tm, d), out_map)],
        out_shape=[jax.ShapeDtypeStruct((out_rows, d), F32),
                   jax.ShapeDtypeStruct((out_rows, d), BF16)],
        compiler_params=_params(("arbitrary",)),
        name="proj_residual",
    )(y, w, u, xc, mods, g, mods, mods)


def _mlstm_in_kernel(x_ref, g_ref, sc_ref, sh_ref, w_ref, b_ref, wg_ref, bg_ref, z_ref, gate_ref,
                     h_scr, *, sub, group, k_scale):
    j = pl.program_id(1)
    tm, tn = z_ref.shape

    @pl.when(j == 0)
    def _():
        _norm_mod_to(h_scr, x_ref, g_ref, sc_ref, sh_ref, sub)
        gate_ref[...] = jnp.dot(h_scr[...], wg_ref[...], preferred_element_type=F32) + bg_ref[...]

    for l in range(tn // group):
        cols = slice(l * group, (l + 1) * group)
        acc = jnp.dot(h_scr[...], w_ref[:, cols], preferred_element_type=F32) + b_ref[:, cols]
        is_k = j * (tn // group) + l == 1
        z_ref[:, cols] = (acc * jnp.where(is_k, k_scale, 1.0)).astype(BF16)


def _mlstm_in_proj(u, g, mods, w, b, wg, bg, *, layer, w_layer, tm, sub, dqk):
    n_rows, d = u.shape
    n = 3 * d
    qk_cols = d // 2
    tn = n // 2 if n // 2 <= PROJ_TILE and (n // 2) % qk_cols == 0 else qk_cols
    nsb = tm // sub
    kern = functools.partial(_mlstm_in_kernel, sub=sub, group=qk_cols, k_scale=dqk ** -0.5)
    return pl.pallas_call(
        kern,
        grid=(n_rows // tm, n // tn),
        in_specs=[pl.BlockSpec((tm, d), lambda i, j: (i, 0)),
                  _layer_spec((1, d), layer, lambda i, j: (0, 0)),
                  _mod_spec(nsb, d, layer, MOD_SCALE1),
                  _mod_spec(nsb, d, layer, MOD_SHIFT1),
                  _layer_spec((d, tn), w_layer, lambda i, j: (0, j)),
                  _layer_spec((1, tn), w_layer, lambda i, j: (0, j)),
                  pl.BlockSpec((d, LANE), lambda i, j: (0, 0)),
                  pl.BlockSpec((1, LANE), lambda i, j: (0, 0))],
        out_specs=[pl.BlockSpec((tm, tn), lambda i, j: (i, j)),
                   pl.BlockSpec((tm, LANE), lambda i, j: (i, 0))],
        out_shape=[jax.ShapeDtypeStruct((n_rows, n), BF16),
                   jax.ShapeDtypeStruct((n_rows, LANE), F32)],
        scratch_shapes=[pltpu.VMEM((tm, d), BF16)],
        compiler_params=_params(("parallel", "arbitrary")),
        name="mlstm_in",
    )(u, g, mods, mods, w, b, wg, bg)


GATE_F_LANE = 64
N_GATE_OUT = 6
N_COL = 5


def _chunk_scan(x, scr, pos, op, ident, reverse):
    r = x.shape[0]
    sh = 1
    while sh < CHUNK:
        scr[CHUNK:CHUNK + r, :] = x
        if reverse:
            other = scr[CHUNK + sh:CHUNK + sh + r, :]
            ok = pos < CHUNK - sh
        else:
            other = scr[CHUNK - sh:CHUNK - sh + r, :]
            ok = pos >= sh
        x = op(x, jnp.where(ok, other, ident))
        sh *= 2
    return x


def _mlstm_gate_kernel(gi_ref, gf_ref, o_ref, b_scr, cm_scr, shift_scr, *, n_heads, n_ctx_chunks, tile):
    p_rows = gi_ref.shape[0]
    n_chunks = p_rows // CHUNK
    shift_scr[...] = jnp.zeros_like(shift_scr)
    o_ref[0:N_COL] = jnp.zeros((N_COL,) + o_ref.shape[1:], F32)
    fwd_tile = lax.broadcasted_iota(jnp.int32, (tile, LANE), 1) % (2 * n_heads) < n_heads
    pos = lax.broadcasted_iota(jnp.int32, (tile, LANE), 0) % CHUNK

    def local(t, _):
        rows = pl.ds(pl.multiple_of(t * tile, tile), tile)
        g = gi_ref[rows, :]
        lf = _log_sigmoid(gf_ref[rows, :])
        b = jnp.where(fwd_tile, _chunk_scan(lf, shift_scr, pos, jnp.add, 0.0, False),
                      _chunk_scan(lf, shift_scr, pos, jnp.add, 0.0, True))
        a = g - b
        cm = jnp.where(fwd_tile, _chunk_scan(a, shift_scr, pos, jnp.maximum, -jnp.inf, False),
                       _chunk_scan(a, shift_scr, pos, jnp.maximum, -jnp.inf, True))
        b_scr[rows, :] = b
        cm_scr[rows, :] = cm
        o_ref[N_COL, rows, :] = a
        return 0

    lax.fori_loop(0, p_rows // tile, local, 0)

    fwd_chunk = lax.broadcasted_iota(jnp.int32, (CHUNK, LANE), 1) % (2 * n_heads) < n_heads

    def step(c, m, forward):
        rows = pl.ds(pl.multiple_of(c * CHUNK, CHUNK), CHUNK)
        end_row = CHUNK - 1 if forward else 0
        mine = fwd_chunk if forward else jnp.logical_not(fwd_chunk)
        b = b_scr[rows, :]
        a = o_ref[N_COL, rows, :]
        cm = cm_scr[rows, :]
        b_end = b[end_row:end_row + 1, :]
        m_t = b + jnp.maximum(m, cm)
        m_new = jnp.maximum(b_end + m, b_end + cm[end_row:end_row + 1, :])
        vals = (b - m_t, jnp.exp(b + m - m_t), jnp.exp(-m_t), jnp.exp(b_end + a - m_new),
                jnp.broadcast_to(jnp.exp(b_end + m - m_new), (CHUNK, LANE)))
        for q, val in enumerate(vals):
            o_ref[q, rows, :] = jnp.where(mine, val, o_ref[q, rows, :])
        return m_new

    def body(j, carry):
        mf, mb = carry
        mf = step(j, mf, True)
        cb = jnp.where(j < n_ctx_chunks, n_ctx_chunks - 1 - j, n_chunks - 1 + n_ctx_chunks - j)
        mb = step(cb, mb, False)
        return mf, mb

    zm = jnp.zeros((1, LANE), F32)
    lax.fori_loop(0, n_chunks, body, (zm, zm))


def _mlstm_gates(gates, *, batch, p_rows, ctx, n_heads):
    width = 2 * n_heads
    assert batch * width <= LANE

    def lanes_by_sample(lane0):
        g = gates[:, lane0:lane0 + width].reshape(batch, p_rows, width).transpose(1, 0, 2)
        return jnp.pad(g.reshape(p_rows, batch * width), ((0, 0), (0, LANE - batch * width)))

    tile = ctx
    kern = functools.partial(_mlstm_gate_kernel, n_heads=n_heads, n_ctx_chunks=ctx // CHUNK, tile=tile)
    return pl.pallas_call(
        kern,
        grid=(1,),
        in_specs=[pl.BlockSpec((p_rows, LANE), lambda b: (0, 0)),
                  pl.BlockSpec((p_rows, LANE), lambda b: (0, 0))],
        out_specs=pl.BlockSpec((N_GATE_OUT, p_rows, LANE), lambda b: (0, 0, 0)),
        out_shape=jax.ShapeDtypeStruct((N_GATE_OUT, p_rows, LANE), F32),
        scratch_shapes=[pltpu.VMEM((p_rows, LANE), F32), pltpu.VMEM((p_rows, LANE), F32),
                        pltpu.VMEM((tile + 2 * CHUNK, LANE), F32)],
        compiler_params=_params(("arbitrary",)),
        name="mlstm_gates",
    )(lanes_by_sample(0), lanes_by_sample(GATE_F_LANE))


def _mlstm_chunk(q, k, v, col, a_row, c_scr, n, d, reverse):
    ch = q.shape[0]
    t_idx = lax.broadcasted_iota(jnp.int32, (ch, ch), 0)
    s_idx = lax.broadcasted_iota(jnp.int32, (ch, ch), 1)
    seen = (s_idx >= t_idx) if reverse else (s_idx <= t_idx)
    c0 = d * N_COL
    row_term = col[:, c0:c0 + 1]
    e_inter = col[:, c0 + 1:c0 + 2]
    floor = col[:, c0 + 2:c0 + 3]
    wg_col = col[:, c0 + 3:c0 + 4]
    e_prev = col[0:1, c0 + 4:c0 + 5]

    w = jnp.where(seen, jnp.exp(row_term + a_row), 0.0)
    s = lax.dot_general(q, k, (((1,), (1,)), ((), ())), preferred_element_type=F32) * w
    c_old = c_scr[...]
    num = (e_inter * jnp.dot(q, c_old.astype(BF16), preferred_element_type=F32)
           + jnp.dot(s.astype(BF16), v, preferred_element_type=F32))
    qn = jnp.sum(q.astype(F32) * n, axis=1, keepdims=True)
    den = e_inter * qn + jnp.sum(s, axis=1, keepdims=True)
    h = num * (1.0 / jnp.maximum(jnp.abs(den), floor))

    vw = (wg_col * v.astype(F32)).astype(BF16)
    c_scr[...] = e_prev * c_old + lax.dot_general(k, vw, (((0,), (0,)), ((), ())),
                                                  preferred_element_type=F32)
    n_new = e_prev * n + jnp.sum(wg_col * k.astype(F32), axis=0, keepdims=True)
    return h, n_new


def _mlstm_core_kernel(q_ref, k_ref, v_ref, o_ref, gr_ref, gc_ref, gh_ref, y_ref,
                       hf_scr, hb_scr, cf_scr, cb_scr, *, n_ctx_chunks, out_rows):
    n_chunks = gr_ref.shape[0]
    dqk = q_ref.shape[1]
    cf_scr[...] = jnp.zeros_like(cf_scr)
    cb_scr[...] = jnp.zeros_like(cb_scr)

    def run(c, c_scr, h_scr, n, d, reverse):
        rows = pl.ds(pl.multiple_of(c * CHUNK, CHUNK), CHUNK)
        h, n = _mlstm_chunk(q_ref[rows, :], k_ref[rows, :], v_ref[rows, :], gc_ref[rows, :],
                            gr_ref[c][d:d + 1, :], c_scr, n, d, reverse)
        h_scr[rows, :] = h
        return n

    def body(j, carry):
        nf, nb = carry
        nf = run(j, cf_scr, hf_scr, nf, 0, False)
        cb = jnp.where(j < n_ctx_chunks, n_ctx_chunks - 1 - j, n_chunks - 1 + n_ctx_chunks - j)
        nb = run(cb, cb_scr, hb_scr, nb, 1, True)
        return nf, nb

    zn = jnp.zeros((1, dqk), F32)
    lax.fori_loop(0, n_chunks, body, (zn, zn), unroll=2)

    gh = gh_ref[...]

    def finish(r, _):
        rows = pl.ds(pl.multiple_of(r * out_rows, out_rows), out_rows)
        h = hf_scr[rows, :] + hb_scr[rows, :]
        hn = h * lax.rsqrt(jnp.mean(h * h, axis=-1, keepdims=True) + RMS_EPS)
        y_ref[rows, :] = (_sigmoid(o_ref[rows, :].astype(F32)) * hn * gh).astype(BF16)
        return 0

    lax.fori_loop(0, y_ref.shape[0] // out_rows, finish, 0)


def _mlstm_core(z, gates_r, gates_c, g_head, *, batch, p_rows, ctx, n_heads, dqk, dv):
    n_rows = z.shape[0]
    n_chunks = p_rows // CHUNK
    kern = functools.partial(_mlstm_core_kernel, n_ctx_chunks=ctx // CHUNK, out_rows=ctx)
    return pl.pallas_call(
        kern,
        grid=(batch, n_heads),
        in_specs=[pl.BlockSpec((p_rows, dqk), lambda b, h: (b, h)),
                  pl.BlockSpec((p_rows, dqk), lambda b, h: (b, n_heads + h)),
                  pl.BlockSpec((p_rows, dv), lambda b, h: (b, n_heads + h)),
                  pl.BlockSpec((p_rows, dv), lambda b, h: (b, 2 * n_heads + h)),
                  pl.BlockSpec((None, None, n_chunks, 2, CHUNK), lambda b, h: (b, h, 0, 0, 0)),
                  pl.BlockSpec((None, None, p_rows, 2 * N_COL), lambda b, h: (b, h, 0, 0)),
                  pl.BlockSpec((1, dv), lambda b, h: (0, h))],
        out_specs=pl.BlockSpec((p_rows, dv), lambda b, h: (b, h)),
        out_shape=jax.ShapeDtypeStruct((n_rows, n_heads * dv), BF16),
        scratch_shapes=[pltpu.VMEM((p_rows, dv), F32), pltpu.VMEM((p_rows, dv), F32),
                        pltpu.VMEM((dqk, dv), F32), pltpu.VMEM((dqk, dv), F32)],
        compiler_params=_params(("parallel", "parallel")),
        name="mlstm_core",
    )(z, z, z, z, gates_r, gates_c, g_head)


def _gate_layouts(og, *, batch, p_rows, n_heads):
    g = og[:, :, :batch * 2 * n_heads].reshape(N_GATE_OUT, p_rows, batch, 2, n_heads)
    cols = g[:N_COL].transpose(2, 4, 1, 3, 0).reshape(batch, n_heads, p_rows, 2 * N_COL)
    rows = g[N_COL].reshape(p_rows // CHUNK, CHUNK, batch, 2, n_heads).transpose(2, 4, 0, 3, 1)
    return rows, cols


def _halo_up_kernel(h_ref, w_ref, o_ref):
    o_ref[...] = jnp.dot(h_ref[...], w_ref[...], preferred_element_type=F32)


def _halo_up(h, w_up, *, layer, n_rows, tm):
    d = h.shape[1]
    n = w_up.shape[2]
    nt = n_rows // tm
    ht = h.reshape(h.shape[0] // tm, tm, d)
    first, last = ht[:nt, 0], ht[:nt, tm - 1]
    prev = jnp.concatenate([last[:1], last[:nt - 1]], axis=0)
    nxt = jnp.concatenate([first[1:], first[nt - 1:]], axis=0)
    hh = jnp.stack([prev, nxt], axis=1).reshape(2 * nt, d)
    rows = 2 * nt
    tn = 2 * FFN_CHUNK
    uh = pl.pallas_call(
        _halo_up_kernel,
        grid=(n // tn,),
        in_specs=[pl.BlockSpec((rows, d), lambda j: (0, 0)),
                  _layer_spec((d, tn), layer, lambda j: (0, j))],
        out_specs=pl.BlockSpec((rows, tn), lambda j: (0, j)),
        out_shape=jax.ShapeDtypeStruct((rows, n), F32),
        compiler_params=_params(("parallel",)),
        name="ffn_halo",
    )(hh, w_up)
    return uh.reshape(nt, 2, n)


def _ffn_kernel(h_ref, x_ref, gt_ref, wa0_ref, wb0_ref, wa1_ref, wb1_ref,
                ha0_ref, hb0_ref, ha1_ref, hb1_ref, cwa_ref, cwb_ref, cba_ref, cbb_ref, wd_ref,
                gf_ref, o_ref, acc_scr, u0_scr, u1_scr, a0_scr, a1_scr,
                *, sub, p_rows, ctx, final_norm):
    i = pl.program_id(0)
    j = pl.program_id(1)
    tm = o_ref.shape[0]
    nsb = tm // sub
    fh = wa0_ref.shape[1]
    lt = fh // LANE
    seg = sub + 2 * HALO

    @pl.when(j == 0)
    def _():
        acc_scr[...] = jnp.zeros_like(acc_scr)

    def edge_open(k):
        pos = (i * tm + k * sub) % p_rows
        return jnp.where((pos == 0) | (pos == ctx), 0.0, 1.0)

    opens = [edge_open(k) for k in range(nsb + 1)]

    def up(w_refs, halo_refs, u_scr):
        for br, (w_ref, halo_ref) in enumerate(zip(w_refs, halo_refs)):
            ue = jnp.dot(h_ref[...], w_ref[...], preferred_element_type=F32)
            halo = halo_ref[...]
            for s in range(nsb):
                r0 = s * sub
                b0 = s * seg
                for t in range(lt):
                    lanes = slice(t * LANE, (t + 1) * LANE)
                    tt = br * lt + t
                    before = (jnp.broadcast_to(halo[0:1, lanes], (HALO, LANE)) if s == 0
                              else ue[r0 - HALO:r0, lanes])
                    after = (jnp.broadcast_to(halo[1:2, lanes], (HALO, LANE)) if s == nsb - 1
                             else ue[r0 + sub:r0 + sub + HALO, lanes])
                    u_scr[tt, b0:b0 + HALO, :] = before * opens[s]
                    u_scr[tt, b0 + HALO:b0 + HALO + sub, :] = ue[r0:r0 + sub, lanes]
                    u_scr[tt, b0 + HALO + sub:b0 + seg, :] = after * opens[s + 1]

    def gate(c, u_scr, act_scr):
        cols = slice(c * fh, (c + 1) * fh)
        cws = (cwa_ref[:, cols], cwb_ref[:, cols])
        cbs = (cba_ref[:, cols], cbb_ref[:, cols])

        def conv_piece(br, t, r0):
            lanes = slice(t * LANE, (t + 1) * LANE)
            tt = br * lt + t + jnp.minimum(j, 0)
            cw, cb = cws[br], cbs[br]
            cur = u_scr[tt, r0:r0 + GATE_ROWS, :]
            prv = u_scr[tt, r0 - 1:r0 - 1 + GATE_ROWS, :]
            nxt = u_scr[tt, r0 + 1:r0 + 1 + GATE_ROWS, :]
            return cb[:, lanes] + prv * cw[0:1, lanes] + cur * cw[1:2, lanes] + nxt * cw[2:3, lanes]

        for s in range(nsb):
            for r in range(sub // GATE_ROWS):
                r0 = s * seg + HALO + r * GATE_ROWS
                o0 = s * sub + r * GATE_ROWS
                for t in range(lt):
                    a = conv_piece(0, t, r0)
                    b = conv_piece(1, t, r0)
                    act_scr[o0:o0 + GATE_ROWS, t * LANE:(t + 1) * LANE] = (a * _sigmoid(a) * b).astype(BF16)

    def down(c, act_scr):
        acc_scr[...] += jnp.dot(act_scr[...], wd_ref[c * fh:(c + 1) * fh, :], preferred_element_type=F32)

    up((wa0_ref, wb0_ref), (ha0_ref, hb0_ref), u0_scr)
    up((wa1_ref, wb1_ref), (ha1_ref, hb1_ref), u1_scr)
    gate(0, u0_scr, a0_scr)
    down(0, a0_scr)
    gate(1, u1_scr, a1_scr)
    down(1, a1_scr)

    @pl.when(j == pl.num_programs(1) - 1)
    def _():
        for s in range(nsb):
            if not final_norm:
                rows = slice(s * sub, (s + 1) * sub)
                o_ref[rows, :] = x_ref[rows, :] + gt_ref[s] * acc_scr[rows, :]
                continue
            for r in range(s * sub, (s + 1) * sub, NORM_ROWS):
                rows = slice(r, r + NORM_ROWS)
                xn = x_ref[rows, :] + gt_ref[s] * acc_scr[rows, :]
                ms = jnp.mean(xn * xn, axis=-1, keepdims=True)
                o_ref[rows, :] = xn * lax.rsqrt(ms + RMS_EPS) * gf_ref[...]


def _conv_ffn(h, u, mods, w_up, conv_w, conv_b, w_down, g_final, *, layer, mod_layer, n_rows, tm, sub,
              p_rows, ctx, final_norm):
    d = u.shape[1]
    d_ff = w_down.shape[1]
    fc = FFN_CHUNK
    assert d_ff % fc == 0
    fh = fc // 2
    nfc = d_ff // fc
    nsb = tm // sub
    kern = functools.partial(_ffn_kernel, sub=sub, p_rows=p_rows, ctx=ctx, final_norm=final_norm)
    u_shape = (fc // LANE, nsb * (sub + 2 * HALO), LANE)
    halo = _halo_up(h, w_up, layer=layer, n_rows=n_rows, tm=tm)

    def half_specs(shape, lead):
        return [pl.BlockSpec(shape, lambda i, j, c=c: lead(i) + (c + 2 * j,))
                for c in (0, 2 * nfc, 1, 2 * nfc + 1)]

    return pl.pallas_call(
        kern,
        grid=(n_rows // tm, nfc),
        in_specs=[pl.BlockSpec((tm, d), lambda i, j: (i, 0)),
                  pl.BlockSpec((tm, d), lambda i, j: (i, 0)),
                  _mod_spec(nsb, d, mod_layer, MOD_GATE2)]
        + half_specs((None, d, fh), lambda i: (layer, 0))
        + half_specs((None, 2, fh), lambda i: (i, 0))
        + [_layer_spec((CONV_W, fc), layer, lambda i, j: (0, j)),
           _layer_spec((CONV_W, fc), layer, lambda i, j: (0, nfc + j)),
           _layer_spec((1, fc), layer, lambda i, j: (0, j)),
           _layer_spec((1, fc), layer, lambda i, j: (0, nfc + j)),
           _layer_spec((fc, d), layer, lambda i, j: (j, 0)),
           pl.BlockSpec((1, d), lambda i, j: (0, 0))],
        out_specs=pl.BlockSpec((tm, d), lambda i, j: (i, 0)),
        out_shape=jax.ShapeDtypeStruct((n_rows, d), F32),
        scratch_shapes=[pltpu.VMEM((tm, d), F32), pltpu.VMEM(u_shape, F32), pltpu.VMEM(u_shape, F32),
                        pltpu.VMEM((tm, fh), BF16), pltpu.VMEM((tm, fh), BF16)],
        compiler_params=_params(("parallel", "arbitrary")),
        name="conv_ffn",
    )(h, u, mods, w_up, w_up, w_up, w_up, halo, halo, halo, halo,
      conv_w, conv_w, conv_b, conv_b, w_down, g_final)


def _rope_tables(batch, seq, ctx):
    rows = seq // GRID_W
    row = jnp.repeat(jnp.arange(rows, dtype=F32), GRID_W)
    col = jnp.tile(jnp.arange(GRID_W, dtype=F32), rows)
    n_freq = HEAD_DIM // 4
    inv_freq = ROPE_THETA ** (-jnp.arange(n_freq, dtype=F32) / n_freq)
    ang = jnp.concatenate([row[:, None] * inv_freq, col[:, None] * inv_freq], axis=-1)
    ang = jnp.concatenate([ang, ang], axis=-1)
    sign = jnp.concatenate([-jnp.ones((HEAD_DIM // 2,), F32), jnp.ones((HEAD_DIM // 2,), F32)])
    cos = jnp.concatenate([jnp.ones((ctx, HEAD_DIM), F32), jnp.cos(ang)], axis=0)
    sin = jnp.concatenate([jnp.zeros((ctx, HEAD_DIM), F32), jnp.sin(ang) * sign], axis=0)
    return jnp.tile(cos, (batch, 1)), jnp.tile(sin, (batch, 1))


def kernel(x, c, ctx, c_ctx, w_mod, b_mod, g_mix, g_ffn, attn_w_qkv, attn_sink, attn_w_o,
           mlstm_w_in, mlstm_b_in, mlstm_g_head, mlstm_w_o, ffn_w_up, ffn_conv_w, ffn_conv_b,
           ffn_w_down, g_final):
    batch, seq, d = x.shape
    lc = ctx.shape[1]
    depth = w_mod.shape[0]
    p_rows = lc + seq
    sub = lc
    assert seq % sub == 0 and sub % WINDOW == 0 and sub % CHUNK == 0 and seq % GRID_W == 0
    n_sub = batch * p_rows // sub
    tm = 2 * sub if n_sub % 2 == 0 else sub
    n_q = attn_sink.shape[1]
    n_kv = (attn_w_qkv.shape[2] // HEAD_DIM - n_q) // 2
    n_heads = (mlstm_w_in.shape[2] - 3 * d) // 4
    dv = d // n_heads
    dqk = dv // 2
    main_cols = 3 * d

    x_rows = x.reshape(batch * seq, d)
    c_rows_in = ctx.reshape(batch * lc, d)
    split = (p_rows // sub, lc // sub)
    u = None

    n_c = batch + 1
    c_rows = -(-n_c // 8) * 8
    cvec = jnp.concatenate([c, c_ctx[None, :], jnp.zeros((c_rows - n_c, d), F32)], axis=0)
    mods = _modulation(cvec, w_mod, b_mod)
    mods_sb = jnp.concatenate(
        [jnp.broadcast_to(mods[:, None, batch:batch + 1], (depth, batch, lc // sub, 6 * d)),
         jnp.broadcast_to(mods[:, :batch, None], (depth, batch, seq // sub, 6 * d))],
        axis=2).reshape(depth, n_sub, 6, 1, d)

    mods_lat = jnp.broadcast_to(mods[depth - 1:, :batch, None], (1, batch, seq // sub, 6 * d)
                                ).reshape(1, batch * seq // sub, 6, 1, d)

    cos_u, sin_u = _rope_tables(batch, seq, lc)

    w_qkv = attn_w_qkv.astype(BF16)
    w_attn_o = attn_w_o.astype(BF16)
    w_in = mlstm_w_in.astype(BF16)
    w_mlstm_o = mlstm_w_o.astype(BF16)
    w_up = ffn_w_up.astype(BF16)
    w_down = ffn_w_down.astype(BF16)
    g_mix3 = g_mix.reshape(depth, 1, d)
    g_ffn3 = g_ffn.reshape(depth, 1, d)
    b_in3 = mlstm_b_in.reshape(mlstm_b_in.shape[0], 1, -1)
    conv_b3 = ffn_conv_b.reshape(depth, 1, -1)
    g_fin = g_final.reshape(1, d)

    for i in range(depth):
        jm = i // 2
        if i % 2 == 0:
            first = dict(split=split, xc=c_rows_in) if i == 0 else {}
            qkv = _qkv_proj(x_rows if i == 0 else u, g_mix3, mods_sb, w_qkv, cos_u, sin_u, layer=i,
                            w_layer=jm, tm=tm, sub=sub, n_q=n_q, n_kv=n_kv, **first)
            y = _attention(qkv, attn_sink[jm], batch=batch, seq=seq, ctx=lc, n_q=n_q, n_kv=n_kv)
            w_o = w_attn_o
        else:
            wgt = mlstm_w_in[jm][:, main_cols:].reshape(d, 4, n_heads)
            bgt = mlstm_b_in[jm][main_cols:].reshape(4, n_heads)
            lane_pad = GATE_F_LANE - 2 * n_heads
            wg = jnp.concatenate([wgt[:, 0], wgt[:, 2], jnp.zeros((d, lane_pad), F32),
                                  wgt[:, 1], wgt[:, 3], jnp.zeros((d, lane_pad), F32)], axis=1).astype(BF16)
            bg = jnp.concatenate([bgt[0], bgt[2], jnp.zeros((lane_pad,), F32),
                                  bgt[1], bgt[3], jnp.zeros((lane_pad,), F32)]).reshape(1, LANE)
            z, gates = _mlstm_in_proj(u, g_mix3, mods_sb, w_in, b_in3, wg, bg, layer=i, w_layer=jm,
                                      tm=tm, sub=sub, dqk=dqk)
            og = _mlstm_gates(gates, batch=batch, p_rows=p_rows, ctx=lc, n_heads=n_heads)
            gates_r, gates_c = _gate_layouts(og, batch=batch, p_rows=p_rows, n_heads=n_heads)
            y = _mlstm_core(z, gates_r, gates_c, mlstm_g_head[jm].reshape(1, n_heads * dv),
                            batch=batch, p_rows=p_rows, ctx=lc, n_heads=n_heads, dqk=dqk, dv=dv)
            w_o = w_mlstm_o
        if i == 0:
            u, h2 = _proj_residual(y, w_o, x_rows, mods_sb, g_ffn3, layer=i, w_layer=jm, tm=tm, sub=sub,
                                   split=split, xc=c_rows_in)
            u = _conv_ffn(h2, u, mods_sb, w_up, ffn_conv_w, conv_b3, w_down, g_fin, layer=i, mod_layer=i,
                          n_rows=batch * p_rows, tm=tm, sub=sub, p_rows=p_rows, ctx=lc,
                          final_norm=False)
        elif i < depth - 1:
            u, h2 = _proj_residual(y, w_o, u, mods_sb, g_ffn3, layer=i, w_layer=jm, tm=tm, sub=sub)
            u = _conv_ffn(h2, u, mods_sb, w_up, ffn_conv_w, conv_b3, w_down, g_fin, layer=i, mod_layer=i,
                          n_rows=batch * p_rows, tm=tm, sub=sub, p_rows=p_rows, ctx=lc,
                          final_norm=False)
        else:
            u, h2 = _proj_residual(y, w_o, u, mods_sb, g_ffn3, layer=i, w_layer=jm, tm=tm, sub=sub,
                                   latent_only=(p_rows // sub, lc // sub))
            u = _conv_ffn(h2, u, mods_lat, w_up, ffn_conv_w, conv_b3, w_down, g_fin, layer=i, mod_layer=0,
                          n_rows=batch * seq, tm=tm, sub=sub, p_rows=seq, ctx=0,
                          final_norm=True)
    return u.reshape(batch, seq, d)
```
